```python
import jax, jax.numpy as jnp
from jax import lax
import numpy as np

D_MODEL = 2048
BATCH = 4
SEQ = 2048
DEPTH = 1

FOX_HEAD_DIM = 128
FOX_WIDTH = D_MODEL // 2
FOX_HEADS = FOX_WIDTH // FOX_HEAD_DIM
RWKV_HEAD_DIM = 64
RWKV_WIDTH = D_MODEL // 2
RWKV_HEADS = RWKV_WIDTH // RWKV_HEAD_DIM
DECAY_LORA = max(32, int(round(D_MODEL ** 0.5 * 1.8 / 32)) * 32)
AAA_LORA = max(32, int(round(D_MODEL ** 0.5 * 1.8 / 32)) * 32)
Q_BLOCK = 128
RMS_EPS = 1e-6
GN_EPS = 64e-5
L2_EPS = 1e-12

FOX_SIZES = (FOX_WIDTH, FOX_WIDTH, FOX_WIDTH, FOX_WIDTH, FOX_HEADS)
RWKV_SIZES = (RWKV_WIDTH, RWKV_WIDTH, RWKV_WIDTH, RWKV_WIDTH, DECAY_LORA, AAA_LORA)
FOX_COLS = sum(FOX_SIZES)
RWKV_COLS = sum(RWKV_SIZES)
IN_COLS = FOX_COLS + RWKV_COLS + 2 * D_MODEL

kernel_name = "fox_rwkv7_gated_parallel_hybrid"


def _split(u, sizes):
    idx = [int(i) for i in np.cumsum(sizes)[:-1]]
    return jnp.split(u, idx, axis=-1)


def _rmsnorm(x, g):
    xf = x.astype(jnp.float32)
    y = xf * lax.rsqrt(jnp.mean(xf * xf, axis=-1, keepdims=True) + RMS_EPS)
    return (y * g.astype(jnp.float32)).astype(x.dtype)


def _token_shift(u, mu):
    prev = jnp.pad(u, ((0, 0), (1, 0), (0, 0)))[:, :-1]
    return u + (prev - u) * mu


def _fox_attention(q, k, v, log_f):
    c = jnp.cumsum(log_f, axis=-1)
    T = q.shape[2]
    scale = FOX_HEAD_DIM ** -0.5
    outs = []
    for i in range(T // Q_BLOCK):
        s0, e = i * Q_BLOCK, (i + 1) * Q_BLOCK
        qb, kb, vb = q[:, :, s0:e], k[:, :, :e], v[:, :, :e]
        logits = (jnp.einsum('bhqd,bhkd->bhqk', qb, kb) * scale
                  + c[:, :, s0:e, None] - c[:, :, None, :e])
        causal = jnp.arange(e)[None, :] <= jnp.arange(s0, e)[:, None]
        logits = jnp.where(causal, logits, -jnp.inf)
        p = jax.nn.softmax(logits, axis=-1)
        outs.append(jnp.einsum('bhqk,bhkd->bhqd', p, vb))
    return jnp.concatenate(outs, axis=2)


def _rwkv7_scan(r, decay, k, v, kk, b):
    B, T, H, N = r.shape
    S0 = jnp.zeros((B, H, N, N), jnp.float32)
    xs = tuple(jnp.moveaxis(t, 1, 0) for t in (r, decay, k, v, kk, b))

    def step(S, inp):
        r_t, w_t, k_t, v_t, kk_t, b_t = inp
        sa = jnp.einsum('bhvk,bhk->bhv', S, -kk_t)
        S = (S * w_t[:, :, None, :] + sa[..., None] * b_t[:, :, None, :]
             + v_t[..., None] * k_t[:, :, None, :])
        y = jnp.einsum('bhvk,bhk->bhv', S, r_t)
        return S, y

    _, ys = lax.scan(step, S0, xs)
    return jnp.moveaxis(ys, 0, 1)


def _hybrid_layer(x, norm_gain, w_in, fox_forget_bias, rwkv_shift_mix, rwkv_w0, rwkv_w2,
                  rwkv_a0, rwkv_a2, rwkv_k_k, rwkv_k_a, rwkv_r_k, rwkv_ln_w, rwkv_ln_b,
                  w_proj_fox, w_proj_rwkv, w_out):
    B, T, _ = x.shape
    f32 = jnp.float32
    h = _rmsnorm(x, norm_gain)
    u = h @ w_in
    u_fox, u_rwkv, g_a, g_b = _split(u, (FOX_COLS, RWKV_COLS, D_MODEL, D_MODEL))

    q, k, v, z_a, f_logit = _split(u_fox.astype(f32), FOX_SIZES)
    to_heads = lambda t: t.reshape(B, T, FOX_HEADS, FOX_HEAD_DIM).transpose(0, 2, 1, 3)
    log_f = jax.nn.log_sigmoid(f_logit + fox_forget_bias.astype(f32)).transpose(0, 2, 1)
    o_a = _fox_attention(to_heads(q), to_heads(k), to_heads(v), log_f)
    o_a = o_a.transpose(0, 2, 1, 3).reshape(B, T, FOX_WIDTH) * jax.nn.silu(z_a)

    u_rwkv = _token_shift(u_rwkv.astype(f32), rwkv_shift_mix.astype(f32))
    r, kr, vr, z_b, w_down, a_down = _split(u_rwkv, RWKV_SIZES)
    w = -jax.nn.softplus(-(rwkv_w0.astype(f32) + jnp.tanh(w_down) @ rwkv_w2.astype(f32))) - 0.5
    decay = jnp.exp(-jnp.exp(w))
    a = jax.nn.sigmoid(rwkv_a0.astype(f32) + a_down @ rwkv_a2.astype(f32))
    heads = lambda t: t.reshape(B, T, RWKV_HEADS, RWKV_HEAD_DIM)
    kk = heads(kr * rwkv_k_k.astype(f32))
    kk = kk / jnp.maximum(jnp.sqrt(jnp.sum(kk * kk, axis=-1, keepdims=True)), L2_EPS)
    kr = kr * (1.0 + (a - 1.0) * rwkv_k_a.astype(f32))
    rh, kh, vh, ah = heads(r), heads(kr), heads(vr), heads(a)
    y = _rwkv7_scan(rh, heads(decay), kh, vh, kk, kk * ah)
    mu = jnp.mean(y, axis=-1, keepdims=True)
    var = jnp.mean(jnp.square(y - mu), axis=-1, keepdims=True)
    y = ((y - mu) * lax.rsqrt(var + GN_EPS)).reshape(B, T, RWKV_WIDTH)
    y = y * rwkv_ln_w.astype(f32) + rwkv_ln_b.astype(f32)
    bonus = jnp.sum(rh * kh * rwkv_r_k.astype(f32), axis=-1, keepdims=True) * vh
    o_b = (y + bonus.reshape(B, T, RWKV_WIDTH)) * jax.nn.silu(z_b)

    o_a = o_a.astype(x.dtype) @ w_proj_fox
    o_b = o_b.astype(x.dtype) @ w_proj_rwkv
    m = jax.nn.sigmoid(g_a) * o_a + jax.nn.sigmoid(g_b) * o_b
    return x + m @ w_out


def setup_inputs(seed: int = 0) -> dict:
    key = jax.random.key(seed)
    ks = jax.random.split(key, 20)
    n = jax.random.normal
    L, D = DEPTH, D_MODEL
    return {
        "x": n(ks[0], (BATCH, SEQ, D), jnp.float32),
        "norm_gain": 1.0 + 0.05 * n(ks[1], (L, D), jnp.float32),
        "w_in": n(ks[2], (L, D, IN_COLS), jnp.float32) * D ** -0.5,
        "fox_forget_bias": 3.0 + 0.5 * n(ks[3], (L, FOX_HEADS), jnp.float32),
        "rwkv_shift_mix": jax.random.uniform(ks[4], (L, RWKV_COLS), jnp.float32),
        "rwkv_w0": jax.random.uniform(ks[5], (L, RWKV_WIDTH), jnp.float32, -6.5, -1.5),
        "rwkv_w2": n(ks[6], (L, DECAY_LORA, RWKV_WIDTH), jnp.float32) * 0.5 * DECAY_LORA ** -0.5,
        "rwkv_a0": 0.1 * n(ks[7], (L, RWKV_WIDTH), jnp.float32),
        "rwkv_a2": n(ks[8], (L, AAA_LORA, RWKV_WIDTH), jnp.float32) * AAA_LORA ** -0.5,
        "rwkv_k_k": 0.85 + 0.05 * n(ks[9], (L, RWKV_WIDTH), jnp.float32),
        "rwkv_k_a": 1.0 + 0.05 * n(ks[10], (L, RWKV_WIDTH), jnp.float32),
        "rwkv_r_k": -0.04 + 0.02 * n(ks[11], (L, RWKV_HEADS, RWKV_HEAD_DIM), jnp.float32),
        "rwkv_ln_w": 1.0 + 0.05 * n(ks[12], (L, RWKV_WIDTH), jnp.float32),
        "rwkv_ln_b": 0.02 * n(ks[13], (L, RWKV_WIDTH), jnp.float32),
        "w_proj_fox": n(ks[14], (L, FOX_WIDTH, D), jnp.float32) * FOX_WIDTH ** -0.5,
        "w_proj_rwkv": n(ks[15], (L, RWKV_WIDTH, D), jnp.float32) * RWKV_WIDTH ** -0.5,
        "w_out": n(ks[16], (L, D, D), jnp.float32) * D ** -0.5,
        "final_norm_gain": 1.0 + 0.05 * n(ks[17], (D,), jnp.float32),
    }


def reference(x, norm_gain, w_in, fox_forget_bias, rwkv_shift_mix, rwkv_w0, rwkv_w2,
              rwkv_a0, rwkv_a2, rwkv_k_k, rwkv_k_a, rwkv_r_k, rwkv_ln_w, rwkv_ln_b,
              w_proj_fox, w_proj_rwkv, w_out, final_norm_gain):
    h = x
    for l in range(DEPTH):
        h = _hybrid_layer(h, norm_gain[l], w_in[l], fox_forget_bias[l], rwkv_shift_mix[l],
                          rwkv_w0[l], rwkv_w2[l], rwkv_a0[l], rwkv_a2[l], rwkv_k_k[l],
                          rwkv_k_a[l], rwkv_r_k[l], rwkv_ln_w[l], rwkv_ln_b[l],
                          w_proj_fox[l], w_proj_rwkv[l], w_out[l])
    return _rmsnorm(h, final_norm_gain)
```

```python
import functools

import jax
import jax.numpy as jnp
from jax import lax
from jax.experimental import pallas as pl
from jax.experimental.pallas import tpu as pltpu

D_MODEL = 2048
FOX_HEADS = 8
FOX_HEAD_DIM = 128
FOX_WIDTH = FOX_HEADS * FOX_HEAD_DIM
RWKV_HEADS = 16
RWKV_HEAD_DIM = 64
RWKV_WIDTH = RWKV_HEADS * RWKV_HEAD_DIM
LORA = 96
RMS_EPS = 1e-6
GN_EPS = 64e-5
L2_EPS = 1e-12

LANES = 128
VMEM_LIMIT = 56 * 1024 * 1024

COL_GA = 0
COL_GB = D_MODEL // LANES
COL_ZA = 2 * D_MODEL // LANES
COL_R = COL_ZA + FOX_WIDTH // LANES
COL_K = COL_R + RWKV_WIDTH // LANES
COL_V = COL_K + RWKV_WIDTH // LANES
COL_ZB = COL_V + RWKV_WIDTH // LANES
WIDE_COLS = (COL_ZB + RWKV_WIDTH // LANES) * LANES
SMALL_COLS = 3 * LANES

RWKV_CHUNK = 64
RWKV_BLOCK = 256
HEADS_PER_GROUP = LANES // RWKV_HEAD_DIM
GROUP_ROWS = HEADS_PER_GROUP * RWKV_CHUNK

f32 = jnp.float32
bf16 = jnp.bfloat16


def _dot(a, b):
    return jnp.dot(a.astype(bf16), b.astype(bf16), preferred_element_type=f32)


def _dot_nt(a, b):
    return lax.dot_general(a.astype(bf16), b.astype(bf16), (((1,), (1,)), ((), ())),
                           preferred_element_type=f32)


def _dot_tn(a, b):
    return lax.dot_general(a.astype(bf16), b.astype(bf16), (((0,), (0,)), ((), ())),
                           preferred_element_type=f32)


def _dot_exact_ones(a, ones_bf16, nt=False):
    a1 = a.astype(bf16)
    r1 = a - a1.astype(f32)
    a2 = r1.astype(bf16)
    a3 = (r1 - a2.astype(f32)).astype(bf16)
    if nt:
        mm = lambda x: lax.dot_general(ones_bf16, x, (((1,), (0,)), ((), ())),
                                       preferred_element_type=f32)
    else:
        mm = lambda x: jnp.dot(x, ones_bf16, preferred_element_type=f32)
    return mm(a1) + mm(a2) + mm(a3)


def _softplus(x):
    return jnp.maximum(x, 0.0) + jnp.log1p(jnp.exp(-jnp.abs(x)))


def _sigmoid(x):
    return 1.0 / (1.0 + jnp.exp(-x))


def _norm_matmul_kernel(x_ref, g_ref, w_ref, o_ref, h_ref):
    @pl.when(pl.program_id(1) == 0)
    def _():
        xf = x_ref[...]
        ms = jnp.mean(xf * xf, axis=-1, keepdims=True)
        h_ref[...] = (xf * lax.rsqrt(ms + RMS_EPS) * g_ref[...]).astype(bf16)

    o_ref[...] = jnp.dot(h_ref[...], w_ref[...],
                         preferred_element_type=f32).astype(o_ref.dtype)


def _norm_matmul(x2d, gain, w, out_dtype, tm, tn):
    m, d = x2d.shape
    n = w.shape[1]
    return pl.pallas_call(
        _norm_matmul_kernel,
        grid=(m // tm, n // tn),
        in_specs=[
            pl.BlockSpec((tm, d), lambda i, j: (i, 0)),
            pl.BlockSpec((1, d), lambda i, j: (0, 0)),
            pl.BlockSpec((d, tn), lambda i, j: (0, j)),
        ],
        out_specs=pl.BlockSpec((tm, tn), lambda i, j: (i, j)),
        out_shape=jax.ShapeDtypeStruct((m, n), out_dtype),
        scratch_shapes=[pltpu.VMEM((tm, d), bf16)],
        compiler_params=pltpu.CompilerParams(
            dimension_semantics=("parallel", "arbitrary"),
            vmem_limit_bytes=VMEM_LIMIT),
        name="norm_matmul",
    )(x2d, gain, w)


def _gate_kernel(f_ref, bias_ref, c_ref, carry_ref, *, tb):
    @pl.when(pl.program_id(1) == 0)
    def _():
        carry_ref[...] = jnp.zeros_like(carry_ref)

    z = f_ref[...] + bias_ref[...]
    log_f = -_softplus(-z)
    row = lax.broadcasted_iota(jnp.int32, (tb, tb), 0)
    col = lax.broadcasted_iota(jnp.int32, (tb, tb), 1)
    tri = jnp.where(col <= row, 1.0, 0.0).astype(bf16)
    c = _dot_exact_ones(log_f, tri, nt=True) + carry_ref[0:1, :]
    c_ref[...] = c
    carry_ref[0:1, :] = c[tb - 1:tb, :]


def _gate_cumsum(small, bias_row, batch, seq):
    tb = 256
    nt = seq // tb
    return pl.pallas_call(
        functools.partial(_gate_kernel, tb=tb),
        grid=(batch, nt),
        in_specs=[
            pl.BlockSpec((tb, LANES), lambda b, t: (b * nt + t, 2)),
            pl.BlockSpec((1, LANES), lambda b, t: (0, 0)),
        ],
        out_specs=pl.BlockSpec((tb, LANES), lambda b, t: (b * nt + t, 0)),
        out_shape=jax.ShapeDtypeStruct((batch * seq, LANES), f32),
        scratch_shapes=[pltpu.VMEM((8, LANES), f32)],
        compiler_params=pltpu.CompilerParams(
            dimension_semantics=("parallel", "arbitrary")),
        name="fox_gate_cumsum",
    )(small, bias_row)


def _fox_kernel(q_ref, k_ref, v_ref, c_ref, z_ref, o_ref, m_ref, l_ref, acc_ref, *, tq, tk):
    i = pl.program_id(2)
    j = pl.program_id(3)
    last_j = (i * tq + tq - 1) // tk

    @pl.when(j == 0)
    def _():
        m_ref[...] = jnp.full_like(m_ref, -jnp.inf)
        l_ref[...] = jnp.zeros_like(l_ref)
        acc_ref[...] = jnp.zeros_like(acc_ref)

    def step(masked):
        s = lax.dot_general(q_ref[...], k_ref[...], (((1,), (1,)), ((), ())),
                            preferred_element_type=f32)
        s = s * (FOX_HEAD_DIM ** -0.5) - c_ref[...]
        if masked:
            rows = i * tq + lax.broadcasted_iota(jnp.int32, (tq, tk), 0)
            cols = j * tk + lax.broadcasted_iota(jnp.int32, (tq, tk), 1)
            s = jnp.where(cols <= rows, s, -jnp.inf)
        m_prev = m_ref[...]
        m_new = jnp.maximum(m_prev, jnp.max(s, axis=-1, keepdims=True))
        p = jnp.exp(s - m_new)
        alpha = jnp.exp(m_prev - m_new)
        l_ref[...] = alpha * l_ref[...] + jnp.sum(p, axis=-1, keepdims=True)
        acc_ref[...] = alpha * acc_ref[...] + jnp.dot(
            p.astype(bf16), v_ref[...], preferred_element_type=f32)
        m_ref[...] = m_new

    @pl.when((j + 1) * tk - 1 <= i * tq)
    def _():
        step(False)

    @pl.when(jnp.logical_and((j + 1) * tk - 1 > i * tq, j <= last_j))
    def _():
        step(True)

    @pl.when(j == last_j)
    def _():
        z = z_ref[...]
        o = acc_ref[...] / l_ref[...]
        o_ref[...] = (o * (z * _sigmoid(z))).astype(o_ref.dtype)


def _fox_attention(qkv, wide, c_rows, batch, seq):
    tq = tk = 512
    nq, nk = seq // tq, seq // tk
    h8 = FOX_HEADS

    def kv_block(b, h, i, j):
        return b * nk + jnp.minimum(j, (i * tq + tq - 1) // tk)

    return pl.pallas_call(
        functools.partial(_fox_kernel, tq=tq, tk=tk),
        grid=(batch, h8, nq, nk),
        in_specs=[
            pl.BlockSpec((tq, LANES), lambda b, h, i, j: (b * nq + i, h)),
            pl.BlockSpec((tk, LANES), lambda b, h, i, j: (kv_block(b, h, i, j), h8 + h)),
            pl.BlockSpec((tk, LANES), lambda b, h, i, j: (kv_block(b, h, i, j), 2 * h8 + h)),
            pl.BlockSpec((None, 1, tk),
                         lambda b, h, i, j: (b * h8 + h, 0, jnp.minimum(j, (i * tq + tq - 1) // tk))),
            pl.BlockSpec((tq, LANES), lambda b, h, i, j: (b * nq + i, COL_ZA + h)),
        ],
        out_specs=pl.BlockSpec((tq, LANES), lambda b, h, i, j: (b * nq + i, h)),
        out_shape=jax.ShapeDtypeStruct((batch * seq, FOX_WIDTH), bf16),
        scratch_shapes=[
            pltpu.VMEM((tq, 1), f32),
            pltpu.VMEM((tq, 1), f32),
            pltpu.VMEM((tq, LANES), f32),
        ],
        compiler_params=pltpu.CompilerParams(
            dimension_semantics=("parallel", "parallel", "parallel", "arbitrary"),
            vmem_limit_bytes=VMEM_LIMIT),
        name="fox_attention",
    )(qkv, qkv, qkv, c_rows, wide)


P_MU_R, P_MU_K, P_MU_V, P_MU_Z, P_W0, P_A0, P_KK, P_KA, P_RK, P_LNW, P_LNB = range(11)
P_ROWS = 16


def _rwkv_kernel(r_ref, k_ref, v_ref, z_ref, wd_ref, ad_ref, pc_ref, ps_ref, w2_ref, a2_ref,
                 o_ref, state_ref, prev_ref, *, tb):
    C = RWKV_CHUNK
    G = GROUP_ROWS
    N = RWKV_HEAD_DIM

    @pl.when(pl.program_id(2) == 0)
    def _():
        state_ref[...] = jnp.zeros_like(state_ref)
        prev_ref[...] = jnp.zeros_like(prev_ref)

    row_in_block = lax.broadcasted_iota(jnp.int32, (tb, LANES), 0)

    def shifted(ref, slot, mu):
        u = ref[...]
        prev = jnp.where(row_in_block == 0, prev_ref[slot, 0:1, :], pltpu.roll(u, 1, 0))
        prev_ref[slot, 0:1, :] = u[tb - 1:tb, :]
        return u + (prev - u) * mu

    prm = lambda idx: pc_ref[idx:idx + 1, :]
    r = shifted(r_ref, 0, prm(P_MU_R))
    kr = shifted(k_ref, 1, prm(P_MU_K))
    vr = shifted(v_ref, 2, prm(P_MU_V))
    zb = shifted(z_ref, 3, prm(P_MU_Z))
    wd = shifted(wd_ref, 4, ps_ref[0:1, :])
    ad = shifted(ad_ref, 5, ps_ref[1:2, :])

    w = -_softplus(-(prm(P_W0) + _dot(jnp.tanh(wd), w2_ref[...]))) - 0.5
    log_decay = -jnp.exp(w)
    rate = _sigmoid(prm(P_A0) + _dot(ad, a2_ref[...]))

    li = lax.broadcasted_iota(jnp.int32, (LANES, LANES), 0)
    lj = lax.broadcasted_iota(jnp.int32, (LANES, LANES), 1)
    head_ones = jnp.where(li // N == lj // N, 1.0, 0.0).astype(bf16)
    head_sum = lambda x: _dot_exact_ones(x, head_ones)

    kk = kr * prm(P_KK)
    kk = kk / jnp.maximum(jnp.sqrt(head_sum(kk * kk)), L2_EPS)
    kp = kr * (1.0 + (rate - 1.0) * prm(P_KA))
    bb = kk * rate
    bonus = head_sum(r * kp * prm(P_RK)) * vr

    ti = lax.broadcasted_iota(jnp.int32, (tb, tb), 0)
    tj = lax.broadcasted_iota(jnp.int32, (tb, tb), 1)
    chunk_tri = jnp.where(jnp.logical_and(tj <= ti, ti // C == tj // C), 1.0, 0.0).astype(bf16)
    ci = _dot_exact_ones(log_decay, chunk_tri, nt=True)

    a_t = -kk * jnp.exp(ci - log_decay)
    r_t = r * jnp.exp(ci)
    inv = jnp.exp(-ci)
    b_t = bb * inv
    k_t = kp * inv

    lane = lax.broadcasted_iota(jnp.int32, (1, LANES), 1)
    head_masks = [jnp.where(lane // N == h, 1.0, 0.0) for h in range(HEADS_PER_GROUP)]
    stack = lambda x: jnp.concatenate([x * hm for hm in head_masks], axis=0)

    gi = lax.broadcasted_iota(jnp.int32, (G, G), 0)
    gj = lax.broadcasted_iota(jnp.int32, (G, G), 1)
    same_head = gi // C == gj // C
    strict = jnp.logical_and(same_head, gj < gi)
    incl = jnp.logical_and(same_head, gj <= gi)
    eye = jnp.where(gi == gj, 1.0, 0.0)

    state = state_ref[...]
    ys = []
    for c in range(tb // C):
        sl = slice(c * C, (c + 1) * C)
        ci_c = ci[sl]
        c_last = ci_c[C - 1:C, :]
        to_end = jnp.exp(c_last - ci_c)
        xa, xr = stack(a_t[sl]), stack(r_t[sl])
        yb, yk = stack(b_t[sl]), stack(k_t[sl])
        vs = stack(vr[sl])
        bh, kh = stack(bb[sl] * to_end), stack(kp[sl] * to_end)

        big = _dot_nt(jnp.concatenate([xa, xr], axis=0), jnp.concatenate([yb, yk], axis=0))
        a_ab = jnp.where(strict, big[0:G, 0:G], 0.0)
        a_ak = jnp.where(strict, big[0:G, G:2 * G], 0.0)
        a_rb = jnp.where(incl, big[G:2 * G, 0:G], 0.0)
        a_rk = jnp.where(incl, big[G:2 * G, G:2 * G], 0.0)

        tinv = eye + a_ab
        pw = a_ab
        for _ in range(5):
            pw = _dot(pw, pw)
            tinv = tinv + _dot(tinv, pw)

        xs = _dot_nt(jnp.concatenate([xa, xr], axis=0), state)
        u = _dot(tinv, xs[0:G] + _dot(a_ak, vs))
        uv = jnp.concatenate([u, vs], axis=0)
        y_st = xs[G:2 * G] + _dot(jnp.concatenate([a_rb, a_rk], axis=1), uv)
        ys.append(y_st[0:C] + y_st[C:2 * C])
        state = state * jnp.exp(c_last) + _dot_tn(uv, jnp.concatenate([bh, kh], axis=0))
    state_ref[...] = state

    y = jnp.concatenate(ys, axis=0)
    mean = head_sum(y) * (1.0 / N)
    yc = y - mean
    var = head_sum(yc * yc) * (1.0 / N)
    yn = yc * lax.rsqrt(var + GN_EPS) * prm(P_LNW) + prm(P_LNB)
    o_ref[...] = ((yn + bonus) * (zb * _sigmoid(zb))).astype(o_ref.dtype)


def _rwkv_mix(wide, small, pc, ps, w2p, a2p, batch, seq):
    tb = RWKV_BLOCK
    nt = seq // tb
    groups = RWKV_WIDTH // LANES
    row = lambda b, g, t: b * nt + t
    return pl.pallas_call(
        functools.partial(_rwkv_kernel, tb=tb),
        grid=(batch, groups, nt),
        in_specs=[
            pl.BlockSpec((tb, LANES), lambda b, g, t: (row(b, g, t), COL_R + g)),
            pl.BlockSpec((tb, LANES), lambda b, g, t: (row(b, g, t), COL_K + g)),
            pl.BlockSpec((tb, LANES), lambda b, g, t: (row(b, g, t), COL_V + g)),
            pl.BlockSpec((tb, LANES), lambda b, g, t: (row(b, g, t), COL_ZB + g)),
            pl.BlockSpec((tb, LANES), lambda b, g, t: (row(b, g, t), 0)),
            pl.BlockSpec((tb, LANES), lambda b, g, t: (row(b, g, t), 1)),
            pl.BlockSpec((P_ROWS, LANES), lambda b, g, t: (0, g)),
            pl.BlockSpec((8, LANES), lambda b, g, t: (0, 0)),
            pl.BlockSpec((LANES, LANES), lambda b, g, t: (0, g)),
            pl.BlockSpec((LANES, LANES), lambda b, g, t: (0, g)),
        ],
        out_specs=pl.BlockSpec((tb, LANES), lambda b, g, t: (row(b, g, t), g)),
        out_shape=jax.ShapeDtypeStruct((batch * seq, RWKV_WIDTH), bf16),
        scratch_shapes=[
            pltpu.VMEM((LANES, LANES), f32),
            pltpu.VMEM((6, 8, LANES), f32),
        ],
        compiler_params=pltpu.CompilerParams(
            dimension_semantics=("parallel", "parallel", "arbitrary"),
            vmem_limit_bytes=VMEM_LIMIT),
        name="rwkv_mix",
    )(wide, wide, wide, wide, small, small, pc, ps, w2p, a2p)


def _out_kernel(oa_ref, ob_ref, ga_ref, gb_ref, x_ref, wpf_ref, wpr_ref, wo_ref, g_ref, o_ref):
    pa = jnp.dot(oa_ref[...], wpf_ref[...], preferred_element_type=f32)
    pb = jnp.dot(ob_ref[...], wpr_ref[...], preferred_element_type=f32)
    m = _sigmoid(ga_ref[...]) * pa + _sigmoid(gb_ref[...]) * pb
    z = x_ref[...] + jnp.dot(m.astype(bf16), wo_ref[...], preferred_element_type=f32)
    ms = jnp.mean(z * z, axis=-1, keepdims=True)
    o_ref[...] = z * lax.rsqrt(ms + RMS_EPS) * g_ref[...]


def _merge_out(oa, ob, wide, x2d, wpf, wpr, wo, gain):
    m, d = x2d.shape
    tm = 256
    resident = lambda shape: pl.BlockSpec(shape, lambda i: (0, 0), pipeline_mode=pl.Buffered(1))
    return pl.pallas_call(
        _out_kernel,
        grid=(m // tm,),
        in_specs=[
            pl.BlockSpec((tm, FOX_WIDTH), lambda i: (i, 0)),
            pl.BlockSpec((tm, RWKV_WIDTH), lambda i: (i, 0)),
            pl.BlockSpec((tm, d), lambda i: (i, COL_GA * LANES // d)),
            pl.BlockSpec((tm, d), lambda i: (i, COL_GB * LANES // d)),
            pl.BlockSpec((tm, d), lambda i: (i, 0)),
            resident((FOX_WIDTH, d)),
            resident((RWKV_WIDTH, d)),
            resident((d, d)),
            resident((1, d)),
        ],
        out_specs=pl.BlockSpec((tm, d), lambda i: (i, 0)),
        out_shape=jax.ShapeDtypeStruct((m, d), f32),
        compiler_params=pltpu.CompilerParams(
            dimension_semantics=("parallel",),
            vmem_limit_bytes=VMEM_LIMIT),
        name="merge_out",
    )(oa, ob, wide, wide, x2d, wpf, wpr, wo, gain)


def _pad_cols(a, width):
    return jnp.pad(a, ((0, 0), (0, width - a.shape[1])))


def _layer(x2d, batch, seq, norm_gain, w_in, fox_forget_bias, rwkv_shift_mix, rwkv_w0, rwkv_w2,
           rwkv_a0, rwkv_a2, rwkv_k_k, rwkv_k_a, rwkv_r_k, rwkv_ln_w, rwkv_ln_b,
           w_proj_fox, w_proj_rwkv, w_out):
    fw, rw = FOX_WIDTH, RWKV_WIDTH
    o = 0
    w_qkv = w_in[:, o:o + 3 * fw]; o += 3 * fw
    w_za = w_in[:, o:o + fw]; o += fw
    w_f = w_in[:, o:o + FOX_HEADS]; o += FOX_HEADS
    w_rkvz = w_in[:, o:o + 4 * rw]; o += 4 * rw
    w_wd = w_in[:, o:o + LORA]; o += LORA
    w_ad = w_in[:, o:o + LORA]; o += LORA
    w_g = w_in[:, o:o + 2 * D_MODEL]
    w_wide = jnp.concatenate([w_g, w_za, w_rkvz], axis=1).astype(bf16)
    w_small = jnp.concatenate(
        [_pad_cols(w_wd, LANES), _pad_cols(w_ad, LANES), _pad_cols(w_f, LANES)], axis=1).astype(bf16)
    gain = norm_gain.reshape(1, D_MODEL)

    qkv = _norm_matmul(x2d, gain, w_qkv.astype(bf16), bf16, 1024, 512)
    wide = _norm_matmul(x2d, gain, w_wide, f32, 1024, 512)
    small = _norm_matmul(x2d, gain, w_small, f32, 1024, SMALL_COLS)

    bias_row = _pad_cols(fox_forget_bias.reshape(1, FOX_HEADS), LANES)
    c = _gate_cumsum(small, bias_row, batch, seq)
    c_rows = c[:, :FOX_HEADS].reshape(batch, seq, FOX_HEADS).transpose(0, 2, 1)
    c_rows = c_rows.reshape(batch * FOX_HEADS, 1, seq)
    oa = _fox_attention(qkv, wide, c_rows, batch, seq)

    mu = rwkv_shift_mix
    pc = jnp.zeros((P_ROWS, rw), f32)
    rows = [mu[0:rw], mu[rw:2 * rw], mu[2 * rw:3 * rw], mu[3 * rw:4 * rw], rwkv_w0, rwkv_a0,
            rwkv_k_k, rwkv_k_a, rwkv_r_k.reshape(rw), rwkv_ln_w, rwkv_ln_b]
    pc = pc.at[:len(rows)].set(jnp.stack(rows))
    ps = jnp.zeros((8, LANES), f32)
    ps = ps.at[0, :LORA].set(mu[4 * rw:4 * rw + LORA]).at[1, :LORA].set(mu[4 * rw + LORA:])
    w2p = jnp.pad(rwkv_w2, ((0, LANES - LORA), (0, 0))).astype(bf16)
    a2p = jnp.pad(rwkv_a2, ((0, LANES - LORA), (0, 0))).astype(bf16)
    ob = _rwkv_mix(wide, small, pc, ps, w2p, a2p, batch, seq)

    return oa, ob, wide


def kernel(x, norm_gain, w_in, fox_forget_bias, rwkv_shift_mix, rwkv_w0, rwkv_w2, rwkv_a0, rwkv_a2, rwkv_k_k, rwkv_k_a, rwkv_r_k, rwkv_ln_w, rwkv_ln_b, w_proj_fox, w_proj_rwkv, w_out, final_norm_gain):
    batch, seq, d = x.shape
    depth = norm_gain.shape[0]
    assert depth == 1, "the final rmsnorm is fused into the single layer's output kernel"
    x2d = x.reshape(batch * seq, d)
    oa, ob, wide = _layer(x2d, batch, seq, norm_gain[0], w_in[0], fox_forget_bias[0],
                          rwkv_shift_mix[0], rwkv_w0[0], rwkv_w2[0], rwkv_a0[0], rwkv_a2[0],
                          rwkv_k_k[0], rwkv_k_a[0], rwkv_r_k[0], rwkv_ln_w[0], rwkv_ln_b[0],
                          w_proj_fox[0], w_proj_rwkv[0], w_out[0])
    out = _merge_out(oa, ob, wide, x2d, w_proj_fox[0].astype(bf16), w_proj_rwkv[0].astype(bf16),
                     w_out[0].astype(bf16), final_norm_gain.reshape(1, d))
    return out.reshape(batch, seq, d)
```

```python
import functools

import jax
import jax.numpy as jnp
from jax import lax
from jax.experimental import pallas as pl
from jax.experimental.pallas import tpu as pltpu

D_MODEL = 2048
FOX_HEADS = 8
FOX_HEAD_DIM = 128
FOX_WIDTH = FOX_HEADS * FOX_HEAD_DIM
RWKV_HEADS = 16
RWKV_HEAD_DIM = 64
RWKV_WIDTH = RWKV_HEADS * RWKV_HEAD_DIM
LORA = 96
RMS_EPS = 1e-6
GN_EPS = 64e-5
L2_EPS = 1e-12

LANES = 128
VMEM_LIMIT = 56 * 1024 * 1024

COL_GA = 0
COL_GB = D_MODEL // LANES
COL_ZA = 2 * D_MODEL // LANES
COL_R = COL_ZA + FOX_WIDTH // LANES
COL_K = COL_R + RWKV_WIDTH // LANES
COL_V = COL_K + RWKV_WIDTH // LANES
COL_ZB = COL_V + RWKV_WIDTH // LANES
WIDE_COLS = (COL_ZB + RWKV_WIDTH // LANES) * LANES
SMALL_COLS = 3 * LANES

RWKV_CHUNK = 64
RWKV_BLOCK = 512
RWKV_GROUP = 8
HEADS_PER_GROUP = LANES // RWKV_HEAD_DIM
GROUP_ROWS = HEADS_PER_GROUP * RWKV_CHUNK

f32 = jnp.float32
bf16 = jnp.bfloat16


def _dot(a, b):
    return jnp.dot(a.astype(bf16), b.astype(bf16), preferred_element_type=f32)


def _dot_nt(a, b):
    return lax.dot_general(a.astype(bf16), b.astype(bf16), (((1,), (1,)), ((), ())),
                           preferred_element_type=f32)


def _dot_tn(a, b):
    return lax.dot_general(a.astype(bf16), b.astype(bf16), (((0,), (0,)), ((), ())),
                           preferred_element_type=f32)


def _dot_exact_ones(a, ones_bf16, nt=False):
    a1 = a.astype(bf16)
    r1 = a - a1.astype(f32)
    a2 = r1.astype(bf16)
    a3 = (r1 - a2.astype(f32)).astype(bf16)
    if nt:
        mm = lambda x: lax.dot_general(ones_bf16, x, (((1,), (0,)), ((), ())),
                                       preferred_element_type=f32)
    else:
        mm = lambda x: jnp.dot(x, ones_bf16, preferred_element_type=f32)
    return mm(a1) + mm(a2) + mm(a3)


def _softplus(x):
    return jnp.maximum(x, 0.0) + jnp.log1p(jnp.exp(-jnp.abs(x)))


def _sigmoid(x):
    return 1.0 / (1.0 + jnp.exp(-x))


def _norm_matmul_kernel(x_ref, g_ref, w_ref, o_ref, h_ref):
    @pl.when(pl.program_id(1) == 0)
    def _():
        xf = x_ref[...]
        ms = jnp.mean(xf * xf, axis=-1, keepdims=True)
        h_ref[...] = (xf * lax.rsqrt(ms + RMS_EPS) * g_ref[...]).astype(bf16)

    o_ref[...] = jnp.dot(h_ref[...], w_ref[...],
                         preferred_element_type=f32).astype(o_ref.dtype)


def _norm_matmul(x2d, gain, w, out_dtype, tm, tn):
    m, d = x2d.shape
    n = w.shape[1]
    return pl.pallas_call(
        _norm_matmul_kernel,
        grid=(m // tm, n // tn),
        in_specs=[
            pl.BlockSpec((tm, d), lambda i, j: (i, 0)),
            pl.BlockSpec((1, d), lambda i, j: (0, 0)),
            pl.BlockSpec((d, tn), lambda i, j: (0, j)),
        ],
        out_specs=pl.BlockSpec((tm, tn), lambda i, j: (i, j)),
        out_shape=jax.ShapeDtypeStruct((m, n), out_dtype),
        scratch_shapes=[pltpu.VMEM((tm, d), bf16)],
        compiler_params=pltpu.CompilerParams(
            dimension_semantics=("parallel", "arbitrary"),
            vmem_limit_bytes=VMEM_LIMIT),
        name="norm_matmul",
    )(x2d, gain, w)


def _gate_kernel(f_ref, bias_ref, c_ref, carry_ref, *, tb):
    @pl.when(pl.program_id(1) == 0)
    def _():
        carry_ref[...] = jnp.zeros_like(carry_ref)

    z = f_ref[...] + bias_ref[...]
    log_f = -_softplus(-z)
    row = lax.broadcasted_iota(jnp.int32, (tb, tb), 0)
    col = lax.broadcasted_iota(jnp.int32, (tb, tb), 1)
    tri = jnp.where(col <= row, 1.0, 0.0).astype(bf16)
    c = _dot_exact_ones(log_f, tri, nt=True) + carry_ref[0:1, :]
    c_ref[...] = c
    carry_ref[0:1, :] = c[tb - 1:tb, :]


def _gate_cumsum(small, bias_row, batch, seq):
    tb = 256
    nt = seq // tb
    return pl.pallas_call(
        functools.partial(_gate_kernel, tb=tb),
        grid=(batch, nt),
        in_specs=[
            pl.BlockSpec((tb, LANES), lambda b, t: (b * nt + t, 2)),
            pl.BlockSpec((1, LANES), lambda b, t: (0, 0)),
        ],
        out_specs=pl.BlockSpec((tb, LANES), lambda b, t: (b * nt + t, 0)),
        out_shape=jax.ShapeDtypeStruct((batch * seq, LANES), f32),
        scratch_shapes=[pltpu.VMEM((8, LANES), f32)],
        compiler_params=pltpu.CompilerParams(
            dimension_semantics=("parallel", "arbitrary")),
        name="fox_gate_cumsum",
    )(small, bias_row)


def _fox_kernel(q_ref, k_ref, v_ref, c_ref, z_ref, o_ref, m_ref, l_ref, acc_ref, *, tq, tk):
    i = pl.program_id(2)
    j = pl.program_id(3)
    last_j = (i * tq + tq - 1) // tk

    @pl.when(j == 0)
    def _():
        m_ref[...] = jnp.full_like(m_ref, -jnp.inf)
        l_ref[...] = jnp.zeros_like(l_ref)
        acc_ref[...] = jnp.zeros_like(acc_ref)

    def step(masked):
        s = lax.dot_general(q_ref[...], k_ref[...], (((1,), (1,)), ((), ())),
                            preferred_element_type=f32)
        s = s * (FOX_HEAD_DIM ** -0.5) - c_ref[...]
        if masked:
            rows = i * tq + lax.broadcasted_iota(jnp.int32, (tq, tk), 0)
            cols = j * tk + lax.broadcasted_iota(jnp.int32, (tq, tk), 1)
            s = jnp.where(cols <= rows, s, -jnp.inf)
        m_prev = m_ref[...]
        m_new = jnp.maximum(m_prev, jnp.max(s, axis=-1, keepdims=True))
        p = jnp.exp(s - m_new)
        alpha = jnp.exp(m_prev - m_new)
        l_ref[...] = alpha * l_ref[...] + jnp.sum(p, axis=-1, keepdims=True)
        acc_ref[...] = alpha * acc_ref[...] + jnp.dot(
            p.astype(bf16), v_ref[...], preferred_element_type=f32)
        m_ref[...] = m_new

    @pl.when((j + 1) * tk - 1 <= i * tq)
    def _():
        step(False)

    @pl.when(jnp.logical_and((j + 1) * tk - 1 > i * tq, j <= last_j))
    def _():
        step(True)

    @pl.when(j == last_j)
    def _():
        z = z_ref[...]
        o = acc_ref[...] / l_ref[...]
        o_ref[...] = (o * (z * _sigmoid(z))).astype(o_ref.dtype)


def _fox_attention(qkv, wide, c_rows, batch, seq):
    tq = tk = 512
    nq, nk = seq // tq, seq // tk
    h8 = FOX_HEADS

    def kv_block(b, h, i, j):
        return b * nk + jnp.minimum(j, (i * tq + tq - 1) // tk)

    return pl.pallas_call(
        functools.partial(_fox_kernel, tq=tq, tk=tk),
        grid=(batch, h8, nq, nk),
        in_specs=[
            pl.BlockSpec((tq, LANES), lambda b, h, i, j: (b * nq + i, h)),
            pl.BlockSpec((tk, LANES), lambda b, h, i, j: (kv_block(b, h, i, j), h8 + h)),
            pl.BlockSpec((tk, LANES), lambda b, h, i, j: (kv_block(b, h, i, j), 2 * h8 + h)),
            pl.BlockSpec((None, 1, tk),
                         lambda b, h, i, j: (b * h8 + h, 0, jnp.minimum(j, (i * tq + tq - 1) // tk))),
            pl.BlockSpec((tq, LANES), lambda b, h, i, j: (b * nq + i, COL_ZA + h)),
        ],
        out_specs=pl.BlockSpec((tq, LANES), lambda b, h, i, j: (b * nq + i, h)),
        out_shape=jax.ShapeDtypeStruct((batch * seq, FOX_WIDTH), bf16),
        scratch_shapes=[
            pltpu.VMEM((tq, 1), f32),
            pltpu.VMEM((tq, 1), f32),
            pltpu.VMEM((tq, LANES), f32),
        ],
        compiler_params=pltpu.CompilerParams(
            dimension_semantics=("parallel", "parallel", "parallel", "arbitrary"),
            vmem_limit_bytes=VMEM_LIMIT),
        name="fox_attention",
    )(qkv, qkv, qkv, c_rows, wide)


P_MU_R, P_MU_K, P_MU_V, P_MU_Z, P_W0, P_A0, P_KK, P_KA, P_RK, P_LNW, P_LNB = range(11)
P_ROWS = 16


def _rwkv_kernel(r_ref, k_ref, v_ref, z_ref, wd_ref, ad_ref, pc_ref, ps_ref, w2_ref, a2_ref,
                 o_ref, state_ref, prev_ref, *, tb):
    C = RWKV_CHUNK
    G = GROUP_ROWS
    N = RWKV_HEAD_DIM

    @pl.when(pl.program_id(2) == 0)
    def _():
        state_ref[...] = jnp.zeros_like(state_ref)
        prev_ref[...] = jnp.zeros_like(prev_ref)

    row_in_block = lax.broadcasted_iota(jnp.int32, (tb, LANES), 0)

    def shifted(ref, slot, mu):
        u = ref[...]
        prev = jnp.where(row_in_block == 0, prev_ref[slot, 0:1, :], pltpu.roll(u, 1, 0))
        prev_ref[slot, 0:1, :] = u[tb - 1:tb, :]
        return u + (prev - u) * mu

    prm = lambda idx: pc_ref[idx:idx + 1, :]
    r = shifted(r_ref, 0, prm(P_MU_R))
    kr = shifted(k_ref, 1, prm(P_MU_K))
    vr = shifted(v_ref, 2, prm(P_MU_V))
    zb = shifted(z_ref, 3, prm(P_MU_Z))
    wd = shifted(wd_ref, 4, ps_ref[0:1, :])
    ad = shifted(ad_ref, 5, ps_ref[1:2, :])

    w = -_softplus(-(prm(P_W0) + _dot(jnp.tanh(wd), w2_ref[...]))) - 0.5
    log_decay = -jnp.exp(w)
    rate = _sigmoid(prm(P_A0) + _dot(ad, a2_ref[...]))

    li = lax.broadcasted_iota(jnp.int32, (LANES, LANES), 0)
    lj = lax.broadcasted_iota(jnp.int32, (LANES, LANES), 1)
    head_ones = jnp.where(li // N == lj // N, 1.0, 0.0).astype(bf16)
    head_sum = lambda x: _dot_exact_ones(x, head_ones)

    kk = kr * prm(P_KK)
    kk = kk / jnp.maximum(jnp.sqrt(head_sum(kk * kk)), L2_EPS)
    kp = kr * (1.0 + (rate - 1.0) * prm(P_KA))
    bb = kk * rate
    bonus = head_sum(r * kp * prm(P_RK)) * vr

    ti = lax.broadcasted_iota(jnp.int32, (tb, tb), 0)
    tj = lax.broadcasted_iota(jnp.int32, (tb, tb), 1)
    chunk_tri = jnp.where(jnp.logical_and(tj <= ti, ti // C == tj // C), 1.0, 0.0).astype(bf16)
    ci = _dot_exact_ones(log_decay, chunk_tri, nt=True)

    a_t = -kk * jnp.exp(ci - log_decay)
    r_t = r * jnp.exp(ci)
    inv = jnp.exp(-ci)
    b_t = bb * inv
    k_t = kp * inv

    lane = lax.broadcasted_iota(jnp.int32, (1, LANES), 1)
    head_masks = [jnp.where(lane // N == h, 1.0, 0.0) for h in range(HEADS_PER_GROUP)]
    stack = lambda x: jnp.concatenate([x * hm for hm in head_masks], axis=0)
    stack16 = lambda x: stack(x).astype(bf16)

    gi = lax.broadcasted_iota(jnp.int32, (G, G), 0)
    gj = lax.broadcasted_iota(jnp.int32, (G, G), 1)
    same_head = gi // C == gj // C
    strict = jnp.logical_and(same_head, gj < gi)
    incl = jnp.logical_and(same_head, gj <= gi)
    eye = jnp.where(gi == gj, 1.0, 0.0)

    mm = lambda a, b: jnp.dot(a, b, preferred_element_type=f32)
    mm_nt = lambda a, b: lax.dot_general(a, b, (((1,), (1,)), ((), ())), preferred_element_type=f32)
    mm_tn = lambda a, b: lax.dot_general(a, b, (((0,), (0,)), ((), ())), preferred_element_type=f32)

    def independent(chunks):
        n = range(len(chunks))
        rows = [slice(c * C, (c + 1) * C) for c in chunks]
        c_last = [ci[sl][C - 1:C, :] for sl in rows]
        to_end = [jnp.exp(c_last[i] - ci[rows[i]]) for i in n]
        xr32 = [stack(r_t[sl]) for sl in rows]
        xa = [stack16(a_t[sl]) for sl in rows]
        xr = [x.astype(bf16) for x in xr32]
        yb = [stack16(b_t[sl]) for sl in rows]
        yk = [stack16(k_t[sl]) for sl in rows]
        vs = [stack16(vr[sl]) for sl in rows]
        bh = [stack16(bb[rows[i]] * to_end[i]) for i in n]
        kh = [stack16(kp[rows[i]] * to_end[i]) for i in n]

        big = [mm_nt(jnp.concatenate([xa[i], xr[i]], axis=0), jnp.concatenate([yb[i], yk[i]], axis=0))
               for i in n]
        a_ab = [jnp.where(strict, big[i][0:G, 0:G], 0.0) for i in n]
        a_ak = [jnp.where(strict, big[i][0:G, G:2 * G], 0.0).astype(bf16) for i in n]
        a_rb = [jnp.where(incl, big[i][G:2 * G, 0:G], 0.0).astype(bf16) for i in n]
        a_rk = [jnp.where(incl, big[i][G:2 * G, G:2 * G], 0.0).astype(bf16) for i in n]

        tinv = [eye + a for a in a_ab]
        pw = [a.astype(bf16) for a in a_ab]
        for _ in range(5):
            pw = [mm(p, p).astype(bf16) for p in pw]
            tinv = [tinv[i] + mm(tinv[i].astype(bf16), pw[i]) for i in n]
        tinv = [t.astype(bf16) for t in tinv]

        akv = [mm(a_ak[i], vs[i]).astype(bf16) for i in n]
        wu = [mm(tinv[i], jnp.concatenate([xa[i], akv[i]], axis=1)).astype(bf16) for i in n]
        e = [mm(a_rb[i], wu[i]) for i in n]
        ry = [(xr32[i] + e[i][:, 0:LANES]).astype(bf16) for i in n]
        y_loc = [e[i][:, LANES:] + mm(a_rk[i], vs[i]) for i in n]
        pm = [mm_tn(wu[i][:, 0:LANES], bh[i]).astype(bf16) for i in n]
        q = [mm_tn(jnp.concatenate([wu[i][:, LANES:], vs[i]], axis=0),
                   jnp.concatenate([bh[i], kh[i]], axis=0)) for i in n]
        return [dict(ry=ry[i], y_loc=y_loc[i], pm=pm[i], q=q[i], decay=jnp.exp(c_last[i])) for i in n]

    def sequential(state, parts):
        out = []
        for p in parts:
            s16 = state.astype(bf16)
            y_st = mm_nt(p["ry"], s16) + p["y_loc"]
            out.append(y_st[0:C] + y_st[C:2 * C])
            state = state * p["decay"] + mm(s16, p["pm"]) + p["q"]
        return state, out

    n_chunks = tb // C
    groups = [list(range(g, min(g + RWKV_GROUP, n_chunks))) for g in range(0, n_chunks, RWKV_GROUP)]
    state = state_ref[...]
    ys = []
    parts = independent(groups[0])
    for g in groups[1:]:
        state, out = sequential(state, parts)
        ys += out
        parts = independent(g)
    state, out = sequential(state, parts)
    ys += out
    state_ref[...] = state

    y = jnp.concatenate(ys, axis=0)
    mean = head_sum(y) * (1.0 / N)
    yc = y - mean
    var = head_sum(yc * yc) * (1.0 / N)
    yn = yc * lax.rsqrt(var + GN_EPS) * prm(P_LNW) + prm(P_LNB)
    o_ref[...] = ((yn + bonus) * (zb * _sigmoid(zb))).astype(o_ref.dtype)


def _rwkv_mix(wide, small, pc, ps, w2p, a2p, batch, seq):
    tb = RWKV_BLOCK
    nt = seq // tb
    groups = RWKV_WIDTH // LANES
    row = lambda b, g, t: b * nt + t
    return pl.pallas_call(
        functools.partial(_rwkv_kernel, tb=tb),
        grid=(batch, groups, nt),
        in_specs=[
            pl.BlockSpec((tb, LANES), lambda b, g, t: (row(b, g, t), COL_R + g)),
            pl.BlockSpec((tb, LANES), lambda b, g, t: (row(b, g, t), COL_K + g)),
            pl.BlockSpec((tb, LANES), lambda b, g, t: (row(b, g, t), COL_V + g)),
            pl.BlockSpec((tb, LANES), lambda b, g, t: (row(b, g, t), COL_ZB + g)),
            pl.BlockSpec((tb, LANES), lambda b, g, t: (row(b, g, t), 0)),
            pl.BlockSpec((tb, LANES), lambda b, g, t: (row(b, g, t), 1)),
            pl.BlockSpec((P_ROWS, LANES), lambda b, g, t: (0, g)),
            pl.BlockSpec((8, LANES), lambda b, g, t: (0, 0)),
            pl.BlockSpec((LANES, LANES), lambda b, g, t: (0, g)),
            pl.BlockSpec((LANES, LANES), lambda b, g, t: (0, g)),
        ],
        out_specs=pl.BlockSpec((tb, LANES), lambda b, g, t: (row(b, g, t), g)),
        out_shape=jax.ShapeDtypeStruct((batch * seq, RWKV_WIDTH), bf16),
        scratch_shapes=[
            pltpu.VMEM((LANES, LANES), f32),
            pltpu.VMEM((6, 8, LANES), f32),
        ],
        compiler_params=pltpu.CompilerParams(
            dimension_semantics=("parallel", "parallel", "arbitrary"),
            vmem_limit_bytes=VMEM_LIMIT),
        name="rwkv_mix",
    )(wide, wide, wide, wide, small, small, pc, ps, w2p, a2p)


def _out_kernel(oa_ref, ob_ref, ga_ref, gb_ref, x_ref, wpf_ref, wpr_ref, wo_ref, g_ref, o_ref):
    pa = jnp.dot(oa_ref[...], wpf_ref[...], preferred_element_type=f32)
    pb = jnp.dot(ob_ref[...], wpr_ref[...], preferred_element_type=f32)
    m = _sigmoid(ga_ref[...]) * pa + _sigmoid(gb_ref[...]) * pb
    z = x_ref[...] + jnp.dot(m.astype(bf16), wo_ref[...], preferred_element_type=f32)
    ms = jnp.mean(z * z, axis=-1, keepdims=True)
    o_ref[...] = z * lax.rsqrt(ms + RMS_EPS) * g_ref[...]


def _merge_out(oa, ob, wide, x2d, wpf, wpr, wo, gain):
    m, d = x2d.shape
    tm = 256
    resident = lambda shape: pl.BlockSpec(shape, lambda i: (0, 0), pipeline_mode=pl.Buffered(1))
    return pl.pallas_call(
        _out_kernel,
        grid=(m // tm,),
        in_specs=[
            pl.BlockSpec((tm, FOX_WIDTH), lambda i: (i, 0)),
            pl.BlockSpec((tm, RWKV_WIDTH), lambda i: (i, 0)),
            pl.BlockSpec((tm, d), lambda i: (i, COL_GA * LANES // d)),
            pl.BlockSpec((tm, d), lambda i: (i, COL_GB * LANES // d)),
            pl.BlockSpec((tm, d), lambda i: (i, 0)),
            resident((FOX_WIDTH, d)),
            resident((RWKV_WIDTH, d)),
            resident((d, d)),
            resident((1, d)),
        ],
        out_specs=pl.BlockSpec((tm, d), lambda i: (i, 0)),
        out_shape=jax.ShapeDtypeStruct((m, d), f32),
        compiler_params=pltpu.CompilerParams(
            dimension_semantics=("parallel",),
            vmem_limit_bytes=VMEM_LIMIT),
        name="merge_out",
    )(oa, ob, wide, wide, x2d, wpf, wpr, wo, gain)


def _pad_cols(a, width):
    return jnp.pad(a, ((0, 0), (0, width - a.shape[1])))


def _layer(x2d, batch, seq, norm_gain, w_in, fox_forget_bias, rwkv_shift_mix, rwkv_w0, rwkv_w2,
           rwkv_a0, rwkv_a2, rwkv_k_k, rwkv_k_a, rwkv_r_k, rwkv_ln_w, rwkv_ln_b,
           w_proj_fox, w_proj_rwkv, w_out):
    fw, rw = FOX_WIDTH, RWKV_WIDTH
    o = 0
    w_qkv = w_in[:, o:o + 3 * fw]; o += 3 * fw
    w_za = w_in[:, o:o + fw]; o += fw
    w_f = w_in[:, o:o + FOX_HEADS]; o += FOX_HEADS
    w_rkvz = w_in[:, o:o + 4 * rw]; o += 4 * rw
    w_wd = w_in[:, o:o + LORA]; o += LORA
    w_ad = w_in[:, o:o + LORA]; o += LORA
    w_g = w_in[:, o:o + 2 * D_MODEL]
    w_wide = jnp.concatenate([w_g, w_za, w_rkvz], axis=1).astype(bf16)
    w_small = jnp.concatenate(
        [_pad_cols(w_wd, LANES), _pad_cols(w_ad, LANES), _pad_cols(w_f, LANES)], axis=1).astype(bf16)
    gain = norm_gain.reshape(1, D_MODEL)

    qkv = _norm_matmul(x2d, gain, w_qkv.astype(bf16), bf16, 1024, 512)
    wide = _norm_matmul(x2d, gain, w_wide, f32, 1024, 512)
    small = _norm_matmul(x2d, gain, w_small, f32, 1024, SMALL_COLS)

    bias_row = _pad_cols(fox_forget_bias.reshape(1, FOX_HEADS), LANES)
    c = _gate_cumsum(small, bias_row, batch, seq)
    c_rows = c[:, :FOX_HEADS].reshape(batch, seq, FOX_HEADS).transpose(0, 2, 1)
    c_rows = c_rows.reshape(batch * FOX_HEADS, 1, seq)
    oa = _fox_attention(qkv, wide, c_rows, batch, seq)

    mu = rwkv_shift_mix
    pc = jnp.zeros((P_ROWS, rw), f32)
    rows = [mu[0:rw], mu[rw:2 * rw], mu[2 * rw:3 * rw], mu[3 * rw:4 * rw], rwkv_w0, rwkv_a0,
            rwkv_k_k, rwkv_k_a, rwkv_r_k.reshape(rw), rwkv_ln_w, rwkv_ln_b]
    pc = pc.at[:len(rows)].set(jnp.stack(rows))
    ps = jnp.zeros((8, LANES), f32)
    ps = ps.at[0, :LORA].set(mu[4 * rw:4 * rw + LORA]).at[1, :LORA].set(mu[4 * rw + LORA:])
    w2p = jnp.pad(rwkv_w2, ((0, LANES - LORA), (0, 0))).astype(bf16)
    a2p = jnp.pad(rwkv_a2, ((0, LANES - LORA), (0, 0))).astype(bf16)
    ob = _rwkv_mix(wide, small, pc, ps, w2p, a2p, batch, seq)

    return oa, ob, wide


def kernel(x, norm_gain, w_in, fox_forget_bias, rwkv_shift_mix, rwkv_w0, rwkv_w2, rwkv_a0, rwkv_a2, rwkv_k_k, rwkv_k_a, rwkv_r_k, rwkv_ln_w, rwkv_ln_b, w_proj_fox, w_proj_rwkv, w_out, final_norm_gain):
    batch, seq, d = x.shape
    depth = norm_gain.shape[0]
    assert depth == 1, "the final rmsnorm is fused into the single layer's output kernel"
    x2d = x.reshape(batch * seq, d)
    oa, ob, wide = _layer(x2d, batch, seq, norm_gain[0], w_in[0], fox_forget_bias[0],
                          rwkv_shift_mix[0], rwkv_w0[0], rwkv_w2[0], rwkv_a0[0], rwkv_a2[0],
                          rwkv_k_k[0], rwkv_k_a[0], rwkv_r_k[0], rwkv_ln_w[0], rwkv_ln_b[0],
                          w_proj_fox[0], w_proj_rwkv[0], w_out[0])
    out = _merge_out(oa, ob, wide, x2d, w_proj_fox[0].astype(bf16), w_proj_rwkv[0].astype(bf16),
                     w_out[0].astype(bf16), final_norm_gain.reshape(1, d))
    return out.reshape(batch, seq, d)
```

```python
import functools

import jax
import jax.numpy as jnp
from jax import lax
from jax.experimental import pallas as pl
from jax.experimental.pallas import tpu as pltpu

D_MODEL = 2048
FOX_HEADS = 8
FOX_HEAD_DIM = 128
FOX_WIDTH = FOX_HEADS * FOX_HEAD_DIM
RWKV_HEADS = 16
RWKV_HEAD_DIM = 64
RWKV_WIDTH = RWKV_HEADS * RWKV_HEAD_DIM
LORA = 96
RMS_EPS = 1e-6
GN_EPS = 64e-5
L2_EPS = 1e-12

LANES = 128
VMEM_LIMIT = 56 * 1024 * 1024

COL_GA = 0
COL_GB = D_MODEL // LANES
COL_ZA = 2 * D_MODEL // LANES
COL_R = COL_ZA + FOX_WIDTH // LANES
COL_K = COL_R + RWKV_WIDTH // LANES
COL_V = COL_K + RWKV_WIDTH // LANES
COL_ZB = COL_V + RWKV_WIDTH // LANES
WIDE_COLS = (COL_ZB + RWKV_WIDTH // LANES) * LANES
SMALL_COLS = 3 * LANES

RWKV_CHUNK = 64
RWKV_BLOCK = 512
RWKV_GROUP = 8
HEADS_PER_GROUP = LANES // RWKV_HEAD_DIM
GROUP_ROWS = HEADS_PER_GROUP * RWKV_CHUNK

f32 = jnp.float32
bf16 = jnp.bfloat16


def _dot(a, b):
    return jnp.dot(a.astype(bf16), b.astype(bf16), preferred_element_type=f32)


def _dot_nt(a, b):
    return lax.dot_general(a.astype(bf16), b.astype(bf16), (((1,), (1,)), ((), ())),
                           preferred_element_type=f32)


def _dot_tn(a, b):
    return lax.dot_general(a.astype(bf16), b.astype(bf16), (((0,), (0,)), ((), ())),
                           preferred_element_type=f32)


def _dot_exact_ones(a, ones_bf16, nt=False):
    a1 = a.astype(bf16)
    r1 = a - a1.astype(f32)
    a2 = r1.astype(bf16)
    a3 = (r1 - a2.astype(f32)).astype(bf16)
    if nt:
        mm = lambda x: lax.dot_general(ones_bf16, x, (((1,), (0,)), ((), ())),
                                       preferred_element_type=f32)
    else:
        mm = lambda x: jnp.dot(x, ones_bf16, preferred_element_type=f32)
    return mm(a1) + mm(a2) + mm(a3)


def _softplus(x):
    return jnp.maximum(x, 0.0) + jnp.log1p(jnp.exp(-jnp.abs(x)))


def _sigmoid(x):
    return 1.0 / (1.0 + jnp.exp(-x))


def _norm_matmul_kernel(x_ref, g_ref, w_ref, o_ref, h_ref):
    @pl.when(pl.program_id(1) == 0)
    def _():
        xf = x_ref[...]
        ms = jnp.mean(xf * xf, axis=-1, keepdims=True)
        h_ref[...] = (xf * lax.rsqrt(ms + RMS_EPS) * g_ref[...]).astype(bf16)

    o_ref[...] = jnp.dot(h_ref[...], w_ref[...],
                         preferred_element_type=f32).astype(o_ref.dtype)


def _norm_matmul(x2d, gain, w, out_dtype, tm, tn):
    m, d = x2d.shape
    n = w.shape[1]
    return pl.pallas_call(
        _norm_matmul_kernel,
        grid=(m // tm, n // tn),
        in_specs=[
            pl.BlockSpec((tm, d), lambda i, j: (i, 0)),
            pl.BlockSpec((1, d), lambda i, j: (0, 0)),
            pl.BlockSpec((d, tn), lambda i, j: (0, j)),
        ],
        out_specs=pl.BlockSpec((tm, tn), lambda i, j: (i, j)),
        out_shape=jax.ShapeDtypeStruct((m, n), out_dtype),
        scratch_shapes=[pltpu.VMEM((tm, d), bf16)],
        compiler_params=pltpu.CompilerParams(
            dimension_semantics=("parallel", "arbitrary"),
            vmem_limit_bytes=VMEM_LIMIT),
        name="norm_matmul",
    )(x2d, gain, w)


def _gate_kernel(f_ref, bias_ref, c_ref, carry_ref, *, tb):
    @pl.when(pl.program_id(1) == 0)
    def _():
        carry_ref[...] = jnp.zeros_like(carry_ref)

    z = f_ref[...] + bias_ref[...]
    log_f = -_softplus(-z)
    row = lax.broadcasted_iota(jnp.int32, (tb, tb), 0)
    col = lax.broadcasted_iota(jnp.int32, (tb, tb), 1)
    tri = jnp.where(col <= row, 1.0, 0.0).astype(bf16)
    c = _dot_exact_ones(log_f, tri, nt=True) + carry_ref[0:1, :]
    c_ref[...] = c
    carry_ref[0:1, :] = c[tb - 1:tb, :]


def _gate_cumsum(small, bias_row, batch, seq):
    tb = 256
    nt = seq // tb
    return pl.pallas_call(
        functools.partial(_gate_kernel, tb=tb),
        grid=(batch, nt),
        in_specs=[
            pl.BlockSpec((tb, LANES), lambda b, t: (b * nt + t, 2)),
            pl.BlockSpec((1, LANES), lambda b, t: (0, 0)),
        ],
        out_specs=pl.BlockSpec((tb, LANES), lambda b, t: (b * nt + t, 0)),
        out_shape=jax.ShapeDtypeStruct((batch * seq, LANES), f32),
        scratch_shapes=[pltpu.VMEM((8, LANES), f32)],
        compiler_params=pltpu.CompilerParams(
            dimension_semantics=("parallel", "arbitrary")),
        name="fox_gate_cumsum",
    )(small, bias_row)


LOG2E = 1.4426950408889634
FOX_TK = 256
FOX_BATCH = 4


def _fox_kernel(q_ref, k_ref, vt_ref, c_ref, z_ref, o_ref, crep_ref, acc_ref, *, seq, tk):
    h = pl.program_id(1)
    ng = seq // LANES
    nt = seq // tk
    gpt = tk // LANES
    mm = lambda a, b: jnp.dot(a, b, preferred_element_type=f32)
    mm_nt = lambda a, b: lax.dot_general(a, b, (((1,), (1,)), ((), ())), preferred_element_type=f32)

    src_lane = lax.broadcasted_iota(jnp.int32, (LANES, LANES), 0)
    pick = jnp.where(src_lane == h, 1.0, 0.0).astype(bf16)
    crep_ref[...] = _dot_exact_ones(c_ref[...], pick) * LOG2E

    key_rel = lax.broadcasted_iota(jnp.int32, (tk, LANES), 0)
    qry_rel = lax.broadcasted_iota(jnp.int32, (tk, LANES), 1)
    future = [key_rel > qry_rel + d * LANES for d in range(gpt)]

    sched = []
    for t in range(nt):
        visible = list(range(t * gpt, ng))
        sched += [[(t, g) for g in visible[k:k + FOX_BATCH]] for k in range(0, len(visible), FOX_BATCH)]

    def qk(batch):
        return [mm_nt(k_ref[t * tk:(t + 1) * tk, :], q_ref[g * LANES:(g + 1) * LANES, :])
                for t, g in batch]

    m = [None] * ng
    l = [None] * ng

    def finish(batch, alpha, pv):
        for (t, g), a, x in zip(batch, alpha, pv):
            acc_ref[g] = x if t == 0 else a * acc_ref[g] + x
            if t == g // gpt:
                rows = slice(g * LANES, (g + 1) * LANES)
                z = z_ref[rows, :]
                o = (acc_ref[g] / l[g]).T
                o_ref[rows, :] = (o * (z * _sigmoid(z))).astype(o_ref.dtype)

    s_next = qk(sched[0])
    pending = None
    for bi, batch in enumerate(sched):
        s_cur = s_next
        if bi + 1 < len(sched):
            s_next = qk(sched[bi + 1])
        alpha, p16 = [], []
        for (t, g), s in zip(batch, s_cur):
            s = s - crep_ref[t * tk:(t + 1) * tk, :]
            if g < (t + 1) * gpt:
                s = jnp.where(future[g - t * gpt], -jnp.inf, s)
            m_tile = jnp.max(s, axis=0, keepdims=True)
            if t == 0:
                m_new, a = m_tile, None
            else:
                m_new = jnp.maximum(m[g], m_tile)
                a = jnp.exp2(m[g] - m_new)
            p = jnp.exp2(s - m_new)
            p_sum = jnp.sum(p, axis=0, keepdims=True)
            l[g] = p_sum if t == 0 else a * l[g] + p_sum
            m[g] = m_new
            alpha.append(a)
            p16.append(p.astype(bf16))
        pv = [mm(vt_ref[t], p) for (t, g), p in zip(batch, p16)]
        if pending is not None:
            finish(*pending)
        pending = (batch, alpha, pv)
    finish(*pending)


def _fox_attention(qkv3, vt, c3, wide3, batch, seq):
    tk = FOX_TK
    h8 = FOX_HEADS
    return pl.pallas_call(
        functools.partial(_fox_kernel, seq=seq, tk=tk),
        grid=(batch, h8),
        in_specs=[
            pl.BlockSpec((None, seq, LANES), lambda b, h: (b, 0, h)),
            pl.BlockSpec((None, seq, LANES), lambda b, h: (b, 0, h8 + h)),
            pl.BlockSpec((None, None, seq // tk, LANES, tk), lambda b, h: (b, h, 0, 0, 0)),
            pl.BlockSpec((None, seq, LANES), lambda b, h: (b, 0, 0)),
            pl.BlockSpec((None, seq, LANES), lambda b, h: (b, 0, COL_ZA + h)),
        ],
        out_specs=pl.BlockSpec((None, seq, LANES), lambda b, h: (b, 0, h)),
        out_shape=jax.ShapeDtypeStruct((batch, seq, FOX_WIDTH), bf16),
        scratch_shapes=[
            pltpu.VMEM((seq, LANES), f32),
            pltpu.VMEM((seq // LANES, LANES, LANES), f32),
        ],
        compiler_params=pltpu.CompilerParams(
            dimension_semantics=("parallel", "arbitrary"),
            vmem_limit_bytes=VMEM_LIMIT),
        name="fox_attention",
    )(qkv3, qkv3, vt, c3, wide3)


P_MU_R, P_MU_K, P_MU_V, P_MU_Z, P_W0, P_A0, P_KK, P_KA, P_RK, P_LNW, P_LNB = range(11)
P_ROWS = 16


def _rwkv_kernel(r_ref, k_ref, v_ref, z_ref, wd_ref, ad_ref, pc_ref, ps_ref, w2_ref, a2_ref,
                 o_ref, state_ref, prev_ref, *, tb):
    C = RWKV_CHUNK
    G = GROUP_ROWS
    N = RWKV_HEAD_DIM

    @pl.when(pl.program_id(2) == 0)
    def _():
        state_ref[...] = jnp.zeros_like(state_ref)
        prev_ref[...] = jnp.zeros_like(prev_ref)

    row_in_block = lax.broadcasted_iota(jnp.int32, (tb, LANES), 0)

    def shifted(ref, slot, mu):
        u = ref[...]
        prev = jnp.where(row_in_block == 0, prev_ref[slot, 0:1, :], pltpu.roll(u, 1, 0))
        prev_ref[slot, 0:1, :] = u[tb - 1:tb, :]
        return u + (prev - u) * mu

    prm = lambda idx: pc_ref[idx:idx + 1, :]
    r = shifted(r_ref, 0, prm(P_MU_R))
    kr = shifted(k_ref, 1, prm(P_MU_K))
    vr = shifted(v_ref, 2, prm(P_MU_V))
    zb = shifted(z_ref, 3, prm(P_MU_Z))
    wd = shifted(wd_ref, 4, ps_ref[0:1, :])
    ad = shifted(ad_ref, 5, ps_ref[1:2, :])

    w = -_softplus(-(prm(P_W0) + _dot(jnp.tanh(wd), w2_ref[...]))) - 0.5
    log_decay = -jnp.exp(w)
    rate = _sigmoid(prm(P_A0) + _dot(ad, a2_ref[...]))

    li = lax.broadcasted_iota(jnp.int32, (LANES, LANES), 0)
    lj = lax.broadcasted_iota(jnp.int32, (LANES, LANES), 1)
    head_ones = jnp.where(li // N == lj // N, 1.0, 0.0).astype(bf16)
    head_sum = lambda x: _dot_exact_ones(x, head_ones)

    kk = kr * prm(P_KK)
    kk = kk / jnp.maximum(jnp.sqrt(head_sum(kk * kk)), L2_EPS)
    kp = kr * (1.0 + (rate - 1.0) * prm(P_KA))
    bb = kk * rate
    bonus = head_sum(r * kp * prm(P_RK)) * vr

    ti = lax.broadcasted_iota(jnp.int32, (tb, tb), 0)
    tj = lax.broadcasted_iota(jnp.int32, (tb, tb), 1)
    chunk_tri = jnp.where(jnp.logical_and(tj <= ti, ti // C == tj // C), 1.0, 0.0).astype(bf16)
    ci = _dot_exact_ones(log_decay, chunk_tri, nt=True)

    a_t = -kk * jnp.exp(ci - log_decay)
    r_t = r * jnp.exp(ci)
    inv = jnp.exp(-ci)
    b_t = bb * inv
    k_t = kp * inv

    lane = lax.broadcasted_iota(jnp.int32, (1, LANES), 1)
    head_masks = [jnp.where(lane // N == h, 1.0, 0.0) for h in range(HEADS_PER_GROUP)]
    stack = lambda x: jnp.concatenate([x * hm for hm in head_masks], axis=0)
    stack16 = lambda x: stack(x).astype(bf16)

    gi = lax.broadcasted_iota(jnp.int32, (G, G), 0)
    gj = lax.broadcasted_iota(jnp.int32, (G, G), 1)
    same_head = gi // C == gj // C
    strict = jnp.logical_and(same_head, gj < gi)
    incl = jnp.logical_and(same_head, gj <= gi)
    eye = jnp.where(gi == gj, 1.0, 0.0)

    mm = lambda a, b: jnp.dot(a, b, preferred_element_type=f32)
    mm_nt = lambda a, b: lax.dot_general(a, b, (((1,), (1,)), ((), ())), preferred_element_type=f32)
    mm_tn = lambda a, b: lax.dot_general(a, b, (((0,), (0,)), ((), ())), preferred_element_type=f32)

    def independent(chunks):
        n = range(len(chunks))
        rows = [slice(c * C, (c + 1) * C) for c in chunks]
        c_last = [ci[sl][C - 1:C, :] for sl in rows]
        to_end = [jnp.exp(c_last[i] - ci[rows[i]]) for i in n]
        xr32 = [stack(r_t[sl]) for sl in rows]
        xa = [stack16(a_t[sl]) for sl in rows]
        xr = [x.astype(bf16) for x in xr32]
        yb = [stack16(b_t[sl]) for sl in rows]
        yk = [stack16(k_t[sl]) for sl in rows]
        vs = [stack16(vr[sl]) for sl in rows]
        bh = [stack16(bb[rows[i]] * to_end[i]) for i in n]
        kh = [stack16(kp[rows[i]] * to_end[i]) for i in n]

        big = [mm_nt(jnp.concatenate([xa[i], xr[i]], axis=0), jnp.concatenate([yb[i], yk[i]], axis=0))
               for i in n]
        a_ab = [jnp.where(strict, big[i][0:G, 0:G], 0.0) for i in n]
        a_ak = [jnp.where(strict, big[i][0:G, G:2 * G], 0.0).astype(bf16) for i in n]
        a_rb = [jnp.where(incl, big[i][G:2 * G, 0:G], 0.0).astype(bf16) for i in n]
        a_rk = [jnp.where(incl, big[i][G:2 * G, G:2 * G], 0.0).astype(bf16) for i in n]

        tinv = [eye + a for a in a_ab]
        pw = [a.astype(bf16) for a in a_ab]
        for _ in range(5):
            pw = [mm(p, p).astype(bf16) for p in pw]
            tinv = [tinv[i] + mm(tinv[i].astype(bf16), pw[i]) for i in n]
        tinv = [t.astype(bf16) for t in tinv]

        akv = [mm(a_ak[i], vs[i]).astype(bf16) for i in n]
        wu = [mm(tinv[i], jnp.concatenate([xa[i], akv[i]], axis=1)).astype(bf16) for i in n]
        e = [mm(a_rb[i], wu[i]) for i in n]
        ry = [(xr32[i] + e[i][:, 0:LANES]).astype(bf16) for i in n]
        y_loc = [e[i][:, LANES:] + mm(a_rk[i], vs[i]) for i in n]
        pm = [mm_tn(wu[i][:, 0:LANES], bh[i]).astype(bf16) for i in n]
        q = [mm_tn(jnp.concatenate([wu[i][:, LANES:], vs[i]], axis=0),
                   jnp.concatenate([bh[i], kh[i]], axis=0)) for i in n]
        return [dict(ry=ry[i], y_loc=y_loc[i], pm=pm[i], q=q[i], decay=jnp.exp(c_last[i])) for i in n]

    def sequential(state, parts):
        out = []
        for p in parts:
            s16 = state.astype(bf16)
            y_st = mm_nt(p["ry"], s16) + p["y_loc"]
            out.append(y_st[0:C] + y_st[C:2 * C])
            state = state * p["decay"] + mm(s16, p["pm"]) + p["q"]
        return state, out

    n_chunks = tb // C
    groups = [list(range(g, min(g + RWKV_GROUP, n_chunks))) for g in range(0, n_chunks, RWKV_GROUP)]
    state = state_ref[...]
    ys = []
    parts = independent(groups[0])
    for g in groups[1:]:
        state, out = sequential(state, parts)
        ys += out
        parts = independent(g)
    state, out = sequential(state, parts)
    ys += out
    state_ref[...] = state

    y = jnp.concatenate(ys, axis=0)
    mean = head_sum(y) * (1.0 / N)
    yc = y - mean
    var = head_sum(yc * yc) * (1.0 / N)
    yn = yc * lax.rsqrt(var + GN_EPS) * prm(P_LNW) + prm(P_LNB)
    o_ref[...] = ((yn + bonus) * (zb * _sigmoid(zb))).astype(o_ref.dtype)


def _rwkv_mix(wide, small, pc, ps, w2p, a2p, batch, seq):
    tb = RWKV_BLOCK
    nt = seq // tb
    groups = RWKV_WIDTH // LANES
    row = lambda b, g, t: b * nt + t
    return pl.pallas_call(
        functools.partial(_rwkv_kernel, tb=tb),
        grid=(batch, groups, nt),
        in_specs=[
            pl.BlockSpec((tb, LANES), lambda b, g, t: (row(b, g, t), COL_R + g)),
            pl.BlockSpec((tb, LANES), lambda b, g, t: (row(b, g, t), COL_K + g)),
            pl.BlockSpec((tb, LANES), lambda b, g, t: (row(b, g, t), COL_V + g)),
            pl.BlockSpec((tb, LANES), lambda b, g, t: (row(b, g, t), COL_ZB + g)),
            pl.BlockSpec((tb, LANES), lambda b, g, t: (row(b, g, t), 0)),
            pl.BlockSpec((tb, LANES), lambda b, g, t: (row(b, g, t), 1)),
            pl.BlockSpec((P_ROWS, LANES), lambda b, g, t: (0, g)),
            pl.BlockSpec((8, LANES), lambda b, g, t: (0, 0)),
            pl.BlockSpec((LANES, LANES), lambda b, g, t: (0, g)),
            pl.BlockSpec((LANES, LANES), lambda b, g, t: (0, g)),
        ],
        out_specs=pl.BlockSpec((tb, LANES), lambda b, g, t: (row(b, g, t), g)),
        out_shape=jax.ShapeDtypeStruct((batch * seq, RWKV_WIDTH), bf16),
        scratch_shapes=[
            pltpu.VMEM((LANES, LANES), f32),
            pltpu.VMEM((6, 8, LANES), f32),
        ],
        compiler_params=pltpu.CompilerParams(
            dimension_semantics=("parallel", "parallel", "arbitrary"),
            vmem_limit_bytes=VMEM_LIMIT),
        name="rwkv_mix",
    )(wide, wide, wide, wide, small, small, pc, ps, w2p, a2p)


def _out_kernel(oa_ref, ob_ref, ga_ref, gb_ref, x_ref, wpf_ref, wpr_ref, wo_ref, g_ref, o_ref):
    pa = jnp.dot(oa_ref[...], wpf_ref[...], preferred_element_type=f32)
    pb = jnp.dot(ob_ref[...], wpr_ref[...], preferred_element_type=f32)
    m = _sigmoid(ga_ref[...]) * pa + _sigmoid(gb_ref[...]) * pb
    z = x_ref[...] + jnp.dot(m.astype(bf16), wo_ref[...], preferred_element_type=f32)
    ms = jnp.mean(z * z, axis=-1, keepdims=True)
    o_ref[...] = z * lax.rsqrt(ms + RMS_EPS) * g_ref[...]


def _merge_out(oa, ob, wide, x2d, wpf, wpr, wo, gain):
    m, d = x2d.shape
    tm = 256
    resident = lambda shape: pl.BlockSpec(shape, lambda i: (0, 0), pipeline_mode=pl.Buffered(1))
    return pl.pallas_call(
        _out_kernel,
        grid=(m // tm,),
        in_specs=[
            pl.BlockSpec((tm, FOX_WIDTH), lambda i: (i, 0)),
            pl.BlockSpec((tm, RWKV_WIDTH), lambda i: (i, 0)),
            pl.BlockSpec((tm, d), lambda i: (i, COL_GA * LANES // d)),
            pl.BlockSpec((tm, d), lambda i: (i, COL_GB * LANES // d)),
            pl.BlockSpec((tm, d), lambda i: (i, 0)),
            resident((FOX_WIDTH, d)),
            resident((RWKV_WIDTH, d)),
            resident((d, d)),
            resident((1, d)),
        ],
        out_specs=pl.BlockSpec((tm, d), lambda i: (i, 0)),
        out_shape=jax.ShapeDtypeStruct((m, d), f32),
        compiler_params=pltpu.CompilerParams(
            dimension_semantics=("parallel",),
            vmem_limit_bytes=VMEM_LIMIT),
        name="merge_out",
    )(oa, ob, wide, wide, x2d, wpf, wpr, wo, gain)


def _pad_cols(a, width):
    return jnp.pad(a, ((0, 0), (0, width - a.shape[1])))


def _layer(x2d, batch, seq, norm_gain, w_in, fox_forget_bias, rwkv_shift_mix, rwkv_w0, rwkv_w2,
           rwkv_a0, rwkv_a2, rwkv_k_k, rwkv_k_a, rwkv_r_k, rwkv_ln_w, rwkv_ln_b,
           w_proj_fox, w_proj_rwkv, w_out):
    fw, rw = FOX_WIDTH, RWKV_WIDTH
    o = 0
    w_qkv = w_in[:, o:o + 3 * fw]; o += 3 * fw
    w_za = w_in[:, o:o + fw]; o += fw
    w_f = w_in[:, o:o + FOX_HEADS]; o += FOX_HEADS
    w_rkvz = w_in[:, o:o + 4 * rw]; o += 4 * rw
    w_wd = w_in[:, o:o + LORA]; o += LORA
    w_ad = w_in[:, o:o + LORA]; o += LORA
    w_g = w_in[:, o:o + 2 * D_MODEL]
    w_wide = jnp.concatenate([w_g, w_za, w_rkvz], axis=1).astype(bf16)
    w_small = jnp.concatenate(
        [_pad_cols(w_wd, LANES), _pad_cols(w_ad, LANES), _pad_cols(w_f, LANES)], axis=1).astype(bf16)
    gain = norm_gain.reshape(1, D_MODEL)

    q_scale = jnp.where(jnp.arange(3 * fw) < fw, FOX_HEAD_DIM ** -0.5 * LOG2E, 1.0).astype(f32)
    qkv = _norm_matmul(x2d, gain, (w_qkv * q_scale).astype(bf16), bf16, 1024, 512)
    wide = _norm_matmul(x2d, gain, w_wide, f32, 1024, 512)
    small = _norm_matmul(x2d, gain, w_small, f32, 1024, SMALL_COLS)

    bias_row = _pad_cols(fox_forget_bias.reshape(1, FOX_HEADS), LANES)
    c = _gate_cumsum(small, bias_row, batch, seq)
    qkv3 = qkv.reshape(batch, seq, 3 * fw)
    vt = qkv3[:, :, 2 * fw:].reshape(batch, seq // FOX_TK, FOX_TK, FOX_HEADS, FOX_HEAD_DIM)
    vt = vt.transpose(0, 3, 1, 4, 2)
    oa = _fox_attention(qkv3, vt, c.reshape(batch, seq, LANES), wide.reshape(batch, seq, WIDE_COLS),
                        batch, seq).reshape(batch * seq, fw)

    mu = rwkv_shift_mix
    pc = jnp.zeros((P_ROWS, rw), f32)
    rows = [mu[0:rw], mu[rw:2 * rw], mu[2 * rw:3 * rw], mu[3 * rw:4 * rw], rwkv_w0, rwkv_a0,
            rwkv_k_k, rwkv_k_a, rwkv_r_k.reshape(rw), rwkv_ln_w, rwkv_ln_b]
    pc = pc.at[:len(rows)].set(jnp.stack(rows))
    ps = jnp.zeros((8, LANES), f32)
    ps = ps.at[0, :LORA].set(mu[4 * rw:4 * rw + LORA]).at[1, :LORA].set(mu[4 * rw + LORA:])
    w2p = jnp.pad(rwkv_w2, ((0, LANES - LORA), (0, 0))).astype(bf16)
    a2p = jnp.pad(rwkv_a2, ((0, LANES - LORA), (0, 0))).astype(bf16)
    ob = _rwkv_mix(wide, small, pc, ps, w2p, a2p, batch, seq)

    return oa, ob, wide


def kernel(x, norm_gain, w_in, fox_forget_bias, rwkv_shift_mix, rwkv_w0, rwkv_w2, rwkv_a0, rwkv_a2, rwkv_k_k, rwkv_k_a, rwkv_r_k, rwkv_ln_w, rwkv_ln_b, w_proj_fox, w_proj_rwkv, w_out, final_norm_gain):
    batch, seq, d = x.shape
    depth = norm_gain.shape[0]
    assert depth == 1, "the final rmsnorm is fused into the single layer's output kernel"
    x2d = x.reshape(batch * seq, d)
    oa, ob, wide = _layer(x2d, batch, seq, norm_gain[0], w_in[0], fox_forget_bias[0],
                          rwkv_shift_mix[0], rwkv_w0[0], rwkv_w2[0], rwkv_a0[0], rwkv_a2[0],
                          rwkv_k_k[0], rwkv_k_a[0], rwkv_r_k[0], rwkv_ln_w[0], rwkv_ln_b[0],
                          w_proj_fox[0], w_proj_rwkv[0], w_out[0])
    out = _merge_out(oa, ob, wide, x2d, w_proj_fox[0].astype(bf16), w_proj_rwkv[0].astype(bf16),
                     w_out[0].astype(bf16), final_norm_gain.reshape(1, d))
    return out.reshape(batch, seq, d)
```

```python
import functools

import jax
import jax.numpy as jnp
from jax import lax
from jax.experimental import pallas as pl
from jax.experimental.pallas import tpu as pltpu

D_MODEL = 2048
FOX_HEADS = 8
FOX_HEAD_DIM = 128
FOX_WIDTH = FOX_HEADS * FOX_HEAD_DIM
RWKV_HEADS = 16
RWKV_HEAD_DIM = 64
RWKV_WIDTH = RWKV_HEADS * RWKV_HEAD_DIM
LORA = 96
RMS_EPS = 1e-6
GN_EPS = 64e-5
L2_EPS = 1e-12

LANES = 128
VMEM_LIMIT = 56 * 1024 * 1024

COL_GA = 0
COL_GB = D_MODEL // LANES
COL_ZA = 2 * D_MODEL // LANES
COL_R = COL_ZA + FOX_WIDTH // LANES
COL_K = COL_R + RWKV_WIDTH // LANES
COL_V = COL_K + RWKV_WIDTH // LANES
COL_ZB = COL_V + RWKV_WIDTH // LANES
WIDE_COLS = (COL_ZB + RWKV_WIDTH // LANES) * LANES
SMALL_COLS = 3 * LANES

RWKV_CHUNK = 64
RWKV_BLOCK = 512
RWKV_GROUP = 8
HEADS_PER_GROUP = LANES // RWKV_HEAD_DIM
GROUP_ROWS = HEADS_PER_GROUP * RWKV_CHUNK

f32 = jnp.float32
bf16 = jnp.bfloat16


def _dot(a, b):
    return jnp.dot(a.astype(bf16), b.astype(bf16), preferred_element_type=f32)


def _dot_nt(a, b):
    return lax.dot_general(a.astype(bf16), b.astype(bf16), (((1,), (1,)), ((), ())),
                           preferred_element_type=f32)


def _dot_tn(a, b):
    return lax.dot_general(a.astype(bf16), b.astype(bf16), (((0,), (0,)), ((), ())),
                           preferred_element_type=f32)


def _dot_exact_ones(a, ones_bf16, nt=False):
    a1 = a.astype(bf16)
    r1 = a - a1.astype(f32)
    a2 = r1.astype(bf16)
    a3 = (r1 - a2.astype(f32)).astype(bf16)
    if nt:
        mm = lambda x: lax.dot_general(ones_bf16, x, (((1,), (0,)), ((), ())),
                                       preferred_element_type=f32)
    else:
        mm = lambda x: jnp.dot(x, ones_bf16, preferred_element_type=f32)
    return mm(a1) + mm(a2) + mm(a3)


def _softplus(x):
    return jnp.maximum(x, 0.0) + jnp.log1p(jnp.exp(-jnp.abs(x)))


def _sigmoid(x):
    return 1.0 / (1.0 + jnp.exp(-x))


def _norm_matmul_kernel(x_ref, g_ref, w_ref, o_ref, h_ref):
    @pl.when(pl.program_id(1) == 0)
    def _():
        xf = x_ref[...]
        ms = jnp.mean(xf * xf, axis=-1, keepdims=True)
        h_ref[...] = (xf * lax.rsqrt(ms + RMS_EPS) * g_ref[...]).astype(bf16)

    o_ref[...] = jnp.dot(h_ref[...], w_ref[...],
                         preferred_element_type=f32).astype(o_ref.dtype)


def _norm_matmul(x2d, gain, w, col_start, n, out_dtype, tm, tn):
    m, d = x2d.shape
    first = col_start // tn
    assert first * tn == col_start and n % tn == 0
    return pl.pallas_call(
        _norm_matmul_kernel,
        grid=(m // tm, n // tn),
        in_specs=[
            pl.BlockSpec((tm, d), lambda i, j: (i, 0)),
            pl.BlockSpec((1, d), lambda i, j: (0, 0)),
            pl.BlockSpec((d, tn), lambda i, j: (0, first + j)),
        ],
        out_specs=pl.BlockSpec((tm, tn), lambda i, j: (i, j)),
        out_shape=jax.ShapeDtypeStruct((m, n), out_dtype),
        scratch_shapes=[pltpu.VMEM((tm, d), bf16)],
        compiler_params=pltpu.CompilerParams(
            dimension_semantics=("parallel", "arbitrary"),
            vmem_limit_bytes=VMEM_LIMIT),
        name="norm_matmul",
    )(x2d, gain, w)


def _gate_kernel(f_ref, bias_ref, c_ref, carry_ref, *, tb):
    @pl.when(pl.program_id(1) == 0)
    def _():
        carry_ref[...] = jnp.zeros_like(carry_ref)

    z = f_ref[...] + bias_ref[...]
    log_f = -_softplus(-z)
    row = lax.broadcasted_iota(jnp.int32, (tb, tb), 0)
    col = lax.broadcasted_iota(jnp.int32, (tb, tb), 1)
    tri = jnp.where(col <= row, 1.0, 0.0).astype(bf16)
    c = _dot_exact_ones(log_f, tri, nt=True) + carry_ref[0:1, :]
    c_ref[...] = c
    carry_ref[0:1, :] = c[tb - 1:tb, :]


def _gate_cumsum(small, bias_row, batch, seq):
    tb = 256
    nt = seq // tb
    return pl.pallas_call(
        functools.partial(_gate_kernel, tb=tb),
        grid=(batch, nt),
        in_specs=[
            pl.BlockSpec((tb, LANES), lambda b, t: (b * nt + t, 2)),
            pl.BlockSpec((1, LANES), lambda b, t: (0, 0)),
        ],
        out_specs=pl.BlockSpec((tb, LANES), lambda b, t: (b * nt + t, 0)),
        out_shape=jax.ShapeDtypeStruct((batch * seq, LANES), f32),
        scratch_shapes=[pltpu.VMEM((8, LANES), f32)],
        compiler_params=pltpu.CompilerParams(
            dimension_semantics=("parallel", "arbitrary")),
        name="fox_gate_cumsum",
    )(small, bias_row)


LOG2E = 1.4426950408889634
FOX_TK = 256
FOX_BATCH = 4


def _fox_kernel(q_ref, k_ref, v_ref, c_ref, z_ref, o_ref, crep_ref, acc_ref, vt_ref, *, seq, tk):
    h = pl.program_id(1)
    ng = seq // LANES
    nt = seq // tk
    gpt = tk // LANES
    mm = lambda a, b: jnp.dot(a, b, preferred_element_type=f32)
    mm_nt = lambda a, b: lax.dot_general(a, b, (((1,), (1,)), ((), ())), preferred_element_type=f32)

    src_lane = lax.broadcasted_iota(jnp.int32, (LANES, LANES), 0)
    pick = jnp.where(src_lane == h, 1.0, 0.0).astype(bf16)
    crep_ref[...] = _dot_exact_ones(c_ref[...], pick) * LOG2E
    for t in range(nt):
        vt_ref[t] = v_ref[t * tk:(t + 1) * tk, :].T

    key_rel = lax.broadcasted_iota(jnp.int32, (tk, LANES), 0)
    qry_rel = lax.broadcasted_iota(jnp.int32, (tk, LANES), 1)
    future = [key_rel > qry_rel + d * LANES for d in range(gpt)]

    sched = []
    for t in range(nt):
        visible = list(range(t * gpt, ng))
        sched += [[(t, g) for g in visible[k:k + FOX_BATCH]] for k in range(0, len(visible), FOX_BATCH)]

    def qk(batch):
        return [mm_nt(k_ref[t * tk:(t + 1) * tk, :], q_ref[g * LANES:(g + 1) * LANES, :])
                for t, g in batch]

    m = [None] * ng
    l = [None] * ng

    def finish(batch, alpha, pv):
        for (t, g), a, x in zip(batch, alpha, pv):
            acc_ref[g] = x if t == 0 else a * acc_ref[g] + x
            if t == g // gpt:
                rows = slice(g * LANES, (g + 1) * LANES)
                z = z_ref[rows, :]
                o = (acc_ref[g] / l[g]).T
                o_ref[rows, :] = (o * (z * _sigmoid(z))).astype(o_ref.dtype)

    s_next = qk(sched[0])
    pending = None
    for bi, batch in enumerate(sched):
        s_cur = s_next
        if bi + 1 < len(sched):
            s_next = qk(sched[bi + 1])
        alpha, p16 = [], []
        for (t, g), s in zip(batch, s_cur):
            s = s - crep_ref[t * tk:(t + 1) * tk, :]
            if g < (t + 1) * gpt:
                s = jnp.where(future[g - t * gpt], -jnp.inf, s)
            m_tile = jnp.max(s, axis=0, keepdims=True)
            if t == 0:
                m_new, a = m_tile, None
            else:
                m_new = jnp.maximum(m[g], m_tile)
                a = jnp.exp2(m[g] - m_new)
            p = jnp.exp2(s - m_new)
            p_sum = jnp.sum(p, axis=0, keepdims=True)
            l[g] = p_sum if t == 0 else a * l[g] + p_sum
            m[g] = m_new
            alpha.append(a)
            p16.append(p.astype(bf16))
        pv = [mm(vt_ref[t], p) for (t, g), p in zip(batch, p16)]
        if pending is not None:
            finish(*pending)
        pending = (batch, alpha, pv)
    finish(*pending)


def _fox_attention(qkv3, c3, wide3, batch, seq):
    tk = FOX_TK
    h8 = FOX_HEADS
    return pl.pallas_call(
        functools.partial(_fox_kernel, seq=seq, tk=tk),
        grid=(batch, h8),
        in_specs=[
            pl.BlockSpec((None, seq, LANES), lambda b, h: (b, 0, h)),
            pl.BlockSpec((None, seq, LANES), lambda b, h: (b, 0, h8 + h)),
            pl.BlockSpec((None, seq, LANES), lambda b, h: (b, 0, 2 * h8 + h)),
            pl.BlockSpec((None, seq, LANES), lambda b, h: (b, 0, 0)),
            pl.BlockSpec((None, seq, LANES), lambda b, h: (b, 0, COL_ZA + h)),
        ],
        out_specs=pl.BlockSpec((None, seq, LANES), lambda b, h: (b, 0, h)),
        out_shape=jax.ShapeDtypeStruct((batch, seq, FOX_WIDTH), bf16),
        scratch_shapes=[
            pltpu.VMEM((seq, LANES), f32),
            pltpu.VMEM((seq // LANES, LANES, LANES), f32),
            pltpu.VMEM((seq // tk, LANES, tk), bf16),
        ],
        compiler_params=pltpu.CompilerParams(
            dimension_semantics=("parallel", "arbitrary"),
            vmem_limit_bytes=VMEM_LIMIT),
        name="fox_attention",
    )(qkv3, qkv3, qkv3, c3, wide3)


P_MU_R, P_MU_K, P_MU_V, P_MU_Z, P_W0, P_A0, P_KK, P_KA, P_RK, P_LNW, P_LNB = range(11)
P_ROWS = 16


def _rwkv_kernel(r_ref, k_ref, v_ref, z_ref, wd_ref, ad_ref, pc_ref, ps_ref, w2_ref, a2_ref,
                 o_ref, state_ref, prev_ref, *, tb):
    C = RWKV_CHUNK
    G = GROUP_ROWS
    N = RWKV_HEAD_DIM

    @pl.when(pl.program_id(2) == 0)
    def _():
        state_ref[...] = jnp.zeros_like(state_ref)
        prev_ref[...] = jnp.zeros_like(prev_ref)

    row_in_block = lax.broadcasted_iota(jnp.int32, (tb, LANES), 0)

    def shifted(ref, slot, mu):
        u = ref[...]
        prev = jnp.where(row_in_block == 0, prev_ref[slot, 0:1, :], pltpu.roll(u, 1, 0))
        prev_ref[slot, 0:1, :] = u[tb - 1:tb, :]
        return u + (prev - u) * mu

    prm = lambda idx: pc_ref[idx:idx + 1, :]
    r = shifted(r_ref, 0, prm(P_MU_R))
    kr = shifted(k_ref, 1, prm(P_MU_K))
    vr = shifted(v_ref, 2, prm(P_MU_V))
    zb = shifted(z_ref, 3, prm(P_MU_Z))
    wd = shifted(wd_ref, 4, ps_ref[0:1, :])
    ad = shifted(ad_ref, 5, ps_ref[1:2, :])

    w = -_softplus(-(prm(P_W0) + _dot(jnp.tanh(wd), w2_ref[...]))) - 0.5
    log_decay = -jnp.exp(w)
    rate = _sigmoid(prm(P_A0) + _dot(ad, a2_ref[...]))

    li = lax.broadcasted_iota(jnp.int32, (LANES, LANES), 0)
    lj = lax.broadcasted_iota(jnp.int32, (LANES, LANES), 1)
    head_ones = jnp.where(li // N == lj // N, 1.0, 0.0).astype(bf16)
    head_sum = lambda x: _dot_exact_ones(x, head_ones)

    kk = kr * prm(P_KK)
    kk = kk / jnp.maximum(jnp.sqrt(head_sum(kk * kk)), L2_EPS)
    kp = kr * (1.0 + (rate - 1.0) * prm(P_KA))
    bb = kk * rate
    bonus = head_sum(r * kp * prm(P_RK)) * vr

    ti = lax.broadcasted_iota(jnp.int32, (tb, tb), 0)
    tj = lax.broadcasted_iota(jnp.int32, (tb, tb), 1)
    chunk_tri = jnp.where(jnp.logical_and(tj <= ti, ti // C == tj // C), 1.0, 0.0).astype(bf16)
    ci = _dot_exact_ones(log_decay, chunk_tri, nt=True)

    a_t = -kk * jnp.exp(ci - log_decay)
    r_t = r * jnp.exp(ci)
    inv = jnp.exp(-ci)
    b_t = bb * inv
    k_t = kp * inv

    lane = lax.broadcasted_iota(jnp.int32, (1, LANES), 1)
    head_masks = [jnp.where(lane // N == h, 1.0, 0.0) for h in range(HEADS_PER_GROUP)]
    stack = lambda x: jnp.concatenate([x * hm for hm in head_masks], axis=0)
    stack16 = lambda x: stack(x).astype(bf16)

    gi = lax.broadcasted_iota(jnp.int32, (G, G), 0)
    gj = lax.broadcasted_iota(jnp.int32, (G, G), 1)
    same_head = gi // C == gj // C
    strict = jnp.logical_and(same_head, gj < gi)
    incl = jnp.logical_and(same_head, gj <= gi)
    eye = jnp.where(gi == gj, 1.0, 0.0)

    mm = lambda a, b: jnp.dot(a, b, preferred_element_type=f32)
    mm_nt = lambda a, b: lax.dot_general(a, b, (((1,), (1,)), ((), ())), preferred_element_type=f32)
    mm_tn = lambda a, b: lax.dot_general(a, b, (((0,), (0,)), ((), ())), preferred_element_type=f32)

    def independent(chunks):
        n = range(len(chunks))
        rows = [slice(c * C, (c + 1) * C) for c in chunks]
        c_last = [ci[sl][C - 1:C, :] for sl in rows]
        to_end = [jnp.exp(c_last[i] - ci[rows[i]]) for i in n]
        xr32 = [stack(r_t[sl]) for sl in rows]
        xa = [stack16(a_t[sl]) for sl in rows]
        xr = [x.astype(bf16) for x in xr32]
        yb = [stack16(b_t[sl]) for sl in rows]
        yk = [stack16(k_t[sl]) for sl in rows]
        vs = [stack16(vr[sl]) for sl in rows]
        bh = [stack16(bb[rows[i]] * to_end[i]) for i in n]
        kh = [stack16(kp[rows[i]] * to_end[i]) for i in n]

        big = [mm_nt(jnp.concatenate([xa[i], xr[i]], axis=0), jnp.concatenate([yb[i], yk[i]], axis=0))
               for i in n]
        a_ab = [jnp.where(strict, big[i][0:G, 0:G], 0.0) for i in n]
        a_ak = [jnp.where(strict, big[i][0:G, G:2 * G], 0.0).astype(bf16) for i in n]
        a_rb = [jnp.where(incl, big[i][G:2 * G, 0:G], 0.0).astype(bf16) for i in n]
        a_rk = [jnp.where(incl, big[i][G:2 * G, G:2 * G], 0.0).astype(bf16) for i in n]

        tinv = [eye + a for a in a_ab]
        pw = [a.astype(bf16) for a in a_ab]
        for _ in range(5):
            pw = [mm(p, p).astype(bf16) for p in pw]
            tinv = [tinv[i] + mm(tinv[i].astype(bf16), pw[i]) for i in n]
        tinv = [t.astype(bf16) for t in tinv]

        akv = [mm(a_ak[i], vs[i]).astype(bf16) for i in n]
        wu = [mm(tinv[i], jnp.concatenate([xa[i], akv[i]], axis=1)).astype(bf16) for i in n]
        e = [mm(a_rb[i], wu[i]) for i in n]
        ry = [(xr32[i] + e[i][:, 0:LANES]).astype(bf16) for i in n]
        y_loc = [e[i][:, LANES:] + mm(a_rk[i], vs[i]) for i in n]
        pm = [mm_tn(wu[i][:, 0:LANES], bh[i]).astype(bf16) for i in n]
        q = [mm_tn(jnp.concatenate([wu[i][:, LANES:], vs[i]], axis=0),
                   jnp.concatenate([bh[i], kh[i]], axis=0)) for i in n]
        return [dict(ry=ry[i], y_loc=y_loc[i], pm=pm[i], q=q[i], decay=jnp.exp(c_last[i])) for i in n]

    def sequential(state, parts):
        out = []
        for p in parts:
            s16 = state.astype(bf16)
            y_st = mm_nt(p["ry"], s16) + p["y_loc"]
            out.append(y_st[0:C] + y_st[C:2 * C])
            state = state * p["decay"] + mm(s16, p["pm"]) + p["q"]
        return state, out

    n_chunks = tb // C
    groups = [list(range(g, min(g + RWKV_GROUP, n_chunks))) for g in range(0, n_chunks, RWKV_GROUP)]
    state = state_ref[...]
    ys = []
    parts = independent(groups[0])
    for g in groups[1:]:
        state, out = sequential(state, parts)
        ys += out
        parts = independent(g)
    state, out = sequential(state, parts)
    ys += out
    state_ref[...] = state

    y = jnp.concatenate(ys, axis=0)
    mean = head_sum(y) * (1.0 / N)
    yc = y - mean
    var = head_sum(yc * yc) * (1.0 / N)
    yn = yc * lax.rsqrt(var + GN_EPS) * prm(P_LNW) + prm(P_LNB)
    o_ref[...] = ((yn + bonus) * (zb * _sigmoid(zb))).astype(o_ref.dtype)


def _rwkv_mix(wide, small, pc, ps, w2p, a2p, batch, seq):
    tb = RWKV_BLOCK
    nt = seq // tb
    groups = RWKV_WIDTH // LANES
    row = lambda b, g, t: b * nt + t
    return pl.pallas_call(
        functools.partial(_rwkv_kernel, tb=tb),
        grid=(batch, groups, nt),
        in_specs=[
            pl.BlockSpec((tb, LANES), lambda b, g, t: (row(b, g, t), COL_R + g)),
            pl.BlockSpec((tb, LANES), lambda b, g, t: (row(b, g, t), COL_K + g)),
            pl.BlockSpec((tb, LANES), lambda b, g, t: (row(b, g, t), COL_V + g)),
            pl.BlockSpec((tb, LANES), lambda b, g, t: (row(b, g, t), COL_ZB + g)),
            pl.BlockSpec((tb, LANES), lambda b, g, t: (row(b, g, t), 0)),
            pl.BlockSpec((tb, LANES), lambda b, g, t: (row(b, g, t), 1)),
            pl.BlockSpec((P_ROWS, LANES), lambda b, g, t: (0, g)),
            pl.BlockSpec((8, LANES), lambda b, g, t: (0, 0)),
            pl.BlockSpec((LANES, LANES), lambda b, g, t: (0, g)),
            pl.BlockSpec((LANES, LANES), lambda b, g, t: (0, g)),
        ],
        out_specs=pl.BlockSpec((tb, LANES), lambda b, g, t: (row(b, g, t), g)),
        out_shape=jax.ShapeDtypeStruct((batch * seq, RWKV_WIDTH), bf16),
        scratch_shapes=[
            pltpu.VMEM((LANES, LANES), f32),
            pltpu.VMEM((6, 8, LANES), f32),
        ],
        compiler_params=pltpu.CompilerParams(
            dimension_semantics=("parallel", "parallel", "arbitrary"),
            vmem_limit_bytes=VMEM_LIMIT),
        name="rwkv_mix",
    )(wide, wide, wide, wide, small, small, pc, ps, w2p, a2p)


def _out_kernel(oa_ref, ob_ref, ga_ref, gb_ref, x_ref, wpf_ref, wpr_ref, wo_ref, g_ref, o_ref):
    pa = jnp.dot(oa_ref[...], wpf_ref[...], preferred_element_type=f32)
    pb = jnp.dot(ob_ref[...], wpr_ref[...], preferred_element_type=f32)
    m = _sigmoid(ga_ref[...]) * pa + _sigmoid(gb_ref[...]) * pb
    z = x_ref[...] + jnp.dot(m.astype(bf16), wo_ref[...], preferred_element_type=f32)
    ms = jnp.mean(z * z, axis=-1, keepdims=True)
    o_ref[...] = z * lax.rsqrt(ms + RMS_EPS) * g_ref[...]


def _merge_out(oa, ob, wide, x2d, wpf, wpr, wo, gain):
    m, d = x2d.shape
    tm = 256
    resident = lambda shape: pl.BlockSpec(shape, lambda i: (0, 0), pipeline_mode=pl.Buffered(1))
    return pl.pallas_call(
        _out_kernel,
        grid=(m // tm,),
        in_specs=[
            pl.BlockSpec((tm, FOX_WIDTH), lambda i: (i, 0)),
            pl.BlockSpec((tm, RWKV_WIDTH), lambda i: (i, 0)),
            pl.BlockSpec((tm, d), lambda i: (i, COL_GA * LANES // d)),
            pl.BlockSpec((tm, d), lambda i: (i, COL_GB * LANES // d)),
            pl.BlockSpec((tm, d), lambda i: (i, 0)),
            resident((FOX_WIDTH, d)),
            resident((RWKV_WIDTH, d)),
            resident((d, d)),
            resident((1, d)),
        ],
        out_specs=pl.BlockSpec((tm, d), lambda i: (i, 0)),
        out_shape=jax.ShapeDtypeStruct((m, d), f32),
        compiler_params=pltpu.CompilerParams(
            dimension_semantics=("parallel",),
            vmem_limit_bytes=VMEM_LIMIT),
        name="merge_out",
    )(oa, ob, wide, wide, x2d, wpf, wpr, wo, gain)


def _pad_cols(a, width):
    return jnp.pad(a, ((0, 0), (0, width - a.shape[1])))


def _layer(x2d, batch, seq, norm_gain, w_in, fox_forget_bias, rwkv_shift_mix, rwkv_w0, rwkv_w2,
           rwkv_a0, rwkv_a2, rwkv_k_k, rwkv_k_a, rwkv_r_k, rwkv_ln_w, rwkv_ln_b,
           w_proj_fox, w_proj_rwkv, w_out):
    fw, rw = FOX_WIDTH, RWKV_WIDTH
    o = 0
    w_qkv = w_in[:, o:o + 3 * fw]; o += 3 * fw
    w_za = w_in[:, o:o + fw]; o += fw
    w_f = w_in[:, o:o + FOX_HEADS]; o += FOX_HEADS
    w_rkvz = w_in[:, o:o + 4 * rw]; o += 4 * rw
    w_wd = w_in[:, o:o + LORA]; o += LORA
    w_ad = w_in[:, o:o + LORA]; o += LORA
    w_g = w_in[:, o:o + 2 * D_MODEL]
    q_scale = jnp.where(jnp.arange(3 * fw) < fw, FOX_HEAD_DIM ** -0.5 * LOG2E, 1.0).astype(f32)
    w_all = jnp.concatenate(
        [w_qkv * q_scale, w_g, w_za, w_rkvz,
         _pad_cols(w_wd, LANES), _pad_cols(w_ad, LANES), _pad_cols(w_f, LANES)], axis=1).astype(bf16)
    gain = norm_gain.reshape(1, D_MODEL)

    qkv = _norm_matmul(x2d, gain, w_all, 0, 3 * fw, bf16, 1024, 512)
    wide = _norm_matmul(x2d, gain, w_all, 3 * fw, WIDE_COLS, f32, 1024, 512)
    small = _norm_matmul(x2d, gain, w_all, 3 * fw + WIDE_COLS, SMALL_COLS, f32, 1024, SMALL_COLS)

    bias_row = _pad_cols(fox_forget_bias.reshape(1, FOX_HEADS), LANES)
    c = _gate_cumsum(small, bias_row, batch, seq)
    qkv3 = qkv.reshape(batch, seq, 3 * fw)
    oa = _fox_attention(qkv3, c.reshape(batch, seq, LANES), wide.reshape(batch, seq, WIDE_COLS),
                        batch, seq).reshape(batch * seq, fw)

    mu = rwkv_shift_mix
    pc = jnp.zeros((P_ROWS, rw), f32)
    rows = [mu[0:rw], mu[rw:2 * rw], mu[2 * rw:3 * rw], mu[3 * rw:4 * rw], rwkv_w0, rwkv_a0,
            rwkv_k_k, rwkv_k_a, rwkv_r_k.reshape(rw), rwkv_ln_w, rwkv_ln_b]
    pc = pc.at[:len(rows)].set(jnp.stack(rows))
    ps = jnp.zeros((8, LANES), f32)
    ps = ps.at[0, :LORA].set(mu[4 * rw:4 * rw + LORA]).at[1, :LORA].set(mu[4 * rw + LORA:])
    w2p = jnp.pad(rwkv_w2, ((0, LANES - LORA), (0, 0))).astype(bf16)
    a2p = jnp.pad(rwkv_a2, ((0, LANES - LORA), (0, 0))).astype(bf16)
    ob = _rwkv_mix(wide, small, pc, ps, w2p, a2p, batch, seq)

    return oa, ob, wide


def kernel(x, norm_gain, w_in, fox_forget_bias, rwkv_shift_mix, rwkv_w0, rwkv_w2, rwkv_a0, rwkv_a2, rwkv_k_k, rwkv_k_a, rwkv_r_k, rwkv_ln_w, rwkv_ln_b, w_proj_fox, w_proj_rwkv, w_out, final_norm_gain):
    batch, seq, d = x.shape
    depth = norm_gain.shape[0]
    assert depth == 1, "the final rmsnorm is fused into the single layer's output kernel"
    x2d = x.reshape(batch * seq, d)
    oa, ob, wide = _layer(x2d, batch, seq, norm_gain[0], w_in[0], fox_forget_bias[0],
                          rwkv_shift_mix[0], rwkv_w0[0], rwkv_w2[0], rwkv_a0[0], rwkv_a2[0],
                          rwkv_k_k[0], rwkv_k_a[0], rwkv_r_k[0], rwkv_ln_w[0], rwkv_ln_b[0],
                          w_proj_fox[0], w_proj_rwkv[0], w_out[0])
    out = _merge_out(oa, ob, wide, x2d, w_proj_fox[0].astype(bf16), w_proj_rwkv[0].astype(bf16),
                     w_out[0].astype(bf16), final_norm_gain.reshape(1, d))
    return out.reshape(batch, seq, d)
```

```python
import functools

import jax
import jax.numpy as jnp
from jax import lax
from jax.experimental import pallas as pl
from jax.experimental.pallas import tpu as pltpu

D_MODEL = 2048
FOX_HEADS = 8
FOX_HEAD_DIM = 128
FOX_WIDTH = FOX_HEADS * FOX_HEAD_DIM
RWKV_HEADS = 16
RWKV_HEAD_DIM = 64
RWKV_WIDTH = RWKV_HEADS * RWKV_HEAD_DIM
LORA = 96
RMS_EPS = 1e-6
GN_EPS = 64e-5
L2_EPS = 1e-12

LANES = 128
VMEM_LIMIT = 56 * 1024 * 1024

COL_GA = 0
COL_GB = D_MODEL // LANES
COL_ZA = 2 * D_MODEL // LANES
COL_R = COL_ZA + FOX_WIDTH // LANES
COL_K = COL_R + RWKV_WIDTH // LANES
COL_V = COL_K + RWKV_WIDTH // LANES
COL_ZB = COL_V + RWKV_WIDTH // LANES
WIDE_COLS = (COL_ZB + RWKV_WIDTH // LANES) * LANES
SMALL_COLS = 3 * LANES

RWKV_CHUNK = 64
RWKV_BLOCK = 512
RWKV_GROUP = 8
HEADS_PER_GROUP = LANES // RWKV_HEAD_DIM
GROUP_ROWS = HEADS_PER_GROUP * RWKV_CHUNK

f32 = jnp.float32
bf16 = jnp.bfloat16


def _dot(a, b):
    return jnp.dot(a.astype(bf16), b.astype(bf16), preferred_element_type=f32)


def _dot_nt(a, b):
    return lax.dot_general(a.astype(bf16), b.astype(bf16), (((1,), (1,)), ((), ())),
                           preferred_element_type=f32)


def _dot_tn(a, b):
    return lax.dot_general(a.astype(bf16), b.astype(bf16), (((0,), (0,)), ((), ())),
                           preferred_element_type=f32)


def _dot_exact_ones(a, ones_bf16, nt=False):
    a1 = a.astype(bf16)
    r1 = a - a1.astype(f32)
    a2 = r1.astype(bf16)
    a3 = (r1 - a2.astype(f32)).astype(bf16)
    if nt:
        mm = lambda x: lax.dot_general(ones_bf16, x, (((1,), (0,)), ((), ())),
                                       preferred_element_type=f32)
    else:
        mm = lambda x: jnp.dot(x, ones_bf16, preferred_element_type=f32)
    return mm(a1) + mm(a2) + mm(a3)


def _softplus(x):
    return jnp.maximum(x, 0.0) + jnp.log1p(jnp.exp(-jnp.abs(x)))


def _sigmoid(x):
    return 1.0 / (1.0 + jnp.exp(-x))


def _norm_matmul_kernel(x_ref, g_ref, w_ref, o_ref, h_ref):
    @pl.when(pl.program_id(1) == 0)
    def _():
        xf = x_ref[...]
        ms = jnp.mean(xf * xf, axis=-1, keepdims=True)
        h_ref[...] = (xf * lax.rsqrt(ms + RMS_EPS) * g_ref[...]).astype(bf16)

    o_ref[...] = lax.dot_general(h_ref[...], w_ref[...], (((1,), (1,)), ((), ())),
                                 preferred_element_type=f32).astype(o_ref.dtype)


def _norm_matmul(x2d, gain, w_t, col_start, n, out_dtype, tm, tn):
    m, d = x2d.shape
    first = col_start // tn
    assert first * tn == col_start and n % tn == 0
    return pl.pallas_call(
        _norm_matmul_kernel,
        grid=(m // tm, n // tn),
        in_specs=[
            pl.BlockSpec((tm, d), lambda i, j: (i, 0)),
            pl.BlockSpec((1, d), lambda i, j: (0, 0)),
            pl.BlockSpec((tn, d), lambda i, j: (first + j, 0)),
        ],
        out_specs=pl.BlockSpec((tm, tn), lambda i, j: (i, j)),
        out_shape=jax.ShapeDtypeStruct((m, n), out_dtype),
        scratch_shapes=[pltpu.VMEM((tm, d), bf16)],
        compiler_params=pltpu.CompilerParams(
            dimension_semantics=("parallel", "arbitrary"),
            vmem_limit_bytes=VMEM_LIMIT),
        name="norm_matmul",
    )(x2d, gain, w_t)


def _gate_kernel(f_ref, bias_ref, c_ref, carry_ref, *, tb):
    @pl.when(pl.program_id(1) == 0)
    def _():
        carry_ref[...] = jnp.zeros_like(carry_ref)

    z = f_ref[...] + bias_ref[...]
    log_f = -_softplus(-z)
    row = lax.broadcasted_iota(jnp.int32, (tb, tb), 0)
    col = lax.broadcasted_iota(jnp.int32, (tb, tb), 1)
    tri = jnp.where(col <= row, 1.0, 0.0).astype(bf16)
    c = _dot_exact_ones(log_f, tri, nt=True) + carry_ref[0:1, :]
    c_ref[...] = c
    carry_ref[0:1, :] = c[tb - 1:tb, :]


def _gate_cumsum(small, bias_row, batch, seq):
    tb = 256
    nt = seq // tb
    return pl.pallas_call(
        functools.partial(_gate_kernel, tb=tb),
        grid=(batch, nt),
        in_specs=[
            pl.BlockSpec((tb, LANES), lambda b, t: (b * nt + t, 2)),
            pl.BlockSpec((1, LANES), lambda b, t: (0, 0)),
        ],
        out_specs=pl.BlockSpec((tb, LANES), lambda b, t: (b * nt + t, 0)),
        out_shape=jax.ShapeDtypeStruct((batch * seq, LANES), f32),
        scratch_shapes=[pltpu.VMEM((8, LANES), f32)],
        compiler_params=pltpu.CompilerParams(
            dimension_semantics=("parallel", "arbitrary")),
        name="fox_gate_cumsum",
    )(small, bias_row)


LOG2E = 1.4426950408889634
FOX_TK = 256
FOX_BATCH = 4


def _fox_kernel(q_ref, k_ref, v_ref, c_ref, z_ref, o_ref, crep_ref, acc_ref, vt_ref, *, seq, tk):
    h = pl.program_id(1)
    ng = seq // LANES
    nt = seq // tk
    gpt = tk // LANES
    mm = lambda a, b: jnp.dot(a, b, preferred_element_type=f32)
    mm_nt = lambda a, b: lax.dot_general(a, b, (((1,), (1,)), ((), ())), preferred_element_type=f32)

    src_lane = lax.broadcasted_iota(jnp.int32, (LANES, LANES), 0)
    pick = jnp.where(src_lane == h, 1.0, 0.0).astype(bf16)
    crep_ref[...] = _dot_exact_ones(c_ref[...], pick) * LOG2E
    for t in range(nt):
        vt_ref[t] = v_ref[t * tk:(t + 1) * tk, :].T

    key_rel = lax.broadcasted_iota(jnp.int32, (tk, LANES), 0)
    qry_rel = lax.broadcasted_iota(jnp.int32, (tk, LANES), 1)
    future = [key_rel > qry_rel + d * LANES for d in range(gpt)]

    sched = []
    for t in range(nt):
        visible = list(range(t * gpt, ng))
        sched += [[(t, g) for g in visible[k:k + FOX_BATCH]] for k in range(0, len(visible), FOX_BATCH)]

    def qk(batch):
        return [mm_nt(k_ref[t * tk:(t + 1) * tk, :], q_ref[g * LANES:(g + 1) * LANES, :])
                for t, g in batch]

    m = [None] * ng
    l = [None] * ng

    def finish(batch, alpha, pv):
        for (t, g), a, x in zip(batch, alpha, pv):
            acc_ref[g] = x if t == 0 else a * acc_ref[g] + x
            if t == g // gpt:
                rows = slice(g * LANES, (g + 1) * LANES)
                z = z_ref[rows, :]
                o = (acc_ref[g] / l[g]).T
                o_ref[rows, :] = (o * (z * _sigmoid(z))).astype(o_ref.dtype)

    s_next = qk(sched[0])
    pending = None
    for bi, batch in enumerate(sched):
        s_cur = s_next
        if bi + 1 < len(sched):
            s_next = qk(sched[bi + 1])
        alpha, p16 = [], []
        for (t, g), s in zip(batch, s_cur):
            s = s - crep_ref[t * tk:(t + 1) * tk, :]
            if g < (t + 1) * gpt:
                s = jnp.where(future[g - t * gpt], -jnp.inf, s)
            m_tile = jnp.max(s, axis=0, keepdims=True)
            if t == 0:
                m_new, a = m_tile, None
            else:
                m_new = jnp.maximum(m[g], m_tile)
                a = jnp.exp2(m[g] - m_new)
            p = jnp.exp2(s - m_new)
            p_sum = jnp.sum(p, axis=0, keepdims=True)
            l[g] = p_sum if t == 0 else a * l[g] + p_sum
            m[g] = m_new
            alpha.append(a)
            p16.append(p.astype(bf16))
        pv = [mm(vt_ref[t], p) for (t, g), p in zip(batch, p16)]
        if pending is not None:
            finish(*pending)
        pending = (batch, alpha, pv)
    finish(*pending)


def _fox_attention(qkv3, c3, wide3, batch, seq):
    tk = FOX_TK
    h8 = FOX_HEADS
    return pl.pallas_call(
        functools.partial(_fox_kernel, seq=seq, tk=tk),
        grid=(batch, h8),
        in_specs=[
            pl.BlockSpec((None, seq, LANES), lambda b, h: (b, 0, h)),
            pl.BlockSpec((None, seq, LANES), lambda b, h: (b, 0, h8 + h)),
            pl.BlockSpec((None, seq, LANES), lambda b, h: (b, 0, 2 * h8 + h)),
            pl.BlockSpec((None, seq, LANES), lambda b, h: (b, 0, 0)),
            pl.BlockSpec((None, seq, LANES), lambda b, h: (b, 0, COL_ZA + h)),
        ],
        out_specs=pl.BlockSpec((None, seq, LANES), lambda b, h: (b, 0, h)),
        out_shape=jax.ShapeDtypeStruct((batch, seq, FOX_WIDTH), bf16),
        scratch_shapes=[
            pltpu.VMEM((seq, LANES), f32),
            pltpu.VMEM((seq // LANES, LANES, LANES), f32),
            pltpu.VMEM((seq // tk, LANES, tk), bf16),
        ],
        compiler_params=pltpu.CompilerParams(
            dimension_semantics=("parallel", "arbitrary"),
            vmem_limit_bytes=VMEM_LIMIT),
        name="fox_attention",
    )(qkv3, qkv3, qkv3, c3, wide3)


P_MU_R, P_MU_K, P_MU_V, P_MU_Z, P_W0, P_A0, P_KK, P_KA, P_RK, P_LNW, P_LNB = range(11)
P_ROWS = 16


def _rwkv_kernel(r_ref, k_ref, v_ref, z_ref, wd_ref, ad_ref, pc_ref, ps_ref, w2_ref, a2_ref,
                 o_ref, state_ref, prev_ref, *, tb):
    C = RWKV_CHUNK
    G = GROUP_ROWS
    N = RWKV_HEAD_DIM

    @pl.when(pl.program_id(2) == 0)
    def _():
        state_ref[...] = jnp.zeros_like(state_ref)
        prev_ref[...] = jnp.zeros_like(prev_ref)

    row_in_block = lax.broadcasted_iota(jnp.int32, (tb, LANES), 0)

    def shifted(ref, slot, mu):
        u = ref[...]
        prev = jnp.where(row_in_block == 0, prev_ref[slot, 0:1, :], pltpu.roll(u, 1, 0))
        prev_ref[slot, 0:1, :] = u[tb - 1:tb, :]
        return u + (prev - u) * mu

    prm = lambda idx: pc_ref[idx:idx + 1, :]
    r = shifted(r_ref, 0, prm(P_MU_R))
    kr = shifted(k_ref, 1, prm(P_MU_K))
    vr = shifted(v_ref, 2, prm(P_MU_V))
    zb = shifted(z_ref, 3, prm(P_MU_Z))
    wd = shifted(wd_ref, 4, ps_ref[0:1, :])
    ad = shifted(ad_ref, 5, ps_ref[1:2, :])

    w = -_softplus(-(prm(P_W0) + _dot(jnp.tanh(wd), w2_ref[...]))) - 0.5
    log_decay = -jnp.exp(w)
    rate = _sigmoid(prm(P_A0) + _dot(ad, a2_ref[...]))

    li = lax.broadcasted_iota(jnp.int32, (LANES, LANES), 0)
    lj = lax.broadcasted_iota(jnp.int32, (LANES, LANES), 1)
    head_ones = jnp.where(li // N == lj // N, 1.0, 0.0).astype(bf16)
    head_sum = lambda x: _dot_exact_ones(x, head_ones)

    kk = kr * prm(P_KK)
    kk = kk / jnp.maximum(jnp.sqrt(head_sum(kk * kk)), L2_EPS)
    kp = kr * (1.0 + (rate - 1.0) * prm(P_KA))
    bb = kk * rate
    bonus = head_sum(r * kp * prm(P_RK)) * vr

    ti = lax.broadcasted_iota(jnp.int32, (tb, tb), 0)
    tj = lax.broadcasted_iota(jnp.int32, (tb, tb), 1)
    chunk_tri = jnp.where(jnp.logical_and(tj <= ti, ti // C == tj // C), 1.0, 0.0).astype(bf16)
    ci = _dot_exact_ones(log_decay, chunk_tri, nt=True)

    a_t = -kk * jnp.exp(ci - log_decay)
    r_t = r * jnp.exp(ci)
    inv = jnp.exp(-ci)
    b_t = bb * inv
    k_t = kp * inv

    lane = lax.broadcasted_iota(jnp.int32, (1, LANES), 1)
    head_masks = [jnp.where(lane // N == h, 1.0, 0.0) for h in range(HEADS_PER_GROUP)]
    stack = lambda x: jnp.concatenate([x * hm for hm in head_masks], axis=0)
    stack16 = lambda x: stack(x).astype(bf16)

    gi = lax.broadcasted_iota(jnp.int32, (G, G), 0)
    gj = lax.broadcasted_iota(jnp.int32, (G, G), 1)
    same_head = gi // C == gj // C
    strict = jnp.logical_and(same_head, gj < gi)
    incl = jnp.logical_and(same_head, gj <= gi)
    eye = jnp.where(gi == gj, 1.0, 0.0)

    mm = lambda a, b: jnp.dot(a, b, preferred_element_type=f32)
    mm_nt = lambda a, b: lax.dot_general(a, b, (((1,), (1,)), ((), ())), preferred_element_type=f32)
    mm_tn = lambda a, b: lax.dot_general(a, b, (((0,), (0,)), ((), ())), preferred_element_type=f32)

    def independent(chunks):
        n = range(len(chunks))
        rows = [slice(c * C, (c + 1) * C) for c in chunks]
        c_last = [ci[sl][C - 1:C, :] for sl in rows]
        to_end = [jnp.exp(c_last[i] - ci[rows[i]]) for i in n]
        xr32 = [stack(r_t[sl]) for sl in rows]
        xa = [stack16(a_t[sl]) for sl in rows]
        xr = [x.astype(bf16) for x in xr32]
        yb = [stack16(b_t[sl]) for sl in rows]
        yk = [stack16(k_t[sl]) for sl in rows]
        vs = [stack16(vr[sl]) for sl in rows]
        bh = [stack16(bb[rows[i]] * to_end[i]) for i in n]
        kh = [stack16(kp[rows[i]] * to_end[i]) for i in n]

        big = [mm_nt(jnp.concatenate([xa[i], xr[i]], axis=0), jnp.concatenate([yb[i], yk[i]], axis=0))
               for i in n]
        a_ab = [jnp.where(strict, big[i][0:G, 0:G], 0.0) for i in n]
        a_ak = [jnp.where(strict, big[i][0:G, G:2 * G], 0.0).astype(bf16) for i in n]
        a_rb = [jnp.where(incl, big[i][G:2 * G, 0:G], 0.0).astype(bf16) for i in n]
        a_rk = [jnp.where(incl, big[i][G:2 * G, G:2 * G], 0.0).astype(bf16) for i in n]

        tinv = [eye + a for a in a_ab]
        pw = [a.astype(bf16) for a in a_ab]
        for _ in range(5):
            pw = [mm(p, p).astype(bf16) for p in pw]
            tinv = [tinv[i] + mm(tinv[i].astype(bf16), pw[i]) for i in n]
        tinv = [t.astype(bf16) for t in tinv]

        akv = [mm(a_ak[i], vs[i]).astype(bf16) for i in n]
        wu = [mm(tinv[i], jnp.concatenate([xa[i], akv[i]], axis=1)).astype(bf16) for i in n]
        e = [mm(a_rb[i], wu[i]) for i in n]
        ry = [(xr32[i] + e[i][:, 0:LANES]).astype(bf16) for i in n]
        y_loc = [e[i][:, LANES:] + mm(a_rk[i], vs[i]) for i in n]
        pm = [mm_tn(wu[i][:, 0:LANES], bh[i]).astype(bf16) for i in n]
        q = [mm_tn(jnp.concatenate([wu[i][:, LANES:], vs[i]], axis=0),
                   jnp.concatenate([bh[i], kh[i]], axis=0)) for i in n]
        return [dict(ry=ry[i], y_loc=y_loc[i], pm=pm[i], q=q[i], decay=jnp.exp(c_last[i])) for i in n]

    def sequential(state, parts):
        out = []
        for p in parts:
            s16 = state.astype(bf16)
            y_st = mm_nt(p["ry"], s16) + p["y_loc"]
            out.append(y_st[0:C] + y_st[C:2 * C])
            state = state * p["decay"] + mm(s16, p["pm"]) + p["q"]
        return state, out

    n_chunks = tb // C
    groups = [list(range(g, min(g + RWKV_GROUP, n_chunks))) for g in range(0, n_chunks, RWKV_GROUP)]
    state = state_ref[...]
    ys = []
    parts = independent(groups[0])
    for g in groups[1:]:
        state, out = sequential(state, parts)
        ys += out
        parts = independent(g)
    state, out = sequential(state, parts)
    ys += out
    state_ref[...] = state

    y = jnp.concatenate(ys, axis=0)
    mean = head_sum(y) * (1.0 / N)
    yc = y - mean
    var = head_sum(yc * yc) * (1.0 / N)
    yn = yc * lax.rsqrt(var + GN_EPS) * prm(P_LNW) + prm(P_LNB)
    o_ref[...] = ((yn + bonus) * (zb * _sigmoid(zb))).astype(o_ref.dtype)


def _rwkv_mix(wide, small, pc, ps, w2p, a2p, batch, seq):
    tb = RWKV_BLOCK
    nt = seq // tb
    groups = RWKV_WIDTH // LANES
    row = lambda b, g, t: b * nt + t
    return pl.pallas_call(
        functools.partial(_rwkv_kernel, tb=tb),
        grid=(batch, groups, nt),
        in_specs=[
            pl.BlockSpec((tb, LANES), lambda b, g, t: (row(b, g, t), COL_R + g)),
            pl.BlockSpec((tb, LANES), lambda b, g, t: (row(b, g, t), COL_K + g)),
            pl.BlockSpec((tb, LANES), lambda b, g, t: (row(b, g, t), COL_V + g)),
            pl.BlockSpec((tb, LANES), lambda b, g, t: (row(b, g, t), COL_ZB + g)),
            pl.BlockSpec((tb, LANES), lambda b, g, t: (row(b, g, t), 0)),
            pl.BlockSpec((tb, LANES), lambda b, g, t: (row(b, g, t), 1)),
            pl.BlockSpec((P_ROWS, LANES), lambda b, g, t: (0, g)),
            pl.BlockSpec((8, LANES), lambda b, g, t: (0, 0)),
            pl.BlockSpec((LANES, LANES), lambda b, g, t: (0, g)),
            pl.BlockSpec((LANES, LANES), lambda b, g, t: (0, g)),
        ],
        out_specs=pl.BlockSpec((tb, LANES), lambda b, g, t: (row(b, g, t), g)),
        out_shape=jax.ShapeDtypeStruct((batch * seq, RWKV_WIDTH), bf16),
        scratch_shapes=[
            pltpu.VMEM((LANES, LANES), f32),
            pltpu.VMEM((6, 8, LANES), f32),
        ],
        compiler_params=pltpu.CompilerParams(
            dimension_semantics=("parallel", "parallel", "arbitrary"),
            vmem_limit_bytes=VMEM_LIMIT),
        name="rwkv_mix",
    )(wide, wide, wide, wide, small, small, pc, ps, w2p, a2p)


def _out_kernel(oa_ref, ob_ref, ga_ref, gb_ref, x_ref, wpf_ref, wpr_ref, wo_ref, g_ref, o_ref):
    pa = jnp.dot(oa_ref[...], wpf_ref[...], preferred_element_type=f32)
    pb = jnp.dot(ob_ref[...], wpr_ref[...], preferred_element_type=f32)
    m = _sigmoid(ga_ref[...]) * pa + _sigmoid(gb_ref[...]) * pb
    z = x_ref[...] + jnp.dot(m.astype(bf16), wo_ref[...], preferred_element_type=f32)
    ms = jnp.mean(z * z, axis=-1, keepdims=True)
    o_ref[...] = z * lax.rsqrt(ms + RMS_EPS) * g_ref[...]


def _merge_out(oa, ob, wide, x2d, wpf, wpr, wo, gain):
    m, d = x2d.shape
    tm = 256
    resident = lambda shape: pl.BlockSpec(shape, lambda i: (0, 0), pipeline_mode=pl.Buffered(1))
    return pl.pallas_call(
        _out_kernel,
        grid=(m // tm,),
        in_specs=[
            pl.BlockSpec((tm, FOX_WIDTH), lambda i: (i, 0)),
            pl.BlockSpec((tm, RWKV_WIDTH), lambda i: (i, 0)),
            pl.BlockSpec((tm, d), lambda i: (i, COL_GA * LANES // d)),
            pl.BlockSpec((tm, d), lambda i: (i, COL_GB * LANES // d)),
            pl.BlockSpec((tm, d), lambda i: (i, 0)),
            resident((FOX_WIDTH, d)),
            resident((RWKV_WIDTH, d)),
            resident((d, d)),
            resident((1, d)),
        ],
        out_specs=pl.BlockSpec((tm, d), lambda i: (i, 0)),
        out_shape=jax.ShapeDtypeStruct((m, d), f32),
        compiler_params=pltpu.CompilerParams(
            dimension_semantics=("parallel",),
            vmem_limit_bytes=VMEM_LIMIT),
        name="merge_out",
    )(oa, ob, wide, wide, x2d, wpf, wpr, wo, gain)


def _pad_cols(a, width):
    return jnp.pad(a, ((0, 0), (0, width - a.shape[1])))


def _layer(x2d, batch, seq, norm_gain, w_in, fox_forget_bias, rwkv_shift_mix, rwkv_w0, rwkv_w2,
           rwkv_a0, rwkv_a2, rwkv_k_k, rwkv_k_a, rwkv_r_k, rwkv_ln_w, rwkv_ln_b,
           w_proj_fox, w_proj_rwkv, w_out):
    fw, rw = FOX_WIDTH, RWKV_WIDTH
    w_t = w_in.T
    o = 0
    w_qkv = w_t[o:o + 3 * fw]; o += 3 * fw
    w_za = w_t[o:o + fw]; o += fw
    w_f = w_t[o:o + FOX_HEADS]; o += FOX_HEADS
    w_rkvz = w_t[o:o + 4 * rw]; o += 4 * rw
    w_wd = w_t[o:o + LORA]; o += LORA
    w_ad = w_t[o:o + LORA]; o += LORA
    w_g = w_t[o:o + 2 * D_MODEL]
    q_scale = jnp.where(jnp.arange(3 * fw) < fw, FOX_HEAD_DIM ** -0.5 * LOG2E, 1.0).astype(f32)
    pad_rows = lambda a: jnp.pad(a, ((0, LANES - a.shape[0]), (0, 0)))
    w_all = jnp.concatenate(
        [w_qkv * q_scale[:, None], w_g, w_za, w_rkvz, pad_rows(w_wd), pad_rows(w_ad), pad_rows(w_f)],
        axis=0).astype(bf16)
    gain = norm_gain.reshape(1, D_MODEL)

    qkv = _norm_matmul(x2d, gain, w_all, 0, 3 * fw, bf16, 1024, 512)
    wide = _norm_matmul(x2d, gain, w_all, 3 * fw, WIDE_COLS, f32, 1024, 512)
    small = _norm_matmul(x2d, gain, w_all, 3 * fw + WIDE_COLS, SMALL_COLS, f32, 1024, SMALL_COLS)

    bias_row = _pad_cols(fox_forget_bias.reshape(1, FOX_HEADS), LANES)
    c = _gate_cumsum(small, bias_row, batch, seq)
    qkv3 = qkv.reshape(batch, seq, 3 * fw)
    oa = _fox_attention(qkv3, c.reshape(batch, seq, LANES), wide.reshape(batch, seq, WIDE_COLS),
                        batch, seq).reshape(batch * seq, fw)

    mu = rwkv_shift_mix
    pc = jnp.zeros((P_ROWS, rw), f32)
    rows = [mu[0:rw], mu[rw:2 * rw], mu[2 * rw:3 * rw], mu[3 * rw:4 * rw], rwkv_w0, rwkv_a0,
            rwkv_k_k, rwkv_k_a, rwkv_r_k.reshape(rw), rwkv_ln_w, rwkv_ln_b]
    pc = pc.at[:len(rows)].set(jnp.stack(rows))
    ps = jnp.zeros((8, LANES), f32)
    ps = ps.at[0, :LORA].set(mu[4 * rw:4 * rw + LORA]).at[1, :LORA].set(mu[4 * rw + LORA:])
    w2p = jnp.pad(rwkv_w2, ((0, LANES - LORA), (0, 0))).astype(bf16)
    a2p = jnp.pad(rwkv_a2, ((0, LANES - LORA), (0, 0))).astype(bf16)
    ob = _rwkv_mix(wide, small, pc, ps, w2p, a2p, batch, seq)

    return oa, ob, wide


def kernel(x, norm_gain, w_in, fox_forget_bias, rwkv_shift_mix, rwkv_w0, rwkv_w2, rwkv_a0, rwkv_a2, rwkv_k_k, rwkv_k_a, rwkv_r_k, rwkv_ln_w, rwkv_ln_b, w_proj_fox, w_proj_rwkv, w_out, final_norm_gain):
    batch, seq, d = x.shape
    depth = norm_gain.shape[0]
    assert depth == 1, "the final rmsnorm is fused into the single layer's output kernel"
    x2d = x.reshape(batch * seq, d)
    oa, ob, wide = _layer(x2d, batch, seq, norm_gain[0], w_in[0], fox_forget_bias[0],
                          rwkv_shift_mix[0], rwkv_w0[0], rwkv_w2[0], rwkv_a0[0], rwkv_a2[0],
                          rwkv_k_k[0], rwkv_k_a[0], rwkv_r_k[0], rwkv_ln_w[0], rwkv_ln_b[0],
                          w_proj_fox[0], w_proj_rwkv[0], w_out[0])
    out = _merge_out(oa, ob, wide, x2d, w_proj_fox[0].astype(bf16), w_proj_rwkv[0].astype(bf16),
                     w_out[0].astype(bf16), final_norm_gain.reshape(1, d))
    return out.reshape(batch, seq, d)
```

```python
import functools

import jax
import jax.numpy as jnp
from jax import lax
from jax.experimental import pallas as pl
from jax.experimental.pallas import tpu as pltpu

D_MODEL = 2048
FOX_HEADS = 8
FOX_HEAD_DIM = 128
FOX_WIDTH = FOX_HEADS * FOX_HEAD_DIM
RWKV_HEADS = 16
RWKV_HEAD_DIM = 64
RWKV_WIDTH = RWKV_HEADS * RWKV_HEAD_DIM
LORA = 96
RMS_EPS = 1e-6
GN_EPS = 64e-5
L2_EPS = 1e-12

LANES = 128
VMEM_LIMIT = 56 * 1024 * 1024

COL_GA = 0
COL_GB = D_MODEL // LANES
COL_ZA = 2 * D_MODEL // LANES
COL_R = COL_ZA + FOX_WIDTH // LANES
COL_K = COL_R + RWKV_WIDTH // LANES
COL_V = COL_K + RWKV_WIDTH // LANES
COL_ZB = COL_V + RWKV_WIDTH // LANES
WIDE_COLS = (COL_ZB + RWKV_WIDTH // LANES) * LANES
SMALL_COLS = 3 * LANES

RWKV_CHUNK = 64
RWKV_BLOCK = 512
RWKV_GROUP = 8
HEADS_PER_GROUP = LANES // RWKV_HEAD_DIM
GROUP_ROWS = HEADS_PER_GROUP * RWKV_CHUNK

f32 = jnp.float32
bf16 = jnp.bfloat16


def _dot(a, b):
    return jnp.dot(a.astype(bf16), b.astype(bf16), preferred_element_type=f32)


def _dot_nt(a, b):
    return lax.dot_general(a.astype(bf16), b.astype(bf16), (((1,), (1,)), ((), ())),
                           preferred_element_type=f32)


def _dot_tn(a, b):
    return lax.dot_general(a.astype(bf16), b.astype(bf16), (((0,), (0,)), ((), ())),
                           preferred_element_type=f32)


def _dot_exact_ones(a, ones_bf16, nt=False, terms=3):
    if nt:
        mm = lambda x: lax.dot_general(ones_bf16, x, (((1,), (0,)), ((), ())),
                                       preferred_element_type=f32)
    else:
        mm = lambda x: jnp.dot(x, ones_bf16, preferred_element_type=f32)
    part = a.astype(bf16)
    out = mm(part)
    rest = a
    for _ in range(terms - 1):
        rest = rest - part.astype(f32)
        part = rest.astype(bf16)
        out = out + mm(part)
    return out


def _softplus(x):
    return jnp.maximum(x, 0.0) + jnp.log1p(jnp.exp(-jnp.abs(x)))


def _sigmoid(x):
    return 1.0 / (1.0 + jnp.exp(-x))


def _norm_matmul_kernel(x_ref, g_ref, w_ref, o_ref, h_ref):
    @pl.when(pl.program_id(1) == 0)
    def _():
        xf = x_ref[...]
        ms = jnp.mean(xf * xf, axis=-1, keepdims=True)
        h_ref[...] = (xf * lax.rsqrt(ms + RMS_EPS) * g_ref[...]).astype(bf16)

    o_ref[...] = lax.dot_general(h_ref[...], w_ref[...], (((1,), (1,)), ((), ())),
                                 preferred_element_type=f32).astype(o_ref.dtype)


def _norm_matmul(x2d, gain, w_t, col_start, n, out_dtype, tm, tn):
    m, d = x2d.shape
    first = col_start // tn
    assert first * tn == col_start and n % tn == 0
    return pl.pallas_call(
        _norm_matmul_kernel,
        grid=(m // tm, n // tn),
        in_specs=[
            pl.BlockSpec((tm, d), lambda i, j: (i, 0)),
            pl.BlockSpec((1, d), lambda i, j: (0, 0)),
            pl.BlockSpec((tn, d), lambda i, j: (first + j, 0)),
        ],
        out_specs=pl.BlockSpec((tm, tn), lambda i, j: (i, j)),
        out_shape=jax.ShapeDtypeStruct((m, n), out_dtype),
        scratch_shapes=[pltpu.VMEM((tm, d), bf16)],
        compiler_params=pltpu.CompilerParams(
            dimension_semantics=("parallel", "arbitrary"),
            vmem_limit_bytes=VMEM_LIMIT),
        name="norm_matmul",
    )(x2d, gain, w_t)


def _gate_kernel(f_ref, bias_ref, c_ref, carry_ref, *, tb):
    @pl.when(pl.program_id(1) == 0)
    def _():
        carry_ref[...] = jnp.zeros_like(carry_ref)

    z = f_ref[...] + bias_ref[...]
    log_f = -_softplus(-z)
    row = lax.broadcasted_iota(jnp.int32, (tb, tb), 0)
    col = lax.broadcasted_iota(jnp.int32, (tb, tb), 1)
    tri = jnp.where(col <= row, 1.0, 0.0).astype(bf16)
    c = _dot_exact_ones(log_f, tri, nt=True) + carry_ref[0:1, :]
    c_ref[...] = c
    carry_ref[0:1, :] = c[tb - 1:tb, :]


def _gate_cumsum(small, bias_row, batch, seq):
    tb = 256
    nt = seq // tb
    return pl.pallas_call(
        functools.partial(_gate_kernel, tb=tb),
        grid=(batch, nt),
        in_specs=[
            pl.BlockSpec((tb, LANES), lambda b, t: (b * nt + t, 2)),
            pl.BlockSpec((1, LANES), lambda b, t: (0, 0)),
        ],
        out_specs=pl.BlockSpec((tb, LANES), lambda b, t: (b * nt + t, 0)),
        out_shape=jax.ShapeDtypeStruct((batch * seq, LANES), f32),
        scratch_shapes=[pltpu.VMEM((8, LANES), f32)],
        compiler_params=pltpu.CompilerParams(
            dimension_semantics=("parallel", "arbitrary")),
        name="fox_gate_cumsum",
    )(small, bias_row)


LOG2E = 1.4426950408889634
FOX_TK = 256
FOX_BATCH = 4


def _fox_kernel(q_ref, k_ref, v_ref, c_ref, z_ref, o_ref, crep_ref, acc_ref, vt_ref, *, seq, tk):
    h = pl.program_id(1)
    ng = seq // LANES
    nt = seq // tk
    gpt = tk // LANES
    mm = lambda a, b: jnp.dot(a, b, preferred_element_type=f32)
    mm_nt = lambda a, b: lax.dot_general(a, b, (((1,), (1,)), ((), ())), preferred_element_type=f32)

    src_lane = lax.broadcasted_iota(jnp.int32, (LANES, LANES), 0)
    pick = jnp.where(src_lane == h, 1.0, 0.0).astype(bf16)
    crep_ref[...] = _dot_exact_ones(c_ref[...], pick) * LOG2E
    for t in range(nt):
        vt_ref[t] = v_ref[t * tk:(t + 1) * tk, :].T

    key_rel = lax.broadcasted_iota(jnp.int32, (tk, LANES), 0)
    qry_rel = lax.broadcasted_iota(jnp.int32, (tk, LANES), 1)
    future = [key_rel > qry_rel + d * LANES for d in range(gpt)]

    sched = []
    for t in range(nt):
        visible = list(range(t * gpt, ng))
        sched += [[(t, g) for g in visible[k:k + FOX_BATCH]] for k in range(0, len(visible), FOX_BATCH)]

    def qk(batch):
        return [mm_nt(k_ref[t * tk:(t + 1) * tk, :], q_ref[g * LANES:(g + 1) * LANES, :])
                for t, g in batch]

    m = [None] * ng
    l = [None] * ng

    def finish(batch, alpha, pv):
        for (t, g), a, x in zip(batch, alpha, pv):
            acc_ref[g] = x if t == 0 else a * acc_ref[g] + x
            if t == g // gpt:
                rows = slice(g * LANES, (g + 1) * LANES)
                z = z_ref[rows, :]
                o = (acc_ref[g] / l[g]).T
                o_ref[rows, :] = (o * (z * _sigmoid(z))).astype(o_ref.dtype)

    s_next = qk(sched[0])
    pending = None
    for bi, batch in enumerate(sched):
        s_cur = s_next
        if bi + 1 < len(sched):
            s_next = qk(sched[bi + 1])
        alpha, p16 = [], []
        for (t, g), s in zip(batch, s_cur):
            s = s - crep_ref[t * tk:(t + 1) * tk, :]
            if g < (t + 1) * gpt:
                s = jnp.where(future[g - t * gpt], -jnp.inf, s)
            m_tile = jnp.max(s, axis=0, keepdims=True)
            if t == 0:
                m_new, a = m_tile, None
            else:
                m_new = jnp.maximum(m[g], m_tile)
                a = jnp.exp2(m[g] - m_new)
            p = jnp.exp2(s - m_new)
            p_sum = jnp.sum(p, axis=0, keepdims=True)
            l[g] = p_sum if t == 0 else a * l[g] + p_sum
            m[g] = m_new
            alpha.append(a)
            p16.append(p.astype(bf16))
        pv = [mm(vt_ref[t], p) for (t, g), p in zip(batch, p16)]
        if pending is not None:
            finish(*pending)
        pending = (batch, alpha, pv)
    finish(*pending)


def _fox_attention(qkv3, c3, wide3, batch, seq):
    tk = FOX_TK
    h8 = FOX_HEADS
    return pl.pallas_call(
        functools.partial(_fox_kernel, seq=seq, tk=tk),
        grid=(batch, h8),
        in_specs=[
            pl.BlockSpec((None, seq, LANES), lambda b, h: (b, 0, h)),
            pl.BlockSpec((None, seq, LANES), lambda b, h: (b, 0, h8 + h)),
            pl.BlockSpec((None, seq, LANES), lambda b, h: (b, 0, 2 * h8 + h)),
            pl.BlockSpec((None, seq, LANES), lambda b, h: (b, 0, 0)),
            pl.BlockSpec((None, seq, LANES), lambda b, h: (b, 0, COL_ZA + h)),
        ],
        out_specs=pl.BlockSpec((None, seq, LANES), lambda b, h: (b, 0, h)),
        out_shape=jax.ShapeDtypeStruct((batch, seq, FOX_WIDTH), bf16),
        scratch_shapes=[
            pltpu.VMEM((seq, LANES), f32),
            pltpu.VMEM((seq // LANES, LANES, LANES), f32),
            pltpu.VMEM((seq // tk, LANES, tk), bf16),
        ],
        compiler_params=pltpu.CompilerParams(
            dimension_semantics=("parallel", "arbitrary"),
            vmem_limit_bytes=VMEM_LIMIT),
        name="fox_attention",
    )(qkv3, qkv3, qkv3, c3, wide3)


P_MU_R, P_MU_K, P_MU_V, P_MU_Z, P_W0, P_A0, P_KK, P_KA, P_RK, P_LNW, P_LNB = range(11)
P_ROWS = 16


def _rwkv_kernel(r_ref, k_ref, v_ref, z_ref, wd_ref, ad_ref, pc_ref, ps_ref, w2_ref, a2_ref,
                 o_ref, state_ref, prev_ref, *, tb):
    C = RWKV_CHUNK
    G = GROUP_ROWS
    N = RWKV_HEAD_DIM

    @pl.when(pl.program_id(2) == 0)
    def _():
        state_ref[...] = jnp.zeros_like(state_ref)
        prev_ref[...] = jnp.zeros_like(prev_ref)

    row_in_block = lax.broadcasted_iota(jnp.int32, (tb, LANES), 0)

    def shifted(ref, slot, mu):
        u = ref[...]
        prev = jnp.where(row_in_block == 0, prev_ref[slot, 0:1, :], pltpu.roll(u, 1, 0))
        prev_ref[slot, 0:1, :] = u[tb - 1:tb, :]
        return u + (prev - u) * mu

    prm = lambda idx: pc_ref[idx:idx + 1, :]
    r = shifted(r_ref, 0, prm(P_MU_R))
    kr = shifted(k_ref, 1, prm(P_MU_K))
    vr = shifted(v_ref, 2, prm(P_MU_V))
    zb = shifted(z_ref, 3, prm(P_MU_Z))
    wd = shifted(wd_ref, 4, ps_ref[0:1, :])
    ad = shifted(ad_ref, 5, ps_ref[1:2, :])

    w = -_softplus(-(prm(P_W0) + _dot(jnp.tanh(wd), w2_ref[...]))) - 0.5
    log_decay = -jnp.exp(w)
    rate = _sigmoid(prm(P_A0) + _dot(ad, a2_ref[...]))

    li = lax.broadcasted_iota(jnp.int32, (LANES, LANES), 0)
    lj = lax.broadcasted_iota(jnp.int32, (LANES, LANES), 1)
    head_ones = jnp.where(li // N == lj // N, 1.0, 0.0).astype(bf16)
    head_sum = lambda x: _dot_exact_ones(x, head_ones)

    kk = kr * prm(P_KK)
    kk = kk / jnp.maximum(jnp.sqrt(head_sum(kk * kk)), L2_EPS)
    kp = kr * (1.0 + (rate - 1.0) * prm(P_KA))
    bb = kk * rate
    bonus = head_sum(r * kp * prm(P_RK)) * vr

    ti = lax.broadcasted_iota(jnp.int32, (tb, tb), 0)
    tj = lax.broadcasted_iota(jnp.int32, (tb, tb), 1)
    chunk_tri = jnp.where(jnp.logical_and(tj <= ti, ti // C == tj // C), 1.0, 0.0).astype(bf16)
    ci = _dot_exact_ones(log_decay, chunk_tri, nt=True)

    a_t = -kk * jnp.exp(ci - log_decay)
    r_t = r * jnp.exp(ci)
    inv = jnp.exp(-ci)
    b_t = bb * inv
    k_t = kp * inv

    lane = lax.broadcasted_iota(jnp.int32, (1, LANES), 1)
    head_masks = [jnp.where(lane // N == h, 1.0, 0.0) for h in range(HEADS_PER_GROUP)]
    stack = lambda x: jnp.concatenate([x * hm for hm in head_masks], axis=0)
    stack16 = lambda x: stack(x).astype(bf16)

    gi = lax.broadcasted_iota(jnp.int32, (G, G), 0)
    gj = lax.broadcasted_iota(jnp.int32, (G, G), 1)
    same_head = gi // C == gj // C
    strict = jnp.logical_and(same_head, gj < gi)
    incl = jnp.logical_and(same_head, gj <= gi)
    eye = jnp.where(gi == gj, 1.0, 0.0)

    mm = lambda a, b: jnp.dot(a, b, preferred_element_type=f32)
    mm_nt = lambda a, b: lax.dot_general(a, b, (((1,), (1,)), ((), ())), preferred_element_type=f32)
    mm_tn = lambda a, b: lax.dot_general(a, b, (((0,), (0,)), ((), ())), preferred_element_type=f32)

    def independent(chunks):
        n = range(len(chunks))
        rows = [slice(c * C, (c + 1) * C) for c in chunks]
        c_last = [ci[sl][C - 1:C, :] for sl in rows]
        to_end = [jnp.exp(c_last[i] - ci[rows[i]]) for i in n]
        xr32 = [stack(r_t[sl]) for sl in rows]
        xa = [stack16(a_t[sl]) for sl in rows]
        xr = [x.astype(bf16) for x in xr32]
        yb = [stack16(b_t[sl]) for sl in rows]
        yk = [stack16(k_t[sl]) for sl in rows]
        vs = [stack16(vr[sl]) for sl in rows]
        bh = [stack16(bb[rows[i]] * to_end[i]) for i in n]
        kh = [stack16(kp[rows[i]] * to_end[i]) for i in n]

        big = [mm_nt(jnp.concatenate([xa[i], xr[i]], axis=0), jnp.concatenate([yb[i], yk[i]], axis=0))
               for i in n]
        a_ab = [jnp.where(strict, big[i][0:G, 0:G], 0.0) for i in n]
        a_ak = [jnp.where(strict, big[i][0:G, G:2 * G], 0.0).astype(bf16) for i in n]
        a_rb = [jnp.where(incl, big[i][G:2 * G, 0:G], 0.0).astype(bf16) for i in n]
        a_rk = [jnp.where(incl, big[i][G:2 * G, G:2 * G], 0.0).astype(bf16) for i in n]

        tinv = [eye + a for a in a_ab]
        pw = [a.astype(bf16) for a in a_ab]
        for _ in range(5):
            pw = [mm(p, p).astype(bf16) for p in pw]
            tinv = [tinv[i] + mm(tinv[i].astype(bf16), pw[i]) for i in n]
        tinv = [t.astype(bf16) for t in tinv]

        akv = [mm(a_ak[i], vs[i]).astype(bf16) for i in n]
        wu = [mm(tinv[i], jnp.concatenate([xa[i], akv[i]], axis=1)).astype(bf16) for i in n]
        e = [mm(a_rb[i], wu[i]) for i in n]
        ry = [(xr32[i] + e[i][:, 0:LANES]).astype(bf16) for i in n]
        y_loc = [e[i][:, LANES:] + mm(a_rk[i], vs[i]) for i in n]
        pm = [mm_tn(wu[i][:, 0:LANES], bh[i]).astype(bf16) for i in n]
        q = [mm_tn(jnp.concatenate([wu[i][:, LANES:], vs[i]], axis=0),
                   jnp.concatenate([bh[i], kh[i]], axis=0)) for i in n]
        return [dict(ry=ry[i], y_loc=y_loc[i], pm=pm[i], q=q[i], decay=jnp.exp(c_last[i])) for i in n]

    def sequential(state, parts):
        out = []
        for p in parts:
            s16 = state.astype(bf16)
            y_st = mm_nt(p["ry"], s16) + p["y_loc"]
            out.append(y_st[0:C] + y_st[C:2 * C])
            state = state * p["decay"] + mm(s16, p["pm"]) + p["q"]
        return state, out

    n_chunks = tb // C
    groups = [list(range(g, min(g + RWKV_GROUP, n_chunks))) for g in range(0, n_chunks, RWKV_GROUP)]
    state = state_ref[...]
    ys = []
    parts = independent(groups[0])
    for g in groups[1:]:
        state, out = sequential(state, parts)
        ys += out
        parts = independent(g)
    state, out = sequential(state, parts)
    ys += out
    state_ref[...] = state

    y = jnp.concatenate(ys, axis=0)
    mean = head_sum(y) * (1.0 / N)
    yc = y - mean
    var = head_sum(yc * yc) * (1.0 / N)
    yn = yc * lax.rsqrt(var + GN_EPS) * prm(P_LNW) + prm(P_LNB)
    o_ref[...] = ((yn + bonus) * (zb * _sigmoid(zb))).astype(o_ref.dtype)


def _rwkv_mix_serial(wide, small, pc, ps, w2p, a2p, batch, seq):
    tb = RWKV_BLOCK
    nt = seq // tb
    groups = RWKV_WIDTH // LANES
    row = lambda b, g, t: b * nt + t
    return pl.pallas_call(
        functools.partial(_rwkv_kernel, tb=tb),
        grid=(batch, groups, nt),
        in_specs=[
            pl.BlockSpec((tb, LANES), lambda b, g, t: (row(b, g, t), COL_R + g)),
            pl.BlockSpec((tb, LANES), lambda b, g, t: (row(b, g, t), COL_K + g)),
            pl.BlockSpec((tb, LANES), lambda b, g, t: (row(b, g, t), COL_V + g)),
            pl.BlockSpec((tb, LANES), lambda b, g, t: (row(b, g, t), COL_ZB + g)),
            pl.BlockSpec((tb, LANES), lambda b, g, t: (row(b, g, t), 0)),
            pl.BlockSpec((tb, LANES), lambda b, g, t: (row(b, g, t), 1)),
            pl.BlockSpec((P_ROWS, LANES), lambda b, g, t: (0, g)),
            pl.BlockSpec((8, LANES), lambda b, g, t: (0, 0)),
            pl.BlockSpec((LANES, LANES), lambda b, g, t: (0, g)),
            pl.BlockSpec((LANES, LANES), lambda b, g, t: (0, g)),
        ],
        out_specs=pl.BlockSpec((tb, LANES), lambda b, g, t: (row(b, g, t), g)),
        out_shape=jax.ShapeDtypeStruct((batch * seq, RWKV_WIDTH), bf16),
        scratch_shapes=[
            pltpu.VMEM((LANES, LANES), f32),
            pltpu.VMEM((6, 8, LANES), f32),
        ],
        compiler_params=pltpu.CompilerParams(
            dimension_semantics=("parallel", "parallel", "arbitrary"),
            vmem_limit_bytes=VMEM_LIMIT),
        name="rwkv_mix",
    )(wide, wide, wide, wide, small, small, pc, ps, w2p, a2p)


K_XA, K_XR, K_YB, K_YK, K_VS, K_BH, K_KH = range(7)


def _interleave(*segment_lists):
    keyed = []
    for prio, segs in enumerate(segment_lists):
        for i, seg in enumerate(segs):
            keyed.append(((i + 0.5) / len(segs), prio, seg))
    keyed.sort(key=lambda x: (x[0], x[1]))
    for _, _, seg in keyed:
        seg()


def _rwkv_pipe_kernel(r_ref, k_ref, v_ref, z_ref, wd_ref, ad_ref, pc_ref, ps_ref, w2_ref, a2_ref, pc3_ref,
                      o_ref,
                      state_ref, prev_ref, stk_ref, xr32_ref, aux1_ref, dec1_ref, p16_ref, p32_ref,
                      aux2_ref, dec2_ref, *, tb, nt):
    C = RWKV_CHUNK
    G = GROUP_ROWS
    N = RWKV_HEAD_DIM
    NC = tb // C
    s = pl.program_id(0)
    first_of_seq_1 = (s % nt) == 0
    first_of_seq_3 = ((s + 2 * nt - 2) % nt) == 0

    @pl.when(s == 0)
    def _():
        for ref in (state_ref, prev_ref, stk_ref, xr32_ref, aux1_ref, dec1_ref, p16_ref, p32_ref,
                    aux2_ref, dec2_ref):
            ref[...] = jnp.zeros_like(ref)

    mm = lambda a, b: jnp.dot(a, b, preferred_element_type=f32)
    mm_nt = lambda a, b: lax.dot_general(a, b, (((1,), (1,)), ((), ())), preferred_element_type=f32)
    mm_tn = lambda a, b: lax.dot_general(a, b, (((0,), (0,)), ((), ())), preferred_element_type=f32)

    def body(cur, prv):
        li = lax.broadcasted_iota(jnp.int32, (LANES, LANES), 0)
        lj = lax.broadcasted_iota(jnp.int32, (LANES, LANES), 1)
        head_ones = jnp.where(li // N == lj // N, 1.0, 0.0).astype(bf16)
        head_sum = lambda x: _dot_exact_ones(x, head_ones, terms=2)
        lane = lax.broadcasted_iota(jnp.int32, (1, LANES), 1)
        head_masks = [jnp.where(lane // N == h, 1.0, 0.0) for h in range(HEADS_PER_GROUP)]
        stack = lambda x: jnp.concatenate([x * hm for hm in head_masks], axis=0)
        gi = lax.broadcasted_iota(jnp.int32, (G, G), 0)
        gj = lax.broadcasted_iota(jnp.int32, (G, G), 1)
        same_head = gi // C == gj // C
        strict = jnp.logical_and(same_head, gj < gi)
        incl = jnp.logical_and(same_head, gj <= gi)
        eye = jnp.where(gi == gj, 1.0, 0.0)
        chunks = range(NC)

        v1 = {}
        prm = lambda idx: pc_ref[idx:idx + 1, :]
        row_in_block = lax.broadcasted_iota(jnp.int32, (tb, LANES), 0)

        def shifted(ref, slot, mu):
            u = ref[...]
            carry = jnp.where(first_of_seq_1, 0.0, prev_ref[slot, 0:1, :])
            prev = jnp.where(row_in_block == 0, carry, pltpu.roll(u, 1, 0))
            prev_ref[slot, 0:1, :] = u[tb - 1:tb, :]
            return u + (prev - u) * mu

        def s1_load():
            v1["r"] = shifted(r_ref, 0, prm(P_MU_R))
            v1["kr"] = shifted(k_ref, 1, prm(P_MU_K))
            v1["vr"] = shifted(v_ref, 2, prm(P_MU_V))
            zb = shifted(z_ref, 3, prm(P_MU_Z))
            aux1_ref[cur, 1] = zb * _sigmoid(zb)
            v1["wd"] = shifted(wd_ref, 4, ps_ref[0:1, :])
            v1["ad"] = shifted(ad_ref, 5, ps_ref[1:2, :])

        def s1_lora():
            w = -_softplus(-(prm(P_W0) + _dot(jnp.tanh(v1["wd"]), w2_ref[...]))) - 0.5
            v1["log_decay"] = -jnp.exp(w)
            v1["rate"] = _sigmoid(prm(P_A0) + _dot(v1["ad"], a2_ref[...]))

        def s1_keys():
            kr, rate = v1["kr"], v1["rate"]
            kk = kr * prm(P_KK)
            kk = kk / jnp.maximum(jnp.sqrt(head_sum(kk * kk)), L2_EPS)
            kp = kr * (1.0 + (rate - 1.0) * prm(P_KA))
            v1["kk"], v1["kp"], v1["bb"] = kk, kp, kk * rate
            aux1_ref[cur, 0] = head_sum(v1["r"] * kp * prm(P_RK)) * v1["vr"]

        def s1_decay():
            ti = lax.broadcasted_iota(jnp.int32, (2 * C, 2 * C), 0)
            tj = lax.broadcasted_iota(jnp.int32, (2 * C, 2 * C), 1)
            pair_tri = jnp.where(jnp.logical_and(tj <= ti, ti // C == tj // C), 1.0, 0.0).astype(bf16)
            ci = jnp.concatenate(
                [_dot_exact_ones(v1["log_decay"][k:k + 2 * C], pair_tri, nt=True)
                 for k in range(0, tb, 2 * C)], axis=0)
            v1["ci"] = ci
            v1["a_t"] = -v1["kk"] * jnp.exp(ci - v1["log_decay"])
            v1["r_t"] = v1["r"] * jnp.exp(ci)
            inv = jnp.exp(-ci)
            v1["b_t"] = v1["bb"] * inv
            v1["k_t"] = v1["kp"] * inv

        def s1_stack(c):
            def emit():
                sl = slice(c * C, (c + 1) * C)
                ci_c = v1["ci"][sl]
                c_last = ci_c[C - 1:C, :]
                to_end = jnp.exp(c_last - ci_c)
                dec1_ref[cur, c] = jnp.broadcast_to(jnp.exp(c_last), (8, LANES))
                xr = stack(v1["r_t"][sl])
                xr32_ref[cur, c] = xr
                stk_ref[cur, K_XR, c] = xr.astype(bf16)
                stk_ref[cur, K_XA, c] = stack(v1["a_t"][sl]).astype(bf16)
                stk_ref[cur, K_YB, c] = stack(v1["b_t"][sl]).astype(bf16)
                stk_ref[cur, K_YK, c] = stack(v1["k_t"][sl]).astype(bf16)
                stk_ref[cur, K_VS, c] = stack(v1["vr"][sl]).astype(bf16)
                stk_ref[cur, K_BH, c] = stack(v1["bb"][sl] * to_end).astype(bf16)
                stk_ref[cur, K_KH, c] = stack(v1["kp"][sl] * to_end).astype(bf16)
            return emit

        segs1 = [s1_load, s1_lora, s1_keys, s1_decay] + [s1_stack(c) for c in chunks]

        v2 = {}
        ld = lambda kind, c: stk_ref[prv, kind, c]

        def s2_big():
            big = [mm_nt(jnp.concatenate([ld(K_XA, c), ld(K_XR, c)], axis=0),
                         jnp.concatenate([ld(K_YB, c), ld(K_YK, c)], axis=0)) for c in chunks]
            v2["a_ab"] = [jnp.where(strict, big[c][0:G, 0:G], 0.0) for c in chunks]
            v2["a_ak"] = [jnp.where(strict, big[c][0:G, G:2 * G], 0.0).astype(bf16) for c in chunks]
            v2["a_rb"] = [jnp.where(incl, big[c][G:2 * G, 0:G], 0.0).astype(bf16) for c in chunks]
            v2["a_rk"] = [jnp.where(incl, big[c][G:2 * G, G:2 * G], 0.0).astype(bf16) for c in chunks]
            v2["tinv"] = [eye + a for a in v2["a_ab"]]
            v2["pw"] = [a.astype(bf16) for a in v2["a_ab"]]

        def s2_square():
            v2["pw"] = [mm(p, p).astype(bf16) for p in v2["pw"]]

        def s2_accumulate():
            v2["tinv"] = [v2["tinv"][c] + mm(v2["tinv"][c].astype(bf16), v2["pw"][c]) for c in chunks]

        def s2_akv():
            v2["tinv"] = [t.astype(bf16) for t in v2["tinv"]]
            v2["akv"] = [mm(v2["a_ak"][c], ld(K_VS, c)).astype(bf16) for c in chunks]

        def s2_wu():
            v2["wu"] = [mm(v2["tinv"][c], jnp.concatenate([ld(K_XA, c), v2["akv"][c]], axis=1)).astype(bf16)
                        for c in chunks]

        def s2_ry():
            for c in chunks:
                e = mm(v2["a_rb"][c], v2["wu"][c])
                p16_ref[cur, 0, c] = (xr32_ref[prv, c] + e[:, 0:LANES]).astype(bf16)
                p32_ref[cur, 0, c] = e[:, LANES:] + mm(v2["a_rk"][c], ld(K_VS, c))

        def s2_pm():
            for c in chunks:
                p16_ref[cur, 1, c] = mm_tn(v2["wu"][c][:, 0:LANES], ld(K_BH, c)).astype(bf16)

        def s2_q():
            for c in chunks:
                p32_ref[cur, 1, c] = mm_tn(jnp.concatenate([v2["wu"][c][:, LANES:], ld(K_VS, c)], axis=0),
                                           jnp.concatenate([ld(K_BH, c), ld(K_KH, c)], axis=0))
            aux2_ref[cur] = aux1_ref[prv]
            dec2_ref[cur] = dec1_ref[prv]

        segs2 = [s2_big] + [s2_square, s2_accumulate] * 5 + [s2_akv, s2_wu, s2_ry, s2_pm, s2_q]

        v3 = {"ys": []}
        prm3 = lambda idx: pc3_ref[idx:idx + 1, :]

        def s3_start():
            v3["state"] = jnp.where(first_of_seq_3, 0.0, state_ref[...])

        def s3_chunk(c):
            def emit():
                state = v3["state"]
                s16 = state.astype(bf16)
                y_st = mm_nt(p16_ref[prv, 0, c], s16) + p32_ref[prv, 0, c]
                v3["ys"].append(y_st[0:C] + y_st[C:2 * C])
                v3["state"] = (state * dec2_ref[prv, c, 0:1, :] + mm(s16, p16_ref[prv, 1, c])
                               + p32_ref[prv, 1, c])
            return emit

        def s3_finish():
            state_ref[...] = v3["state"]
            y = jnp.concatenate(v3["ys"], axis=0)
            mean = head_sum(y) * (1.0 / N)
            yc = y - mean
            var = head_sum(yc * yc) * (1.0 / N)
            yn = yc * lax.rsqrt(var + GN_EPS) * prm3(P_LNW) + prm3(P_LNB)
            o_ref[...] = ((yn + aux2_ref[prv, 0]) * aux2_ref[prv, 1]).astype(o_ref.dtype)

        segs3 = [s3_start] + [s3_chunk(c) for c in chunks] + [s3_finish]

        _interleave(segs3, segs2, segs1)

    for parity in (0, 1):
        @pl.when(s % 2 == parity)
        def _(parity=parity):
            body(cur=parity, prv=1 - parity)


def _rwkv_mix(wide, small, pc, ps, w2p, a2p, batch, seq):
    tb = RWKV_BLOCK
    nt = seq // tb
    groups = RWKV_WIDTH // LANES
    nc = tb // RWKV_CHUNK
    n_blocks = batch * groups * nt

    def where(s):
        s = jnp.clip(s, 0, n_blocks - 1)
        return (s // (nt * groups)) * nt + s % nt, (s // nt) % groups

    col = lambda c0: (lambda s: (where(s)[0], c0 + where(s)[1]))
    fixed_col = lambda c: (lambda s: (where(s)[0], c))
    grp = lambda s: (0, where(s)[1])
    return pl.pallas_call(
        functools.partial(_rwkv_pipe_kernel, tb=tb, nt=nt),
        grid=(n_blocks + 2,),
        in_specs=[
            pl.BlockSpec((tb, LANES), col(COL_R)),
            pl.BlockSpec((tb, LANES), col(COL_K)),
            pl.BlockSpec((tb, LANES), col(COL_V)),
            pl.BlockSpec((tb, LANES), col(COL_ZB)),
            pl.BlockSpec((tb, LANES), fixed_col(0)),
            pl.BlockSpec((tb, LANES), fixed_col(1)),
            pl.BlockSpec((P_ROWS, LANES), grp),
            pl.BlockSpec((8, LANES), lambda s: (0, 0)),
            pl.BlockSpec((LANES, LANES), grp),
            pl.BlockSpec((LANES, LANES), grp),
            pl.BlockSpec((P_ROWS, LANES), lambda s: grp(s - 2)),
        ],
        out_specs=pl.BlockSpec((tb, LANES), lambda s: (where(s - 2)[0], where(s - 2)[1])),
        out_shape=jax.ShapeDtypeStruct((batch * seq, RWKV_WIDTH), bf16),
        scratch_shapes=[
            pltpu.VMEM((LANES, LANES), f32),
            pltpu.VMEM((6, 8, LANES), f32),
            pltpu.VMEM((2, 7, nc, GROUP_ROWS, LANES), bf16),
            pltpu.VMEM((2, nc, GROUP_ROWS, LANES), f32),
            pltpu.VMEM((2, 2, tb, LANES), f32),
            pltpu.VMEM((2, nc, 8, LANES), f32),
            pltpu.VMEM((2, 2, nc, GROUP_ROWS, LANES), bf16),
            pltpu.VMEM((2, 2, nc, GROUP_ROWS, LANES), f32),
            pltpu.VMEM((2, 2, tb, LANES), f32),
            pltpu.VMEM((2, nc, 8, LANES), f32),
        ],
        compiler_params=pltpu.CompilerParams(
            dimension_semantics=("arbitrary",),
            vmem_limit_bytes=VMEM_LIMIT),
        name="rwkv_mix",
    )(wide, wide, wide, wide, small, small, pc, ps, w2p, a2p, pc)


def _out_kernel(oa_ref, ob_ref, ga_ref, gb_ref, x_ref, wpf_ref, wpr_ref, wo_ref, g_ref, o_ref):
    pa = jnp.dot(oa_ref[...], wpf_ref[...], preferred_element_type=f32)
    pb = jnp.dot(ob_ref[...], wpr_ref[...], preferred_element_type=f32)
    m = _sigmoid(ga_ref[...]) * pa + _sigmoid(gb_ref[...]) * pb
    z = x_ref[...] + jnp.dot(m.astype(bf16), wo_ref[...], preferred_element_type=f32)
    ms = jnp.mean(z * z, axis=-1, keepdims=True)
    o_ref[...] = z * lax.rsqrt(ms + RMS_EPS) * g_ref[...]


def _merge_out(oa, ob, wide, x2d, wpf, wpr, wo, gain):
    m, d = x2d.shape
    tm = 256
    resident = lambda shape: pl.BlockSpec(shape, lambda i: (0, 0), pipeline_mode=pl.Buffered(1))
    return pl.pallas_call(
        _out_kernel,
        grid=(m // tm,),
        in_specs=[
            pl.BlockSpec((tm, FOX_WIDTH), lambda i: (i, 0)),
            pl.BlockSpec((tm, RWKV_WIDTH), lambda i: (i, 0)),
            pl.BlockSpec((tm, d), lambda i: (i, COL_GA * LANES // d)),
            pl.BlockSpec((tm, d), lambda i: (i, COL_GB * LANES // d)),
            pl.BlockSpec((tm, d), lambda i: (i, 0)),
            resident((FOX_WIDTH, d)),
            resident((RWKV_WIDTH, d)),
            resident((d, d)),
            resident((1, d)),
        ],
        out_specs=pl.BlockSpec((tm, d), lambda i: (i, 0)),
        out_shape=jax.ShapeDtypeStruct((m, d), f32),
        compiler_params=pltpu.CompilerParams(
            dimension_semantics=("parallel",),
            vmem_limit_bytes=VMEM_LIMIT),
        name="merge_out",
    )(oa, ob, wide, wide, x2d, wpf, wpr, wo, gain)


def _pad_cols(a, width):
    return jnp.pad(a, ((0, 0), (0, width - a.shape[1])))


def _layer(x2d, batch, seq, norm_gain, w_in, fox_forget_bias, rwkv_shift_mix, rwkv_w0, rwkv_w2,
           rwkv_a0, rwkv_a2, rwkv_k_k, rwkv_k_a, rwkv_r_k, rwkv_ln_w, rwkv_ln_b,
           w_proj_fox, w_proj_rwkv, w_out):
    fw, rw = FOX_WIDTH, RWKV_WIDTH
    w_t = w_in.T
    o = 0
    w_qkv = w_t[o:o + 3 * fw]; o += 3 * fw
    w_za = w_t[o:o + fw]; o += fw
    w_f = w_t[o:o + FOX_HEADS]; o += FOX_HEADS
    w_rkvz = w_t[o:o + 4 * rw]; o += 4 * rw
    w_wd = w_t[o:o + LORA]; o += LORA
    w_ad = w_t[o:o + LORA]; o += LORA
    w_g = w_t[o:o + 2 * D_MODEL]
    q_scale = jnp.where(jnp.arange(3 * fw) < fw, FOX_HEAD_DIM ** -0.5 * LOG2E, 1.0).astype(f32)
    pad_rows = lambda a: jnp.pad(a, ((0, LANES - a.shape[0]), (0, 0)))
    w_all = jnp.concatenate(
        [w_qkv * q_scale[:, None], w_g, w_za, w_rkvz, pad_rows(w_wd), pad_rows(w_ad), pad_rows(w_f)],
        axis=0).astype(bf16)
    gain = norm_gain.reshape(1, D_MODEL)

    qkv = _norm_matmul(x2d, gain, w_all, 0, 3 * fw, bf16, 1024, 512)
    wide = _norm_matmul(x2d, gain, w_all, 3 * fw, WIDE_COLS, f32, 1024, 512)
    small = _norm_matmul(x2d, gain, w_all, 3 * fw + WIDE_COLS, SMALL_COLS, f32, 1024, SMALL_COLS)

    bias_row = _pad_cols(fox_forget_bias.reshape(1, FOX_HEADS), LANES)
    c = _gate_cumsum(small, bias_row, batch, seq)
    qkv3 = qkv.reshape(batch, seq, 3 * fw)
    oa = _fox_attention(qkv3, c.reshape(batch, seq, LANES), wide.reshape(batch, seq, WIDE_COLS),
                        batch, seq).reshape(batch * seq, fw)

    mu = rwkv_shift_mix
    pc = jnp.zeros((P_ROWS, rw), f32)
    rows = [mu[0:rw], mu[rw:2 * rw], mu[2 * rw:3 * rw], mu[3 * rw:4 * rw], rwkv_w0, rwkv_a0,
            rwkv_k_k, rwkv_k_a, rwkv_r_k.reshape(rw), rwkv_ln_w, rwkv_ln_b]
    pc = pc.at[:len(rows)].set(jnp.stack(rows))
    ps = jnp.zeros((8, LANES), f32)
    ps = ps.at[0, :LORA].set(mu[4 * rw:4 * rw + LORA]).at[1, :LORA].set(mu[4 * rw + LORA:])
    w2p = jnp.pad(rwkv_w2, ((0, LANES - LORA), (0, 0))).astype(bf16)
    a2p = jnp.pad(rwkv_a2, ((0, LANES - LORA), (0, 0))).astype(bf16)
    ob = _rwkv_mix(wide, small, pc, ps, w2p, a2p, batch, seq)

    return oa, ob, wide


def kernel(x, norm_gain, w_in, fox_forget_bias, rwkv_shift_mix, rwkv_w0, rwkv_w2, rwkv_a0, rwkv_a2, rwkv_k_k, rwkv_k_a, rwkv_r_k, rwkv_ln_w, rwkv_ln_b, w_proj_fox, w_proj_rwkv, w_out, final_norm_gain):
    batch, seq, d = x.shape
    depth = norm_gain.shape[0]
    assert depth == 1, "the final rmsnorm is fused into the single layer's output kernel"
    x2d = x.reshape(batch * seq, d)
    oa, ob, wide = _layer(x2d, batch, seq, norm_gain[0], w_in[0], fox_forget_bias[0],
                          rwkv_shift_mix[0], rwkv_w0[0], rwkv_w2[0], rwkv_a0[0], rwkv_a2[0],
                          rwkv_k_k[0], rwkv_k_a[0], rwkv_r_k[0], rwkv_ln_w[0], rwkv_ln_b[0],
                          w_proj_fox[0], w_proj_rwkv[0], w_out[0])
    out = _merge_out(oa, ob, wide, x2d, w_proj_fox[0].astype(bf16), w_proj_rwkv[0].astype(bf16),
                     w_out[0].astype(bf16), final_norm_gain.reshape(1, d))
    return out.reshape(batch, seq, d)
```

```python
import functools

import jax
import jax.numpy as jnp
from jax import lax
from jax.experimental import pallas as pl
from jax.experimental.pallas import tpu as pltpu

D_MODEL = 2048
FOX_HEADS = 8
FOX_HEAD_DIM = 128
FOX_WIDTH = FOX_HEADS * FOX_HEAD_DIM
RWKV_HEADS = 16
RWKV_HEAD_DIM = 64
RWKV_WIDTH = RWKV_HEADS * RWKV_HEAD_DIM
LORA = 96
RMS_EPS = 1e-6
GN_EPS = 64e-5
L2_EPS = 1e-12

LANES = 128
VMEM_LIMIT = 56 * 1024 * 1024

COL_GA = 0
COL_GB = D_MODEL // LANES
COL_ZA = 2 * D_MODEL // LANES
COL_R = COL_ZA + FOX_WIDTH // LANES
COL_K = COL_R + RWKV_WIDTH // LANES
COL_V = COL_K + RWKV_WIDTH // LANES
COL_ZB = COL_V + RWKV_WIDTH // LANES
WIDE_COLS = (COL_ZB + RWKV_WIDTH // LANES) * LANES
SMALL_COLS = 3 * LANES

RWKV_CHUNK = 64
RWKV_BLOCK = 512
RWKV_TAIL_START = 0.3
RWKV_GROUP = 8
HEADS_PER_GROUP = LANES // RWKV_HEAD_DIM
GROUP_ROWS = HEADS_PER_GROUP * RWKV_CHUNK

f32 = jnp.float32
bf16 = jnp.bfloat16


def _dot(a, b):
    return jnp.dot(a.astype(bf16), b.astype(bf16), preferred_element_type=f32)


def _dot_nt(a, b):
    return lax.dot_general(a.astype(bf16), b.astype(bf16), (((1,), (1,)), ((), ())),
                           preferred_element_type=f32)


def _dot_tn(a, b):
    return lax.dot_general(a.astype(bf16), b.astype(bf16), (((0,), (0,)), ((), ())),
                           preferred_element_type=f32)


def _dot_exact_ones(a, ones_bf16, nt=False, terms=3):
    if nt:
        mm = lambda x: lax.dot_general(ones_bf16, x, (((1,), (0,)), ((), ())),
                                       preferred_element_type=f32)
    else:
        mm = lambda x: jnp.dot(x, ones_bf16, preferred_element_type=f32)
    part = a.astype(bf16)
    out = mm(part)
    rest = a
    for _ in range(terms - 1):
        rest = rest - part.astype(f32)
        part = rest.astype(bf16)
        out = out + mm(part)
    return out


def _softplus(x):
    return jnp.maximum(x, 0.0) + jnp.log(1.0 + jnp.exp(-jnp.abs(x)))


def _sigmoid(x):
    return 1.0 / (1.0 + jnp.exp(-x))


def _norm_matmul_kernel(x_ref, g_ref, w_ref, o_ref, h_ref):
    @pl.when(pl.program_id(1) == 0)
    def _():
        xf = x_ref[...]
        ms = jnp.mean(xf * xf, axis=-1, keepdims=True)
        h_ref[...] = (xf * lax.rsqrt(ms + RMS_EPS) * g_ref[...]).astype(bf16)

    o_ref[...] = lax.dot_general(h_ref[...], w_ref[...], (((1,), (1,)), ((), ())),
                                 preferred_element_type=f32).astype(o_ref.dtype)


def _norm_matmul(x2d, gain, w_t, col_start, n, out_dtype, tm, tn):
    m, d = x2d.shape
    first = col_start // tn
    assert first * tn == col_start and n % tn == 0
    return pl.pallas_call(
        _norm_matmul_kernel,
        grid=(m // tm, n // tn),
        in_specs=[
            pl.BlockSpec((tm, d), lambda i, j: (i, 0)),
            pl.BlockSpec((1, d), lambda i, j: (0, 0)),
            pl.BlockSpec((tn, d), lambda i, j: (first + j, 0)),
        ],
        out_specs=pl.BlockSpec((tm, tn), lambda i, j: (i, j)),
        out_shape=jax.ShapeDtypeStruct((m, n), out_dtype),
        scratch_shapes=[pltpu.VMEM((tm, d), bf16)],
        compiler_params=pltpu.CompilerParams(
            dimension_semantics=("parallel", "arbitrary"),
            vmem_limit_bytes=VMEM_LIMIT),
        name="norm_matmul",
    )(x2d, gain, w_t)


def _gate_kernel(f_ref, bias_ref, c_ref, carry_ref, *, tb):
    @pl.when(pl.program_id(1) == 0)
    def _():
        carry_ref[...] = jnp.zeros_like(carry_ref)

    z = f_ref[...] + bias_ref[...]
    log_f = -_softplus(-z)
    row = lax.broadcasted_iota(jnp.int32, (tb, tb), 0)
    col = lax.broadcasted_iota(jnp.int32, (tb, tb), 1)
    tri = jnp.where(col <= row, 1.0, 0.0).astype(bf16)
    c = _dot_exact_ones(log_f, tri, nt=True) + carry_ref[0:1, :]
    c_ref[...] = c
    carry_ref[0:1, :] = c[tb - 1:tb, :]


def _gate_cumsum(small, bias_row, batch, seq):
    tb = 256
    nt = seq // tb
    return pl.pallas_call(
        functools.partial(_gate_kernel, tb=tb),
        grid=(batch, nt),
        in_specs=[
            pl.BlockSpec((tb, LANES), lambda b, t: (b * nt + t, 2)),
            pl.BlockSpec((1, LANES), lambda b, t: (0, 0)),
        ],
        out_specs=pl.BlockSpec((tb, LANES), lambda b, t: (b * nt + t, 0)),
        out_shape=jax.ShapeDtypeStruct((batch * seq, LANES), f32),
        scratch_shapes=[pltpu.VMEM((8, LANES), f32)],
        compiler_params=pltpu.CompilerParams(
            dimension_semantics=("parallel", "arbitrary")),
        name="fox_gate_cumsum",
    )(small, bias_row)


LOG2E = 1.4426950408889634
FOX_TK = 256
FOX_BATCH = 4


def _fox_kernel(q_ref, k_ref, v_ref, c_ref, z_ref, o_ref, crep_ref, acc_ref, vt_ref, *, seq, tk):
    h = pl.program_id(1)
    ng = seq // LANES
    nt = seq // tk
    gpt = tk // LANES
    mm = lambda a, b: jnp.dot(a, b, preferred_element_type=f32)
    mm_nt = lambda a, b: lax.dot_general(a, b, (((1,), (1,)), ((), ())), preferred_element_type=f32)

    src_lane = lax.broadcasted_iota(jnp.int32, (LANES, LANES), 0)
    pick = jnp.where(src_lane == h, 1.0, 0.0).astype(bf16)
    crep_ref[...] = _dot_exact_ones(c_ref[...], pick) * LOG2E
    for t in range(nt):
        vt_ref[t] = v_ref[t * tk:(t + 1) * tk, :].T

    key_rel = lax.broadcasted_iota(jnp.int32, (tk, LANES), 0)
    qry_rel = lax.broadcasted_iota(jnp.int32, (tk, LANES), 1)
    future = [key_rel > qry_rel + d * LANES for d in range(gpt)]

    sched = []
    for t in range(nt):
        visible = list(range(t * gpt, ng))
        sched += [[(t, g) for g in visible[k:k + FOX_BATCH]] for k in range(0, len(visible), FOX_BATCH)]

    def qk(batch):
        return [mm_nt(k_ref[t * tk:(t + 1) * tk, :], q_ref[g * LANES:(g + 1) * LANES, :])
                for t, g in batch]

    m = [None] * ng
    l = [None] * ng

    def finish(batch, alpha, pv):
        for (t, g), a, x in zip(batch, alpha, pv):
            acc_ref[g] = x if t == 0 else a * acc_ref[g] + x
            if t == g // gpt:
                rows = slice(g * LANES, (g + 1) * LANES)
                z = z_ref[rows, :]
                o = (acc_ref[g] / l[g]).T
                o_ref[rows, :] = (o * (z * _sigmoid(z))).astype(o_ref.dtype)

    s_next = qk(sched[0])
    pending = None
    for bi, batch in enumerate(sched):
        s_cur = s_next
        if bi + 1 < len(sched):
            s_next = qk(sched[bi + 1])
        alpha, p16 = [], []
        for (t, g), s in zip(batch, s_cur):
            s = s - crep_ref[t * tk:(t + 1) * tk, :]
            if g < (t + 1) * gpt:
                s = jnp.where(future[g - t * gpt], -jnp.inf, s)
            m_tile = jnp.max(s, axis=0, keepdims=True)
            if t == 0:
                m_new, a = m_tile, None
            else:
                m_new = jnp.maximum(m[g], m_tile)
                a = jnp.exp2(m[g] - m_new)
            p = jnp.exp2(s - m_new)
            p_sum = jnp.sum(p, axis=0, keepdims=True)
            l[g] = p_sum if t == 0 else a * l[g] + p_sum
            m[g] = m_new
            alpha.append(a)
            p16.append(p.astype(bf16))
        pv = [mm(vt_ref[t], p) for (t, g), p in zip(batch, p16)]
        if pending is not None:
            finish(*pending)
        pending = (batch, alpha, pv)
    finish(*pending)


def _fox_attention(qkv3, c3, wide3, batch, seq):
    tk = FOX_TK
    h8 = FOX_HEADS
    return pl.pallas_call(
        functools.partial(_fox_kernel, seq=seq, tk=tk),
        grid=(batch, h8),
        in_specs=[
            pl.BlockSpec((None, seq, LANES), lambda b, h: (b, 0, h)),
            pl.BlockSpec((None, seq, LANES), lambda b, h: (b, 0, h8 + h)),
            pl.BlockSpec((None, seq, LANES), lambda b, h: (b, 0, 2 * h8 + h)),
            pl.BlockSpec((None, seq, LANES), lambda b, h: (b, 0, 0)),
            pl.BlockSpec((None, seq, LANES), lambda b, h: (b, 0, COL_ZA + h)),
        ],
        out_specs=pl.BlockSpec((None, seq, LANES), lambda b, h: (b, 0, h)),
        out_shape=jax.ShapeDtypeStruct((batch, seq, FOX_WIDTH), bf16),
        scratch_shapes=[
            pltpu.VMEM((seq, LANES), f32),
            pltpu.VMEM((seq // LANES, LANES, LANES), f32),
            pltpu.VMEM((seq // tk, LANES, tk), bf16),
        ],
        compiler_params=pltpu.CompilerParams(
            dimension_semantics=("parallel", "arbitrary"),
            vmem_limit_bytes=VMEM_LIMIT),
        name="fox_attention",
    )(qkv3, qkv3, qkv3, c3, wide3)


P_MU_R, P_MU_K, P_MU_V, P_MU_Z, P_W0, P_A0, P_KK, P_KA, P_RK, P_LNW, P_LNB = range(11)
P_ROWS = 16


def _rwkv_kernel(r_ref, k_ref, v_ref, z_ref, wd_ref, ad_ref, pc_ref, ps_ref, w2_ref, a2_ref,
                 o_ref, state_ref, prev_ref, *, tb):
    C = RWKV_CHUNK
    G = GROUP_ROWS
    N = RWKV_HEAD_DIM

    @pl.when(pl.program_id(2) == 0)
    def _():
        state_ref[...] = jnp.zeros_like(state_ref)
        prev_ref[...] = jnp.zeros_like(prev_ref)

    row_in_block = lax.broadcasted_iota(jnp.int32, (tb, LANES), 0)

    def shifted(ref, slot, mu):
        u = ref[...]
        prev = jnp.where(row_in_block == 0, prev_ref[slot, 0:1, :], pltpu.roll(u, 1, 0))
        prev_ref[slot, 0:1, :] = u[tb - 1:tb, :]
        return u + (prev - u) * mu

    prm = lambda idx: pc_ref[idx:idx + 1, :]
    r = shifted(r_ref, 0, prm(P_MU_R))
    kr = shifted(k_ref, 1, prm(P_MU_K))
    vr = shifted(v_ref, 2, prm(P_MU_V))
    zb = shifted(z_ref, 3, prm(P_MU_Z))
    wd = shifted(wd_ref, 4, ps_ref[0:1, :])
    ad = shifted(ad_ref, 5, ps_ref[1:2, :])

    w = -_softplus(-(prm(P_W0) + _dot(jnp.tanh(wd), w2_ref[...]))) - 0.5
    log_decay = -jnp.exp(w)
    rate = _sigmoid(prm(P_A0) + _dot(ad, a2_ref[...]))

    li = lax.broadcasted_iota(jnp.int32, (LANES, LANES), 0)
    lj = lax.broadcasted_iota(jnp.int32, (LANES, LANES), 1)
    head_ones = jnp.where(li // N == lj // N, 1.0, 0.0).astype(bf16)
    head_sum = lambda x: _dot_exact_ones(x, head_ones)

    kk = kr * prm(P_KK)
    kk = kk / jnp.maximum(jnp.sqrt(head_sum(kk * kk)), L2_EPS)
    kp = kr * (1.0 + (rate - 1.0) * prm(P_KA))
    bb = kk * rate
    bonus = head_sum(r * kp * prm(P_RK)) * vr

    ti = lax.broadcasted_iota(jnp.int32, (tb, tb), 0)
    tj = lax.broadcasted_iota(jnp.int32, (tb, tb), 1)
    chunk_tri = jnp.where(jnp.logical_and(tj <= ti, ti // C == tj // C), 1.0, 0.0).astype(bf16)
    ci = _dot_exact_ones(log_decay, chunk_tri, nt=True)

    a_t = -kk * jnp.exp(ci - log_decay)
    r_t = r * jnp.exp(ci)
    inv = jnp.exp(-ci)
    b_t = bb * inv
    k_t = kp * inv

    lane = lax.broadcasted_iota(jnp.int32, (1, LANES), 1)
    head_masks = [jnp.where(lane // N == h, 1.0, 0.0) for h in range(HEADS_PER_GROUP)]
    stack = lambda x: jnp.concatenate([x * hm for hm in head_masks], axis=0)
    stack16 = lambda x: stack(x).astype(bf16)

    gi = lax.broadcasted_iota(jnp.int32, (G, G), 0)
    gj = lax.broadcasted_iota(jnp.int32, (G, G), 1)
    same_head = gi // C == gj // C
    strict = jnp.logical_and(same_head, gj < gi)
    incl = jnp.logical_and(same_head, gj <= gi)
    eye = jnp.where(gi == gj, 1.0, 0.0)

    mm = lambda a, b: jnp.dot(a, b, preferred_element_type=f32)
    mm_nt = lambda a, b: lax.dot_general(a, b, (((1,), (1,)), ((), ())), preferred_element_type=f32)
    mm_tn = lambda a, b: lax.dot_general(a, b, (((0,), (0,)), ((), ())), preferred_element_type=f32)

    def independent(chunks):
        n = range(len(chunks))
        rows = [slice(c * C, (c + 1) * C) for c in chunks]
        c_last = [ci[sl][C - 1:C, :] for sl in rows]
        to_end = [jnp.exp(c_last[i] - ci[rows[i]]) for i in n]
        xr32 = [stack(r_t[sl]) for sl in rows]
        xa = [stack16(a_t[sl]) for sl in rows]
        xr = [x.astype(bf16) for x in xr32]
        yb = [stack16(b_t[sl]) for sl in rows]
        yk = [stack16(k_t[sl]) for sl in rows]
        vs = [stack16(vr[sl]) for sl in rows]
        bh = [stack16(bb[rows[i]] * to_end[i]) for i in n]
        kh = [stack16(kp[rows[i]] * to_end[i]) for i in n]

        big = [mm_nt(jnp.concatenate([xa[i], xr[i]], axis=0), jnp.concatenate([yb[i], yk[i]], axis=0))
               for i in n]
        a_ab = [jnp.where(strict, big[i][0:G, 0:G], 0.0) for i in n]
        a_ak = [jnp.where(strict, big[i][0:G, G:2 * G], 0.0).astype(bf16) for i in n]
        a_rb = [jnp.where(incl, big[i][G:2 * G, 0:G], 0.0).astype(bf16) for i in n]
        a_rk = [jnp.where(incl, big[i][G:2 * G, G:2 * G], 0.0).astype(bf16) for i in n]

        tinv = [eye + a for a in a_ab]
        pw = [a.astype(bf16) for a in a_ab]
        for _ in range(5):
            pw = [mm(p, p).astype(bf16) for p in pw]
            tinv = [tinv[i] + mm(tinv[i].astype(bf16), pw[i]) for i in n]
        tinv = [t.astype(bf16) for t in tinv]

        akv = [mm(a_ak[i], vs[i]).astype(bf16) for i in n]
        wu = [mm(tinv[i], jnp.concatenate([xa[i], akv[i]], axis=1)).astype(bf16) for i in n]
        e = [mm(a_rb[i], wu[i]) for i in n]
        ry = [(xr32[i] + e[i][:, 0:LANES]).astype(bf16) for i in n]
        y_loc = [e[i][:, LANES:] + mm(a_rk[i], vs[i]) for i in n]
        pm = [mm_tn(wu[i][:, 0:LANES], bh[i]).astype(bf16) for i in n]
        q = [mm_tn(jnp.concatenate([wu[i][:, LANES:], vs[i]], axis=0),
                   jnp.concatenate([bh[i], kh[i]], axis=0)) for i in n]
        return [dict(ry=ry[i], y_loc=y_loc[i], pm=pm[i], q=q[i], decay=jnp.exp(c_last[i])) for i in n]

    def sequential(state, parts):
        out = []
        for p in parts:
            s16 = state.astype(bf16)
            y_st = mm_nt(p["ry"], s16) + p["y_loc"]
            out.append(y_st[0:C] + y_st[C:2 * C])
            state = state * p["decay"] + mm(s16, p["pm"]) + p["q"]
        return state, out

    n_chunks = tb // C
    groups = [list(range(g, min(g + RWKV_GROUP, n_chunks))) for g in range(0, n_chunks, RWKV_GROUP)]
    state = state_ref[...]
    ys = []
    parts = independent(groups[0])
    for g in groups[1:]:
        state, out = sequential(state, parts)
        ys += out
        parts = independent(g)
    state, out = sequential(state, parts)
    ys += out
    state_ref[...] = state

    y = jnp.concatenate(ys, axis=0)
    mean = head_sum(y) * (1.0 / N)
    yc = y - mean
    var = head_sum(yc * yc) * (1.0 / N)
    yn = yc * lax.rsqrt(var + GN_EPS) * prm(P_LNW) + prm(P_LNB)
    o_ref[...] = ((yn + bonus) * (zb * _sigmoid(zb))).astype(o_ref.dtype)


def _rwkv_mix_serial(wide, small, pc, ps, w2p, a2p, batch, seq):
    tb = RWKV_BLOCK
    nt = seq // tb
    groups = RWKV_WIDTH // LANES
    row = lambda b, g, t: b * nt + t
    return pl.pallas_call(
        functools.partial(_rwkv_kernel, tb=tb),
        grid=(batch, groups, nt),
        in_specs=[
            pl.BlockSpec((tb, LANES), lambda b, g, t: (row(b, g, t), COL_R + g)),
            pl.BlockSpec((tb, LANES), lambda b, g, t: (row(b, g, t), COL_K + g)),
            pl.BlockSpec((tb, LANES), lambda b, g, t: (row(b, g, t), COL_V + g)),
            pl.BlockSpec((tb, LANES), lambda b, g, t: (row(b, g, t), COL_ZB + g)),
            pl.BlockSpec((tb, LANES), lambda b, g, t: (row(b, g, t), 0)),
            pl.BlockSpec((tb, LANES), lambda b, g, t: (row(b, g, t), 1)),
            pl.BlockSpec((P_ROWS, LANES), lambda b, g, t: (0, g)),
            pl.BlockSpec((8, LANES), lambda b, g, t: (0, 0)),
            pl.BlockSpec((LANES, LANES), lambda b, g, t: (0, g)),
            pl.BlockSpec((LANES, LANES), lambda b, g, t: (0, g)),
        ],
        out_specs=pl.BlockSpec((tb, LANES), lambda b, g, t: (row(b, g, t), g)),
        out_shape=jax.ShapeDtypeStruct((batch * seq, RWKV_WIDTH), bf16),
        scratch_shapes=[
            pltpu.VMEM((LANES, LANES), f32),
            pltpu.VMEM((6, 8, LANES), f32),
        ],
        compiler_params=pltpu.CompilerParams(
            dimension_semantics=("parallel", "parallel", "arbitrary"),
            vmem_limit_bytes=VMEM_LIMIT),
        name="rwkv_mix",
    )(wide, wide, wide, wide, small, small, pc, ps, w2p, a2p)


K_XA, K_XR, K_YB, K_YK, K_VS, K_BH, K_KH = range(7)


def _spread(emitters, lo=0.0, hi=1.0):
    n = len(emitters)
    return [(lo + (hi - lo) * (i + 0.5) / n, e) for i, e in enumerate(emitters)]


def _interleave(*segment_lists):
    keyed = [(pos, prio, seg) for prio, segs in enumerate(segment_lists) for pos, seg in segs]
    keyed.sort(key=lambda x: (x[0], x[1]))
    for _, _, seg in keyed:
        seg()


def _rwkv_pipe_kernel(r_ref, k_ref, v_ref, z_ref, wd_ref, ad_ref, pc_ref, ps_ref, w2_ref, a2_ref, pc3_ref,
                      o_ref,
                      state_ref, prev_ref, stk_ref, xr32_ref, aux1_ref, dec1_ref, p16_ref, p32_ref,
                      aux2_ref, dec2_ref, *, tb, nt):
    C = RWKV_CHUNK
    G = GROUP_ROWS
    N = RWKV_HEAD_DIM
    NC = tb // C
    s = pl.program_id(0)
    first_of_seq_1 = (s % nt) == 0
    first_of_seq_3 = ((s + 2 * nt - 2) % nt) == 0

    @pl.when(s == 0)
    def _():
        for ref in (state_ref, prev_ref, stk_ref, xr32_ref, aux1_ref, dec1_ref, p16_ref, p32_ref,
                    aux2_ref, dec2_ref):
            ref[...] = jnp.zeros_like(ref)

    mm = lambda a, b: jnp.dot(a, b, preferred_element_type=f32)
    mm_nt = lambda a, b: lax.dot_general(a, b, (((1,), (1,)), ((), ())), preferred_element_type=f32)
    mm_tn = lambda a, b: lax.dot_general(a, b, (((0,), (0,)), ((), ())), preferred_element_type=f32)

    def body(cur, prv):
        li = lax.broadcasted_iota(jnp.int32, (LANES, LANES), 0)
        lj = lax.broadcasted_iota(jnp.int32, (LANES, LANES), 1)
        head_ones = jnp.where(li // N == lj // N, 1.0, 0.0).astype(bf16)
        head_sum = lambda x: _dot_exact_ones(x, head_ones, terms=2)
        lane = lax.broadcasted_iota(jnp.int32, (1, LANES), 1)
        head_masks = [jnp.where(lane // N == h, 1.0, 0.0) for h in range(HEADS_PER_GROUP)]
        stack = lambda x: jnp.concatenate([x * hm for hm in head_masks], axis=0)
        lane_c = lax.broadcasted_iota(jnp.int32, (C, LANES), 1)
        head_sel = [lane_c // N == h for h in range(HEADS_PER_GROUP)]

        def stack16(x):
            xb = x.astype(bf16)
            return jnp.concatenate([jnp.where(m, xb, jnp.zeros_like(xb)) for m in head_sel], axis=0)

        gi = lax.broadcasted_iota(jnp.int32, (G, G), 0)
        gj = lax.broadcasted_iota(jnp.int32, (G, G), 1)
        same_head = gi // C == gj // C
        strict = jnp.logical_and(same_head, gj < gi)
        incl = jnp.logical_and(same_head, gj <= gi)
        eye = jnp.where(gi == gj, 1.0, 0.0)
        chunks = range(NC)

        v1 = {}
        prm = lambda idx: pc_ref[idx:idx + 1, :]
        row_in_block = lax.broadcasted_iota(jnp.int32, (tb, LANES), 0)

        def shifted(ref, slot, mu):
            u = ref[...]
            carry = jnp.where(first_of_seq_1, 0.0, prev_ref[slot, 0:1, :])
            prev = jnp.where(row_in_block == 0, carry, pltpu.roll(u, 1, 0))
            prev_ref[slot, 0:1, :] = u[tb - 1:tb, :]
            return u + (prev - u) * mu

        def s1_lora():
            v1["wd"] = shifted(wd_ref, 4, ps_ref[0:1, :])
            v1["ad"] = shifted(ad_ref, 5, ps_ref[1:2, :])
            v1["w_lin"] = _dot(jnp.tanh(v1["wd"]), w2_ref[...])
            v1["a_lin"] = _dot(v1["ad"], a2_ref[...])

        def s1_key_norm():
            v1["kr"] = shifted(k_ref, 1, prm(P_MU_K))
            kk = v1["kr"] * prm(P_KK)
            v1["kk_raw"] = kk
            v1["kk_ss"] = head_sum(kk * kk)

        def s1_decay():
            w = -_softplus(-(prm(P_W0) + v1["w_lin"])) - 0.5
            v1["log_decay"] = -jnp.exp(w)
            ti = lax.broadcasted_iota(jnp.int32, (2 * C, 2 * C), 0)
            tj = lax.broadcasted_iota(jnp.int32, (2 * C, 2 * C), 1)
            pair_tri = jnp.where(jnp.logical_and(tj <= ti, ti // C == tj // C), 1.0, 0.0).astype(bf16)
            v1["ci"] = jnp.concatenate(
                [_dot_exact_ones(v1["log_decay"][k:k + 2 * C], pair_tri, nt=True)
                 for k in range(0, tb, 2 * C)], axis=0)

        def s1_bonus():
            v1["rate"] = _sigmoid(prm(P_A0) + v1["a_lin"])
            v1["r"] = shifted(r_ref, 0, prm(P_MU_R))
            v1["vr"] = shifted(v_ref, 2, prm(P_MU_V))
            v1["kp"] = v1["kr"] * (1.0 + (v1["rate"] - 1.0) * prm(P_KA))
            aux1_ref[cur, 0] = head_sum(v1["r"] * v1["kp"] * prm(P_RK)) * v1["vr"]
            zb = shifted(z_ref, 3, prm(P_MU_Z))
            aux1_ref[cur, 1] = zb * _sigmoid(zb)

        def s1_scale():
            kk = v1["kk_raw"] / jnp.maximum(jnp.sqrt(v1["kk_ss"]), L2_EPS)
            v1["bb"] = kk * v1["rate"]
            ci = v1["ci"]
            v1["a_t"] = -kk * jnp.exp(ci - v1["log_decay"])
            v1["r_t"] = v1["r"] * jnp.exp(ci)
            inv = jnp.exp(-ci)
            v1["b_t"] = v1["bb"] * inv
            v1["k_t"] = v1["kp"] * inv

        def s1_stack(c):
            def emit():
                sl = slice(c * C, (c + 1) * C)
                ci_c = v1["ci"][sl]
                c_last = ci_c[C - 1:C, :]
                to_end = jnp.exp(c_last - ci_c)
                dec1_ref[cur, c] = jnp.broadcast_to(jnp.exp(c_last), (8, LANES))
                xr = stack(v1["r_t"][sl])
                xr32_ref[cur, c] = xr
                stk_ref[cur, K_XR, c] = xr.astype(bf16)
                stk_ref[cur, K_XA, c] = stack16(v1["a_t"][sl])
                stk_ref[cur, K_YB, c] = stack16(v1["b_t"][sl])
                stk_ref[cur, K_YK, c] = stack16(v1["k_t"][sl])
                stk_ref[cur, K_VS, c] = stack16(v1["vr"][sl])
                stk_ref[cur, K_BH, c] = stack16(v1["bb"][sl] * to_end)
                stk_ref[cur, K_KH, c] = stack16(v1["kp"][sl] * to_end)
            return emit

        segs1 = ([(0.0, s1_lora), (0.04, s1_key_norm), (0.10, s1_decay), (0.16, s1_bonus)]
                 + _spread([s1_scale] + [s1_stack(c) for c in chunks], RWKV_TAIL_START, 1.0))

        v2 = {}
        ld = lambda kind, c: stk_ref[prv, kind, c]

        def s2_big():
            big = [mm_nt(jnp.concatenate([ld(K_XA, c), ld(K_XR, c)], axis=0),
                         jnp.concatenate([ld(K_YB, c), ld(K_YK, c)], axis=0)) for c in chunks]
            v2["a_ab"] = [jnp.where(strict, big[c][0:G, 0:G], 0.0) for c in chunks]
            v2["a_ak"] = [jnp.where(strict, big[c][0:G, G:2 * G], 0.0).astype(bf16) for c in chunks]
            v2["a_rb"] = [jnp.where(incl, big[c][G:2 * G, 0:G], 0.0).astype(bf16) for c in chunks]
            v2["a_rk"] = [jnp.where(incl, big[c][G:2 * G, G:2 * G], 0.0).astype(bf16) for c in chunks]
            v2["tinv"] = [eye + a for a in v2["a_ab"]]
            v2["pw"] = [a.astype(bf16) for a in v2["a_ab"]]

        def s2_first_square():
            v2["pw"] = [mm(p, p).astype(bf16) for p in v2["pw"]]

        def s2_level(last):
            def emit():
                for c in chunks:
                    pw = v2["pw"][c]
                    t16 = v2["tinv"][c].astype(bf16)
                    if last:
                        v2["tinv"][c] = v2["tinv"][c] + mm(t16, pw)
                    else:
                        both = mm(jnp.concatenate([t16, pw], axis=0), pw)
                        v2["tinv"][c] = v2["tinv"][c] + both[0:G]
                        v2["pw"][c] = both[G:2 * G].astype(bf16)
            return emit

        def s2_av():
            v2["tinv"] = [t.astype(bf16) for t in v2["tinv"]]
            av = [mm(jnp.concatenate([v2["a_ak"][c], v2["a_rk"][c]], axis=0), ld(K_VS, c)) for c in chunks]
            v2["akv"] = [x[0:G].astype(bf16) for x in av]
            v2["arkv"] = [x[G:2 * G] for x in av]

        def s2_wu():
            v2["wu"] = [mm(v2["tinv"][c], jnp.concatenate([ld(K_XA, c), v2["akv"][c]], axis=1)).astype(bf16)
                        for c in chunks]

        def s2_ry():
            for c in chunks:
                e = mm(v2["a_rb"][c], v2["wu"][c])
                p16_ref[cur, 0, c] = (xr32_ref[prv, c] + e[:, 0:LANES]).astype(bf16)
                p32_ref[cur, 0, c] = e[:, LANES:] + v2["arkv"][c]

        def s2_pm():
            for c in chunks:
                p16_ref[cur, 1, c] = mm_tn(v2["wu"][c][:, 0:LANES], ld(K_BH, c)).astype(bf16)

        def s2_q():
            for c in chunks:
                p32_ref[cur, 1, c] = mm_tn(jnp.concatenate([v2["wu"][c][:, LANES:], ld(K_VS, c)], axis=0),
                                           jnp.concatenate([ld(K_BH, c), ld(K_KH, c)], axis=0))
            aux2_ref[cur] = aux1_ref[prv]
            dec2_ref[cur] = dec1_ref[prv]

        segs2 = _spread([s2_big, s2_first_square] + [s2_level(False)] * 4 + [s2_level(True)]
                        + [s2_av, s2_wu, s2_ry, s2_pm, s2_q])

        v3 = {"ys": []}
        prm3 = lambda idx: pc3_ref[idx:idx + 1, :]

        def s3_start():
            v3["state"] = jnp.where(first_of_seq_3, 0.0, state_ref[...])

        def s3_chunk(c):
            def emit():
                state = v3["state"]
                s16 = state.astype(bf16)
                y_st = mm_nt(p16_ref[prv, 0, c], s16) + p32_ref[prv, 0, c]
                v3["ys"].append(y_st[0:C] + y_st[C:2 * C])
                v3["state"] = (state * dec2_ref[prv, c, 0:1, :] + mm(s16, p16_ref[prv, 1, c])
                               + p32_ref[prv, 1, c])
            return emit

        def s3_finish():
            state_ref[...] = v3["state"]
            y = jnp.concatenate(v3["ys"], axis=0)
            mean = head_sum(y) * (1.0 / N)
            yc = y - mean
            var = head_sum(yc * yc) * (1.0 / N)
            yn = yc * lax.rsqrt(var + GN_EPS) * prm3(P_LNW) + prm3(P_LNB)
            o_ref[...] = ((yn + aux2_ref[prv, 0]) * aux2_ref[prv, 1]).astype(o_ref.dtype)

        segs3 = _spread([s3_start] + [s3_chunk(c) for c in chunks] + [s3_finish])

        _interleave(segs3, segs2, segs1)

    for parity in (0, 1):
        @pl.when(s % 2 == parity)
        def _(parity=parity):
            body(cur=parity, prv=1 - parity)


def _rwkv_mix(wide, small, pc, ps, w2p, a2p, batch, seq):
    tb = RWKV_BLOCK
    nt = seq // tb
    groups = RWKV_WIDTH // LANES
    nc = tb // RWKV_CHUNK
    n_blocks = batch * groups * nt

    def where(s):
        s = jnp.clip(s, 0, n_blocks - 1)
        return (s // (nt * groups)) * nt + s % nt, (s // nt) % groups

    col = lambda c0: (lambda s: (where(s)[0], c0 + where(s)[1]))
    fixed_col = lambda c: (lambda s: (where(s)[0], c))
    grp = lambda s: (0, where(s)[1])
    return pl.pallas_call(
        functools.partial(_rwkv_pipe_kernel, tb=tb, nt=nt),
        grid=(n_blocks + 2,),
        in_specs=[
            pl.BlockSpec((tb, LANES), col(COL_R)),
            pl.BlockSpec((tb, LANES), col(COL_K)),
            pl.BlockSpec((tb, LANES), col(COL_V)),
            pl.BlockSpec((tb, LANES), col(COL_ZB)),
            pl.BlockSpec((tb, LANES), fixed_col(0)),
            pl.BlockSpec((tb, LANES), fixed_col(1)),
            pl.BlockSpec((P_ROWS, LANES), grp),
            pl.BlockSpec((8, LANES), lambda s: (0, 0)),
            pl.BlockSpec((LANES, LANES), grp),
            pl.BlockSpec((LANES, LANES), grp),
            pl.BlockSpec((P_ROWS, LANES), lambda s: grp(s - 2)),
        ],
        out_specs=pl.BlockSpec((tb, LANES), lambda s: (where(s - 2)[0], where(s - 2)[1])),
        out_shape=jax.ShapeDtypeStruct((batch * seq, RWKV_WIDTH), bf16),
        scratch_shapes=[
            pltpu.VMEM((LANES, LANES), f32),
            pltpu.VMEM((6, 8, LANES), f32),
            pltpu.VMEM((2, 7, nc, GROUP_ROWS, LANES), bf16),
            pltpu.VMEM((2, nc, GROUP_ROWS, LANES), f32),
            pltpu.VMEM((2, 2, tb, LANES), f32),
            pltpu.VMEM((2, nc, 8, LANES), f32),
            pltpu.VMEM((2, 2, nc, GROUP_ROWS, LANES), bf16),
            pltpu.VMEM((2, 2, nc, GROUP_ROWS, LANES), f32),
            pltpu.VMEM((2, 2, tb, LANES), f32),
            pltpu.VMEM((2, nc, 8, LANES), f32),
        ],
        compiler_params=pltpu.CompilerParams(
            dimension_semantics=("arbitrary",),
            vmem_limit_bytes=VMEM_LIMIT),
        name="rwkv_mix",
    )(wide, wide, wide, wide, small, small, pc, ps, w2p, a2p, pc)


def _out_kernel(oa_ref, ob_ref, ga_ref, gb_ref, x_ref, wpf_ref, wpr_ref, wo_ref, g_ref, o_ref):
    pa = jnp.dot(oa_ref[...], wpf_ref[...], preferred_element_type=f32)
    pb = jnp.dot(ob_ref[...], wpr_ref[...], preferred_element_type=f32)
    m = _sigmoid(ga_ref[...]) * pa + _sigmoid(gb_ref[...]) * pb
    z = x_ref[...] + jnp.dot(m.astype(bf16), wo_ref[...], preferred_element_type=f32)
    ms = jnp.mean(z * z, axis=-1, keepdims=True)
    o_ref[...] = z * lax.rsqrt(ms + RMS_EPS) * g_ref[...]


def _merge_out(oa, ob, wide, x2d, wpf, wpr, wo, gain):
    m, d = x2d.shape
    tm = 256
    resident = lambda shape: pl.BlockSpec(shape, lambda i: (0, 0), pipeline_mode=pl.Buffered(1))
    return pl.pallas_call(
        _out_kernel,
        grid=(m // tm,),
        in_specs=[
            pl.BlockSpec((tm, FOX_WIDTH), lambda i: (i, 0)),
            pl.BlockSpec((tm, RWKV_WIDTH), lambda i: (i, 0)),
            pl.BlockSpec((tm, d), lambda i: (i, COL_GA * LANES // d)),
            pl.BlockSpec((tm, d), lambda i: (i, COL_GB * LANES // d)),
            pl.BlockSpec((tm, d), lambda i: (i, 0)),
            resident((FOX_WIDTH, d)),
            resident((RWKV_WIDTH, d)),
            resident((d, d)),
            resident((1, d)),
        ],
        out_specs=pl.BlockSpec((tm, d), lambda i: (i, 0)),
        out_shape=jax.ShapeDtypeStruct((m, d), f32),
        compiler_params=pltpu.CompilerParams(
            dimension_semantics=("parallel",),
            vmem_limit_bytes=VMEM_LIMIT),
        name="merge_out",
    )(oa, ob, wide, wide, x2d, wpf, wpr, wo, gain)


def _pad_cols(a, width):
    return jnp.pad(a, ((0, 0), (0, width - a.shape[1])))


def _layer(x2d, batch, seq, norm_gain, w_in, fox_forget_bias, rwkv_shift_mix, rwkv_w0, rwkv_w2,
           rwkv_a0, rwkv_a2, rwkv_k_k, rwkv_k_a, rwkv_r_k, rwkv_ln_w, rwkv_ln_b,
           w_proj_fox, w_proj_rwkv, w_out):
    fw, rw = FOX_WIDTH, RWKV_WIDTH
    w_t = w_in.T
    o = 0
    w_qkv = w_t[o:o + 3 * fw]; o += 3 * fw
    w_za = w_t[o:o + fw]; o += fw
    w_f = w_t[o:o + FOX_HEADS]; o += FOX_HEADS
    w_rkvz = w_t[o:o + 4 * rw]; o += 4 * rw
    w_wd = w_t[o:o + LORA]; o += LORA
    w_ad = w_t[o:o + LORA]; o += LORA
    w_g = w_t[o:o + 2 * D_MODEL]
    q_scale = jnp.where(jnp.arange(3 * fw) < fw, FOX_HEAD_DIM ** -0.5 * LOG2E, 1.0).astype(f32)
    pad_rows = lambda a: jnp.pad(a, ((0, LANES - a.shape[0]), (0, 0)))
    w_all = jnp.concatenate(
        [w_qkv * q_scale[:, None], w_g, w_za, w_rkvz, pad_rows(w_wd), pad_rows(w_ad), pad_rows(w_f)],
        axis=0).astype(bf16)
    gain = norm_gain.reshape(1, D_MODEL)

    qkv = _norm_matmul(x2d, gain, w_all, 0, 3 * fw, bf16, 1024, 512)
    wide = _norm_matmul(x2d, gain, w_all, 3 * fw, WIDE_COLS, f32, 1024, 512)
    small = _norm_matmul(x2d, gain, w_all, 3 * fw + WIDE_COLS, SMALL_COLS, f32, 1024, SMALL_COLS)

    bias_row = _pad_cols(fox_forget_bias.reshape(1, FOX_HEADS), LANES)
    c = _gate_cumsum(small, bias_row, batch, seq)
    qkv3 = qkv.reshape(batch, seq, 3 * fw)
    oa = _fox_attention(qkv3, c.reshape(batch, seq, LANES), wide.reshape(batch, seq, WIDE_COLS),
                        batch, seq).reshape(batch * seq, fw)

    mu = rwkv_shift_mix
    pc = jnp.zeros((P_ROWS, rw), f32)
    rows = [mu[0:rw], mu[rw:2 * rw], mu[2 * rw:3 * rw], mu[3 * rw:4 * rw], rwkv_w0, rwkv_a0,
            rwkv_k_k, rwkv_k_a, rwkv_r_k.reshape(rw), rwkv_ln_w, rwkv_ln_b]
    pc = pc.at[:len(rows)].set(jnp.stack(rows))
    ps = jnp.zeros((8, LANES), f32)
    ps = ps.at[0, :LORA].set(mu[4 * rw:4 * rw + LORA]).at[1, :LORA].set(mu[4 * rw + LORA:])
    w2p = jnp.pad(rwkv_w2, ((0, LANES - LORA), (0, 0))).astype(bf16)
    a2p = jnp.pad(rwkv_a2, ((0, LANES - LORA), (0, 0))).astype(bf16)
    ob = _rwkv_mix(wide, small, pc, ps, w2p, a2p, batch, seq)

    return oa, ob, wide


def kernel(x, norm_gain, w_in, fox_forget_bias, rwkv_shift_mix, rwkv_w0, rwkv_w2, rwkv_a0, rwkv_a2, rwkv_k_k, rwkv_k_a, rwkv_r_k, rwkv_ln_w, rwkv_ln_b, w_proj_fox, w_proj_rwkv, w_out, final_norm_gain):
    batch, seq, d = x.shape
    depth = norm_gain.shape[0]
    assert depth == 1, "the final rmsnorm is fused into the single layer's output kernel"
    x2d = x.reshape(batch * seq, d)
    oa, ob, wide = _layer(x2d, batch, seq, norm_gain[0], w_in[0], fox_forget_bias[0],
                          rwkv_shift_mix[0], rwkv_w0[0], rwkv_w2[0], rwkv_a0[0], rwkv_a2[0],
                          rwkv_k_k[0], rwkv_k_a[0], rwkv_r_k[0], rwkv_ln_w[0], rwkv_ln_b[0],
                          w_proj_fox[0], w_proj_rwkv[0], w_out[0])
    out = _merge_out(oa, ob, wide, x2d, w_proj_fox[0].astype(bf16), w_proj_rwkv[0].astype(bf16),
                     w_out[0].astype(bf16), final_norm_gain.reshape(1, d))
    return out.reshape(batch, seq, d)
```

```python
import functools

import jax
import jax.numpy as jnp
from jax import lax
from jax.experimental import pallas as pl
from jax.experimental.pallas import tpu as pltpu

D_MODEL = 2048
FOX_HEADS = 8
FOX_HEAD_DIM = 128
FOX_WIDTH = FOX_HEADS * FOX_HEAD_DIM
RWKV_HEADS = 16
RWKV_HEAD_DIM = 64
RWKV_WIDTH = RWKV_HEADS * RWKV_HEAD_DIM
LORA = 96
RMS_EPS = 1e-6
GN_EPS = 64e-5
L2_EPS = 1e-12

LANES = 128
VMEM_LIMIT = 56 * 1024 * 1024

COL_GA = 0
COL_GB = D_MODEL // LANES
COL_ZA = 2 * D_MODEL // LANES
COL_R = COL_ZA + FOX_WIDTH // LANES
COL_K = COL_R + RWKV_WIDTH // LANES
COL_V = COL_K + RWKV_WIDTH // LANES
COL_ZB = COL_V + RWKV_WIDTH // LANES
WIDE_COLS = (COL_ZB + RWKV_WIDTH // LANES) * LANES
SMALL_COLS = 3 * LANES

RWKV_CHUNK = 64
RWKV_BLOCK = 512
RWKV_TAIL_START = 0.3
RWKV_GROUP = 8
HEADS_PER_GROUP = LANES // RWKV_HEAD_DIM
GROUP_ROWS = HEADS_PER_GROUP * RWKV_CHUNK

f32 = jnp.float32
bf16 = jnp.bfloat16


def _dot(a, b):
    return jnp.dot(a.astype(bf16), b.astype(bf16), preferred_element_type=f32)


def _dot_nt(a, b):
    return lax.dot_general(a.astype(bf16), b.astype(bf16), (((1,), (1,)), ((), ())),
                           preferred_element_type=f32)


def _dot_tn(a, b):
    return lax.dot_general(a.astype(bf16), b.astype(bf16), (((0,), (0,)), ((), ())),
                           preferred_element_type=f32)


def _dot_exact_ones(a, ones_bf16, nt=False, terms=3):
    if nt:
        mm = lambda x: lax.dot_general(ones_bf16, x, (((1,), (0,)), ((), ())),
                                       preferred_element_type=f32)
    else:
        mm = lambda x: jnp.dot(x, ones_bf16, preferred_element_type=f32)
    part = a.astype(bf16)
    out = mm(part)
    rest = a
    for _ in range(terms - 1):
        rest = rest - part.astype(f32)
        part = rest.astype(bf16)
        out = out + mm(part)
    return out


def _softplus(x):
    return jnp.maximum(x, 0.0) + jnp.log(1.0 + jnp.exp(-jnp.abs(x)))


def _sigmoid(x):
    return 1.0 / (1.0 + jnp.exp(-x))


def _norm_matmul_kernel(x_ref, g_ref, w_ref, o_ref, h_ref):
    @pl.when(pl.program_id(1) == 0)
    def _():
        xf = x_ref[...]
        ms = jnp.mean(xf * xf, axis=-1, keepdims=True)
        h_ref[...] = (xf * lax.rsqrt(ms + RMS_EPS) * g_ref[...]).astype(bf16)

    o_ref[...] = lax.dot_general(h_ref[...], w_ref[...], (((1,), (1,)), ((), ())),
                                 preferred_element_type=f32).astype(o_ref.dtype)


PROJ_TM = 1024
PROJ_TN = 1024


def _in_proj_kernel(x_ref, g_ref, w_ref, qkv_ref, wide_ref, small_ref, h_ref, *, n_qkv, n_wide):
    j = pl.program_id(1)

    @pl.when(j == 0)
    def _():
        xf = x_ref[...]
        ms = jnp.mean(xf * xf, axis=-1, keepdims=True)
        h_ref[...] = (xf * lax.rsqrt(ms + RMS_EPS) * g_ref[...]).astype(bf16)

    project = lambda w: lax.dot_general(h_ref[...], w, (((1,), (1,)), ((), ())), preferred_element_type=f32)

    @pl.when(j < n_qkv)
    def _():
        qkv_ref[...] = project(w_ref[...]).astype(qkv_ref.dtype)

    @pl.when(jnp.logical_and(j >= n_qkv, j < n_qkv + n_wide))
    def _():
        wide_ref[...] = project(w_ref[...])

    @pl.when(j == n_qkv + n_wide)
    def _():
        small_ref[...] = project(w_ref[0:SMALL_COLS, :])


def _in_proj(x2d, gain, w_t):
    m, d = x2d.shape
    tm, tn = PROJ_TM, PROJ_TN
    n_qkv = 3 * FOX_WIDTH // tn
    n_wide = WIDE_COLS // tn
    assert n_qkv * tn == 3 * FOX_WIDTH and n_wide * tn == WIDE_COLS and SMALL_COLS <= tn
    assert w_t.shape[0] == 3 * FOX_WIDTH + WIDE_COLS + SMALL_COLS
    return pl.pallas_call(
        functools.partial(_in_proj_kernel, n_qkv=n_qkv, n_wide=n_wide),
        grid=(m // tm, n_qkv + n_wide + 1),
        in_specs=[
            pl.BlockSpec((tm, d), lambda i, j: (i, 0)),
            pl.BlockSpec((1, d), lambda i, j: (0, 0)),
            pl.BlockSpec((tn, d), lambda i, j: (j, 0)),
        ],
        out_specs=[
            pl.BlockSpec((tm, tn), lambda i, j: (i, jnp.minimum(j, n_qkv - 1))),
            pl.BlockSpec((tm, tn), lambda i, j: (i, jnp.clip(j - n_qkv, 0, n_wide - 1))),
            pl.BlockSpec((tm, SMALL_COLS), lambda i, j: (i, 0)),
        ],
        out_shape=[
            jax.ShapeDtypeStruct((m, 3 * FOX_WIDTH), bf16),
            jax.ShapeDtypeStruct((m, WIDE_COLS), f32),
            jax.ShapeDtypeStruct((m, SMALL_COLS), f32),
        ],
        scratch_shapes=[pltpu.VMEM((tm, d), bf16)],
        compiler_params=pltpu.CompilerParams(
            dimension_semantics=("parallel", "arbitrary"),
            vmem_limit_bytes=VMEM_LIMIT),
        name="in_proj",
    )(x2d, gain, w_t)


def _norm_matmul(x2d, gain, w_t, col_start, n, out_dtype, tm, tn):
    m, d = x2d.shape
    first = col_start // tn
    assert first * tn == col_start and n % tn == 0
    return pl.pallas_call(
        _norm_matmul_kernel,
        grid=(m // tm, n // tn),
        in_specs=[
            pl.BlockSpec((tm, d), lambda i, j: (i, 0)),
            pl.BlockSpec((1, d), lambda i, j: (0, 0)),
            pl.BlockSpec((tn, d), lambda i, j: (first + j, 0)),
        ],
        out_specs=pl.BlockSpec((tm, tn), lambda i, j: (i, j)),
        out_shape=jax.ShapeDtypeStruct((m, n), out_dtype),
        scratch_shapes=[pltpu.VMEM((tm, d), bf16)],
        compiler_params=pltpu.CompilerParams(
            dimension_semantics=("parallel", "arbitrary"),
            vmem_limit_bytes=VMEM_LIMIT),
        name="norm_matmul",
    )(x2d, gain, w_t)


def _gate_kernel(f_ref, bias_ref, c_ref, carry_ref, *, tb):
    @pl.when(pl.program_id(1) == 0)
    def _():
        carry_ref[...] = jnp.zeros_like(carry_ref)

    z = f_ref[...] + bias_ref[...]
    log_f = -_softplus(-z)
    row = lax.broadcasted_iota(jnp.int32, (tb, tb), 0)
    col = lax.broadcasted_iota(jnp.int32, (tb, tb), 1)
    tri = jnp.where(col <= row, 1.0, 0.0).astype(bf16)
    c = _dot_exact_ones(log_f, tri, nt=True) + carry_ref[0:1, :]
    c_ref[...] = c
    carry_ref[0:1, :] = c[tb - 1:tb, :]


def _gate_cumsum(small, bias_row, batch, seq):
    tb = 256
    nt = seq // tb
    return pl.pallas_call(
        functools.partial(_gate_kernel, tb=tb),
        grid=(batch, nt),
        in_specs=[
            pl.BlockSpec((tb, LANES), lambda b, t: (b * nt + t, 2)),
            pl.BlockSpec((1, LANES), lambda b, t: (0, 0)),
        ],
        out_specs=pl.BlockSpec((tb, LANES), lambda b, t: (b * nt + t, 0)),
        out_shape=jax.ShapeDtypeStruct((batch * seq, LANES), f32),
        scratch_shapes=[pltpu.VMEM((8, LANES), f32)],
        compiler_params=pltpu.CompilerParams(
            dimension_semantics=("parallel", "arbitrary")),
        name="fox_gate_cumsum",
    )(small, bias_row)


LOG2E = 1.4426950408889634
FOX_TK = 256
FOX_BATCH = 4


def _fox_kernel(q_ref, k_ref, v_ref, c_ref, z_ref, o_ref, crep_ref, acc_ref, vt_ref, *, seq, tk):
    h = pl.program_id(1)
    ng = seq // LANES
    nt = seq // tk
    gpt = tk // LANES
    mm = lambda a, b: jnp.dot(a, b, preferred_element_type=f32)
    mm_nt = lambda a, b: lax.dot_general(a, b, (((1,), (1,)), ((), ())), preferred_element_type=f32)

    src_lane = lax.broadcasted_iota(jnp.int32, (LANES, LANES), 0)
    pick = jnp.where(src_lane == h, 1.0, 0.0).astype(bf16)
    crep_ref[...] = _dot_exact_ones(c_ref[...], pick) * LOG2E
    for t in range(nt):
        vt_ref[t] = v_ref[t * tk:(t + 1) * tk, :].T

    key_rel = lax.broadcasted_iota(jnp.int32, (tk, LANES), 0)
    qry_rel = lax.broadcasted_iota(jnp.int32, (tk, LANES), 1)
    future = [key_rel > qry_rel + d * LANES for d in range(gpt)]

    sched = []
    for t in range(nt):
        visible = list(range(t * gpt, ng))
        sched += [[(t, g) for g in visible[k:k + FOX_BATCH]] for k in range(0, len(visible), FOX_BATCH)]

    def qk(batch):
        return [mm_nt(k_ref[t * tk:(t + 1) * tk, :], q_ref[g * LANES:(g + 1) * LANES, :])
                for t, g in batch]

    m = [None] * ng
    l = [None] * ng

    def finish(batch, alpha, pv):
        for (t, g), a, x in zip(batch, alpha, pv):
            acc_ref[g] = x if t == 0 else a * acc_ref[g] + x
            if t == g // gpt:
                rows = slice(g * LANES, (g + 1) * LANES)
                z = z_ref[rows, :]
                o = (acc_ref[g] / l[g]).T
                o_ref[rows, :] = (o * (z * _sigmoid(z))).astype(o_ref.dtype)

    s_next = qk(sched[0])
    pending = None
    for bi, batch in enumerate(sched):
        s_cur = s_next
        if bi + 1 < len(sched):
            s_next = qk(sched[bi + 1])
        alpha, p16 = [], []
        for (t, g), s in zip(batch, s_cur):
            s = s - crep_ref[t * tk:(t + 1) * tk, :]
            if g < (t + 1) * gpt:
                s = jnp.where(future[g - t * gpt], -jnp.inf, s)
            m_tile = jnp.max(s, axis=0, keepdims=True)
            if t == 0:
                m_new, a = m_tile, None
            else:
                m_new = jnp.maximum(m[g], m_tile)
                a = jnp.exp2(m[g] - m_new)
            p = jnp.exp2(s - m_new)
            p_sum = jnp.sum(p, axis=0, keepdims=True)
            l[g] = p_sum if t == 0 else a * l[g] + p_sum
            m[g] = m_new
            alpha.append(a)
            p16.append(p.astype(bf16))
        pv = [mm(vt_ref[t], p) for (t, g), p in zip(batch, p16)]
        if pending is not None:
            finish(*pending)
        pending = (batch, alpha, pv)
    finish(*pending)


def _fox_attention(qkv3, c3, wide3, batch, seq):
    tk = FOX_TK
    h8 = FOX_HEADS
    return pl.pallas_call(
        functools.partial(_fox_kernel, seq=seq, tk=tk),
        grid=(batch, h8),
        in_specs=[
            pl.BlockSpec((None, seq, LANES), lambda b, h: (b, 0, h)),
            pl.BlockSpec((None, seq, LANES), lambda b, h: (b, 0, h8 + h)),
            pl.BlockSpec((None, seq, LANES), lambda b, h: (b, 0, 2 * h8 + h)),
            pl.BlockSpec((None, seq, LANES), lambda b, h: (b, 0, 0)),
            pl.BlockSpec((None, seq, LANES), lambda b, h: (b, 0, COL_ZA + h)),
        ],
        out_specs=pl.BlockSpec((None, seq, LANES), lambda b, h: (b, 0, h)),
        out_shape=jax.ShapeDtypeStruct((batch, seq, FOX_WIDTH), bf16),
        scratch_shapes=[
            pltpu.VMEM((seq, LANES), f32),
            pltpu.VMEM((seq // LANES, LANES, LANES), f32),
            pltpu.VMEM((seq // tk, LANES, tk), bf16),
        ],
        compiler_params=pltpu.CompilerParams(
            dimension_semantics=("parallel", "arbitrary"),
            vmem_limit_bytes=VMEM_LIMIT),
        name="fox_attention",
    )(qkv3, qkv3, qkv3, c3, wide3)


P_MU_R, P_MU_K, P_MU_V, P_MU_Z, P_W0, P_A0, P_KK, P_KA, P_RK, P_LNW, P_LNB = range(11)
P_ROWS = 16


def _rwkv_kernel(r_ref, k_ref, v_ref, z_ref, wd_ref, ad_ref, pc_ref, ps_ref, w2_ref, a2_ref,
                 o_ref, state_ref, prev_ref, *, tb):
    C = RWKV_CHUNK
    G = GROUP_ROWS
    N = RWKV_HEAD_DIM

    @pl.when(pl.program_id(2) == 0)
    def _():
        state_ref[...] = jnp.zeros_like(state_ref)
        prev_ref[...] = jnp.zeros_like(prev_ref)

    row_in_block = lax.broadcasted_iota(jnp.int32, (tb, LANES), 0)

    def shifted(ref, slot, mu):
        u = ref[...]
        prev = jnp.where(row_in_block == 0, prev_ref[slot, 0:1, :], pltpu.roll(u, 1, 0))
        prev_ref[slot, 0:1, :] = u[tb - 1:tb, :]
        return u + (prev - u) * mu

    prm = lambda idx: pc_ref[idx:idx + 1, :]
    r = shifted(r_ref, 0, prm(P_MU_R))
    kr = shifted(k_ref, 1, prm(P_MU_K))
    vr = shifted(v_ref, 2, prm(P_MU_V))
    zb = shifted(z_ref, 3, prm(P_MU_Z))
    wd = shifted(wd_ref, 4, ps_ref[0:1, :])
    ad = shifted(ad_ref, 5, ps_ref[1:2, :])

    w = -_softplus(-(prm(P_W0) + _dot(jnp.tanh(wd), w2_ref[...]))) - 0.5
    log_decay = -jnp.exp(w)
    rate = _sigmoid(prm(P_A0) + _dot(ad, a2_ref[...]))

    li = lax.broadcasted_iota(jnp.int32, (LANES, LANES), 0)
    lj = lax.broadcasted_iota(jnp.int32, (LANES, LANES), 1)
    head_ones = jnp.where(li // N == lj // N, 1.0, 0.0).astype(bf16)
    head_sum = lambda x: _dot_exact_ones(x, head_ones)

    kk = kr * prm(P_KK)
    kk = kk / jnp.maximum(jnp.sqrt(head_sum(kk * kk)), L2_EPS)
    kp = kr * (1.0 + (rate - 1.0) * prm(P_KA))
    bb = kk * rate
    bonus = head_sum(r * kp * prm(P_RK)) * vr

    ti = lax.broadcasted_iota(jnp.int32, (tb, tb), 0)
    tj = lax.broadcasted_iota(jnp.int32, (tb, tb), 1)
    chunk_tri = jnp.where(jnp.logical_and(tj <= ti, ti // C == tj // C), 1.0, 0.0).astype(bf16)
    ci = _dot_exact_ones(log_decay, chunk_tri, nt=True)

    a_t = -kk * jnp.exp(ci - log_decay)
    r_t = r * jnp.exp(ci)
    inv = jnp.exp(-ci)
    b_t = bb * inv
    k_t = kp * inv

    lane = lax.broadcasted_iota(jnp.int32, (1, LANES), 1)
    head_masks = [jnp.where(lane // N == h, 1.0, 0.0) for h in range(HEADS_PER_GROUP)]
    stack = lambda x: jnp.concatenate([x * hm for hm in head_masks], axis=0)
    stack16 = lambda x: stack(x).astype(bf16)

    gi = lax.broadcasted_iota(jnp.int32, (G, G), 0)
    gj = lax.broadcasted_iota(jnp.int32, (G, G), 1)
    same_head = gi // C == gj // C
    strict = jnp.logical_and(same_head, gj < gi)
    incl = jnp.logical_and(same_head, gj <= gi)
    eye = jnp.where(gi == gj, 1.0, 0.0)

    mm = lambda a, b: jnp.dot(a, b, preferred_element_type=f32)
    mm_nt = lambda a, b: lax.dot_general(a, b, (((1,), (1,)), ((), ())), preferred_element_type=f32)
    mm_tn = lambda a, b: lax.dot_general(a, b, (((0,), (0,)), ((), ())), preferred_element_type=f32)

    def independent(chunks):
        n = range(len(chunks))
        rows = [slice(c * C, (c + 1) * C) for c in chunks]
        c_last = [ci[sl][C - 1:C, :] for sl in rows]
        to_end = [jnp.exp(c_last[i] - ci[rows[i]]) for i in n]
        xr32 = [stack(r_t[sl]) for sl in rows]
        xa = [stack16(a_t[sl]) for sl in rows]
        xr = [x.astype(bf16) for x in xr32]
        yb = [stack16(b_t[sl]) for sl in rows]
        yk = [stack16(k_t[sl]) for sl in rows]
        vs = [stack16(vr[sl]) for sl in rows]
        bh = [stack16(bb[rows[i]] * to_end[i]) for i in n]
        kh = [stack16(kp[rows[i]] * to_end[i]) for i in n]

        big = [mm_nt(jnp.concatenate([xa[i], xr[i]], axis=0), jnp.concatenate([yb[i], yk[i]], axis=0))
               for i in n]
        a_ab = [jnp.where(strict, big[i][0:G, 0:G], 0.0) for i in n]
        a_ak = [jnp.where(strict, big[i][0:G, G:2 * G], 0.0).astype(bf16) for i in n]
        a_rb = [jnp.where(incl, big[i][G:2 * G, 0:G], 0.0).astype(bf16) for i in n]
        a_rk = [jnp.where(incl, big[i][G:2 * G, G:2 * G], 0.0).astype(bf16) for i in n]

        tinv = [eye + a for a in a_ab]
        pw = [a.astype(bf16) for a in a_ab]
        for _ in range(5):
            pw = [mm(p, p).astype(bf16) for p in pw]
            tinv = [tinv[i] + mm(tinv[i].astype(bf16), pw[i]) for i in n]
        tinv = [t.astype(bf16) for t in tinv]

        akv = [mm(a_ak[i], vs[i]).astype(bf16) for i in n]
        wu = [mm(tinv[i], jnp.concatenate([xa[i], akv[i]], axis=1)).astype(bf16) for i in n]
        e = [mm(a_rb[i], wu[i]) for i in n]
        ry = [(xr32[i] + e[i][:, 0:LANES]).astype(bf16) for i in n]
        y_loc = [e[i][:, LANES:] + mm(a_rk[i], vs[i]) for i in n]
        pm = [mm_tn(wu[i][:, 0:LANES], bh[i]).astype(bf16) for i in n]
        q = [mm_tn(jnp.concatenate([wu[i][:, LANES:], vs[i]], axis=0),
                   jnp.concatenate([bh[i], kh[i]], axis=0)) for i in n]
        return [dict(ry=ry[i], y_loc=y_loc[i], pm=pm[i], q=q[i], decay=jnp.exp(c_last[i])) for i in n]

    def sequential(state, parts):
        out = []
        for p in parts:
            s16 = state.astype(bf16)
            y_st = mm_nt(p["ry"], s16) + p["y_loc"]
            out.append(y_st[0:C] + y_st[C:2 * C])
            state = state * p["decay"] + mm(s16, p["pm"]) + p["q"]
        return state, out

    n_chunks = tb // C
    groups = [list(range(g, min(g + RWKV_GROUP, n_chunks))) for g in range(0, n_chunks, RWKV_GROUP)]
    state = state_ref[...]
    ys = []
    parts = independent(groups[0])
    for g in groups[1:]:
        state, out = sequential(state, parts)
        ys += out
        parts = independent(g)
    state, out = sequential(state, parts)
    ys += out
    state_ref[...] = state

    y = jnp.concatenate(ys, axis=0)
    mean = head_sum(y) * (1.0 / N)
    yc = y - mean
    var = head_sum(yc * yc) * (1.0 / N)
    yn = yc * lax.rsqrt(var + GN_EPS) * prm(P_LNW) + prm(P_LNB)
    o_ref[...] = ((yn + bonus) * (zb * _sigmoid(zb))).astype(o_ref.dtype)


def _rwkv_mix_serial(wide, small, pc, ps, w2p, a2p, batch, seq):
    tb = RWKV_BLOCK
    nt = seq // tb
    groups = RWKV_WIDTH // LANES
    row = lambda b, g, t: b * nt + t
    return pl.pallas_call(
        functools.partial(_rwkv_kernel, tb=tb),
        grid=(batch, groups, nt),
        in_specs=[
            pl.BlockSpec((tb, LANES), lambda b, g, t: (row(b, g, t), COL_R + g)),
            pl.BlockSpec((tb, LANES), lambda b, g, t: (row(b, g, t), COL_K + g)),
            pl.BlockSpec((tb, LANES), lambda b, g, t: (row(b, g, t), COL_V + g)),
            pl.BlockSpec((tb, LANES), lambda b, g, t: (row(b, g, t), COL_ZB + g)),
            pl.BlockSpec((tb, LANES), lambda b, g, t: (row(b, g, t), 0)),
            pl.BlockSpec((tb, LANES), lambda b, g, t: (row(b, g, t), 1)),
            pl.BlockSpec((P_ROWS, LANES), lambda b, g, t: (0, g)),
            pl.BlockSpec((8, LANES), lambda b, g, t: (0, 0)),
            pl.BlockSpec((LANES, LANES), lambda b, g, t: (0, g)),
            pl.BlockSpec((LANES, LANES), lambda b, g, t: (0, g)),
        ],
        out_specs=pl.BlockSpec((tb, LANES), lambda b, g, t: (row(b, g, t), g)),
        out_shape=jax.ShapeDtypeStruct((batch * seq, RWKV_WIDTH), bf16),
        scratch_shapes=[
            pltpu.VMEM((LANES, LANES), f32),
            pltpu.VMEM((6, 8, LANES), f32),
        ],
        compiler_params=pltpu.CompilerParams(
            dimension_semantics=("parallel", "parallel", "arbitrary"),
            vmem_limit_bytes=VMEM_LIMIT),
        name="rwkv_mix",
    )(wide, wide, wide, wide, small, small, pc, ps, w2p, a2p)


K_XA, K_XR, K_YB, K_YK, K_VS, K_BH, K_KH = range(7)


def _spread(emitters, lo=0.0, hi=1.0):
    n = len(emitters)
    return [(lo + (hi - lo) * (i + 0.5) / n, e) for i, e in enumerate(emitters)]


def _interleave(*segment_lists):
    keyed = [(pos, prio, seg) for prio, segs in enumerate(segment_lists) for pos, seg in segs]
    keyed.sort(key=lambda x: (x[0], x[1]))
    for _, _, seg in keyed:
        seg()


def _rwkv_pipe_kernel(r_ref, k_ref, v_ref, z_ref, wd_ref, ad_ref, pc_ref, ps_ref, w2_ref, a2_ref, pc3_ref,
                      o_ref,
                      state_ref, prev_ref, stk_ref, xr32_ref, aux1_ref, dec1_ref, p16_ref, p32_ref,
                      aux2_ref, dec2_ref, *, tb, nt):
    C = RWKV_CHUNK
    G = GROUP_ROWS
    N = RWKV_HEAD_DIM
    NC = tb // C
    s = pl.program_id(0)
    first_of_seq_1 = (s % nt) == 0
    first_of_seq_3 = ((s + 2 * nt - 2) % nt) == 0

    @pl.when(s == 0)
    def _():
        for ref in (state_ref, prev_ref, stk_ref, xr32_ref, aux1_ref, dec1_ref, p16_ref, p32_ref,
                    aux2_ref, dec2_ref):
            ref[...] = jnp.zeros_like(ref)

    mm = lambda a, b: jnp.dot(a, b, preferred_element_type=f32)
    mm_nt = lambda a, b: lax.dot_general(a, b, (((1,), (1,)), ((), ())), preferred_element_type=f32)
    mm_tn = lambda a, b: lax.dot_general(a, b, (((0,), (0,)), ((), ())), preferred_element_type=f32)

    def body(cur, prv):
        li = lax.broadcasted_iota(jnp.int32, (LANES, LANES), 0)
        lj = lax.broadcasted_iota(jnp.int32, (LANES, LANES), 1)
        head_ones = jnp.where(li // N == lj // N, 1.0, 0.0).astype(bf16)
        head_sum = lambda x: _dot_exact_ones(x, head_ones, terms=2)
        lane = lax.broadcasted_iota(jnp.int32, (1, LANES), 1)
        head_masks = [jnp.where(lane // N == h, 1.0, 0.0) for h in range(HEADS_PER_GROUP)]
        stack = lambda x: jnp.concatenate([x * hm for hm in head_masks], axis=0)
        lane_c = lax.broadcasted_iota(jnp.int32, (C, LANES), 1)
        head_sel = [lane_c // N == h for h in range(HEADS_PER_GROUP)]

        def stack16(x):
            xb = x.astype(bf16)
            return jnp.concatenate([jnp.where(m, xb, jnp.zeros_like(xb)) for m in head_sel], axis=0)

        gi = lax.broadcasted_iota(jnp.int32, (G, G), 0)
        gj = lax.broadcasted_iota(jnp.int32, (G, G), 1)
        same_head = gi // C == gj // C
        strict = jnp.logical_and(same_head, gj < gi)
        incl = jnp.logical_and(same_head, gj <= gi)
        eye = jnp.where(gi == gj, 1.0, 0.0)
        chunks = range(NC)

        v1 = {}
        prm = lambda idx: pc_ref[idx:idx + 1, :]
        row_in_block = lax.broadcasted_iota(jnp.int32, (tb, LANES), 0)

        def shifted(ref, slot, mu):
            u = ref[...]
            carry = jnp.where(first_of_seq_1, 0.0, prev_ref[slot, 0:1, :])
            prev = jnp.where(row_in_block == 0, carry, pltpu.roll(u, 1, 0))
            prev_ref[slot, 0:1, :] = u[tb - 1:tb, :]
            return u + (prev - u) * mu

        def s1_lora():
            v1["wd"] = shifted(wd_ref, 4, ps_ref[0:1, :])
            v1["ad"] = shifted(ad_ref, 5, ps_ref[1:2, :])
            v1["w_lin"] = _dot(jnp.tanh(v1["wd"]), w2_ref[...])
            v1["a_lin"] = _dot(v1["ad"], a2_ref[...])

        def s1_key_norm():
            v1["kr"] = shifted(k_ref, 1, prm(P_MU_K))
            kk = v1["kr"] * prm(P_KK)
            v1["kk_raw"] = kk
            v1["kk_ss"] = head_sum(kk * kk)

        def s1_decay():
            w = -_softplus(-(prm(P_W0) + v1["w_lin"])) - 0.5
            v1["log_decay"] = -jnp.exp(w)
            ti = lax.broadcasted_iota(jnp.int32, (2 * C, 2 * C), 0)
            tj = lax.broadcasted_iota(jnp.int32, (2 * C, 2 * C), 1)
            pair_tri = jnp.where(jnp.logical_and(tj <= ti, ti // C == tj // C), 1.0, 0.0).astype(bf16)
            v1["ci"] = jnp.concatenate(
                [_dot_exact_ones(v1["log_decay"][k:k + 2 * C], pair_tri, nt=True)
                 for k in range(0, tb, 2 * C)], axis=0)

        def s1_bonus():
            v1["rate"] = _sigmoid(prm(P_A0) + v1["a_lin"])
            v1["r"] = shifted(r_ref, 0, prm(P_MU_R))
            v1["vr"] = shifted(v_ref, 2, prm(P_MU_V))
            v1["kp"] = v1["kr"] * (1.0 + (v1["rate"] - 1.0) * prm(P_KA))
            aux1_ref[cur, 0] = head_sum(v1["r"] * v1["kp"] * prm(P_RK)) * v1["vr"]
            zb = shifted(z_ref, 3, prm(P_MU_Z))
            aux1_ref[cur, 1] = zb * _sigmoid(zb)

        def s1_scale():
            kk = v1["kk_raw"] / jnp.maximum(jnp.sqrt(v1["kk_ss"]), L2_EPS)
            v1["bb"] = kk * v1["rate"]
            ci = v1["ci"]
            v1["a_t"] = -kk * jnp.exp(ci - v1["log_decay"])
            v1["r_t"] = v1["r"] * jnp.exp(ci)
            inv = jnp.exp(-ci)
            v1["b_t"] = v1["bb"] * inv
            v1["k_t"] = v1["kp"] * inv

        def s1_stack(c):
            def emit():
                sl = slice(c * C, (c + 1) * C)
                ci_c = v1["ci"][sl]
                c_last = ci_c[C - 1:C, :]
                to_end = jnp.exp(c_last - ci_c)
                dec1_ref[cur, c] = jnp.broadcast_to(jnp.exp(c_last), (8, LANES))
                xr = stack(v1["r_t"][sl])
                xr32_ref[cur, c] = xr
                stk_ref[cur, K_XR, c] = xr.astype(bf16)
                stk_ref[cur, K_XA, c] = stack16(v1["a_t"][sl])
                stk_ref[cur, K_YB, c] = stack16(v1["b_t"][sl])
                stk_ref[cur, K_YK, c] = stack16(v1["k_t"][sl])
                stk_ref[cur, K_VS, c] = stack16(v1["vr"][sl])
                stk_ref[cur, K_BH, c] = stack16(v1["bb"][sl] * to_end)
                stk_ref[cur, K_KH, c] = stack16(v1["kp"][sl] * to_end)
            return emit

        segs1 = ([(0.0, s1_lora), (0.04, s1_key_norm), (0.10, s1_decay), (0.16, s1_bonus)]
                 + _spread([s1_scale] + [s1_stack(c) for c in chunks], RWKV_TAIL_START, 1.0))

        v2 = {}
        ld = lambda kind, c: stk_ref[prv, kind, c]

        def s2_big():
            big = [mm_nt(jnp.concatenate([ld(K_XA, c), ld(K_XR, c)], axis=0),
                         jnp.concatenate([ld(K_YB, c), ld(K_YK, c)], axis=0)) for c in chunks]
            v2["a_ab"] = [jnp.where(strict, big[c][0:G, 0:G], 0.0) for c in chunks]
            v2["a_ak"] = [jnp.where(strict, big[c][0:G, G:2 * G], 0.0).astype(bf16) for c in chunks]
            v2["a_rb"] = [jnp.where(incl, big[c][G:2 * G, 0:G], 0.0).astype(bf16) for c in chunks]
            v2["a_rk"] = [jnp.where(incl, big[c][G:2 * G, G:2 * G], 0.0).astype(bf16) for c in chunks]
            v2["tinv"] = [eye + a for a in v2["a_ab"]]
            v2["pw"] = [a.astype(bf16) for a in v2["a_ab"]]

        def s2_first_square():
            v2["pw"] = [mm(p, p).astype(bf16) for p in v2["pw"]]

        def s2_level(last):
            def emit():
                for c in chunks:
                    pw = v2["pw"][c]
                    t16 = v2["tinv"][c].astype(bf16)
                    if last:
                        v2["tinv"][c] = v2["tinv"][c] + mm(t16, pw)
                    else:
                        both = mm(jnp.concatenate([t16, pw], axis=0), pw)
                        v2["tinv"][c] = v2["tinv"][c] + both[0:G]
                        v2["pw"][c] = both[G:2 * G].astype(bf16)
            return emit

        def s2_av():
            v2["tinv"] = [t.astype(bf16) for t in v2["tinv"]]
            av = [mm(jnp.concatenate([v2["a_ak"][c], v2["a_rk"][c]], axis=0), ld(K_VS, c)) for c in chunks]
            v2["akv"] = [x[0:G].astype(bf16) for x in av]
            v2["arkv"] = [x[G:2 * G] for x in av]

        def s2_wu():
            v2["wu"] = [mm(v2["tinv"][c], jnp.concatenate([ld(K_XA, c), v2["akv"][c]], axis=1)).astype(bf16)
                        for c in chunks]

        def s2_ry():
            for c in chunks:
                e = mm(v2["a_rb"][c], v2["wu"][c])
                p16_ref[cur, 0, c] = (xr32_ref[prv, c] + e[:, 0:LANES]).astype(bf16)
                p32_ref[cur, 0, c] = e[:, LANES:] + v2["arkv"][c]

        def s2_pm():
            for c in chunks:
                p16_ref[cur, 1, c] = mm_tn(v2["wu"][c][:, 0:LANES], ld(K_BH, c)).astype(bf16)

        def s2_q():
            for c in chunks:
                p32_ref[cur, 1, c] = mm_tn(jnp.concatenate([v2["wu"][c][:, LANES:], ld(K_VS, c)], axis=0),
                                           jnp.concatenate([ld(K_BH, c), ld(K_KH, c)], axis=0))
            aux2_ref[cur] = aux1_ref[prv]
            dec2_ref[cur] = dec1_ref[prv]

        segs2 = _spread([s2_big, s2_first_square] + [s2_level(False)] * 4 + [s2_level(True)]
                        + [s2_av, s2_wu, s2_ry, s2_pm, s2_q])

        v3 = {"ys": []}
        prm3 = lambda idx: pc3_ref[idx:idx + 1, :]

        def s3_start():
            v3["state"] = jnp.where(first_of_seq_3, 0.0, state_ref[...])

        def s3_chunk(c):
            def emit():
                state = v3["state"]
                s16 = state.astype(bf16)
                y_st = mm_nt(p16_ref[prv, 0, c], s16) + p32_ref[prv, 0, c]
                v3["ys"].append(y_st[0:C] + y_st[C:2 * C])
                v3["state"] = (state * dec2_ref[prv, c, 0:1, :] + mm(s16, p16_ref[prv, 1, c])
                               + p32_ref[prv, 1, c])
            return emit

        def s3_finish():
            state_ref[...] = v3["state"]
            y = jnp.concatenate(v3["ys"], axis=0)
            mean = head_sum(y) * (1.0 / N)
            yc = y - mean
            var = head_sum(yc * yc) * (1.0 / N)
            yn = yc * lax.rsqrt(var + GN_EPS) * prm3(P_LNW) + prm3(P_LNB)
            o_ref[...] = ((yn + aux2_ref[prv, 0]) * aux2_ref[prv, 1]).astype(o_ref.dtype)

        segs3 = _spread([s3_start] + [s3_chunk(c) for c in chunks] + [s3_finish])

        _interleave(segs3, segs2, segs1)

    for parity in (0, 1):
        @pl.when(s % 2 == parity)
        def _(parity=parity):
            body(cur=parity, prv=1 - parity)


def _rwkv_mix(wide, small, pc, ps, w2p, a2p, batch, seq):
    tb = RWKV_BLOCK
    nt = seq // tb
    groups = RWKV_WIDTH // LANES
    nc = tb // RWKV_CHUNK
    n_blocks = batch * groups * nt

    def where(s):
        s = jnp.clip(s, 0, n_blocks - 1)
        return (s // (nt * groups)) * nt + s % nt, (s // nt) % groups

    col = lambda c0: (lambda s: (where(s)[0], c0 + where(s)[1]))
    fixed_col = lambda c: (lambda s: (where(s)[0], c))
    grp = lambda s: (0, where(s)[1])
    return pl.pallas_call(
        functools.partial(_rwkv_pipe_kernel, tb=tb, nt=nt),
        grid=(n_blocks + 2,),
        in_specs=[
            pl.BlockSpec((tb, LANES), col(COL_R)),
            pl.BlockSpec((tb, LANES), col(COL_K)),
            pl.BlockSpec((tb, LANES), col(COL_V)),
            pl.BlockSpec((tb, LANES), col(COL_ZB)),
            pl.BlockSpec((tb, LANES), fixed_col(0)),
            pl.BlockSpec((tb, LANES), fixed_col(1)),
            pl.BlockSpec((P_ROWS, LANES), grp),
            pl.BlockSpec((8, LANES), lambda s: (0, 0)),
            pl.BlockSpec((LANES, LANES), grp),
            pl.BlockSpec((LANES, LANES), grp),
            pl.BlockSpec((P_ROWS, LANES), lambda s: grp(s - 2)),
        ],
        out_specs=pl.BlockSpec((tb, LANES), lambda s: (where(s - 2)[0], where(s - 2)[1])),
        out_shape=jax.ShapeDtypeStruct((batch * seq, RWKV_WIDTH), bf16),
        scratch_shapes=[
            pltpu.VMEM((LANES, LANES), f32),
            pltpu.VMEM((6, 8, LANES), f32),
            pltpu.VMEM((2, 7, nc, GROUP_ROWS, LANES), bf16),
            pltpu.VMEM((2, nc, GROUP_ROWS, LANES), f32),
            pltpu.VMEM((2, 2, tb, LANES), f32),
            pltpu.VMEM((2, nc, 8, LANES), f32),
            pltpu.VMEM((2, 2, nc, GROUP_ROWS, LANES), bf16),
            pltpu.VMEM((2, 2, nc, GROUP_ROWS, LANES), f32),
            pltpu.VMEM((2, 2, tb, LANES), f32),
            pltpu.VMEM((2, nc, 8, LANES), f32),
        ],
        compiler_params=pltpu.CompilerParams(
            dimension_semantics=("arbitrary",),
            vmem_limit_bytes=VMEM_LIMIT),
        name="rwkv_mix",
    )(wide, wide, wide, wide, small, small, pc, ps, w2p, a2p, pc)


def _out_kernel(oa_ref, ob_ref, ga_ref, gb_ref, x_ref, wpf_ref, wpr_ref, wo_ref, g_ref, o_ref):
    pa = jnp.dot(oa_ref[...], wpf_ref[...], preferred_element_type=f32)
    pb = jnp.dot(ob_ref[...], wpr_ref[...], preferred_element_type=f32)
    m = _sigmoid(ga_ref[...]) * pa + _sigmoid(gb_ref[...]) * pb
    z = x_ref[...] + jnp.dot(m.astype(bf16), wo_ref[...], preferred_element_type=f32)
    ms = jnp.mean(z * z, axis=-1, keepdims=True)
    o_ref[...] = z * lax.rsqrt(ms + RMS_EPS) * g_ref[...]


def _merge_out(oa, ob, wide, x2d, wpf, wpr, wo, gain):
    m, d = x2d.shape
    tm = 256
    resident = lambda shape: pl.BlockSpec(shape, lambda i: (0, 0), pipeline_mode=pl.Buffered(1))
    return pl.pallas_call(
        _out_kernel,
        grid=(m // tm,),
        in_specs=[
            pl.BlockSpec((tm, FOX_WIDTH), lambda i: (i, 0)),
            pl.BlockSpec((tm, RWKV_WIDTH), lambda i: (i, 0)),
            pl.BlockSpec((tm, d), lambda i: (i, COL_GA * LANES // d)),
            pl.BlockSpec((tm, d), lambda i: (i, COL_GB * LANES // d)),
            pl.BlockSpec((tm, d), lambda i: (i, 0)),
            resident((FOX_WIDTH, d)),
            resident((RWKV_WIDTH, d)),
            resident((d, d)),
            resident((1, d)),
        ],
        out_specs=pl.BlockSpec((tm, d), lambda i: (i, 0)),
        out_shape=jax.ShapeDtypeStruct((m, d), f32),
        compiler_params=pltpu.CompilerParams(
            dimension_semantics=("parallel",),
            vmem_limit_bytes=VMEM_LIMIT),
        name="merge_out",
    )(oa, ob, wide, wide, x2d, wpf, wpr, wo, gain)


def _pad_cols(a, width):
    return jnp.pad(a, ((0, 0), (0, width - a.shape[1])))


def _layer(x2d, batch, seq, norm_gain, w_in, fox_forget_bias, rwkv_shift_mix, rwkv_w0, rwkv_w2,
           rwkv_a0, rwkv_a2, rwkv_k_k, rwkv_k_a, rwkv_r_k, rwkv_ln_w, rwkv_ln_b,
           w_proj_fox, w_proj_rwkv, w_out):
    fw, rw = FOX_WIDTH, RWKV_WIDTH
    w_t = w_in.T
    o = 0
    w_qkv = w_t[o:o + 3 * fw]; o += 3 * fw
    w_za = w_t[o:o + fw]; o += fw
    w_f = w_t[o:o + FOX_HEADS]; o += FOX_HEADS
    w_rkvz = w_t[o:o + 4 * rw]; o += 4 * rw
    w_wd = w_t[o:o + LORA]; o += LORA
    w_ad = w_t[o:o + LORA]; o += LORA
    w_g = w_t[o:o + 2 * D_MODEL]
    q_scale = jnp.where(jnp.arange(3 * fw) < fw, FOX_HEAD_DIM ** -0.5 * LOG2E, 1.0).astype(f32)
    pad_rows = lambda a: jnp.pad(a, ((0, LANES - a.shape[0]), (0, 0)))
    w_all = jnp.concatenate(
        [w_qkv * q_scale[:, None], w_g, w_za, w_rkvz, pad_rows(w_wd), pad_rows(w_ad), pad_rows(w_f)],
        axis=0).astype(bf16)
    gain = norm_gain.reshape(1, D_MODEL)

    qkv, wide, small = _in_proj(x2d, gain, w_all)

    bias_row = _pad_cols(fox_forget_bias.reshape(1, FOX_HEADS), LANES)
    c = _gate_cumsum(small, bias_row, batch, seq)
    qkv3 = qkv.reshape(batch, seq, 3 * fw)
    oa = _fox_attention(qkv3, c.reshape(batch, seq, LANES), wide.reshape(batch, seq, WIDE_COLS),
                        batch, seq).reshape(batch * seq, fw)

    mu = rwkv_shift_mix
    pc = jnp.zeros((P_ROWS, rw), f32)
    rows = [mu[0:rw], mu[rw:2 * rw], mu[2 * rw:3 * rw], mu[3 * rw:4 * rw], rwkv_w0, rwkv_a0,
            rwkv_k_k, rwkv_k_a, rwkv_r_k.reshape(rw), rwkv_ln_w, rwkv_ln_b]
    pc = pc.at[:len(rows)].set(jnp.stack(rows))
    ps = jnp.zeros((8, LANES), f32)
    ps = ps.at[0, :LORA].set(mu[4 * rw:4 * rw + LORA]).at[1, :LORA].set(mu[4 * rw + LORA:])
    w2p = jnp.pad(rwkv_w2, ((0, LANES - LORA), (0, 0))).astype(bf16)
    a2p = jnp.pad(rwkv_a2, ((0, LANES - LORA), (0, 0))).astype(bf16)
    ob = _rwkv_mix(wide, small, pc, ps, w2p, a2p, batch, seq)

    return oa, ob, wide


def kernel(x, norm_gain, w_in, fox_forget_bias, rwkv_shift_mix, rwkv_w0, rwkv_w2, rwkv_a0, rwkv_a2, rwkv_k_k, rwkv_k_a, rwkv_r_k, rwkv_ln_w, rwkv_ln_b, w_proj_fox, w_proj_rwkv, w_out, final_norm_gain):
    batch, seq, d = x.shape
    depth = norm_gain.shape[0]
    assert depth == 1, "the final rmsnorm is fused into the single layer's output kernel"
    x2d = x.reshape(batch * seq, d)
    oa, ob, wide = _layer(x2d, batch, seq, norm_gain[0], w_in[0], fox_forget_bias[0],
                          rwkv_shift_mix[0], rwkv_w0[0], rwkv_w2[0], rwkv_a0[0], rwkv_a2[0],
                          rwkv_k_k[0], rwkv_k_a[0], rwkv_r_k[0], rwkv_ln_w[0], rwkv_ln_b[0],
                          w_proj_fox[0], w_proj_rwkv[0], w_out[0])
    out = _merge_out(oa, ob, wide, x2d, w_proj_fox[0].astype(bf16), w_proj_rwkv[0].astype(bf16),
                     w_out[0].astype(bf16), final_norm_gain.reshape(1, d))
    return out.reshape(batch, seq, d)
```

```python
import functools

import jax
import jax.numpy as jnp
from jax import lax
from jax.experimental import pallas as pl
from jax.experimental.pallas import tpu as pltpu

D_MODEL = 2048
FOX_HEADS = 8
FOX_HEAD_DIM = 128
FOX_WIDTH = FOX_HEADS * FOX_HEAD_DIM
RWKV_HEADS = 16
RWKV_HEAD_DIM = 64
RWKV_WIDTH = RWKV_HEADS * RWKV_HEAD_DIM
LORA = 96
RMS_EPS = 1e-6
GN_EPS = 64e-5
L2_EPS = 1e-12

LANES = 128
VMEM_LIMIT = 56 * 1024 * 1024

COL_GA = 0
COL_GB = D_MODEL // LANES
COL_ZA = 2 * D_MODEL // LANES
COL_R = COL_ZA + FOX_WIDTH // LANES
COL_K = COL_R + RWKV_WIDTH // LANES
COL_V = COL_K + RWKV_WIDTH // LANES
COL_ZB = COL_V + RWKV_WIDTH // LANES
WIDE_COLS = (COL_ZB + RWKV_WIDTH // LANES) * LANES
SMALL_COLS = 3 * LANES

RWKV_CHUNK = 64
RWKV_BLOCK = 512
RWKV_TAIL_START = 0.3
RWKV_GROUP = 8
HEADS_PER_GROUP = LANES // RWKV_HEAD_DIM
GROUP_ROWS = HEADS_PER_GROUP * RWKV_CHUNK

f32 = jnp.float32
bf16 = jnp.bfloat16


def _dot(a, b):
    return jnp.dot(a.astype(bf16), b.astype(bf16), preferred_element_type=f32)


def _dot_nt(a, b):
    return lax.dot_general(a.astype(bf16), b.astype(bf16), (((1,), (1,)), ((), ())),
                           preferred_element_type=f32)


def _dot_tn(a, b):
    return lax.dot_general(a.astype(bf16), b.astype(bf16), (((0,), (0,)), ((), ())),
                           preferred_element_type=f32)


def _dot_exact_ones(a, ones_bf16, nt=False, terms=3):
    if nt:
        mm = lambda x: lax.dot_general(ones_bf16, x, (((1,), (0,)), ((), ())),
                                       preferred_element_type=f32)
    else:
        mm = lambda x: jnp.dot(x, ones_bf16, preferred_element_type=f32)
    part = a.astype(bf16)
    out = mm(part)
    rest = a
    for _ in range(terms - 1):
        rest = rest - part.astype(f32)
        part = rest.astype(bf16)
        out = out + mm(part)
    return out


def _softplus(x):
    return jnp.maximum(x, 0.0) + jnp.log(1.0 + jnp.exp(-jnp.abs(x)))


def _sigmoid(x):
    return 1.0 / (1.0 + jnp.exp(-x))


def _norm_matmul_kernel(x_ref, g_ref, w_ref, o_ref, h_ref):
    @pl.when(pl.program_id(1) == 0)
    def _():
        xf = x_ref[...]
        ms = jnp.mean(xf * xf, axis=-1, keepdims=True)
        h_ref[...] = (xf * lax.rsqrt(ms + RMS_EPS) * g_ref[...]).astype(bf16)

    o_ref[...] = lax.dot_general(h_ref[...], w_ref[...], (((1,), (1,)), ((), ())),
                                 preferred_element_type=f32).astype(o_ref.dtype)


PROJ_TM = 1024
PROJ_TN = 1024


def _relayout_kernel(w_ref, o_ref, *, scaled_blocks, scale):
    factor = jnp.where(pl.program_id(0) < scaled_blocks, scale, 1.0)
    o_ref[...] = (w_ref[...] * factor).astype(o_ref.dtype)


def _relayout_weights(w_t, pieces, scaled_rows, scale):
    d = w_t.shape[1]
    rb = PROJ_TN
    assert scaled_rows % rb == 0 and all(n % rb == 0 and src % 8 == 0 for src, n in pieces)
    starts, first_block = [], 0
    for src, n in pieces:
        starts.append((first_block, src))
        first_block += n // rb

    def src_row(i):
        row8 = jnp.int32(0)
        for blk, src in starts:
            row8 = jnp.where(i >= blk, src // 8 + (i - blk) * (rb // 8), row8)
        return pl.multiple_of(row8 * 8, 8)

    return pl.pallas_call(
        functools.partial(_relayout_kernel, scaled_blocks=scaled_rows // rb, scale=scale),
        grid=(first_block,),
        in_specs=[pl.BlockSpec((pl.Element(rb), pl.Element(d)), lambda i: (src_row(i), 0))],
        out_specs=pl.BlockSpec((rb, d), lambda i: (i, 0)),
        out_shape=jax.ShapeDtypeStruct((first_block * rb, d), bf16),
        compiler_params=pltpu.CompilerParams(
            dimension_semantics=("parallel",),
            vmem_limit_bytes=VMEM_LIMIT),
        name="relayout_weights",
    )(w_t)


def _in_proj_kernel(x_ref, g_ref, w_ref, wt_ref, qkv_ref, wide_ref, small_ref, h_ref, *, n_qkv, n_wide):
    j = pl.program_id(1)

    @pl.when(j == 0)
    def _():
        xf = x_ref[...]
        ms = jnp.mean(xf * xf, axis=-1, keepdims=True)
        h_ref[...] = (xf * lax.rsqrt(ms + RMS_EPS) * g_ref[...]).astype(bf16)

    project = lambda w: lax.dot_general(h_ref[...], w, (((1,), (1,)), ((), ())), preferred_element_type=f32)

    @pl.when(j < n_qkv)
    def _():
        qkv_ref[...] = project(w_ref[...]).astype(qkv_ref.dtype)

    @pl.when(jnp.logical_and(j >= n_qkv, j < n_qkv + n_wide))
    def _():
        wide_ref[...] = project(w_ref[...])

    @pl.when(j == n_qkv + n_wide)
    def _():
        small_ref[...] = project(wt_ref[...].astype(bf16))


def _in_proj(x2d, gain, w_main, w_tail):
    m, d = x2d.shape
    tm, tn = PROJ_TM, PROJ_TN
    n_qkv = 3 * FOX_WIDTH // tn
    n_wide = WIDE_COLS // tn
    assert n_qkv * tn == 3 * FOX_WIDTH and n_wide * tn == WIDE_COLS
    assert w_main.shape[0] == 3 * FOX_WIDTH + WIDE_COLS and w_tail.shape[0] == SMALL_COLS
    return pl.pallas_call(
        functools.partial(_in_proj_kernel, n_qkv=n_qkv, n_wide=n_wide),
        grid=(m // tm, n_qkv + n_wide + 1),
        in_specs=[
            pl.BlockSpec((tm, d), lambda i, j: (i, 0)),
            pl.BlockSpec((1, d), lambda i, j: (0, 0)),
            pl.BlockSpec((tn, d), lambda i, j: (jnp.minimum(j, n_qkv + n_wide - 1), 0)),
            pl.BlockSpec((SMALL_COLS, d), lambda i, j: (0, 0)),
        ],
        out_specs=[
            pl.BlockSpec((tm, tn), lambda i, j: (i, jnp.minimum(j, n_qkv - 1))),
            pl.BlockSpec((tm, tn), lambda i, j: (i, jnp.clip(j - n_qkv, 0, n_wide - 1))),
            pl.BlockSpec((tm, SMALL_COLS), lambda i, j: (i, 0)),
        ],
        out_shape=[
            jax.ShapeDtypeStruct((m, 3 * FOX_WIDTH), bf16),
            jax.ShapeDtypeStruct((m, WIDE_COLS), f32),
            jax.ShapeDtypeStruct((m, SMALL_COLS), f32),
        ],
        scratch_shapes=[pltpu.VMEM((tm, d), bf16)],
        compiler_params=pltpu.CompilerParams(
            dimension_semantics=("parallel", "arbitrary"),
            vmem_limit_bytes=VMEM_LIMIT),
        name="in_proj",
    )(x2d, gain, w_main, w_tail)


def _norm_matmul(x2d, gain, w_t, col_start, n, out_dtype, tm, tn):
    m, d = x2d.shape
    first = col_start // tn
    assert first * tn == col_start and n % tn == 0
    return pl.pallas_call(
        _norm_matmul_kernel,
        grid=(m // tm, n // tn),
        in_specs=[
            pl.BlockSpec((tm, d), lambda i, j: (i, 0)),
            pl.BlockSpec((1, d), lambda i, j: (0, 0)),
            pl.BlockSpec((tn, d), lambda i, j: (first + j, 0)),
        ],
        out_specs=pl.BlockSpec((tm, tn), lambda i, j: (i, j)),
        out_shape=jax.ShapeDtypeStruct((m, n), out_dtype),
        scratch_shapes=[pltpu.VMEM((tm, d), bf16)],
        compiler_params=pltpu.CompilerParams(
            dimension_semantics=("parallel", "arbitrary"),
            vmem_limit_bytes=VMEM_LIMIT),
        name="norm_matmul",
    )(x2d, gain, w_t)


def _gate_kernel(f_ref, bias_ref, c_ref, carry_ref, *, tb):
    @pl.when(pl.program_id(1) == 0)
    def _():
        carry_ref[...] = jnp.zeros_like(carry_ref)

    z = f_ref[...] + bias_ref[...]
    log_f = -_softplus(-z)
    row = lax.broadcasted_iota(jnp.int32, (tb, tb), 0)
    col = lax.broadcasted_iota(jnp.int32, (tb, tb), 1)
    tri = jnp.where(col <= row, 1.0, 0.0).astype(bf16)
    c = _dot_exact_ones(log_f, tri, nt=True) + carry_ref[0:1, :]
    c_ref[...] = c
    carry_ref[0:1, :] = c[tb - 1:tb, :]


def _gate_cumsum(small, bias_row, batch, seq):
    tb = 512
    nt = seq // tb
    return pl.pallas_call(
        functools.partial(_gate_kernel, tb=tb),
        grid=(batch, nt),
        in_specs=[
            pl.BlockSpec((tb, LANES), lambda b, t: (b * nt + t, 2)),
            pl.BlockSpec((1, LANES), lambda b, t: (0, 0)),
        ],
        out_specs=pl.BlockSpec((tb, LANES), lambda b, t: (b * nt + t, 0)),
        out_shape=jax.ShapeDtypeStruct((batch * seq, LANES), f32),
        scratch_shapes=[pltpu.VMEM((8, LANES), f32)],
        compiler_params=pltpu.CompilerParams(
            dimension_semantics=("parallel", "arbitrary")),
        name="fox_gate_cumsum",
    )(small, bias_row)


LOG2E = 1.4426950408889634
FOX_TK = 256
FOX_BATCH = 4


def _fox_kernel(q_ref, k_ref, v_ref, c_ref, z_ref, o_ref, crep_ref, acc_ref, vt_ref, *, seq, tk):
    h = pl.program_id(1)
    ng = seq // LANES
    nt = seq // tk
    gpt = tk // LANES
    mm = lambda a, b: jnp.dot(a, b, preferred_element_type=f32)
    mm_nt = lambda a, b: lax.dot_general(a, b, (((1,), (1,)), ((), ())), preferred_element_type=f32)

    src_lane = lax.broadcasted_iota(jnp.int32, (LANES, LANES), 0)
    pick = jnp.where(src_lane == h, 1.0, 0.0).astype(bf16)
    crep_ref[...] = _dot_exact_ones(c_ref[...], pick) * LOG2E
    for t in range(nt):
        vt_ref[t] = v_ref[t * tk:(t + 1) * tk, :].T

    key_rel = lax.broadcasted_iota(jnp.int32, (tk, LANES), 0)
    qry_rel = lax.broadcasted_iota(jnp.int32, (tk, LANES), 1)
    future = [key_rel > qry_rel + d * LANES for d in range(gpt)]

    sched = []
    for t in range(nt):
        visible = list(range(t * gpt, ng))
        sched += [[(t, g) for g in visible[k:k + FOX_BATCH]] for k in range(0, len(visible), FOX_BATCH)]

    def qk(batch):
        return [mm_nt(k_ref[t * tk:(t + 1) * tk, :], q_ref[g * LANES:(g + 1) * LANES, :])
                for t, g in batch]

    m = [None] * ng
    l = [None] * ng

    def finish(batch, alpha, pv):
        for (t, g), a, x in zip(batch, alpha, pv):
            acc_ref[g] = x if t == 0 else a * acc_ref[g] + x
            if t == g // gpt:
                rows = slice(g * LANES, (g + 1) * LANES)
                z = z_ref[rows, :]
                o = (acc_ref[g] / l[g]).T
                o_ref[rows, :] = (o * (z * _sigmoid(z))).astype(o_ref.dtype)

    s_next = qk(sched[0])
    pending = None
    for bi, batch in enumerate(sched):
        s_cur = s_next
        if bi + 1 < len(sched):
            s_next = qk(sched[bi + 1])
        alpha, p16 = [], []
        for (t, g), s in zip(batch, s_cur):
            s = s - crep_ref[t * tk:(t + 1) * tk, :]
            if g < (t + 1) * gpt:
                s = jnp.where(future[g - t * gpt], -jnp.inf, s)
            m_tile = jnp.max(s, axis=0, keepdims=True)
            if t == 0:
                m_new, a = m_tile, None
            else:
                m_new = jnp.maximum(m[g], m_tile)
                a = jnp.exp2(m[g] - m_new)
            p = jnp.exp2(s - m_new)
            p_sum = jnp.sum(p, axis=0, keepdims=True)
            l[g] = p_sum if t == 0 else a * l[g] + p_sum
            m[g] = m_new
            alpha.append(a)
            p16.append(p.astype(bf16))
        pv = [mm(vt_ref[t], p) for (t, g), p in zip(batch, p16)]
        if pending is not None:
            finish(*pending)
        pending = (batch, alpha, pv)
    finish(*pending)


def _fox_attention(qkv3, c3, wide3, batch, seq):
    tk = FOX_TK
    h8 = FOX_HEADS
    return pl.pallas_call(
        functools.partial(_fox_kernel, seq=seq, tk=tk),
        grid=(batch, h8),
        in_specs=[
            pl.BlockSpec((None, seq, LANES), lambda b, h: (b, 0, h)),
            pl.BlockSpec((None, seq, LANES), lambda b, h: (b, 0, h8 + h)),
            pl.BlockSpec((None, seq, LANES), lambda b, h: (b, 0, 2 * h8 + h)),
            pl.BlockSpec((None, seq, LANES), lambda b, h: (b, 0, 0)),
            pl.BlockSpec((None, seq, LANES), lambda b, h: (b, 0, COL_ZA + h)),
        ],
        out_specs=pl.BlockSpec((None, seq, LANES), lambda b, h: (b, 0, h)),
        out_shape=jax.ShapeDtypeStruct((batch, seq, FOX_WIDTH), bf16),
        scratch_shapes=[
            pltpu.VMEM((seq, LANES), f32),
            pltpu.VMEM((seq // LANES, LANES, LANES), f32),
            pltpu.VMEM((seq // tk, LANES, tk), bf16),
        ],
        compiler_params=pltpu.CompilerParams(
            dimension_semantics=("parallel", "arbitrary"),
            vmem_limit_bytes=VMEM_LIMIT),
        name="fox_attention",
    )(qkv3, qkv3, qkv3, c3, wide3)


P_MU_R, P_MU_K, P_MU_V, P_MU_Z, P_W0, P_A0, P_KK, P_KA, P_RK, P_LNW, P_LNB = range(11)
P_ROWS = 16


def _rwkv_kernel(r_ref, k_ref, v_ref, z_ref, wd_ref, ad_ref, pc_ref, ps_ref, w2_ref, a2_ref,
                 o_ref, state_ref, prev_ref, *, tb):
    C = RWKV_CHUNK
    G = GROUP_ROWS
    N = RWKV_HEAD_DIM

    @pl.when(pl.program_id(2) == 0)
    def _():
        state_ref[...] = jnp.zeros_like(state_ref)
        prev_ref[...] = jnp.zeros_like(prev_ref)

    row_in_block = lax.broadcasted_iota(jnp.int32, (tb, LANES), 0)

    def shifted(ref, slot, mu):
        u = ref[...]
        prev = jnp.where(row_in_block == 0, prev_ref[slot, 0:1, :], pltpu.roll(u, 1, 0))
        prev_ref[slot, 0:1, :] = u[tb - 1:tb, :]
        return u + (prev - u) * mu

    prm = lambda idx: pc_ref[idx:idx + 1, :]
    r = shifted(r_ref, 0, prm(P_MU_R))
    kr = shifted(k_ref, 1, prm(P_MU_K))
    vr = shifted(v_ref, 2, prm(P_MU_V))
    zb = shifted(z_ref, 3, prm(P_MU_Z))
    wd = shifted(wd_ref, 4, ps_ref[0:1, :])
    ad = shifted(ad_ref, 5, ps_ref[1:2, :])

    w = -_softplus(-(prm(P_W0) + _dot(jnp.tanh(wd), w2_ref[...]))) - 0.5
    log_decay = -jnp.exp(w)
    rate = _sigmoid(prm(P_A0) + _dot(ad, a2_ref[...]))

    li = lax.broadcasted_iota(jnp.int32, (LANES, LANES), 0)
    lj = lax.broadcasted_iota(jnp.int32, (LANES, LANES), 1)
    head_ones = jnp.where(li // N == lj // N, 1.0, 0.0).astype(bf16)
    head_sum = lambda x: _dot_exact_ones(x, head_ones)

    kk = kr * prm(P_KK)
    kk = kk / jnp.maximum(jnp.sqrt(head_sum(kk * kk)), L2_EPS)
    kp = kr * (1.0 + (rate - 1.0) * prm(P_KA))
    bb = kk * rate
    bonus = head_sum(r * kp * prm(P_RK)) * vr

    ti = lax.broadcasted_iota(jnp.int32, (tb, tb), 0)
    tj = lax.broadcasted_iota(jnp.int32, (tb, tb), 1)
    chunk_tri = jnp.where(jnp.logical_and(tj <= ti, ti // C == tj // C), 1.0, 0.0).astype(bf16)
    ci = _dot_exact_ones(log_decay, chunk_tri, nt=True)

    a_t = -kk * jnp.exp(ci - log_decay)
    r_t = r * jnp.exp(ci)
    inv = jnp.exp(-ci)
    b_t = bb * inv
    k_t = kp * inv

    lane = lax.broadcasted_iota(jnp.int32, (1, LANES), 1)
    head_masks = [jnp.where(lane // N == h, 1.0, 0.0) for h in range(HEADS_PER_GROUP)]
    stack = lambda x: jnp.concatenate([x * hm for hm in head_masks], axis=0)
    stack16 = lambda x: stack(x).astype(bf16)

    gi = lax.broadcasted_iota(jnp.int32, (G, G), 0)
    gj = lax.broadcasted_iota(jnp.int32, (G, G), 1)
    same_head = gi // C == gj // C
    strict = jnp.logical_and(same_head, gj < gi)
    incl = jnp.logical_and(same_head, gj <= gi)
    eye = jnp.where(gi == gj, 1.0, 0.0)

    mm = lambda a, b: jnp.dot(a, b, preferred_element_type=f32)
    mm_nt = lambda a, b: lax.dot_general(a, b, (((1,), (1,)), ((), ())), preferred_element_type=f32)
    mm_tn = lambda a, b: lax.dot_general(a, b, (((0,), (0,)), ((), ())), preferred_element_type=f32)

    def independent(chunks):
        n = range(len(chunks))
        rows = [slice(c * C, (c + 1) * C) for c in chunks]
        c_last = [ci[sl][C - 1:C, :] for sl in rows]
        to_end = [jnp.exp(c_last[i] - ci[rows[i]]) for i in n]
        xr32 = [stack(r_t[sl]) for sl in rows]
        xa = [stack16(a_t[sl]) for sl in rows]
        xr = [x.astype(bf16) for x in xr32]
        yb = [stack16(b_t[sl]) for sl in rows]
        yk = [stack16(k_t[sl]) for sl in rows]
        vs = [stack16(vr[sl]) for sl in rows]
        bh = [stack16(bb[rows[i]] * to_end[i]) for i in n]
        kh = [stack16(kp[rows[i]] * to_end[i]) for i in n]

        big = [mm_nt(jnp.concatenate([xa[i], xr[i]], axis=0), jnp.concatenate([yb[i], yk[i]], axis=0))
               for i in n]
        a_ab = [jnp.where(strict, big[i][0:G, 0:G], 0.0) for i in n]
        a_ak = [jnp.where(strict, big[i][0:G, G:2 * G], 0.0).astype(bf16) for i in n]
        a_rb = [jnp.where(incl, big[i][G:2 * G, 0:G], 0.0).astype(bf16) for i in n]
        a_rk = [jnp.where(incl, big[i][G:2 * G, G:2 * G], 0.0).astype(bf16) for i in n]

        tinv = [eye + a for a in a_ab]
        pw = [a.astype(bf16) for a in a_ab]
        for _ in range(5):
            pw = [mm(p, p).astype(bf16) for p in pw]
            tinv = [tinv[i] + mm(tinv[i].astype(bf16), pw[i]) for i in n]
        tinv = [t.astype(bf16) for t in tinv]

        akv = [mm(a_ak[i], vs[i]).astype(bf16) for i in n]
        wu = [mm(tinv[i], jnp.concatenate([xa[i], akv[i]], axis=1)).astype(bf16) for i in n]
        e = [mm(a_rb[i], wu[i]) for i in n]
        ry = [(xr32[i] + e[i][:, 0:LANES]).astype(bf16) for i in n]
        y_loc = [e[i][:, LANES:] + mm(a_rk[i], vs[i]) for i in n]
        pm = [mm_tn(wu[i][:, 0:LANES], bh[i]).astype(bf16) for i in n]
        q = [mm_tn(jnp.concatenate([wu[i][:, LANES:], vs[i]], axis=0),
                   jnp.concatenate([bh[i], kh[i]], axis=0)) for i in n]
        return [dict(ry=ry[i], y_loc=y_loc[i], pm=pm[i], q=q[i], decay=jnp.exp(c_last[i])) for i in n]

    def sequential(state, parts):
        out = []
        for p in parts:
            s16 = state.astype(bf16)
            y_st = mm_nt(p["ry"], s16) + p["y_loc"]
            out.append(y_st[0:C] + y_st[C:2 * C])
            state = state * p["decay"] + mm(s16, p["pm"]) + p["q"]
        return state, out

    n_chunks = tb // C
    groups = [list(range(g, min(g + RWKV_GROUP, n_chunks))) for g in range(0, n_chunks, RWKV_GROUP)]
    state = state_ref[...]
    ys = []
    parts = independent(groups[0])
    for g in groups[1:]:
        state, out = sequential(state, parts)
        ys += out
        parts = independent(g)
    state, out = sequential(state, parts)
    ys += out
    state_ref[...] = state

    y = jnp.concatenate(ys, axis=0)
    mean = head_sum(y) * (1.0 / N)
    yc = y - mean
    var = head_sum(yc * yc) * (1.0 / N)
    yn = yc * lax.rsqrt(var + GN_EPS) * prm(P_LNW) + prm(P_LNB)
    o_ref[...] = ((yn + bonus) * (zb * _sigmoid(zb))).astype(o_ref.dtype)


def _rwkv_mix_serial(wide, small, pc, ps, w2p, a2p, batch, seq):
    tb = RWKV_BLOCK
    nt = seq // tb
    groups = RWKV_WIDTH // LANES
    row = lambda b, g, t: b * nt + t
    return pl.pallas_call(
        functools.partial(_rwkv_kernel, tb=tb),
        grid=(batch, groups, nt),
        in_specs=[
            pl.BlockSpec((tb, LANES), lambda b, g, t: (row(b, g, t), COL_R + g)),
            pl.BlockSpec((tb, LANES), lambda b, g, t: (row(b, g, t), COL_K + g)),
            pl.BlockSpec((tb, LANES), lambda b, g, t: (row(b, g, t), COL_V + g)),
            pl.BlockSpec((tb, LANES), lambda b, g, t: (row(b, g, t), COL_ZB + g)),
            pl.BlockSpec((tb, LANES), lambda b, g, t: (row(b, g, t), 0)),
            pl.BlockSpec((tb, LANES), lambda b, g, t: (row(b, g, t), 1)),
            pl.BlockSpec((P_ROWS, LANES), lambda b, g, t: (0, g)),
            pl.BlockSpec((8, LANES), lambda b, g, t: (0, 0)),
            pl.BlockSpec((LANES, LANES), lambda b, g, t: (0, g)),
            pl.BlockSpec((LANES, LANES), lambda b, g, t: (0, g)),
        ],
        out_specs=pl.BlockSpec((tb, LANES), lambda b, g, t: (row(b, g, t), g)),
        out_shape=jax.ShapeDtypeStruct((batch * seq, RWKV_WIDTH), bf16),
        scratch_shapes=[
            pltpu.VMEM((LANES, LANES), f32),
            pltpu.VMEM((6, 8, LANES), f32),
        ],
        compiler_params=pltpu.CompilerParams(
            dimension_semantics=("parallel", "parallel", "arbitrary"),
            vmem_limit_bytes=VMEM_LIMIT),
        name="rwkv_mix",
    )(wide, wide, wide, wide, small, small, pc, ps, w2p, a2p)


K_XA, K_XR, K_YB, K_YK, K_VS, K_BH, K_KH = range(7)


def _spread(emitters, lo=0.0, hi=1.0):
    n = len(emitters)
    return [(lo + (hi - lo) * (i + 0.5) / n, e) for i, e in enumerate(emitters)]


def _interleave(*segment_lists):
    keyed = [(pos, prio, seg) for prio, segs in enumerate(segment_lists) for pos, seg in segs]
    keyed.sort(key=lambda x: (x[0], x[1]))
    for _, _, seg in keyed:
        seg()


def _rwkv_pipe_kernel(r_ref, k_ref, v_ref, z_ref, wd_ref, ad_ref, pc_ref, ps_ref, w2_ref, a2_ref, pc3_ref,
                      o_ref,
                      state_ref, prev_ref, stk_ref, xr32_ref, aux1_ref, dec1_ref, p16_ref, p32_ref,
                      aux2_ref, dec2_ref, *, tb, nt):
    C = RWKV_CHUNK
    G = GROUP_ROWS
    N = RWKV_HEAD_DIM
    NC = tb // C
    s = pl.program_id(0)
    first_of_seq_1 = (s % nt) == 0
    first_of_seq_3 = ((s + 2 * nt - 2) % nt) == 0

    @pl.when(s == 0)
    def _():
        for ref in (state_ref, prev_ref, stk_ref, xr32_ref, aux1_ref, dec1_ref, p16_ref, p32_ref,
                    aux2_ref, dec2_ref):
            ref[...] = jnp.zeros_like(ref)

    mm = lambda a, b: jnp.dot(a, b, preferred_element_type=f32)
    mm_nt = lambda a, b: lax.dot_general(a, b, (((1,), (1,)), ((), ())), preferred_element_type=f32)
    mm_tn = lambda a, b: lax.dot_general(a, b, (((0,), (0,)), ((), ())), preferred_element_type=f32)

    def body(cur, prv):
        li = lax.broadcasted_iota(jnp.int32, (LANES, LANES), 0)
        lj = lax.broadcasted_iota(jnp.int32, (LANES, LANES), 1)
        head_ones = jnp.where(li // N == lj // N, 1.0, 0.0).astype(bf16)
        head_sum = lambda x: _dot_exact_ones(x, head_ones, terms=2)
        lane = lax.broadcasted_iota(jnp.int32, (1, LANES), 1)
        head_masks = [jnp.where(lane // N == h, 1.0, 0.0) for h in range(HEADS_PER_GROUP)]
        stack = lambda x: jnp.concatenate([x * hm for hm in head_masks], axis=0)
        lane_c = lax.broadcasted_iota(jnp.int32, (C, LANES), 1)
        head_sel = [lane_c // N == h for h in range(HEADS_PER_GROUP)]

        def stack16(x):
            xb = x.astype(bf16)
            return jnp.concatenate([jnp.where(m, xb, jnp.zeros_like(xb)) for m in head_sel], axis=0)

        gi = lax.broadcasted_iota(jnp.int32, (G, G), 0)
        gj = lax.broadcasted_iota(jnp.int32, (G, G), 1)
        same_head = gi // C == gj // C
        strict = jnp.logical_and(same_head, gj < gi)
        incl = jnp.logical_and(same_head, gj <= gi)
        eye = jnp.where(gi == gj, 1.0, 0.0)
        chunks = range(NC)

        v1 = {}
        prm = lambda idx: pc_ref[idx:idx + 1, :]
        row_in_block = lax.broadcasted_iota(jnp.int32, (tb, LANES), 0)

        def shifted(ref, slot, mu):
            u = ref[...]
            carry = jnp.where(first_of_seq_1, 0.0, prev_ref[slot, 0:1, :])
            prev = jnp.where(row_in_block == 0, carry, pltpu.roll(u, 1, 0))
            prev_ref[slot, 0:1, :] = u[tb - 1:tb, :]
            return u + (prev - u) * mu

        def s1_lora():
            v1["wd"] = shifted(wd_ref, 4, ps_ref[0:1, :])
            v1["ad"] = shifted(ad_ref, 5, ps_ref[1:2, :])
            v1["w_lin"] = _dot(jnp.tanh(v1["wd"]), w2_ref[...])
            v1["a_lin"] = _dot(v1["ad"], a2_ref[...])

        def s1_key_norm():
            v1["kr"] = shifted(k_ref, 1, prm(P_MU_K))
            kk = v1["kr"] * prm(P_KK)
            v1["kk_raw"] = kk
            v1["kk_ss"] = head_sum(kk * kk)

        def s1_decay():
            w = -_softplus(-(prm(P_W0) + v1["w_lin"])) - 0.5
            v1["log_decay"] = -jnp.exp(w)
            ti = lax.broadcasted_iota(jnp.int32, (2 * C, 2 * C), 0)
            tj = lax.broadcasted_iota(jnp.int32, (2 * C, 2 * C), 1)
            pair_tri = jnp.where(jnp.logical_and(tj <= ti, ti // C == tj // C), 1.0, 0.0).astype(bf16)
            v1["ci"] = jnp.concatenate(
                [_dot_exact_ones(v1["log_decay"][k:k + 2 * C], pair_tri, nt=True)
                 for k in range(0, tb, 2 * C)], axis=0)

        def s1_bonus():
            v1["rate"] = _sigmoid(prm(P_A0) + v1["a_lin"])
            v1["r"] = shifted(r_ref, 0, prm(P_MU_R))
            v1["vr"] = shifted(v_ref, 2, prm(P_MU_V))
            v1["kp"] = v1["kr"] * (1.0 + (v1["rate"] - 1.0) * prm(P_KA))
            aux1_ref[cur, 0] = head_sum(v1["r"] * v1["kp"] * prm(P_RK)) * v1["vr"]
            zb = shifted(z_ref, 3, prm(P_MU_Z))
            aux1_ref[cur, 1] = zb * _sigmoid(zb)

        def s1_scale():
            kk = v1["kk_raw"] / jnp.maximum(jnp.sqrt(v1["kk_ss"]), L2_EPS)
            v1["bb"] = kk * v1["rate"]
            ci = v1["ci"]
            v1["a_t"] = -kk * jnp.exp(ci - v1["log_decay"])
            v1["r_t"] = v1["r"] * jnp.exp(ci)
            inv = jnp.exp(-ci)
            v1["b_t"] = v1["bb"] * inv
            v1["k_t"] = v1["kp"] * inv

        def s1_stack(c):
            def emit():
                sl = slice(c * C, (c + 1) * C)
                ci_c = v1["ci"][sl]
                c_last = ci_c[C - 1:C, :]
                to_end = jnp.exp(c_last - ci_c)
                dec1_ref[cur, c] = jnp.broadcast_to(jnp.exp(c_last), (8, LANES))
                xr = stack(v1["r_t"][sl])
                xr32_ref[cur, c] = xr
                stk_ref[cur, K_XR, c] = xr.astype(bf16)
                stk_ref[cur, K_XA, c] = stack16(v1["a_t"][sl])
                stk_ref[cur, K_YB, c] = stack16(v1["b_t"][sl])
                stk_ref[cur, K_YK, c] = stack16(v1["k_t"][sl])
                stk_ref[cur, K_VS, c] = stack16(v1["vr"][sl])
                stk_ref[cur, K_BH, c] = stack16(v1["bb"][sl] * to_end)
                stk_ref[cur, K_KH, c] = stack16(v1["kp"][sl] * to_end)
            return emit

        segs1 = ([(0.0, s1_lora), (0.04, s1_key_norm), (0.10, s1_decay), (0.16, s1_bonus)]
                 + _spread([s1_scale] + [s1_stack(c) for c in chunks], RWKV_TAIL_START, 1.0))

        v2 = {}
        ld = lambda kind, c: stk_ref[prv, kind, c]

        def s2_big():
            big = [mm_nt(jnp.concatenate([ld(K_XA, c), ld(K_XR, c)], axis=0),
                         jnp.concatenate([ld(K_YB, c), ld(K_YK, c)], axis=0)) for c in chunks]
            v2["a_ab"] = [jnp.where(strict, big[c][0:G, 0:G], 0.0) for c in chunks]
            v2["a_ak"] = [jnp.where(strict, big[c][0:G, G:2 * G], 0.0).astype(bf16) for c in chunks]
            v2["a_rb"] = [jnp.where(incl, big[c][G:2 * G, 0:G], 0.0).astype(bf16) for c in chunks]
            v2["a_rk"] = [jnp.where(incl, big[c][G:2 * G, G:2 * G], 0.0).astype(bf16) for c in chunks]
            v2["tinv"] = [eye + a for a in v2["a_ab"]]
            v2["pw"] = [a.astype(bf16) for a in v2["a_ab"]]

        def s2_first_square():
            v2["pw"] = [mm(p, p).astype(bf16) for p in v2["pw"]]

        def s2_level(last):
            def emit():
                for c in chunks:
                    pw = v2["pw"][c]
                    t16 = v2["tinv"][c].astype(bf16)
                    if last:
                        v2["tinv"][c] = v2["tinv"][c] + mm(t16, pw)
                    else:
                        both = mm(jnp.concatenate([t16, pw], axis=0), pw)
                        v2["tinv"][c] = v2["tinv"][c] + both[0:G]
                        v2["pw"][c] = both[G:2 * G].astype(bf16)
            return emit

        def s2_av():
            v2["tinv"] = [t.astype(bf16) for t in v2["tinv"]]
            av = [mm(jnp.concatenate([v2["a_ak"][c], v2["a_rk"][c]], axis=0), ld(K_VS, c)) for c in chunks]
            v2["akv"] = [x[0:G].astype(bf16) for x in av]
            v2["arkv"] = [x[G:2 * G] for x in av]

        def s2_wu():
            v2["wu"] = [mm(v2["tinv"][c], jnp.concatenate([ld(K_XA, c), v2["akv"][c]], axis=1)).astype(bf16)
                        for c in chunks]

        def s2_ry():
            for c in chunks:
                e = mm(v2["a_rb"][c], v2["wu"][c])
                p16_ref[cur, 0, c] = (xr32_ref[prv, c] + e[:, 0:LANES]).astype(bf16)
                p32_ref[cur, 0, c] = e[:, LANES:] + v2["arkv"][c]

        def s2_pm():
            for c in chunks:
                p16_ref[cur, 1, c] = mm_tn(v2["wu"][c][:, 0:LANES], ld(K_BH, c)).astype(bf16)

        def s2_q():
            for c in chunks:
                p32_ref[cur, 1, c] = mm_tn(jnp.concatenate([v2["wu"][c][:, LANES:], ld(K_VS, c)], axis=0),
                                           jnp.concatenate([ld(K_BH, c), ld(K_KH, c)], axis=0))
            aux2_ref[cur] = aux1_ref[prv]
            dec2_ref[cur] = dec1_ref[prv]

        segs2 = _spread([s2_big, s2_first_square] + [s2_level(False)] * 4 + [s2_level(True)]
                        + [s2_av, s2_wu, s2_ry, s2_pm, s2_q])

        v3 = {"ys": []}
        prm3 = lambda idx: pc3_ref[idx:idx + 1, :]

        def s3_start():
            v3["state"] = jnp.where(first_of_seq_3, 0.0, state_ref[...])

        def s3_chunk(c):
            def emit():
                state = v3["state"]
                s16 = state.astype(bf16)
                y_st = mm_nt(p16_ref[prv, 0, c], s16) + p32_ref[prv, 0, c]
                v3["ys"].append(y_st[0:C] + y_st[C:2 * C])
                v3["state"] = (state * dec2_ref[prv, c, 0:1, :] + mm(s16, p16_ref[prv, 1, c])
                               + p32_ref[prv, 1, c])
            return emit

        def s3_finish():
            state_ref[...] = v3["state"]
            y = jnp.concatenate(v3["ys"], axis=0)
            mean = head_sum(y) * (1.0 / N)
            yc = y - mean
            var = head_sum(yc * yc) * (1.0 / N)
            yn = yc * lax.rsqrt(var + GN_EPS) * prm3(P_LNW) + prm3(P_LNB)
            o_ref[...] = ((yn + aux2_ref[prv, 0]) * aux2_ref[prv, 1]).astype(o_ref.dtype)

        segs3 = _spread([s3_start] + [s3_chunk(c) for c in chunks] + [s3_finish])

        _interleave(segs3, segs2, segs1)

    for parity in (0, 1):
        @pl.when(s % 2 == parity)
        def _(parity=parity):
            body(cur=parity, prv=1 - parity)


def _rwkv_mix(wide, small, pc, ps, w2p, a2p, batch, seq):
    tb = RWKV_BLOCK
    nt = seq // tb
    groups = RWKV_WIDTH // LANES
    nc = tb // RWKV_CHUNK
    n_blocks = batch * groups * nt

    def where(s):
        s = jnp.clip(s, 0, n_blocks - 1)
        return (s // (nt * groups)) * nt + s % nt, (s // nt) % groups

    col = lambda c0: (lambda s: (where(s)[0], c0 + where(s)[1]))
    fixed_col = lambda c: (lambda s: (where(s)[0], c))
    grp = lambda s: (0, where(s)[1])
    return pl.pallas_call(
        functools.partial(_rwkv_pipe_kernel, tb=tb, nt=nt),
        grid=(n_blocks + 2,),
        in_specs=[
            pl.BlockSpec((tb, LANES), col(COL_R)),
            pl.BlockSpec((tb, LANES), col(COL_K)),
            pl.BlockSpec((tb, LANES), col(COL_V)),
            pl.BlockSpec((tb, LANES), col(COL_ZB)),
            pl.BlockSpec((tb, LANES), fixed_col(0)),
            pl.BlockSpec((tb, LANES), fixed_col(1)),
            pl.BlockSpec((P_ROWS, LANES), grp),
            pl.BlockSpec((8, LANES), lambda s: (0, 0)),
            pl.BlockSpec((LANES, LANES), grp),
            pl.BlockSpec((LANES, LANES), grp),
            pl.BlockSpec((P_ROWS, LANES), lambda s: grp(s - 2)),
        ],
        out_specs=pl.BlockSpec((tb, LANES), lambda s: (where(s - 2)[0], where(s - 2)[1])),
        out_shape=jax.ShapeDtypeStruct((batch * seq, RWKV_WIDTH), bf16),
        scratch_shapes=[
            pltpu.VMEM((LANES, LANES), f32),
            pltpu.VMEM((6, 8, LANES), f32),
            pltpu.VMEM((2, 7, nc, GROUP_ROWS, LANES), bf16),
            pltpu.VMEM((2, nc, GROUP_ROWS, LANES), f32),
            pltpu.VMEM((2, 2, tb, LANES), f32),
            pltpu.VMEM((2, nc, 8, LANES), f32),
            pltpu.VMEM((2, 2, nc, GROUP_ROWS, LANES), bf16),
            pltpu.VMEM((2, 2, nc, GROUP_ROWS, LANES), f32),
            pltpu.VMEM((2, 2, tb, LANES), f32),
            pltpu.VMEM((2, nc, 8, LANES), f32),
        ],
        compiler_params=pltpu.CompilerParams(
            dimension_semantics=("arbitrary",),
            vmem_limit_bytes=VMEM_LIMIT),
        name="rwkv_mix",
    )(wide, wide, wide, wide, small, small, pc, ps, w2p, a2p, pc)


def _out_kernel(oa_ref, ob_ref, ga_ref, gb_ref, x_ref, wpf_ref, wpr_ref, wo_ref, g_ref, o_ref):
    pa = jnp.dot(oa_ref[...], wpf_ref[...], preferred_element_type=f32)
    pb = jnp.dot(ob_ref[...], wpr_ref[...], preferred_element_type=f32)
    m = _sigmoid(ga_ref[...]) * pa + _sigmoid(gb_ref[...]) * pb
    z = x_ref[...] + jnp.dot(m.astype(bf16), wo_ref[...], preferred_element_type=f32)
    ms = jnp.mean(z * z, axis=-1, keepdims=True)
    o_ref[...] = z * lax.rsqrt(ms + RMS_EPS) * g_ref[...]


def _merge_out(oa, ob, wide, x2d, wpf, wpr, wo, gain):
    m, d = x2d.shape
    tm = 256
    resident = lambda shape: pl.BlockSpec(shape, lambda i: (0, 0), pipeline_mode=pl.Buffered(1))
    return pl.pallas_call(
        _out_kernel,
        grid=(m // tm,),
        in_specs=[
            pl.BlockSpec((tm, FOX_WIDTH), lambda i: (i, 0)),
            pl.BlockSpec((tm, RWKV_WIDTH), lambda i: (i, 0)),
            pl.BlockSpec((tm, d), lambda i: (i, COL_GA * LANES // d)),
            pl.BlockSpec((tm, d), lambda i: (i, COL_GB * LANES // d)),
            pl.BlockSpec((tm, d), lambda i: (i, 0)),
            resident((FOX_WIDTH, d)),
            resident((RWKV_WIDTH, d)),
            resident((d, d)),
            resident((1, d)),
        ],
        out_specs=pl.BlockSpec((tm, d), lambda i: (i, 0)),
        out_shape=jax.ShapeDtypeStruct((m, d), f32),
        compiler_params=pltpu.CompilerParams(
            dimension_semantics=("parallel",),
            vmem_limit_bytes=VMEM_LIMIT),
        name="merge_out",
    )(oa, ob, wide, wide, x2d, wpf, wpr, wo, gain)


def _pad_cols(a, width):
    return jnp.pad(a, ((0, 0), (0, width - a.shape[1])))


def _layer(x2d, batch, seq, norm_gain, w_in, fox_forget_bias, rwkv_shift_mix, rwkv_w0, rwkv_w2,
           rwkv_a0, rwkv_a2, rwkv_k_k, rwkv_k_a, rwkv_r_k, rwkv_ln_w, rwkv_ln_b,
           w_proj_fox, w_proj_rwkv, w_out):
    fw, rw = FOX_WIDTH, RWKV_WIDTH
    w_t = w_in.T
    r_qkv = 0
    r_za = r_qkv + 3 * fw
    r_f = r_za + fw
    r_rkvz = r_f + FOX_HEADS
    r_wd = r_rkvz + 4 * rw
    r_ad = r_wd + LORA
    r_g = r_ad + LORA
    w_main = _relayout_weights(
        w_t, [(r_qkv, 3 * fw), (r_g, 2 * D_MODEL), (r_za, fw), (r_rkvz, 4 * rw)],
        scaled_rows=fw, scale=FOX_HEAD_DIM ** -0.5 * LOG2E)
    pad_rows = lambda a: jnp.pad(a, ((0, LANES - a.shape[0]), (0, 0)))
    w_tail = jnp.concatenate([pad_rows(w_t[r_wd:r_wd + LORA]), pad_rows(w_t[r_ad:r_ad + LORA]),
                              pad_rows(w_t[r_f:r_f + FOX_HEADS])], axis=0)
    gain = norm_gain.reshape(1, D_MODEL)

    qkv, wide, small = _in_proj(x2d, gain, w_main, w_tail)

    bias_row = _pad_cols(fox_forget_bias.reshape(1, FOX_HEADS), LANES)
    c = _gate_cumsum(small, bias_row, batch, seq)
    qkv3 = qkv.reshape(batch, seq, 3 * fw)
    oa = _fox_attention(qkv3, c.reshape(batch, seq, LANES), wide.reshape(batch, seq, WIDE_COLS),
                        batch, seq).reshape(batch * seq, fw)

    mu = rwkv_shift_mix
    pc = jnp.zeros((P_ROWS, rw), f32)
    rows = [mu[0:rw], mu[rw:2 * rw], mu[2 * rw:3 * rw], mu[3 * rw:4 * rw], rwkv_w0, rwkv_a0,
            rwkv_k_k, rwkv_k_a, rwkv_r_k.reshape(rw), rwkv_ln_w, rwkv_ln_b]
    pc = pc.at[:len(rows)].set(jnp.stack(rows))
    ps = jnp.zeros((8, LANES), f32)
    ps = ps.at[0, :LORA].set(mu[4 * rw:4 * rw + LORA]).at[1, :LORA].set(mu[4 * rw + LORA:])
    w2p = jnp.pad(rwkv_w2, ((0, LANES - LORA), (0, 0))).astype(bf16)
    a2p = jnp.pad(rwkv_a2, ((0, LANES - LORA), (0, 0))).astype(bf16)
    ob = _rwkv_mix(wide, small, pc, ps, w2p, a2p, batch, seq)

    return oa, ob, wide


def kernel(x, norm_gain, w_in, fox_forget_bias, rwkv_shift_mix, rwkv_w0, rwkv_w2, rwkv_a0, rwkv_a2, rwkv_k_k, rwkv_k_a, rwkv_r_k, rwkv_ln_w, rwkv_ln_b, w_proj_fox, w_proj_rwkv, w_out, final_norm_gain):
    batch, seq, d = x.shape
    depth = norm_gain.shape[0]
    assert depth == 1, "the final rmsnorm is fused into the single layer's output kernel"
    x2d = x.reshape(batch * seq, d)
    oa, ob, wide = _layer(x2d, batch, seq, norm_gain[0], w_in[0], fox_forget_bias[0],
                          rwkv_shift_mix[0], rwkv_w0[0], rwkv_w2[0], rwkv_a0[0], rwkv_a2[0],
                          rwkv_k_k[0], rwkv_k_a[0], rwkv_r_k[0], rwkv_ln_w[0], rwkv_ln_b[0],
                          w_proj_fox[0], w_proj_rwkv[0], w_out[0])
    out = _merge_out(oa, ob, wide, x2d, w_proj_fox[0].astype(bf16), w_proj_rwkv[0].astype(bf16),
                     w_out[0].astype(bf16), final_norm_gain.reshape(1, d))
    return out.reshape(batch, seq, d)
```

```python
import functools

import jax
import jax.numpy as jnp
from jax import lax
from jax.experimental import pallas as pl
from jax.experimental.pallas import tpu as pltpu

D_MODEL = 2048
FOX_HEADS = 8
FOX_HEAD_DIM = 128
FOX_WIDTH = FOX_HEADS * FOX_HEAD_DIM
RWKV_HEADS = 16
RWKV_HEAD_DIM = 64
RWKV_WIDTH = RWKV_HEADS * RWKV_HEAD_DIM
LORA = 96
RMS_EPS = 1e-6
GN_EPS = 64e-5
L2_EPS = 1e-12

LANES = 128
SUBLANES = 8
VMEM_LIMIT = 56 * 1024 * 1024

COL_GA = 0
COL_GB = D_MODEL // LANES
COL_ZA = 2 * D_MODEL // LANES
COL_R = COL_ZA + FOX_WIDTH // LANES
COL_K = COL_R + RWKV_WIDTH // LANES
COL_V = COL_K + RWKV_WIDTH // LANES
COL_ZB = COL_V + RWKV_WIDTH // LANES
WIDE_COLS = (COL_ZB + RWKV_WIDTH // LANES) * LANES
SMALL_COLS = 3 * LANES

RWKV_CHUNK = 64
RWKV_BLOCK = 512
RWKV_TAIL_START = 0.3
HEADS_PER_GROUP = LANES // RWKV_HEAD_DIM
GROUP_ROWS = HEADS_PER_GROUP * RWKV_CHUNK

f32 = jnp.float32
bf16 = jnp.bfloat16


def _dot(a, b):
    return jnp.dot(a.astype(bf16), b.astype(bf16), preferred_element_type=f32)


def _dot_exact_ones(a, ones_bf16, nt=False, terms=3):
    if nt:
        mm = lambda x: lax.dot_general(ones_bf16, x, (((1,), (0,)), ((), ())),
                                       preferred_element_type=f32)
    else:
        mm = lambda x: jnp.dot(x, ones_bf16, preferred_element_type=f32)
    part = a.astype(bf16)
    out = mm(part)
    rest = a
    for _ in range(terms - 1):
        rest = rest - part.astype(f32)
        part = rest.astype(bf16)
        out = out + mm(part)
    return out


def _softplus(x):
    return jnp.maximum(x, 0.0) + jnp.log(1.0 + jnp.exp(-jnp.abs(x)))


def _sigmoid(x):
    return 1.0 / (1.0 + jnp.exp(-x))


PROJ_TM = 1024
PROJ_TN = 1024


def _relayout_kernel(w_ref, o_ref, *, scaled_blocks, scale):
    factor = jnp.where(pl.program_id(0) < scaled_blocks, scale, 1.0)
    o_ref[...] = (w_ref[...] * factor).astype(o_ref.dtype)


def _relayout_weights(w_t, pieces, scaled_rows, scale):
    d = w_t.shape[1]
    rb = PROJ_TN
    assert scaled_rows % rb == 0 and all(n % rb == 0 and src % SUBLANES == 0 for src, n in pieces)
    starts, first_block = [], 0
    for src, n in pieces:
        starts.append((first_block, src))
        first_block += n // rb

    def src_row(i):
        tile = jnp.int32(0)
        for blk, src in starts:
            tile = jnp.where(i >= blk, src // SUBLANES + (i - blk) * (rb // SUBLANES), tile)
        return pl.multiple_of(tile * SUBLANES, SUBLANES)

    return pl.pallas_call(
        functools.partial(_relayout_kernel, scaled_blocks=scaled_rows // rb, scale=scale),
        grid=(first_block,),
        in_specs=[pl.BlockSpec((pl.Element(rb), pl.Element(d)), lambda i: (src_row(i), 0))],
        out_specs=pl.BlockSpec((rb, d), lambda i: (i, 0)),
        out_shape=jax.ShapeDtypeStruct((first_block * rb, d), bf16),
        compiler_params=pltpu.CompilerParams(
            dimension_semantics=("parallel",),
            vmem_limit_bytes=VMEM_LIMIT),
        name="relayout_weights",
    )(w_t)


def _in_proj_kernel(x_ref, g_ref, w_ref, wt_ref, qkv_ref, wide_ref, small_ref, h_ref, *, n_qkv, n_wide):
    j = pl.program_id(1)

    @pl.when(j == 0)
    def _():
        xf = x_ref[...]
        ms = jnp.mean(xf * xf, axis=-1, keepdims=True)
        h_ref[...] = (xf * lax.rsqrt(ms + RMS_EPS) * g_ref[...]).astype(bf16)

    project = lambda w: lax.dot_general(h_ref[...], w, (((1,), (1,)), ((), ())), preferred_element_type=f32)

    @pl.when(j < n_qkv)
    def _():
        qkv_ref[...] = project(w_ref[...]).astype(qkv_ref.dtype)

    @pl.when(jnp.logical_and(j >= n_qkv, j < n_qkv + n_wide))
    def _():
        wide_ref[...] = project(w_ref[...])

    @pl.when(j == n_qkv + n_wide)
    def _():
        small_ref[...] = project(wt_ref[...].astype(bf16))


def _in_proj(x2d, gain, w_main, w_tail):
    m, d = x2d.shape
    tm, tn = PROJ_TM, PROJ_TN
    n_qkv = 3 * FOX_WIDTH // tn
    n_wide = WIDE_COLS // tn
    assert n_qkv * tn == 3 * FOX_WIDTH and n_wide * tn == WIDE_COLS
    assert w_main.shape[0] == 3 * FOX_WIDTH + WIDE_COLS and w_tail.shape[0] == SMALL_COLS
    return pl.pallas_call(
        functools.partial(_in_proj_kernel, n_qkv=n_qkv, n_wide=n_wide),
        grid=(m // tm, n_qkv + n_wide + 1),
        in_specs=[
            pl.BlockSpec((tm, d), lambda i, j: (i, 0)),
            pl.BlockSpec((1, d), lambda i, j: (0, 0)),
            pl.BlockSpec((tn, d), lambda i, j: (jnp.minimum(j, n_qkv + n_wide - 1), 0)),
            pl.BlockSpec((SMALL_COLS, d), lambda i, j: (0, 0)),
        ],
        out_specs=[
            pl.BlockSpec((tm, tn), lambda i, j: (i, jnp.minimum(j, n_qkv - 1))),
            pl.BlockSpec((tm, tn), lambda i, j: (i, jnp.clip(j - n_qkv, 0, n_wide - 1))),
            pl.BlockSpec((tm, SMALL_COLS), lambda i, j: (i, 0)),
        ],
        out_shape=[
            jax.ShapeDtypeStruct((m, 3 * FOX_WIDTH), bf16),
            jax.ShapeDtypeStruct((m, WIDE_COLS), f32),
            jax.ShapeDtypeStruct((m, SMALL_COLS), f32),
        ],
        scratch_shapes=[pltpu.VMEM((tm, d), bf16)],
        compiler_params=pltpu.CompilerParams(
            dimension_semantics=("parallel", "arbitrary"),
            vmem_limit_bytes=VMEM_LIMIT),
        name="in_proj",
    )(x2d, gain, w_main, w_tail)


def _gate_kernel(f_ref, bias_ref, c_ref, carry_ref, *, tb):
    @pl.when(pl.program_id(1) == 0)
    def _():
        carry_ref[...] = jnp.zeros_like(carry_ref)

    z = f_ref[...] + bias_ref[...]
    log_f = -_softplus(-z)
    row = lax.broadcasted_iota(jnp.int32, (tb, tb), 0)
    col = lax.broadcasted_iota(jnp.int32, (tb, tb), 1)
    tri = jnp.where(col <= row, 1.0, 0.0).astype(bf16)
    c = _dot_exact_ones(log_f, tri, nt=True) + carry_ref[0:1, :]
    c_ref[...] = c
    carry_ref[0:1, :] = c[tb - 1:tb, :]


def _gate_cumsum(small, bias_row, batch, seq):
    tb = 512
    nt = seq // tb
    return pl.pallas_call(
        functools.partial(_gate_kernel, tb=tb),
        grid=(batch, nt),
        in_specs=[
            pl.BlockSpec((tb, LANES), lambda b, t: (b * nt + t, 2)),
            pl.BlockSpec((1, LANES), lambda b, t: (0, 0)),
        ],
        out_specs=pl.BlockSpec((tb, LANES), lambda b, t: (b * nt + t, 0)),
        out_shape=jax.ShapeDtypeStruct((batch * seq, LANES), f32),
        scratch_shapes=[pltpu.VMEM((SUBLANES, LANES), f32)],
        compiler_params=pltpu.CompilerParams(
            dimension_semantics=("parallel", "arbitrary")),
        name="fox_gate_cumsum",
    )(small, bias_row)


LOG2E = 1.4426950408889634
FOX_TK = 256
FOX_GW = 256
FOX_BATCH = 4


def _fox_kernel(q_ref, k_ref, v_ref, c_ref, z_ref, o_ref, crep_ref, acc_ref, vt_ref, *, seq, tk, gw):
    h = pl.program_id(1)
    ng = seq // gw
    nt = seq // tk
    assert seq % gw == 0 and seq % tk == 0 and (tk % gw == 0 or gw % tk == 0)
    mm = lambda a, b: jnp.dot(a, b, preferred_element_type=f32)
    mm_nt = lambda a, b: lax.dot_general(a, b, (((1,), (1,)), ((), ())), preferred_element_type=f32)

    src_lane = lax.broadcasted_iota(jnp.int32, (LANES, LANES), 0)
    pick = jnp.where(src_lane == h, 1.0, 0.0).astype(bf16)
    crep = _dot_exact_ones(c_ref[...], pick) * LOG2E
    crep_ref[...] = jnp.concatenate([crep] * (gw // LANES), axis=1)
    for t in range(nt):
        vt_ref[t] = v_ref[t * tk:(t + 1) * tk, :].T

    key_rel = lax.broadcasted_iota(jnp.int32, (tk, gw), 0)
    qry_rel = lax.broadcasted_iota(jnp.int32, (tk, gw), 1)

    def visible(t, g):
        return t * tk <= g * gw + gw - 1

    def needs_mask(t, g):
        return t * tk + tk - 1 > g * gw

    last_tile = [max(t for t in range(nt) if visible(t, g)) for g in range(ng)]
    sched = []
    for t in range(nt):
        groups = [g for g in range(ng) if visible(t, g)]
        sched += [[(t, g) for g in groups[k:k + FOX_BATCH]] for k in range(0, len(groups), FOX_BATCH)]

    def qk(batch):
        return [mm_nt(k_ref[t * tk:(t + 1) * tk, :], q_ref[g * gw:(g + 1) * gw, :]) for t, g in batch]

    m = [None] * ng
    l = [None] * ng

    def finish(batch, alpha, pv):
        for (t, g), a, x in zip(batch, alpha, pv):
            acc_ref[g] = x if t == 0 else a * acc_ref[g] + x
            if t == last_tile[g]:
                rows = slice(g * gw, (g + 1) * gw)
                z = z_ref[rows, :]
                o = (acc_ref[g] / l[g]).T
                o_ref[rows, :] = (o * (z * _sigmoid(z))).astype(o_ref.dtype)

    s_next = qk(sched[0])
    pending = None
    for bi, batch in enumerate(sched):
        s_cur = s_next
        if bi + 1 < len(sched):
            s_next = qk(sched[bi + 1])
        alpha, p16 = [], []
        for (t, g), s in zip(batch, s_cur):
            s = s - crep_ref[t * tk:(t + 1) * tk, :]
            if needs_mask(t, g):
                s = jnp.where(key_rel + t * tk > qry_rel + g * gw, -jnp.inf, s)
            m_tile = jnp.max(s, axis=0, keepdims=True)
            if t == 0:
                m_new, a = m_tile, None
            else:
                m_new = jnp.maximum(m[g], m_tile)
                a = jnp.exp2(m[g] - m_new)
            p = jnp.exp2(s - m_new)
            p_sum = jnp.sum(p, axis=0, keepdims=True)
            l[g] = p_sum if t == 0 else a * l[g] + p_sum
            m[g] = m_new
            alpha.append(a)
            p16.append(p.astype(bf16))
        pv = [mm(vt_ref[t], p) for (t, g), p in zip(batch, p16)]
        if pending is not None:
            finish(*pending)
        pending = (batch, alpha, pv)
    finish(*pending)


def _fox_attention(qkv3, c3, wide3, batch, seq):
    tk, gw = FOX_TK, FOX_GW
    h8 = FOX_HEADS
    return pl.pallas_call(
        functools.partial(_fox_kernel, seq=seq, tk=tk, gw=gw),
        grid=(batch, h8),
        in_specs=[
            pl.BlockSpec((None, seq, LANES), lambda b, h: (b, 0, h)),
            pl.BlockSpec((None, seq, LANES), lambda b, h: (b, 0, h8 + h)),
            pl.BlockSpec((None, seq, LANES), lambda b, h: (b, 0, 2 * h8 + h)),
            pl.BlockSpec((None, seq, LANES), lambda b, h: (b, 0, 0)),
            pl.BlockSpec((None, seq, LANES), lambda b, h: (b, 0, COL_ZA + h)),
        ],
        out_specs=pl.BlockSpec((None, seq, LANES), lambda b, h: (b, 0, h)),
        out_shape=jax.ShapeDtypeStruct((batch, seq, FOX_WIDTH), bf16),
        scratch_shapes=[
            pltpu.VMEM((seq, gw), f32),
            pltpu.VMEM((seq // gw, LANES, gw), f32),
            pltpu.VMEM((seq // tk, LANES, tk), bf16),
        ],
        compiler_params=pltpu.CompilerParams(
            dimension_semantics=("parallel", "arbitrary"),
            vmem_limit_bytes=VMEM_LIMIT),
        name="fox_attention",
    )(qkv3, qkv3, qkv3, c3, wide3)


P_MU_R, P_MU_K, P_MU_V, P_MU_Z, P_W0, P_A0, P_KK, P_KA, P_RK, P_LNW, P_LNB = range(11)
P_ROWS = 16
K_XA, K_XR, K_YB, K_YK, K_VS, K_BH, K_KH = range(7)


def _spread(emitters, lo=0.0, hi=1.0):
    n = len(emitters)
    return [(lo + (hi - lo) * (i + 0.5) / n, e) for i, e in enumerate(emitters)]


def _interleave(*segment_lists):
    keyed = [(pos, prio, seg) for prio, segs in enumerate(segment_lists) for pos, seg in segs]
    keyed.sort(key=lambda x: (x[0], x[1]))
    for _, _, seg in keyed:
        seg()


def _rwkv_kernel(r_ref, k_ref, v_ref, z_ref, wd_ref, ad_ref, pc_ref, ps_ref, w2_ref, a2_ref, pc3_ref,
                 o_ref,
                 state_ref, prev_ref, stk_ref, xr32_ref, aux1_ref, dec1_ref, p16_ref, p32_ref,
                 aux2_ref, dec2_ref, *, tb, nt):
    C = RWKV_CHUNK
    G = GROUP_ROWS
    N = RWKV_HEAD_DIM
    NC = tb // C
    s = pl.program_id(0)
    first_of_seq_1 = (s % nt) == 0
    first_of_seq_3 = ((s + 2 * nt - 2) % nt) == 0

    @pl.when(s == 0)
    def _():
        for ref in (state_ref, prev_ref, stk_ref, xr32_ref, aux1_ref, dec1_ref, p16_ref, p32_ref,
                    aux2_ref, dec2_ref):
            ref[...] = jnp.zeros_like(ref)

    mm = lambda a, b: jnp.dot(a, b, preferred_element_type=f32)
    mm_nt = lambda a, b: lax.dot_general(a, b, (((1,), (1,)), ((), ())), preferred_element_type=f32)
    mm_tn = lambda a, b: lax.dot_general(a, b, (((0,), (0,)), ((), ())), preferred_element_type=f32)

    def body(cur, prv):
        li = lax.broadcasted_iota(jnp.int32, (LANES, LANES), 0)
        lj = lax.broadcasted_iota(jnp.int32, (LANES, LANES), 1)
        head_ones = jnp.where(li // N == lj // N, 1.0, 0.0).astype(bf16)
        head_sum = lambda x: _dot_exact_ones(x, head_ones, terms=2)
        lane = lax.broadcasted_iota(jnp.int32, (1, LANES), 1)
        head_masks = [jnp.where(lane // N == h, 1.0, 0.0) for h in range(HEADS_PER_GROUP)]
        stack = lambda x: jnp.concatenate([x * hm for hm in head_masks], axis=0)
        lane_c = lax.broadcasted_iota(jnp.int32, (C, LANES), 1)
        head_sel = [lane_c // N == h for h in range(HEADS_PER_GROUP)]

        def stack16(x):
            xb = x.astype(bf16)
            return jnp.concatenate([jnp.where(m, xb, jnp.zeros_like(xb)) for m in head_sel], axis=0)

        gi = lax.broadcasted_iota(jnp.int32, (G, G), 0)
        gj = lax.broadcasted_iota(jnp.int32, (G, G), 1)
        same_head = gi // C == gj // C
        strict = jnp.logical_and(same_head, gj < gi)
        incl = jnp.logical_and(same_head, gj <= gi)
        eye = jnp.where(gi == gj, 1.0, 0.0)
        chunks = range(NC)

        v1 = {}
        prm = lambda idx: pc_ref[idx:idx + 1, :]
        row_in_block = lax.broadcasted_iota(jnp.int32, (tb, LANES), 0)

        def shifted(ref, slot, mu):
            u = ref[...]
            carry = jnp.where(first_of_seq_1, 0.0, prev_ref[slot, 0:1, :])
            prev = jnp.where(row_in_block == 0, carry, pltpu.roll(u, 1, 0))
            prev_ref[slot, 0:1, :] = u[tb - 1:tb, :]
            return u + (prev - u) * mu

        def s1_lora():
            v1["wd"] = shifted(wd_ref, 4, ps_ref[0:1, :])
            v1["ad"] = shifted(ad_ref, 5, ps_ref[1:2, :])
            v1["w_lin"] = _dot(jnp.tanh(v1["wd"]), w2_ref[...])
            v1["a_lin"] = _dot(v1["ad"], a2_ref[...])

        def s1_key_norm():
            v1["kr"] = shifted(k_ref, 1, prm(P_MU_K))
            kk = v1["kr"] * prm(P_KK)
            v1["kk_raw"] = kk
            v1["kk_ss"] = head_sum(kk * kk)

        def s1_decay():
            w = -_softplus(-(prm(P_W0) + v1["w_lin"])) - 0.5
            v1["log_decay"] = -jnp.exp(w)
            ti = lax.broadcasted_iota(jnp.int32, (2 * C, 2 * C), 0)
            tj = lax.broadcasted_iota(jnp.int32, (2 * C, 2 * C), 1)
            pair_tri = jnp.where(jnp.logical_and(tj <= ti, ti // C == tj // C), 1.0, 0.0).astype(bf16)
            v1["ci"] = jnp.concatenate(
                [_dot_exact_ones(v1["log_decay"][k:k + 2 * C], pair_tri, nt=True)
                 for k in range(0, tb, 2 * C)], axis=0)

        def s1_bonus():
            v1["rate"] = _sigmoid(prm(P_A0) + v1["a_lin"])
            v1["r"] = shifted(r_ref, 0, prm(P_MU_R))
            v1["vr"] = shifted(v_ref, 2, prm(P_MU_V))
            v1["kp"] = v1["kr"] * (1.0 + (v1["rate"] - 1.0) * prm(P_KA))
            aux1_ref[cur, 0] = head_sum(v1["r"] * v1["kp"] * prm(P_RK)) * v1["vr"]
            zb = shifted(z_ref, 3, prm(P_MU_Z))
            aux1_ref[cur, 1] = zb * _sigmoid(zb)

        def s1_scale():
            kk = v1["kk_raw"] / jnp.maximum(jnp.sqrt(v1["kk_ss"]), L2_EPS)
            v1["bb"] = kk * v1["rate"]
            ci = v1["ci"]
            v1["a_t"] = -kk * jnp.exp(ci - v1["log_decay"])
            v1["r_t"] = v1["r"] * jnp.exp(ci)
            inv = jnp.exp(-ci)
            v1["b_t"] = v1["bb"] * inv
            v1["k_t"] = v1["kp"] * inv

        def s1_stack(c):
            def emit():
                sl = slice(c * C, (c + 1) * C)
                ci_c = v1["ci"][sl]
                c_last = ci_c[C - 1:C, :]
                to_end = jnp.exp(c_last - ci_c)
                dec1_ref[cur, c] = jnp.broadcast_to(jnp.exp(c_last), (SUBLANES, LANES))
                xr = stack(v1["r_t"][sl])
                xr32_ref[cur, c] = xr
                stk_ref[cur, K_XR, c] = xr.astype(bf16)
                stk_ref[cur, K_XA, c] = stack16(v1["a_t"][sl])
                stk_ref[cur, K_YB, c] = stack16(v1["b_t"][sl])
                stk_ref[cur, K_YK, c] = stack16(v1["k_t"][sl])
                stk_ref[cur, K_VS, c] = stack16(v1["vr"][sl])
                stk_ref[cur, K_BH, c] = stack16(v1["bb"][sl] * to_end)
                stk_ref[cur, K_KH, c] = stack16(v1["kp"][sl] * to_end)
            return emit

        segs1 = ([(0.0, s1_lora), (0.04, s1_key_norm), (0.10, s1_decay), (0.16, s1_bonus)]
                 + _spread([s1_scale] + [s1_stack(c) for c in chunks], RWKV_TAIL_START, 1.0))

        v2 = {}
        ld = lambda kind, c: stk_ref[prv, kind, c]

        def s2_big():
            big = [mm_nt(jnp.concatenate([ld(K_XA, c), ld(K_XR, c)], axis=0),
                         jnp.concatenate([ld(K_YB, c), ld(K_YK, c)], axis=0)) for c in chunks]
            v2["a_ab"] = [jnp.where(strict, big[c][0:G, 0:G], 0.0) for c in chunks]
            v2["a_ak"] = [jnp.where(strict, big[c][0:G, G:2 * G], 0.0).astype(bf16) for c in chunks]
            v2["a_rb"] = [jnp.where(incl, big[c][G:2 * G, 0:G], 0.0).astype(bf16) for c in chunks]
            v2["a_rk"] = [jnp.where(incl, big[c][G:2 * G, G:2 * G], 0.0).astype(bf16) for c in chunks]
            v2["tinv"] = [eye + a for a in v2["a_ab"]]
            v2["pw"] = [a.astype(bf16) for a in v2["a_ab"]]

        def s2_first_square():
            v2["pw"] = [mm(p, p).astype(bf16) for p in v2["pw"]]

        def s2_level(last):
            def emit():
                for c in chunks:
                    pw = v2["pw"][c]
                    t16 = v2["tinv"][c].astype(bf16)
                    if last:
                        v2["tinv"][c] = v2["tinv"][c] + mm(t16, pw)
                    else:
                        both = mm(jnp.concatenate([t16, pw], axis=0), pw)
                        v2["tinv"][c] = v2["tinv"][c] + both[0:G]
                        v2["pw"][c] = both[G:2 * G].astype(bf16)
            return emit

        def s2_av():
            v2["tinv"] = [t.astype(bf16) for t in v2["tinv"]]
            av = [mm(jnp.concatenate([v2["a_ak"][c], v2["a_rk"][c]], axis=0), ld(K_VS, c)) for c in chunks]
            v2["akv"] = [x[0:G].astype(bf16) for x in av]
            v2["arkv"] = [x[G:2 * G] for x in av]

        def s2_wu():
            v2["wu"] = [mm(v2["tinv"][c], jnp.concatenate([ld(K_XA, c), v2["akv"][c]], axis=1)).astype(bf16)
                        for c in chunks]

        def s2_ry():
            for c in chunks:
                e = mm(v2["a_rb"][c], v2["wu"][c])
                p16_ref[cur, 0, c] = (xr32_ref[prv, c] + e[:, 0:LANES]).astype(bf16)
                p32_ref[cur, 0, c] = e[:, LANES:] + v2["arkv"][c]

        def s2_pm():
            for c in chunks:
                p16_ref[cur, 1, c] = mm_tn(v2["wu"][c][:, 0:LANES], ld(K_BH, c)).astype(bf16)

        def s2_q():
            for c in chunks:
                p32_ref[cur, 1, c] = mm_tn(jnp.concatenate([v2["wu"][c][:, LANES:], ld(K_VS, c)], axis=0),
                                           jnp.concatenate([ld(K_BH, c), ld(K_KH, c)], axis=0))
            aux2_ref[cur] = aux1_ref[prv]
            dec2_ref[cur] = dec1_ref[prv]

        segs2 = _spread([s2_big, s2_first_square] + [s2_level(False)] * 4 + [s2_level(True)]
                        + [s2_av, s2_wu, s2_ry, s2_pm, s2_q])

        v3 = {"ys": []}
        prm3 = lambda idx: pc3_ref[idx:idx + 1, :]

        def s3_start():
            v3["state"] = jnp.where(first_of_seq_3, 0.0, state_ref[...])

        def s3_chunk(c):
            def emit():
                state = v3["state"]
                s16 = state.astype(bf16)
                y_st = mm_nt(p16_ref[prv, 0, c], s16) + p32_ref[prv, 0, c]
                v3["ys"].append(y_st[0:C] + y_st[C:2 * C])
                v3["state"] = (state * dec2_ref[prv, c, 0:1, :] + mm(s16, p16_ref[prv, 1, c])
                               + p32_ref[prv, 1, c])
            return emit

        def s3_finish():
            state_ref[...] = v3["state"]
            y = jnp.concatenate(v3["ys"], axis=0)
            mean = head_sum(y) * (1.0 / N)
            yc = y - mean
            var = head_sum(yc * yc) * (1.0 / N)
            yn = yc * lax.rsqrt(var + GN_EPS) * prm3(P_LNW) + prm3(P_LNB)
            o_ref[...] = ((yn + aux2_ref[prv, 0]) * aux2_ref[prv, 1]).astype(o_ref.dtype)

        segs3 = _spread([s3_start] + [s3_chunk(c) for c in chunks] + [s3_finish])

        _interleave(segs3, segs2, segs1)

    for parity in (0, 1):
        @pl.when(s % 2 == parity)
        def _(parity=parity):
            body(cur=parity, prv=1 - parity)


def _rwkv_mix(wide, small, pc, ps, w2p, a2p, batch, seq):
    tb = RWKV_BLOCK
    nt = seq // tb
    groups = RWKV_WIDTH // LANES
    nc = tb // RWKV_CHUNK
    n_blocks = batch * groups * nt

    def where(s):
        s = jnp.clip(s, 0, n_blocks - 1)
        return (s // (nt * groups)) * nt + s % nt, (s // nt) % groups

    col = lambda c0: (lambda s: (where(s)[0], c0 + where(s)[1]))
    fixed_col = lambda c: (lambda s: (where(s)[0], c))
    grp = lambda s: (0, where(s)[1])
    return pl.pallas_call(
        functools.partial(_rwkv_kernel, tb=tb, nt=nt),
        grid=(n_blocks + 2,),
        in_specs=[
            pl.BlockSpec((tb, LANES), col(COL_R)),
            pl.BlockSpec((tb, LANES), col(COL_K)),
            pl.BlockSpec((tb, LANES), col(COL_V)),
            pl.BlockSpec((tb, LANES), col(COL_ZB)),
            pl.BlockSpec((tb, LANES), fixed_col(0)),
            pl.BlockSpec((tb, LANES), fixed_col(1)),
            pl.BlockSpec((P_ROWS, LANES), grp),
            pl.BlockSpec((SUBLANES, LANES), lambda s: (0, 0)),
            pl.BlockSpec((LANES, LANES), grp),
            pl.BlockSpec((LANES, LANES), grp),
            pl.BlockSpec((P_ROWS, LANES), lambda s: grp(s - 2)),
        ],
        out_specs=pl.BlockSpec((tb, LANES), lambda s: (where(s - 2)[0], where(s - 2)[1])),
        out_shape=jax.ShapeDtypeStruct((batch * seq, RWKV_WIDTH), bf16),
        scratch_shapes=[
            pltpu.VMEM((LANES, LANES), f32),
            pltpu.VMEM((6, SUBLANES, LANES), f32),
            pltpu.VMEM((2, 7, nc, GROUP_ROWS, LANES), bf16),
            pltpu.VMEM((2, nc, GROUP_ROWS, LANES), f32),
            pltpu.VMEM((2, 2, tb, LANES), f32),
            pltpu.VMEM((2, nc, SUBLANES, LANES), f32),
            pltpu.VMEM((2, 2, nc, GROUP_ROWS, LANES), bf16),
            pltpu.VMEM((2, 2, nc, GROUP_ROWS, LANES), f32),
            pltpu.VMEM((2, 2, tb, LANES), f32),
            pltpu.VMEM((2, nc, SUBLANES, LANES), f32),
        ],
        compiler_params=pltpu.CompilerParams(
            dimension_semantics=("arbitrary",),
            vmem_limit_bytes=VMEM_LIMIT),
        name="rwkv_mix",
    )(wide, wide, wide, wide, small, small, pc, ps, w2p, a2p, pc)


def _out_kernel(oa_ref, ob_ref, ga_ref, gb_ref, x_ref, wpf_ref, wpr_ref, wo_ref, g_ref, o_ref):
    pa = jnp.dot(oa_ref[...], wpf_ref[...], preferred_element_type=f32)
    pb = jnp.dot(ob_ref[...], wpr_ref[...], preferred_element_type=f32)
    m = _sigmoid(ga_ref[...]) * pa + _sigmoid(gb_ref[...]) * pb
    z = x_ref[...] + jnp.dot(m.astype(bf16), wo_ref[...], preferred_element_type=f32)
    ms = jnp.mean(z * z, axis=-1, keepdims=True)
    o_ref[...] = z * lax.rsqrt(ms + RMS_EPS) * g_ref[...]


def _merge_out(oa, ob, wide, x2d, wpf, wpr, wo, gain):
    m, d = x2d.shape
    tm = 256
    resident = lambda shape: pl.BlockSpec(shape, lambda i: (0, 0), pipeline_mode=pl.Buffered(1))
    return pl.pallas_call(
        _out_kernel,
        grid=(m // tm,),
        in_specs=[
            pl.BlockSpec((tm, FOX_WIDTH), lambda i: (i, 0)),
            pl.BlockSpec((tm, RWKV_WIDTH), lambda i: (i, 0)),
            pl.BlockSpec((tm, d), lambda i: (i, COL_GA * LANES // d)),
            pl.BlockSpec((tm, d), lambda i: (i, COL_GB * LANES // d)),
            pl.BlockSpec((tm, d), lambda i: (i, 0)),
            resident((FOX_WIDTH, d)),
            resident((RWKV_WIDTH, d)),
            resident((d, d)),
            resident((1, d)),
        ],
        out_specs=pl.BlockSpec((tm, d), lambda i: (i, 0)),
        out_shape=jax.ShapeDtypeStruct((m, d), f32),
        compiler_params=pltpu.CompilerParams(
            dimension_semantics=("parallel",),
            vmem_limit_bytes=VMEM_LIMIT),
        name="merge_out",
    )(oa, ob, wide, wide, x2d, wpf, wpr, wo, gain)


def _pad_cols(a, width):
    return jnp.pad(a, ((0, 0), (0, width - a.shape[1])))


def _branches(x2d, batch, seq, norm_gain, w_in, fox_forget_bias, rwkv_shift_mix, rwkv_w0, rwkv_w2,
              rwkv_a0, rwkv_a2, rwkv_k_k, rwkv_k_a, rwkv_r_k, rwkv_ln_w, rwkv_ln_b):
    fw, rw = FOX_WIDTH, RWKV_WIDTH
    w_t = w_in.T
    r_qkv = 0
    r_za = r_qkv + 3 * fw
    r_f = r_za + fw
    r_rkvz = r_f + FOX_HEADS
    r_wd = r_rkvz + 4 * rw
    r_ad = r_wd + LORA
    r_g = r_ad + LORA
    w_main = _relayout_weights(
        w_t, [(r_qkv, 3 * fw), (r_g, 2 * D_MODEL), (r_za, fw), (r_rkvz, 4 * rw)],
        scaled_rows=fw, scale=FOX_HEAD_DIM ** -0.5 * LOG2E)
    pad_rows = lambda a: jnp.pad(a, ((0, LANES - a.shape[0]), (0, 0)))
    w_tail = jnp.concatenate([pad_rows(w_t[r_wd:r_wd + LORA]), pad_rows(w_t[r_ad:r_ad + LORA]),
                              pad_rows(w_t[r_f:r_f + FOX_HEADS])], axis=0)
    gain = norm_gain.reshape(1, D_MODEL)

    qkv, wide, small = _in_proj(x2d, gain, w_main, w_tail)

    bias_row = _pad_cols(fox_forget_bias.reshape(1, FOX_HEADS), LANES)
    c = _gate_cumsum(small, bias_row, batch, seq)
    qkv3 = qkv.reshape(batch, seq, 3 * fw)
    oa = _fox_attention(qkv3, c.reshape(batch, seq, LANES), wide.reshape(batch, seq, WIDE_COLS),
                        batch, seq).reshape(batch * seq, fw)

    mu = rwkv_shift_mix
    pc = jnp.zeros((P_ROWS, rw), f32)
    rows = [mu[0:rw], mu[rw:2 * rw], mu[2 * rw:3 * rw], mu[3 * rw:4 * rw], rwkv_w0, rwkv_a0,
            rwkv_k_k, rwkv_k_a, rwkv_r_k.reshape(rw), rwkv_ln_w, rwkv_ln_b]
    pc = pc.at[:len(rows)].set(jnp.stack(rows))
    ps = jnp.zeros((SUBLANES, LANES), f32)
    ps = ps.at[0, :LORA].set(mu[4 * rw:4 * rw + LORA]).at[1, :LORA].set(mu[4 * rw + LORA:])
    w2p = jnp.pad(rwkv_w2, ((0, LANES - LORA), (0, 0))).astype(bf16)
    a2p = jnp.pad(rwkv_a2, ((0, LANES - LORA), (0, 0))).astype(bf16)
    ob = _rwkv_mix(wide, small, pc, ps, w2p, a2p, batch, seq)

    return oa, ob, wide


def kernel(x, norm_gain, w_in, fox_forget_bias, rwkv_shift_mix, rwkv_w0, rwkv_w2, rwkv_a0, rwkv_a2, rwkv_k_k, rwkv_k_a, rwkv_r_k, rwkv_ln_w, rwkv_ln_b, w_proj_fox, w_proj_rwkv, w_out, final_norm_gain):
    batch, seq, d = x.shape
    depth = norm_gain.shape[0]
    assert depth == 1, "the final rmsnorm is fused into the single layer's output kernel"
    x2d = x.reshape(batch * seq, d)
    oa, ob, wide = _branches(x2d, batch, seq, norm_gain[0], w_in[0], fox_forget_bias[0],
                             rwkv_shift_mix[0], rwkv_w0[0], rwkv_w2[0], rwkv_a0[0], rwkv_a2[0],
                             rwkv_k_k[0], rwkv_k_a[0], rwkv_r_k[0], rwkv_ln_w[0], rwkv_ln_b[0])
    out = _merge_out(oa, ob, wide, x2d, w_proj_fox[0].astype(bf16), w_proj_rwkv[0].astype(bf16),
                     w_out[0].astype(bf16), final_norm_gain.reshape(1, d))
    return out.reshape(batch, seq, d)
```

```python
import functools

import jax
import jax.numpy as jnp
from jax import lax
from jax.experimental import pallas as pl
from jax.experimental.pallas import tpu as pltpu

D_MODEL = 2048
FOX_HEADS = 8
FOX_HEAD_DIM = 128
FOX_WIDTH = FOX_HEADS * FOX_HEAD_DIM
RWKV_HEADS = 16
RWKV_HEAD_DIM = 64
RWKV_WIDTH = RWKV_HEADS * RWKV_HEAD_DIM
LORA = 96
RMS_EPS = 1e-6
GN_EPS = 64e-5
L2_EPS = 1e-12

LANES = 128
SUBLANES = 8
VMEM_LIMIT = 56 * 1024 * 1024

COL_GA = 0
COL_GB = D_MODEL // LANES
COL_ZA = 2 * D_MODEL // LANES
COL_R = COL_ZA + FOX_WIDTH // LANES
COL_K = COL_R + RWKV_WIDTH // LANES
COL_V = COL_K + RWKV_WIDTH // LANES
COL_ZB = COL_V + RWKV_WIDTH // LANES
WIDE_COLS = (COL_ZB + RWKV_WIDTH // LANES) * LANES
SMALL_COLS = 3 * LANES

RWKV_CHUNK = 64
RWKV_BLOCK = 512
RWKV_TAIL_START = 0.3
RWKV_CHAIN_END = 0.8
HEADS_PER_GROUP = LANES // RWKV_HEAD_DIM
GROUP_ROWS = HEADS_PER_GROUP * RWKV_CHUNK

f32 = jnp.float32
bf16 = jnp.bfloat16


def _dot(a, b):
    return jnp.dot(a.astype(bf16), b.astype(bf16), preferred_element_type=f32)


def _dot_exact_ones(a, ones_bf16, nt=False, terms=3):
    if nt:
        mm = lambda x: lax.dot_general(ones_bf16, x, (((1,), (0,)), ((), ())),
                                       preferred_element_type=f32)
    else:
        mm = lambda x: jnp.dot(x, ones_bf16, preferred_element_type=f32)
    part = a.astype(bf16)
    out = mm(part)
    rest = a
    for _ in range(terms - 1):
        rest = rest - part.astype(f32)
        part = rest.astype(bf16)
        out = out + mm(part)
    return out


def _softplus(x):
    return jnp.maximum(x, 0.0) + jnp.log(1.0 + jnp.exp(-jnp.abs(x)))


def _sigmoid(x):
    return 1.0 / (1.0 + jnp.exp(-x))


PROJ_TM = 1024
PROJ_TN = 1024


def _relayout_kernel(w_ref, o_ref, *, scaled_blocks, scale):
    factor = jnp.where(pl.program_id(0) < scaled_blocks, scale, 1.0)
    o_ref[...] = (w_ref[...] * factor).astype(o_ref.dtype)


def _relayout_weights(w_t, pieces, scaled_rows, scale):
    d = w_t.shape[1]
    rb = PROJ_TN
    assert scaled_rows % rb == 0 and all(n % rb == 0 and src % SUBLANES == 0 for src, n in pieces)
    starts, first_block = [], 0
    for src, n in pieces:
        starts.append((first_block, src))
        first_block += n // rb

    def src_row(i):
        tile = jnp.int32(0)
        for blk, src in starts:
            tile = jnp.where(i >= blk, src // SUBLANES + (i - blk) * (rb // SUBLANES), tile)
        return pl.multiple_of(tile * SUBLANES, SUBLANES)

    return pl.pallas_call(
        functools.partial(_relayout_kernel, scaled_blocks=scaled_rows // rb, scale=scale),
        grid=(first_block,),
        in_specs=[pl.BlockSpec((pl.Element(rb), pl.Element(d)), lambda i: (src_row(i), 0))],
        out_specs=pl.BlockSpec((rb, d), lambda i: (i, 0)),
        out_shape=jax.ShapeDtypeStruct((first_block * rb, d), bf16),
        compiler_params=pltpu.CompilerParams(
            dimension_semantics=("parallel",),
            vmem_limit_bytes=VMEM_LIMIT),
        name="relayout_weights",
    )(w_t)


def _in_proj_kernel(x_ref, g_ref, w_ref, wt_ref, qkv_ref, wide_ref, small_ref, h_ref, *, n_qkv, n_wide):
    j = pl.program_id(1)

    @pl.when(j == 0)
    def _():
        xf = x_ref[...]
        ms = jnp.mean(xf * xf, axis=-1, keepdims=True)
        h_ref[...] = (xf * lax.rsqrt(ms + RMS_EPS) * g_ref[...]).astype(bf16)

    project = lambda w: lax.dot_general(h_ref[...], w, (((1,), (1,)), ((), ())), preferred_element_type=f32)

    @pl.when(j < n_qkv)
    def _():
        qkv_ref[...] = project(w_ref[...]).astype(qkv_ref.dtype)

    @pl.when(jnp.logical_and(j >= n_qkv, j < n_qkv + n_wide))
    def _():
        wide_ref[...] = project(w_ref[...])

    @pl.when(j == n_qkv + n_wide)
    def _():
        small_ref[...] = project(wt_ref[...].astype(bf16))


def _in_proj(x2d, gain, w_main, w_tail):
    m, d = x2d.shape
    tm, tn = PROJ_TM, PROJ_TN
    n_qkv = 3 * FOX_WIDTH // tn
    n_wide = WIDE_COLS // tn
    assert n_qkv * tn == 3 * FOX_WIDTH and n_wide * tn == WIDE_COLS
    assert w_main.shape[0] == 3 * FOX_WIDTH + WIDE_COLS and w_tail.shape[0] == SMALL_COLS
    return pl.pallas_call(
        functools.partial(_in_proj_kernel, n_qkv=n_qkv, n_wide=n_wide),
        grid=(m // tm, n_qkv + n_wide + 1),
        in_specs=[
            pl.BlockSpec((tm, d), lambda i, j: (i, 0)),
            pl.BlockSpec((1, d), lambda i, j: (0, 0)),
            pl.BlockSpec((tn, d), lambda i, j: (jnp.minimum(j, n_qkv + n_wide - 1), 0)),
            pl.BlockSpec((SMALL_COLS, d), lambda i, j: (0, 0)),
        ],
        out_specs=[
            pl.BlockSpec((tm, tn), lambda i, j: (i, jnp.minimum(j, n_qkv - 1))),
            pl.BlockSpec((tm, tn), lambda i, j: (i, jnp.clip(j - n_qkv, 0, n_wide - 1))),
            pl.BlockSpec((tm, SMALL_COLS), lambda i, j: (i, 0)),
        ],
        out_shape=[
            jax.ShapeDtypeStruct((m, 3 * FOX_WIDTH), bf16),
            jax.ShapeDtypeStruct((m, WIDE_COLS), f32),
            jax.ShapeDtypeStruct((m, SMALL_COLS), f32),
        ],
        scratch_shapes=[pltpu.VMEM((tm, d), bf16)],
        compiler_params=pltpu.CompilerParams(
            dimension_semantics=("parallel", "arbitrary"),
            vmem_limit_bytes=VMEM_LIMIT),
        name="in_proj",
    )(x2d, gain, w_main, w_tail)


def _gate_kernel(f_ref, bias_ref, c_ref, carry_ref, *, tb):
    @pl.when(pl.program_id(1) == 0)
    def _():
        carry_ref[...] = jnp.zeros_like(carry_ref)

    z = f_ref[...] + bias_ref[...]
    log_f = -_softplus(-z)
    row = lax.broadcasted_iota(jnp.int32, (tb, tb), 0)
    col = lax.broadcasted_iota(jnp.int32, (tb, tb), 1)
    tri = jnp.where(col <= row, 1.0, 0.0).astype(bf16)
    c = _dot_exact_ones(log_f, tri, nt=True) + carry_ref[0:1, :]
    c_ref[...] = c
    carry_ref[0:1, :] = c[tb - 1:tb, :]


def _gate_cumsum(small, bias_row, batch, seq):
    tb = 512
    nt = seq // tb
    return pl.pallas_call(
        functools.partial(_gate_kernel, tb=tb),
        grid=(batch, nt),
        in_specs=[
            pl.BlockSpec((tb, LANES), lambda b, t: (b * nt + t, 2)),
            pl.BlockSpec((1, LANES), lambda b, t: (0, 0)),
        ],
        out_specs=pl.BlockSpec((tb, LANES), lambda b, t: (b * nt + t, 0)),
        out_shape=jax.ShapeDtypeStruct((batch * seq, LANES), f32),
        scratch_shapes=[pltpu.VMEM((SUBLANES, LANES), f32)],
        compiler_params=pltpu.CompilerParams(
            dimension_semantics=("parallel", "arbitrary")),
        name="fox_gate_cumsum",
    )(small, bias_row)


LOG2E = 1.4426950408889634
FOX_TK = 256
FOX_GW = 128
FOX_BATCH = 4


def _fox_kernel(q_ref, k_ref, v_ref, c_ref, z_ref, o_ref, crep_ref, acc_ref, vt_ref, *, seq, tk, gw):
    h = pl.program_id(1)
    ng = seq // gw
    nt = seq // tk
    assert seq % gw == 0 and seq % tk == 0 and (tk % gw == 0 or gw % tk == 0)
    mm = lambda a, b: jnp.dot(a, b, preferred_element_type=f32)
    mm_nt = lambda a, b: lax.dot_general(a, b, (((1,), (1,)), ((), ())), preferred_element_type=f32)

    src_lane = lax.broadcasted_iota(jnp.int32, (LANES, LANES), 0)
    pick = jnp.where(src_lane == h, 1.0, 0.0).astype(bf16)
    crep = _dot_exact_ones(c_ref[...], pick) * LOG2E
    crep_ref[...] = jnp.concatenate([crep] * (gw // LANES), axis=1)
    for t in range(nt):
        vt_ref[t] = v_ref[t * tk:(t + 1) * tk, :].T

    key_rel = lax.broadcasted_iota(jnp.int32, (tk, gw), 0)
    qry_rel = lax.broadcasted_iota(jnp.int32, (tk, gw), 1)

    def visible(t, g):
        return t * tk <= g * gw + gw - 1

    def needs_mask(t, g):
        return t * tk + tk - 1 > g * gw

    last_tile = [max(t for t in range(nt) if visible(t, g)) for g in range(ng)]
    sched = []
    for t in range(nt):
        groups = [g for g in range(ng) if visible(t, g)]
        sched += [[(t, g) for g in groups[k:k + FOX_BATCH]] for k in range(0, len(groups), FOX_BATCH)]

    def qk(batch):
        return [mm_nt(k_ref[t * tk:(t + 1) * tk, :], q_ref[g * gw:(g + 1) * gw, :]) for t, g in batch]

    m = [None] * ng
    l = [None] * ng

    def finish(batch, alpha, pv):
        for (t, g), a, x in zip(batch, alpha, pv):
            acc_ref[g] = x if t == 0 else a * acc_ref[g] + x
            if t == last_tile[g]:
                rows = slice(g * gw, (g + 1) * gw)
                z = z_ref[rows, :]
                o = (acc_ref[g] / l[g]).T
                o_ref[rows, :] = (o * (z * _sigmoid(z))).astype(o_ref.dtype)

    s_next = qk(sched[0])
    pending = None
    for bi, batch in enumerate(sched):
        s_cur = s_next
        if bi + 1 < len(sched):
            s_next = qk(sched[bi + 1])
        alpha, p16 = [], []
        for (t, g), s in zip(batch, s_cur):
            s = s - crep_ref[t * tk:(t + 1) * tk, :]
            if needs_mask(t, g):
                s = jnp.where(key_rel + t * tk > qry_rel + g * gw, -jnp.inf, s)
            m_tile = jnp.max(s, axis=0, keepdims=True)
            if t == 0:
                m_new, a = m_tile, None
            else:
                m_new = jnp.maximum(m[g], m_tile)
                a = jnp.exp2(m[g] - m_new)
            p = jnp.exp2(s - m_new)
            p_sum = jnp.sum(p, axis=0, keepdims=True)
            l[g] = p_sum if t == 0 else a * l[g] + p_sum
            m[g] = m_new
            alpha.append(a)
            p16.append(p.astype(bf16))
        pv = [mm(vt_ref[t], p) for (t, g), p in zip(batch, p16)]
        if pending is not None:
            finish(*pending)
        pending = (batch, alpha, pv)
    finish(*pending)


def _fox_attention(qkv3, c3, wide3, batch, seq):
    tk, gw = FOX_TK, FOX_GW
    h8 = FOX_HEADS
    return pl.pallas_call(
        functools.partial(_fox_kernel, seq=seq, tk=tk, gw=gw),
        grid=(batch, h8),
        in_specs=[
            pl.BlockSpec((None, seq, LANES), lambda b, h: (b, 0, h)),
            pl.BlockSpec((None, seq, LANES), lambda b, h: (b, 0, h8 + h)),
            pl.BlockSpec((None, seq, LANES), lambda b, h: (b, 0, 2 * h8 + h)),
            pl.BlockSpec((None, seq, LANES), lambda b, h: (b, 0, 0)),
            pl.BlockSpec((None, seq, LANES), lambda b, h: (b, 0, COL_ZA + h)),
        ],
        out_specs=pl.BlockSpec((None, seq, LANES), lambda b, h: (b, 0, h)),
        out_shape=jax.ShapeDtypeStruct((batch, seq, FOX_WIDTH), bf16),
        scratch_shapes=[
            pltpu.VMEM((seq, gw), f32),
            pltpu.VMEM((seq // gw, LANES, gw), f32),
            pltpu.VMEM((seq // tk, LANES, tk), bf16),
        ],
        compiler_params=pltpu.CompilerParams(
            dimension_semantics=("parallel", "arbitrary"),
            vmem_limit_bytes=VMEM_LIMIT),
        name="fox_attention",
    )(qkv3, qkv3, qkv3, c3, wide3)


P_MU_R, P_MU_K, P_MU_V, P_MU_Z, P_W0, P_A0, P_KK, P_KA, P_RK, P_LNW, P_LNB = range(11)
P_ROWS = 16
K_XA, K_XR, K_YB, K_YK, K_VS, K_BH, K_KH = range(7)


def _spread(emitters, lo=0.0, hi=1.0):
    n = len(emitters)
    return [(lo + (hi - lo) * (i + 0.5) / n, e) for i, e in enumerate(emitters)]


def _interleave(*segment_lists):
    keyed = [(pos, prio, seg) for prio, segs in enumerate(segment_lists) for pos, seg in segs]
    keyed.sort(key=lambda x: (x[0], x[1]))
    for _, _, seg in keyed:
        seg()


def _rwkv_kernel(r_ref, k_ref, v_ref, z_ref, wd_ref, ad_ref, pc_ref, ps_ref, w2_ref, a2_ref, pc3_ref,
                 o_ref,
                 state_ref, prev_ref, stk_ref, xr32_ref, aux1_ref, dec1_ref, p16_ref, p32_ref,
                 aux2_ref, dec2_ref, *, tb, nt):
    C = RWKV_CHUNK
    G = GROUP_ROWS
    N = RWKV_HEAD_DIM
    NC = tb // C
    s = pl.program_id(0)
    first_of_seq_1 = (s % nt) == 0
    first_of_seq_3 = ((s + 2 * nt - 2) % nt) == 0

    @pl.when(s == 0)
    def _():
        for ref in (state_ref, prev_ref, stk_ref, xr32_ref, aux1_ref, dec1_ref, p16_ref, p32_ref,
                    aux2_ref, dec2_ref):
            ref[...] = jnp.zeros_like(ref)

    mm = lambda a, b: jnp.dot(a, b, preferred_element_type=f32)
    mm_nt = lambda a, b: lax.dot_general(a, b, (((1,), (1,)), ((), ())), preferred_element_type=f32)
    mm_tn = lambda a, b: lax.dot_general(a, b, (((0,), (0,)), ((), ())), preferred_element_type=f32)

    def body(cur, prv):
        li = lax.broadcasted_iota(jnp.int32, (LANES, LANES), 0)
        lj = lax.broadcasted_iota(jnp.int32, (LANES, LANES), 1)
        head_ones = jnp.where(li // N == lj // N, 1.0, 0.0).astype(bf16)
        head_sum = lambda x: _dot_exact_ones(x, head_ones, terms=2)
        lane = lax.broadcasted_iota(jnp.int32, (1, LANES), 1)
        head_masks = [jnp.where(lane // N == h, 1.0, 0.0) for h in range(HEADS_PER_GROUP)]
        stack = lambda x: jnp.concatenate([x * hm for hm in head_masks], axis=0)
        lane_c = lax.broadcasted_iota(jnp.int32, (C, LANES), 1)
        head_sel = [lane_c // N == h for h in range(HEADS_PER_GROUP)]

        def stack16(x):
            xb = x.astype(bf16)
            return jnp.concatenate([jnp.where(m, xb, jnp.zeros_like(xb)) for m in head_sel], axis=0)

        gi = lax.broadcasted_iota(jnp.int32, (G, G), 0)
        gj = lax.broadcasted_iota(jnp.int32, (G, G), 1)
        same_head = gi // C == gj // C
        strict = jnp.logical_and(same_head, gj < gi)
        incl = jnp.logical_and(same_head, gj <= gi)
        eye = jnp.where(gi == gj, 1.0, 0.0)
        chunks = range(NC)

        v1 = {}
        prm = lambda idx: pc_ref[idx:idx + 1, :]
        row_in_block = lax.broadcasted_iota(jnp.int32, (tb, LANES), 0)

        def shifted(ref, slot, mu):
            u = ref[...]
            carry = jnp.where(first_of_seq_1, 0.0, prev_ref[slot, 0:1, :])
            prev = jnp.where(row_in_block == 0, carry, pltpu.roll(u, 1, 0))
            prev_ref[slot, 0:1, :] = u[tb - 1:tb, :]
            return u + (prev - u) * mu

        def s1_lora():
            v1["wd"] = shifted(wd_ref, 4, ps_ref[0:1, :])
            v1["ad"] = shifted(ad_ref, 5, ps_ref[1:2, :])
            v1["w_lin"] = _dot(jnp.tanh(v1["wd"]), w2_ref[...])
            v1["a_lin"] = _dot(v1["ad"], a2_ref[...])

        def s1_key_norm():
            v1["kr"] = shifted(k_ref, 1, prm(P_MU_K))
            kk = v1["kr"] * prm(P_KK)
            v1["kk_raw"] = kk
            v1["kk_ss"] = head_sum(kk * kk)

        def s1_decay():
            w = -_softplus(-(prm(P_W0) + v1["w_lin"])) - 0.5
            v1["log_decay"] = -jnp.exp(w)
            ti = lax.broadcasted_iota(jnp.int32, (2 * C, 2 * C), 0)
            tj = lax.broadcasted_iota(jnp.int32, (2 * C, 2 * C), 1)
            pair_tri = jnp.where(jnp.logical_and(tj <= ti, ti // C == tj // C), 1.0, 0.0).astype(bf16)
            v1["ci"] = jnp.concatenate(
                [_dot_exact_ones(v1["log_decay"][k:k + 2 * C], pair_tri, nt=True)
                 for k in range(0, tb, 2 * C)], axis=0)

        def s1_bonus():
            v1["rate"] = _sigmoid(prm(P_A0) + v1["a_lin"])
            v1["r"] = shifted(r_ref, 0, prm(P_MU_R))
            v1["vr"] = shifted(v_ref, 2, prm(P_MU_V))
            v1["kp"] = v1["kr"] * (1.0 + (v1["rate"] - 1.0) * prm(P_KA))
            aux1_ref[cur, 0] = head_sum(v1["r"] * v1["kp"] * prm(P_RK)) * v1["vr"]
            zb = shifted(z_ref, 3, prm(P_MU_Z))
            aux1_ref[cur, 1] = zb * _sigmoid(zb)

        def s1_scale():
            kk = v1["kk_raw"] / jnp.maximum(jnp.sqrt(v1["kk_ss"]), L2_EPS)
            v1["bb"] = kk * v1["rate"]
            ci = v1["ci"]
            v1["a_t"] = -kk * jnp.exp(ci - v1["log_decay"])
            v1["r_t"] = v1["r"] * jnp.exp(ci)
            inv = jnp.exp(-ci)
            v1["b_t"] = v1["bb"] * inv
            v1["k_t"] = v1["kp"] * inv

        def s1_stack(c):
            def emit():
                sl = slice(c * C, (c + 1) * C)
                ci_c = v1["ci"][sl]
                c_last = ci_c[C - 1:C, :]
                to_end = jnp.exp(c_last - ci_c)
                dec1_ref[cur, c] = jnp.broadcast_to(jnp.exp(c_last), (SUBLANES, LANES))
                xr = stack(v1["r_t"][sl])
                xr32_ref[cur, c] = xr
                stk_ref[cur, K_XR, c] = xr.astype(bf16)
                stk_ref[cur, K_XA, c] = stack16(v1["a_t"][sl])
                stk_ref[cur, K_YB, c] = stack16(v1["b_t"][sl])
                stk_ref[cur, K_YK, c] = stack16(v1["k_t"][sl])
                stk_ref[cur, K_VS, c] = stack16(v1["vr"][sl])
                stk_ref[cur, K_BH, c] = stack16(v1["bb"][sl] * to_end)
                stk_ref[cur, K_KH, c] = stack16(v1["kp"][sl] * to_end)
            return emit

        segs1 = ([(0.0, s1_lora), (0.02, s1_key_norm), (0.05, s1_decay), (0.08, s1_bonus)]
                 + _spread([s1_scale] + [s1_stack(c) for c in chunks], RWKV_TAIL_START, 1.0))

        v2 = {}
        ld = lambda kind, c: stk_ref[prv, kind, c]

        def s2_big():
            big = [mm_nt(jnp.concatenate([ld(K_XA, c), ld(K_XR, c)], axis=0),
                         jnp.concatenate([ld(K_YB, c), ld(K_YK, c)], axis=0)) for c in chunks]
            v2["a_ab"] = [jnp.where(strict, big[c][0:G, 0:G], 0.0) for c in chunks]
            v2["a_ak"] = [jnp.where(strict, big[c][0:G, G:2 * G], 0.0).astype(bf16) for c in chunks]
            v2["a_rb"] = [jnp.where(incl, big[c][G:2 * G, 0:G], 0.0).astype(bf16) for c in chunks]
            v2["a_rk"] = [jnp.where(incl, big[c][G:2 * G, G:2 * G], 0.0).astype(bf16) for c in chunks]
            v2["tinv"] = [eye + a for a in v2["a_ab"]]
            v2["pw"] = [a.astype(bf16) for a in v2["a_ab"]]

        def s2_first_square():
            v2["pw"] = [mm(p, p).astype(bf16) for p in v2["pw"]]

        def s2_level(last):
            def emit():
                for c in chunks:
                    pw = v2["pw"][c]
                    t16 = v2["tinv"][c].astype(bf16)
                    if last:
                        v2["tinv"][c] = v2["tinv"][c] + mm(t16, pw)
                    else:
                        both = mm(jnp.concatenate([t16, pw], axis=0), pw)
                        v2["tinv"][c] = v2["tinv"][c] + both[0:G]
                        v2["pw"][c] = both[G:2 * G].astype(bf16)
            return emit

        def s2_av():
            v2["tinv"] = [t.astype(bf16) for t in v2["tinv"]]
            av = [mm(jnp.concatenate([v2["a_ak"][c], v2["a_rk"][c]], axis=0), ld(K_VS, c)) for c in chunks]
            v2["akv"] = [x[0:G].astype(bf16) for x in av]
            v2["arkv"] = [x[G:2 * G] for x in av]

        def s2_wu():
            v2["wu"] = [mm(v2["tinv"][c], jnp.concatenate([ld(K_XA, c), v2["akv"][c]], axis=1)).astype(bf16)
                        for c in chunks]

        def s2_ry():
            for c in chunks:
                e = mm(v2["a_rb"][c], v2["wu"][c])
                p16_ref[cur, 0, c] = (xr32_ref[prv, c] + e[:, 0:LANES]).astype(bf16)
                p32_ref[cur, 0, c] = e[:, LANES:] + v2["arkv"][c]

        def s2_pm():
            for c in chunks:
                p16_ref[cur, 1, c] = mm_tn(v2["wu"][c][:, 0:LANES], ld(K_BH, c)).astype(bf16)

        def s2_q():
            for c in chunks:
                p32_ref[cur, 1, c] = mm_tn(jnp.concatenate([v2["wu"][c][:, LANES:], ld(K_VS, c)], axis=0),
                                           jnp.concatenate([ld(K_BH, c), ld(K_KH, c)], axis=0))
            aux2_ref[cur] = aux1_ref[prv]
            dec2_ref[cur] = dec1_ref[prv]

        segs2 = _spread([s2_big, s2_first_square] + [s2_level(False)] * 4 + [s2_level(True)]
                        + [s2_av, s2_wu, s2_ry, s2_pm, s2_q])

        v3 = {"ys": []}
        prm3 = lambda idx: pc3_ref[idx:idx + 1, :]

        def s3_start():
            v3["state"] = jnp.where(first_of_seq_3, 0.0, state_ref[...])

        def s3_chunk(c):
            def emit():
                state = v3["state"]
                s16 = state.astype(bf16)
                y_st = mm_nt(p16_ref[prv, 0, c], s16) + p32_ref[prv, 0, c]
                v3["ys"].append(y_st[0:C] + y_st[C:2 * C])
                v3["state"] = (state * dec2_ref[prv, c, 0:1, :] + mm(s16, p16_ref[prv, 1, c])
                               + p32_ref[prv, 1, c])
            return emit

        def s3_finish():
            state_ref[...] = v3["state"]
            y = jnp.concatenate(v3["ys"], axis=0)
            mean = head_sum(y) * (1.0 / N)
            yc = y - mean
            var = head_sum(yc * yc) * (1.0 / N)
            yn = yc * lax.rsqrt(var + GN_EPS) * prm3(P_LNW) + prm3(P_LNB)
            o_ref[...] = ((yn + aux2_ref[prv, 0]) * aux2_ref[prv, 1]).astype(o_ref.dtype)

        segs3 = _spread([s3_start] + [s3_chunk(c) for c in chunks] + [s3_finish], 0.0, RWKV_CHAIN_END)

        _interleave(segs3, segs2, segs1)

    for parity in (0, 1):
        @pl.when(s % 2 == parity)
        def _(parity=parity):
            body(cur=parity, prv=1 - parity)


def _rwkv_mix(wide, small, pc, ps, w2p, a2p, batch, seq):
    tb = RWKV_BLOCK
    nt = seq // tb
    groups = RWKV_WIDTH // LANES
    nc = tb // RWKV_CHUNK
    n_blocks = batch * groups * nt

    def where(s):
        s = jnp.clip(s, 0, n_blocks - 1)
        return (s // (nt * groups)) * nt + s % nt, (s // nt) % groups

    col = lambda c0: (lambda s: (where(s)[0], c0 + where(s)[1]))
    fixed_col = lambda c: (lambda s: (where(s)[0], c))
    grp = lambda s: (0, where(s)[1])
    return pl.pallas_call(
        functools.partial(_rwkv_kernel, tb=tb, nt=nt),
        grid=(n_blocks + 2,),
        in_specs=[
            pl.BlockSpec((tb, LANES), col(COL_R)),
            pl.BlockSpec((tb, LANES), col(COL_K)),
            pl.BlockSpec((tb, LANES), col(COL_V)),
            pl.BlockSpec((tb, LANES), col(COL_ZB)),
            pl.BlockSpec((tb, LANES), fixed_col(0)),
            pl.BlockSpec((tb, LANES), fixed_col(1)),
            pl.BlockSpec((P_ROWS, LANES), grp),
            pl.BlockSpec((SUBLANES, LANES), lambda s: (0, 0)),
            pl.BlockSpec((LANES, LANES), grp),
            pl.BlockSpec((LANES, LANES), grp),
            pl.BlockSpec((P_ROWS, LANES), lambda s: grp(s - 2)),
        ],
        out_specs=pl.BlockSpec((tb, LANES), lambda s: (where(s - 2)[0], where(s - 2)[1])),
        out_shape=jax.ShapeDtypeStruct((batch * seq, RWKV_WIDTH), bf16),
        scratch_shapes=[
            pltpu.VMEM((LANES, LANES), f32),
            pltpu.VMEM((6, SUBLANES, LANES), f32),
            pltpu.VMEM((2, 7, nc, GROUP_ROWS, LANES), bf16),
            pltpu.VMEM((2, nc, GROUP_ROWS, LANES), f32),
            pltpu.VMEM((2, 2, tb, LANES), f32),
            pltpu.VMEM((2, nc, SUBLANES, LANES), f32),
            pltpu.VMEM((2, 2, nc, GROUP_ROWS, LANES), bf16),
            pltpu.VMEM((2, 2, nc, GROUP_ROWS, LANES), f32),
            pltpu.VMEM((2, 2, tb, LANES), f32),
            pltpu.VMEM((2, nc, SUBLANES, LANES), f32),
        ],
        compiler_params=pltpu.CompilerParams(
            dimension_semantics=("arbitrary",),
            vmem_limit_bytes=VMEM_LIMIT),
        name="rwkv_mix",
    )(wide, wide, wide, wide, small, small, pc, ps, w2p, a2p, pc)


def _out_kernel(oa_ref, ob_ref, ga_ref, gb_ref, x_ref, wpf_ref, wpr_ref, wo_ref, g_ref, o_ref):
    pa = jnp.dot(oa_ref[...], wpf_ref[...], preferred_element_type=f32)
    pb = jnp.dot(ob_ref[...], wpr_ref[...], preferred_element_type=f32)
    m = _sigmoid(ga_ref[...]) * pa + _sigmoid(gb_ref[...]) * pb
    z = x_ref[...] + jnp.dot(m.astype(bf16), wo_ref[...], preferred_element_type=f32)
    ms = jnp.mean(z * z, axis=-1, keepdims=True)
    o_ref[...] = z * lax.rsqrt(ms + RMS_EPS) * g_ref[...]


def _merge_out(oa, ob, wide, x2d, wpf, wpr, wo, gain):
    m, d = x2d.shape
    tm = 256
    resident = lambda shape: pl.BlockSpec(shape, lambda i: (0, 0), pipeline_mode=pl.Buffered(1))
    return pl.pallas_call(
        _out_kernel,
        grid=(m // tm,),
        in_specs=[
            pl.BlockSpec((tm, FOX_WIDTH), lambda i: (i, 0)),
            pl.BlockSpec((tm, RWKV_WIDTH), lambda i: (i, 0)),
            pl.BlockSpec((tm, d), lambda i: (i, COL_GA * LANES // d)),
            pl.BlockSpec((tm, d), lambda i: (i, COL_GB * LANES // d)),
            pl.BlockSpec((tm, d), lambda i: (i, 0)),
            resident((FOX_WIDTH, d)),
            resident((RWKV_WIDTH, d)),
            resident((d, d)),
            resident((1, d)),
        ],
        out_specs=pl.BlockSpec((tm, d), lambda i: (i, 0)),
        out_shape=jax.ShapeDtypeStruct((m, d), f32),
        compiler_params=pltpu.CompilerParams(
            dimension_semantics=("parallel",),
            vmem_limit_bytes=VMEM_LIMIT),
        name="merge_out",
    )(oa, ob, wide, wide, x2d, wpf, wpr, wo, gain)


def _pad_cols(a, width):
    return jnp.pad(a, ((0, 0), (0, width - a.shape[1])))


def _branches(x2d, batch, seq, norm_gain, w_in, fox_forget_bias, rwkv_shift_mix, rwkv_w0, rwkv_w2,
              rwkv_a0, rwkv_a2, rwkv_k_k, rwkv_k_a, rwkv_r_k, rwkv_ln_w, rwkv_ln_b):
    fw, rw = FOX_WIDTH, RWKV_WIDTH
    w_t = w_in.T
    r_qkv = 0
    r_za = r_qkv + 3 * fw
    r_f = r_za + fw
    r_rkvz = r_f + FOX_HEADS
    r_wd = r_rkvz + 4 * rw
    r_ad = r_wd + LORA
    r_g = r_ad + LORA
    w_main = _relayout_weights(
        w_t, [(r_qkv, 3 * fw), (r_g, 2 * D_MODEL), (r_za, fw), (r_rkvz, 4 * rw)],
        scaled_rows=fw, scale=FOX_HEAD_DIM ** -0.5 * LOG2E)
    pad_rows = lambda a: jnp.pad(a, ((0, LANES - a.shape[0]), (0, 0)))
    w_tail = jnp.concatenate([pad_rows(w_t[r_wd:r_wd + LORA]), pad_rows(w_t[r_ad:r_ad + LORA]),
                              pad_rows(w_t[r_f:r_f + FOX_HEADS])], axis=0)
    gain = norm_gain.reshape(1, D_MODEL)

    qkv, wide, small = _in_proj(x2d, gain, w_main, w_tail)

    bias_row = _pad_cols(fox_forget_bias.reshape(1, FOX_HEADS), LANES)
    c = _gate_cumsum(small, bias_row, batch, seq)
    qkv3 = qkv.reshape(batch, seq, 3 * fw)
    oa = _fox_attention(qkv3, c.reshape(batch, seq, LANES), wide.reshape(batch, seq, WIDE_COLS),
                        batch, seq).reshape(batch * seq, fw)

    mu = rwkv_shift_mix
    pc = jnp.zeros((P_ROWS, rw), f32)
    rows = [mu[0:rw], mu[rw:2 * rw], mu[2 * rw:3 * rw], mu[3 * rw:4 * rw], rwkv_w0, rwkv_a0,
            rwkv_k_k, rwkv_k_a, rwkv_r_k.reshape(rw), rwkv_ln_w, rwkv_ln_b]
    pc = pc.at[:len(rows)].set(jnp.stack(rows))
    ps = jnp.zeros((SUBLANES, LANES), f32)
    ps = ps.at[0, :LORA].set(mu[4 * rw:4 * rw + LORA]).at[1, :LORA].set(mu[4 * rw + LORA:])
    w2p = jnp.pad(rwkv_w2, ((0, LANES - LORA), (0, 0))).astype(bf16)
    a2p = jnp.pad(rwkv_a2, ((0, LANES - LORA), (0, 0))).astype(bf16)
    ob = _rwkv_mix(wide, small, pc, ps, w2p, a2p, batch, seq)

    return oa, ob, wide


def kernel(x, norm_gain, w_in, fox_forget_bias, rwkv_shift_mix, rwkv_w0, rwkv_w2, rwkv_a0, rwkv_a2, rwkv_k_k, rwkv_k_a, rwkv_r_k, rwkv_ln_w, rwkv_ln_b, w_proj_fox, w_proj_rwkv, w_out, final_norm_gain):
    batch, seq, d = x.shape
    depth = norm_gain.shape[0]
    assert depth == 1, "the final rmsnorm is fused into the single layer's output kernel"
    x2d = x.reshape(batch * seq, d)
    oa, ob, wide = _branches(x2d, batch, seq, norm_gain[0], w_in[0], fox_forget_bias[0],
                             rwkv_shift_mix[0], rwkv_w0[0], rwkv_w2[0], rwkv_a0[0], rwkv_a2[0],
                             rwkv_k_k[0], rwkv_k_a[0], rwkv_r_k[0], rwkv_ln_w[0], rwkv_ln_b[0])
    out = _merge_out(oa, ob, wide, x2d, w_proj_fox[0].astype(bf16), w_proj_rwkv[0].astype(bf16),
                     w_out[0].astype(bf16), final_norm_gain.reshape(1, d))
    return out.reshape(batch, seq, d)
```

```python
import functools

import jax
import jax.numpy as jnp
from jax import lax
from jax.experimental import pallas as pl
from jax.experimental.pallas import tpu as pltpu

D_MODEL = 2048
FOX_HEADS = 8
FOX_HEAD_DIM = 128
FOX_WIDTH = FOX_HEADS * FOX_HEAD_DIM
RWKV_HEADS = 16
RWKV_HEAD_DIM = 64
RWKV_WIDTH = RWKV_HEADS * RWKV_HEAD_DIM
LORA = 96
RMS_EPS = 1e-6
GN_EPS = 64e-5
L2_EPS = 1e-12

LANES = 128
SUBLANES = 8
VMEM_LIMIT = 56 * 1024 * 1024

COL_GA = 0
COL_GB = D_MODEL // LANES
COL_ZA = 2 * D_MODEL // LANES
COL_R = COL_ZA + FOX_WIDTH // LANES
COL_K = COL_R + RWKV_WIDTH // LANES
COL_V = COL_K + RWKV_WIDTH // LANES
COL_ZB = COL_V + RWKV_WIDTH // LANES
WIDE_COLS = (COL_ZB + RWKV_WIDTH // LANES) * LANES
SMALL_COLS = 2 * LANES
F_LANE = LORA

RWKV_CHUNK = 64
RWKV_BLOCK = 512
RWKV_TAIL_START = 0.3
RWKV_CHAIN_END = 0.8
HEADS_PER_GROUP = LANES // RWKV_HEAD_DIM
GROUP_ROWS = HEADS_PER_GROUP * RWKV_CHUNK

f32 = jnp.float32
bf16 = jnp.bfloat16


def _dot(a, b):
    return jnp.dot(a.astype(bf16), b.astype(bf16), preferred_element_type=f32)


def _dot_exact_ones(a, ones_bf16, nt=False, terms=3):
    if nt:
        mm = lambda x: lax.dot_general(ones_bf16, x, (((1,), (0,)), ((), ())),
                                       preferred_element_type=f32)
    else:
        mm = lambda x: jnp.dot(x, ones_bf16, preferred_element_type=f32)
    part = a.astype(bf16)
    out = mm(part)
    rest = a
    for _ in range(terms - 1):
        rest = rest - part.astype(f32)
        part = rest.astype(bf16)
        out = out + mm(part)
    return out


def _softplus(x):
    return jnp.maximum(x, 0.0) + jnp.log(1.0 + jnp.exp(-jnp.abs(x)))


def _sigmoid(x):
    return 1.0 / (1.0 + jnp.exp(-x))


PROJ_TM = 1024
PROJ_TN = 1024


def _relayout_kernel(w_ref, o_ref, *, scaled_blocks, scale):
    factor = jnp.where(pl.program_id(0) < scaled_blocks, scale, 1.0)
    o_ref[...] = (w_ref[...] * factor).astype(o_ref.dtype)


def _relayout_weights(w_t, pieces, scaled_rows, scale):
    d = w_t.shape[1]
    rb = PROJ_TN
    assert scaled_rows % rb == 0 and all(n % rb == 0 and src % SUBLANES == 0 for src, n in pieces)
    starts, first_block = [], 0
    for src, n in pieces:
        starts.append((first_block, src))
        first_block += n // rb

    def src_row(i):
        tile = jnp.int32(0)
        for blk, src in starts:
            tile = jnp.where(i >= blk, src // SUBLANES + (i - blk) * (rb // SUBLANES), tile)
        return pl.multiple_of(tile * SUBLANES, SUBLANES)

    return pl.pallas_call(
        functools.partial(_relayout_kernel, scaled_blocks=scaled_rows // rb, scale=scale),
        grid=(first_block,),
        in_specs=[pl.BlockSpec((pl.Element(rb), pl.Element(d)), lambda i: (src_row(i), 0))],
        out_specs=pl.BlockSpec((rb, d), lambda i: (i, 0)),
        out_shape=jax.ShapeDtypeStruct((first_block * rb, d), bf16),
        compiler_params=pltpu.CompilerParams(
            dimension_semantics=("parallel",),
            vmem_limit_bytes=VMEM_LIMIT),
        name="relayout_weights",
    )(w_t)


def _in_proj_kernel(x_ref, g_ref, w_ref, wt_ref, qkv_ref, wide_ref, small_ref, h_ref, *, n_qkv, n_wide):
    j = pl.program_id(1)

    @pl.when(j == 0)
    def _():
        xf = x_ref[...]
        ms = jnp.mean(xf * xf, axis=-1, keepdims=True)
        h_ref[...] = (xf * lax.rsqrt(ms + RMS_EPS) * g_ref[...]).astype(bf16)

    project = lambda w: lax.dot_general(h_ref[...], w, (((1,), (1,)), ((), ())), preferred_element_type=f32)

    @pl.when(j < n_qkv)
    def _():
        qkv_ref[...] = project(w_ref[...]).astype(qkv_ref.dtype)

    @pl.when(jnp.logical_and(j >= n_qkv, j < n_qkv + n_wide))
    def _():
        wide_ref[...] = project(w_ref[...])

    @pl.when(j == n_qkv + n_wide)
    def _():
        small_ref[...] = project(wt_ref[...].astype(bf16))


def _in_proj(x2d, gain, w_main, w_tail):
    m, d = x2d.shape
    tm, tn = PROJ_TM, PROJ_TN
    n_qkv = 3 * FOX_WIDTH // tn
    n_wide = WIDE_COLS // tn
    assert n_qkv * tn == 3 * FOX_WIDTH and n_wide * tn == WIDE_COLS
    assert w_main.shape[0] == 3 * FOX_WIDTH + WIDE_COLS and w_tail.shape[0] == SMALL_COLS
    return pl.pallas_call(
        functools.partial(_in_proj_kernel, n_qkv=n_qkv, n_wide=n_wide),
        grid=(m // tm, n_qkv + n_wide + 1),
        in_specs=[
            pl.BlockSpec((tm, d), lambda i, j: (i, 0)),
            pl.BlockSpec((1, d), lambda i, j: (0, 0)),
            pl.BlockSpec((tn, d), lambda i, j: (jnp.minimum(j, n_qkv + n_wide - 1), 0)),
            pl.BlockSpec((SMALL_COLS, d), lambda i, j: (0, 0)),
        ],
        out_specs=[
            pl.BlockSpec((tm, tn), lambda i, j: (i, jnp.minimum(j, n_qkv - 1))),
            pl.BlockSpec((tm, tn), lambda i, j: (i, jnp.clip(j - n_qkv, 0, n_wide - 1))),
            pl.BlockSpec((tm, SMALL_COLS), lambda i, j: (i, 0)),
        ],
        out_shape=[
            jax.ShapeDtypeStruct((m, 3 * FOX_WIDTH), bf16),
            jax.ShapeDtypeStruct((m, WIDE_COLS), f32),
            jax.ShapeDtypeStruct((m, SMALL_COLS), f32),
        ],
        scratch_shapes=[pltpu.VMEM((tm, d), bf16)],
        compiler_params=pltpu.CompilerParams(
            dimension_semantics=("parallel", "arbitrary"),
            vmem_limit_bytes=VMEM_LIMIT),
        name="in_proj",
    )(x2d, gain, w_main, w_tail)


def _gate_kernel(f_ref, bias_ref, c_ref, carry_ref, *, tb):
    @pl.when(pl.program_id(1) == 0)
    def _():
        carry_ref[...] = jnp.zeros_like(carry_ref)

    z = f_ref[...] + bias_ref[...]
    log_f = -_softplus(-z)
    row = lax.broadcasted_iota(jnp.int32, (tb, tb), 0)
    col = lax.broadcasted_iota(jnp.int32, (tb, tb), 1)
    tri = jnp.where(col <= row, 1.0, 0.0).astype(bf16)
    c = _dot_exact_ones(log_f, tri, nt=True) + carry_ref[0:1, :]
    c_ref[...] = c
    carry_ref[0:1, :] = c[tb - 1:tb, :]


def _gate_cumsum(small, bias_row, batch, seq):
    tb = 512
    nt = seq // tb
    return pl.pallas_call(
        functools.partial(_gate_kernel, tb=tb),
        grid=(batch, nt),
        in_specs=[
            pl.BlockSpec((tb, LANES), lambda b, t: (b * nt + t, 1)),
            pl.BlockSpec((1, LANES), lambda b, t: (0, 0)),
        ],
        out_specs=pl.BlockSpec((tb, LANES), lambda b, t: (b * nt + t, 0)),
        out_shape=jax.ShapeDtypeStruct((batch * seq, LANES), f32),
        scratch_shapes=[pltpu.VMEM((SUBLANES, LANES), f32)],
        compiler_params=pltpu.CompilerParams(
            dimension_semantics=("parallel", "arbitrary")),
        name="fox_gate_cumsum",
    )(small, bias_row)


LOG2E = 1.4426950408889634
FOX_TK = 256
FOX_GW = 128
FOX_BATCH = 4


def _fox_kernel(q_ref, k_ref, v_ref, c_ref, z_ref, o_ref, crep_ref, acc_ref, vt_ref, *, seq, tk, gw):
    h = pl.program_id(1)
    ng = seq // gw
    nt = seq // tk
    assert seq % gw == 0 and seq % tk == 0 and (tk % gw == 0 or gw % tk == 0)
    mm = lambda a, b: jnp.dot(a, b, preferred_element_type=f32)
    mm_nt = lambda a, b: lax.dot_general(a, b, (((1,), (1,)), ((), ())), preferred_element_type=f32)

    src_lane = lax.broadcasted_iota(jnp.int32, (LANES, LANES), 0)
    pick = jnp.where(src_lane == F_LANE + h, 1.0, 0.0).astype(bf16)
    crep = _dot_exact_ones(c_ref[...], pick) * LOG2E
    crep_ref[...] = jnp.concatenate([crep] * (gw // LANES), axis=1)
    for t in range(nt):
        vt_ref[t] = v_ref[t * tk:(t + 1) * tk, :].T

    key_rel = lax.broadcasted_iota(jnp.int32, (tk, gw), 0)
    qry_rel = lax.broadcasted_iota(jnp.int32, (tk, gw), 1)

    def visible(t, g):
        return t * tk <= g * gw + gw - 1

    def needs_mask(t, g):
        return t * tk + tk - 1 > g * gw

    last_tile = [max(t for t in range(nt) if visible(t, g)) for g in range(ng)]
    sched = []
    for t in range(nt):
        groups = [g for g in range(ng) if visible(t, g)]
        sched += [[(t, g) for g in groups[k:k + FOX_BATCH]] for k in range(0, len(groups), FOX_BATCH)]

    def qk(batch):
        return [mm_nt(k_ref[t * tk:(t + 1) * tk, :], q_ref[g * gw:(g + 1) * gw, :]) for t, g in batch]

    m = [None] * ng
    l = [None] * ng

    def finish(batch, alpha, pv):
        for (t, g), a, x in zip(batch, alpha, pv):
            acc_ref[g] = x if t == 0 else a * acc_ref[g] + x
            if t == last_tile[g]:
                rows = slice(g * gw, (g + 1) * gw)
                z = z_ref[rows, :]
                o = (acc_ref[g] / l[g]).T
                o_ref[rows, :] = (o * (z * _sigmoid(z))).astype(o_ref.dtype)

    s_next = qk(sched[0])
    pending = None
    for bi, batch in enumerate(sched):
        s_cur = s_next
        if bi + 1 < len(sched):
            s_next = qk(sched[bi + 1])
        alpha, p16 = [], []
        for (t, g), s in zip(batch, s_cur):
            s = s - crep_ref[t * tk:(t + 1) * tk, :]
            if needs_mask(t, g):
                s = jnp.where(key_rel + t * tk > qry_rel + g * gw, -jnp.inf, s)
            m_tile = jnp.max(s, axis=0, keepdims=True)
            if t == 0:
                m_new, a = m_tile, None
            else:
                m_new = jnp.maximum(m[g], m_tile)
                a = jnp.exp2(m[g] - m_new)
            p = jnp.exp2(s - m_new)
            p_sum = jnp.sum(p, axis=0, keepdims=True)
            l[g] = p_sum if t == 0 else a * l[g] + p_sum
            m[g] = m_new
            alpha.append(a)
            p16.append(p.astype(bf16))
        pv = [mm(vt_ref[t], p) for (t, g), p in zip(batch, p16)]
        if pending is not None:
            finish(*pending)
        pending = (batch, alpha, pv)
    finish(*pending)


def _fox_attention(qkv3, c3, wide3, batch, seq):
    tk, gw = FOX_TK, FOX_GW
    h8 = FOX_HEADS
    return pl.pallas_call(
        functools.partial(_fox_kernel, seq=seq, tk=tk, gw=gw),
        grid=(batch, h8),
        in_specs=[
            pl.BlockSpec((None, seq, LANES), lambda b, h: (b, 0, h)),
            pl.BlockSpec((None, seq, LANES), lambda b, h: (b, 0, h8 + h)),
            pl.BlockSpec((None, seq, LANES), lambda b, h: (b, 0, 2 * h8 + h)),
            pl.BlockSpec((None, seq, LANES), lambda b, h: (b, 0, 0)),
            pl.BlockSpec((None, seq, LANES), lambda b, h: (b, 0, COL_ZA + h)),
        ],
        out_specs=pl.BlockSpec((None, seq, LANES), lambda b, h: (b, 0, h)),
        out_shape=jax.ShapeDtypeStruct((batch, seq, FOX_WIDTH), bf16),
        scratch_shapes=[
            pltpu.VMEM((seq, gw), f32),
            pltpu.VMEM((seq // gw, LANES, gw), f32),
            pltpu.VMEM((seq // tk, LANES, tk), bf16),
        ],
        compiler_params=pltpu.CompilerParams(
            dimension_semantics=("parallel", "arbitrary"),
            vmem_limit_bytes=VMEM_LIMIT),
        name="fox_attention",
    )(qkv3, qkv3, qkv3, c3, wide3)


P_MU_R, P_MU_K, P_MU_V, P_MU_Z, P_W0, P_A0, P_KK, P_KA, P_RK, P_LNW, P_LNB = range(11)
P_ROWS = 16
K_XA, K_XR, K_YB, K_YK, K_VS, K_BH, K_KH = range(7)


def _spread(emitters, lo=0.0, hi=1.0):
    n = len(emitters)
    return [(lo + (hi - lo) * (i + 0.5) / n, e) for i, e in enumerate(emitters)]


def _interleave(*segment_lists):
    keyed = [(pos, prio, seg) for prio, segs in enumerate(segment_lists) for pos, seg in segs]
    keyed.sort(key=lambda x: (x[0], x[1]))
    for _, _, seg in keyed:
        seg()


def _rwkv_kernel(r_ref, k_ref, v_ref, z_ref, wd_ref, ad_ref, pc_ref, ps_ref, w2_ref, a2_ref, pc3_ref,
                 o_ref,
                 state_ref, prev_ref, stk_ref, xr32_ref, aux1_ref, dec1_ref, p16_ref, p32_ref,
                 aux2_ref, dec2_ref, *, tb, nt):
    C = RWKV_CHUNK
    G = GROUP_ROWS
    N = RWKV_HEAD_DIM
    NC = tb // C
    s = pl.program_id(0)
    first_of_seq_1 = (s % nt) == 0
    first_of_seq_3 = ((s + 2 * nt - 2) % nt) == 0

    @pl.when(s == 0)
    def _():
        for ref in (state_ref, prev_ref, stk_ref, xr32_ref, aux1_ref, dec1_ref, p16_ref, p32_ref,
                    aux2_ref, dec2_ref):
            ref[...] = jnp.zeros_like(ref)

    mm = lambda a, b: jnp.dot(a, b, preferred_element_type=f32)
    mm_nt = lambda a, b: lax.dot_general(a, b, (((1,), (1,)), ((), ())), preferred_element_type=f32)
    mm_tn = lambda a, b: lax.dot_general(a, b, (((0,), (0,)), ((), ())), preferred_element_type=f32)

    def body(cur, prv):
        li = lax.broadcasted_iota(jnp.int32, (LANES, LANES), 0)
        lj = lax.broadcasted_iota(jnp.int32, (LANES, LANES), 1)
        head_ones = jnp.where(li // N == lj // N, 1.0, 0.0).astype(bf16)
        head_sum = lambda x: _dot_exact_ones(x, head_ones, terms=2)
        lane = lax.broadcasted_iota(jnp.int32, (1, LANES), 1)
        head_masks = [jnp.where(lane // N == h, 1.0, 0.0) for h in range(HEADS_PER_GROUP)]
        stack = lambda x: jnp.concatenate([x * hm for hm in head_masks], axis=0)
        lane_c = lax.broadcasted_iota(jnp.int32, (C, LANES), 1)
        head_sel = [lane_c // N == h for h in range(HEADS_PER_GROUP)]

        def stack16(x):
            xb = x.astype(bf16)
            return jnp.concatenate([jnp.where(m, xb, jnp.zeros_like(xb)) for m in head_sel], axis=0)

        gi = lax.broadcasted_iota(jnp.int32, (G, G), 0)
        gj = lax.broadcasted_iota(jnp.int32, (G, G), 1)
        same_head = gi // C == gj // C
        strict = jnp.logical_and(same_head, gj < gi)
        incl = jnp.logical_and(same_head, gj <= gi)
        eye = jnp.where(gi == gj, 1.0, 0.0)
        chunks = range(NC)

        v1 = {}
        prm = lambda idx: pc_ref[idx:idx + 1, :]
        row_in_block = lax.broadcasted_iota(jnp.int32, (tb, LANES), 0)

        def shifted(ref, slot, mu):
            u = ref[...]
            carry = jnp.where(first_of_seq_1, 0.0, prev_ref[slot, 0:1, :])
            prev = jnp.where(row_in_block == 0, carry, pltpu.roll(u, 1, 0))
            prev_ref[slot, 0:1, :] = u[tb - 1:tb, :]
            return u + (prev - u) * mu

        def s1_lora():
            v1["wd"] = shifted(wd_ref, 4, ps_ref[0:1, :])
            v1["ad"] = shifted(ad_ref, 5, ps_ref[1:2, :])
            v1["w_lin"] = _dot(jnp.tanh(v1["wd"]), w2_ref[...])
            v1["a_lin"] = _dot(v1["ad"], a2_ref[...])

        def s1_key_norm():
            v1["kr"] = shifted(k_ref, 1, prm(P_MU_K))
            kk = v1["kr"] * prm(P_KK)
            v1["kk_raw"] = kk
            v1["kk_ss"] = head_sum(kk * kk)

        def s1_decay():
            w = -_softplus(-(prm(P_W0) + v1["w_lin"])) - 0.5
            v1["log_decay"] = -jnp.exp(w)
            ti = lax.broadcasted_iota(jnp.int32, (2 * C, 2 * C), 0)
            tj = lax.broadcasted_iota(jnp.int32, (2 * C, 2 * C), 1)
            pair_tri = jnp.where(jnp.logical_and(tj <= ti, ti // C == tj // C), 1.0, 0.0).astype(bf16)
            v1["ci"] = jnp.concatenate(
                [_dot_exact_ones(v1["log_decay"][k:k + 2 * C], pair_tri, nt=True)
                 for k in range(0, tb, 2 * C)], axis=0)

        def s1_bonus():
            v1["rate"] = _sigmoid(prm(P_A0) + v1["a_lin"])
            v1["r"] = shifted(r_ref, 0, prm(P_MU_R))
            v1["vr"] = shifted(v_ref, 2, prm(P_MU_V))
            v1["kp"] = v1["kr"] * (1.0 + (v1["rate"] - 1.0) * prm(P_KA))
            aux1_ref[cur, 0] = head_sum(v1["r"] * v1["kp"] * prm(P_RK)) * v1["vr"]
            zb = shifted(z_ref, 3, prm(P_MU_Z))
            aux1_ref[cur, 1] = zb * _sigmoid(zb)

        def s1_scale():
            kk = v1["kk_raw"] / jnp.maximum(jnp.sqrt(v1["kk_ss"]), L2_EPS)
            v1["bb"] = kk * v1["rate"]
            ci = v1["ci"]
            v1["a_t"] = -kk * jnp.exp(ci - v1["log_decay"])
            v1["r_t"] = v1["r"] * jnp.exp(ci)
            inv = jnp.exp(-ci)
            v1["b_t"] = v1["bb"] * inv
            v1["k_t"] = v1["kp"] * inv

        def s1_stack(c):
            def emit():
                sl = slice(c * C, (c + 1) * C)
                ci_c = v1["ci"][sl]
                c_last = ci_c[C - 1:C, :]
                to_end = jnp.exp(c_last - ci_c)
                dec1_ref[cur, c] = jnp.broadcast_to(jnp.exp(c_last), (SUBLANES, LANES))
                xr = stack(v1["r_t"][sl])
                xr32_ref[cur, c] = xr
                stk_ref[cur, K_XR, c] = xr.astype(bf16)
                stk_ref[cur, K_XA, c] = stack16(v1["a_t"][sl])
                stk_ref[cur, K_YB, c] = stack16(v1["b_t"][sl])
                stk_ref[cur, K_YK, c] = stack16(v1["k_t"][sl])
                stk_ref[cur, K_VS, c] = stack16(v1["vr"][sl])
                stk_ref[cur, K_BH, c] = stack16(v1["bb"][sl] * to_end)
                stk_ref[cur, K_KH, c] = stack16(v1["kp"][sl] * to_end)
            return emit

        segs1 = ([(0.0, s1_lora), (0.02, s1_key_norm), (0.05, s1_decay), (0.08, s1_bonus)]
                 + _spread([s1_scale] + [s1_stack(c) for c in chunks], RWKV_TAIL_START, 1.0))

        v2 = {}
        ld = lambda kind, c: stk_ref[prv, kind, c]

        def s2_big():
            big = [mm_nt(jnp.concatenate([ld(K_XA, c), ld(K_XR, c)], axis=0),
                         jnp.concatenate([ld(K_YB, c), ld(K_YK, c)], axis=0)) for c in chunks]
            v2["a_ab"] = [jnp.where(strict, big[c][0:G, 0:G], 0.0) for c in chunks]
            v2["a_ak"] = [jnp.where(strict, big[c][0:G, G:2 * G], 0.0).astype(bf16) for c in chunks]
            v2["a_rb"] = [jnp.where(incl, big[c][G:2 * G, 0:G], 0.0).astype(bf16) for c in chunks]
            v2["a_rk"] = [jnp.where(incl, big[c][G:2 * G, G:2 * G], 0.0).astype(bf16) for c in chunks]
            v2["tinv"] = [eye + a for a in v2["a_ab"]]
            v2["pw"] = [a.astype(bf16) for a in v2["a_ab"]]

        def s2_first_square():
            v2["pw"] = [mm(p, p).astype(bf16) for p in v2["pw"]]

        def s2_level(last):
            def emit():
                for c in chunks:
                    pw = v2["pw"][c]
                    t16 = v2["tinv"][c].astype(bf16)
                    if last:
                        v2["tinv"][c] = v2["tinv"][c] + mm(t16, pw)
                    else:
                        both = mm(jnp.concatenate([t16, pw], axis=0), pw)
                        v2["tinv"][c] = v2["tinv"][c] + both[0:G]
                        v2["pw"][c] = both[G:2 * G].astype(bf16)
            return emit

        def s2_av():
            v2["tinv"] = [t.astype(bf16) for t in v2["tinv"]]
            av = [mm(jnp.concatenate([v2["a_ak"][c], v2["a_rk"][c]], axis=0), ld(K_VS, c)) for c in chunks]
            v2["akv"] = [x[0:G].astype(bf16) for x in av]
            v2["arkv"] = [x[G:2 * G] for x in av]

        def s2_wu():
            v2["wu"] = [mm(v2["tinv"][c], jnp.concatenate([ld(K_XA, c), v2["akv"][c]], axis=1)).astype(bf16)
                        for c in chunks]

        def s2_ry():
            for c in chunks:
                e = mm(v2["a_rb"][c], v2["wu"][c])
                p16_ref[cur, 0, c] = (xr32_ref[prv, c] + e[:, 0:LANES]).astype(bf16)
                p32_ref[cur, 0, c] = e[:, LANES:] + v2["arkv"][c]

        def s2_pm():
            for c in chunks:
                p16_ref[cur, 1, c] = mm_tn(v2["wu"][c][:, 0:LANES], ld(K_BH, c)).astype(bf16)

        def s2_q():
            for c in chunks:
                p32_ref[cur, 1, c] = mm_tn(jnp.concatenate([v2["wu"][c][:, LANES:], ld(K_VS, c)], axis=0),
                                           jnp.concatenate([ld(K_BH, c), ld(K_KH, c)], axis=0))
            aux2_ref[cur] = aux1_ref[prv]
            dec2_ref[cur] = dec1_ref[prv]

        segs2 = _spread([s2_big, s2_first_square] + [s2_level(False)] * 4 + [s2_level(True)]
                        + [s2_av, s2_wu, s2_ry, s2_pm, s2_q])

        v3 = {"ys": []}
        prm3 = lambda idx: pc3_ref[idx:idx + 1, :]

        def s3_start():
            v3["state"] = jnp.where(first_of_seq_3, 0.0, state_ref[...])

        def s3_chunk(c):
            def emit():
                state = v3["state"]
                s16 = state.astype(bf16)
                y_st = mm_nt(p16_ref[prv, 0, c], s16) + p32_ref[prv, 0, c]
                v3["ys"].append(y_st[0:C] + y_st[C:2 * C])
                v3["state"] = (state * dec2_ref[prv, c, 0:1, :] + mm(s16, p16_ref[prv, 1, c])
                               + p32_ref[prv, 1, c])
            return emit

        def s3_finish():
            state_ref[...] = v3["state"]
            y = jnp.concatenate(v3["ys"], axis=0)
            mean = head_sum(y) * (1.0 / N)
            yc = y - mean
            var = head_sum(yc * yc) * (1.0 / N)
            yn = yc * lax.rsqrt(var + GN_EPS) * prm3(P_LNW) + prm3(P_LNB)
            o_ref[...] = ((yn + aux2_ref[prv, 0]) * aux2_ref[prv, 1]).astype(o_ref.dtype)

        segs3 = _spread([s3_start] + [s3_chunk(c) for c in chunks] + [s3_finish], 0.0, RWKV_CHAIN_END)

        _interleave(segs3, segs2, segs1)

    for parity in (0, 1):
        @pl.when(s % 2 == parity)
        def _(parity=parity):
            body(cur=parity, prv=1 - parity)


def _rwkv_mix(wide, small, pc, ps, w2p, a2p, batch, seq):
    tb = RWKV_BLOCK
    nt = seq // tb
    groups = RWKV_WIDTH // LANES
    nc = tb // RWKV_CHUNK
    n_blocks = batch * groups * nt

    def where(s):
        s = jnp.clip(s, 0, n_blocks - 1)
        return (s // (nt * groups)) * nt + s % nt, (s // nt) % groups

    col = lambda c0: (lambda s: (where(s)[0], c0 + where(s)[1]))
    fixed_col = lambda c: (lambda s: (where(s)[0], c))
    grp = lambda s: (0, where(s)[1])
    return pl.pallas_call(
        functools.partial(_rwkv_kernel, tb=tb, nt=nt),
        grid=(n_blocks + 2,),
        in_specs=[
            pl.BlockSpec((tb, LANES), col(COL_R)),
            pl.BlockSpec((tb, LANES), col(COL_K)),
            pl.BlockSpec((tb, LANES), col(COL_V)),
            pl.BlockSpec((tb, LANES), col(COL_ZB)),
            pl.BlockSpec((tb, LANES), fixed_col(0)),
            pl.BlockSpec((tb, LANES), fixed_col(1)),
            pl.BlockSpec((P_ROWS, LANES), grp),
            pl.BlockSpec((SUBLANES, LANES), lambda s: (0, 0)),
            pl.BlockSpec((LANES, LANES), grp),
            pl.BlockSpec((LANES, LANES), grp),
            pl.BlockSpec((P_ROWS, LANES), lambda s: grp(s - 2)),
        ],
        out_specs=pl.BlockSpec((tb, LANES), lambda s: (where(s - 2)[0], where(s - 2)[1])),
        out_shape=jax.ShapeDtypeStruct((batch * seq, RWKV_WIDTH), bf16),
        scratch_shapes=[
            pltpu.VMEM((LANES, LANES), f32),
            pltpu.VMEM((6, SUBLANES, LANES), f32),
            pltpu.VMEM((2, 7, nc, GROUP_ROWS, LANES), bf16),
            pltpu.VMEM((2, nc, GROUP_ROWS, LANES), f32),
            pltpu.VMEM((2, 2, tb, LANES), f32),
            pltpu.VMEM((2, nc, SUBLANES, LANES), f32),
            pltpu.VMEM((2, 2, nc, GROUP_ROWS, LANES), bf16),
            pltpu.VMEM((2, 2, nc, GROUP_ROWS, LANES), f32),
            pltpu.VMEM((2, 2, tb, LANES), f32),
            pltpu.VMEM((2, nc, SUBLANES, LANES), f32),
        ],
        compiler_params=pltpu.CompilerParams(
            dimension_semantics=("arbitrary",),
            vmem_limit_bytes=VMEM_LIMIT),
        name="rwkv_mix",
    )(wide, wide, wide, wide, small, small, pc, ps, w2p, a2p, pc)


def _out_kernel(oa_ref, ob_ref, ga_ref, gb_ref, x_ref, wpf_ref, wpr_ref, wo_ref, g_ref, o_ref):
    pa = jnp.dot(oa_ref[...], wpf_ref[...], preferred_element_type=f32)
    pb = jnp.dot(ob_ref[...], wpr_ref[...], preferred_element_type=f32)
    m = _sigmoid(ga_ref[...]) * pa + _sigmoid(gb_ref[...]) * pb
    z = x_ref[...] + jnp.dot(m.astype(bf16), wo_ref[...], preferred_element_type=f32)
    ms = jnp.mean(z * z, axis=-1, keepdims=True)
    o_ref[...] = z * lax.rsqrt(ms + RMS_EPS) * g_ref[...]


def _merge_out(oa, ob, wide, x2d, wpf, wpr, wo, gain):
    m, d = x2d.shape
    tm = 256
    resident = lambda shape: pl.BlockSpec(shape, lambda i: (0, 0), pipeline_mode=pl.Buffered(1))
    return pl.pallas_call(
        _out_kernel,
        grid=(m // tm,),
        in_specs=[
            pl.BlockSpec((tm, FOX_WIDTH), lambda i: (i, 0)),
            pl.BlockSpec((tm, RWKV_WIDTH), lambda i: (i, 0)),
            pl.BlockSpec((tm, d), lambda i: (i, COL_GA * LANES // d)),
            pl.BlockSpec((tm, d), lambda i: (i, COL_GB * LANES // d)),
            pl.BlockSpec((tm, d), lambda i: (i, 0)),
            resident((FOX_WIDTH, d)),
            resident((RWKV_WIDTH, d)),
            resident((d, d)),
            resident((1, d)),
        ],
        out_specs=pl.BlockSpec((tm, d), lambda i: (i, 0)),
        out_shape=jax.ShapeDtypeStruct((m, d), f32),
        compiler_params=pltpu.CompilerParams(
            dimension_semantics=("parallel",),
            vmem_limit_bytes=VMEM_LIMIT),
        name="merge_out",
    )(oa, ob, wide, wide, x2d, wpf, wpr, wo, gain)


def _branches(x2d, batch, seq, norm_gain, w_in, fox_forget_bias, rwkv_shift_mix, rwkv_w0, rwkv_w2,
              rwkv_a0, rwkv_a2, rwkv_k_k, rwkv_k_a, rwkv_r_k, rwkv_ln_w, rwkv_ln_b):
    fw, rw = FOX_WIDTH, RWKV_WIDTH
    w_t = w_in.T
    r_qkv = 0
    r_za = r_qkv + 3 * fw
    r_f = r_za + fw
    r_rkvz = r_f + FOX_HEADS
    r_wd = r_rkvz + 4 * rw
    r_ad = r_wd + LORA
    r_g = r_ad + LORA
    w_main = _relayout_weights(
        w_t, [(r_qkv, 3 * fw), (r_g, 2 * D_MODEL), (r_za, fw), (r_rkvz, 4 * rw)],
        scaled_rows=fw, scale=FOX_HEAD_DIM ** -0.5 * LOG2E)
    pad_rows = lambda a: jnp.pad(a, ((0, LANES - a.shape[0]), (0, 0)))
    w_tail = jnp.concatenate(
        [pad_rows(w_t[r_wd:r_wd + LORA]),
         pad_rows(jnp.concatenate([w_t[r_ad:r_ad + LORA], w_t[r_f:r_f + FOX_HEADS]], axis=0))],
        axis=0)
    gain = norm_gain.reshape(1, D_MODEL)

    qkv, wide, small = _in_proj(x2d, gain, w_main, w_tail)

    bias_row = jnp.pad(fox_forget_bias.reshape(1, FOX_HEADS), ((0, 0), (F_LANE, LANES - F_LANE - FOX_HEADS)))
    c = _gate_cumsum(small, bias_row, batch, seq)
    qkv3 = qkv.reshape(batch, seq, 3 * fw)
    oa = _fox_attention(qkv3, c.reshape(batch, seq, LANES), wide.reshape(batch, seq, WIDE_COLS),
                        batch, seq).reshape(batch * seq, fw)

    mu = rwkv_shift_mix
    pc = jnp.zeros((P_ROWS, rw), f32)
    rows = [mu[0:rw], mu[rw:2 * rw], mu[2 * rw:3 * rw], mu[3 * rw:4 * rw], rwkv_w0, rwkv_a0,
            rwkv_k_k, rwkv_k_a, rwkv_r_k.reshape(rw), rwkv_ln_w, rwkv_ln_b]
    pc = pc.at[:len(rows)].set(jnp.stack(rows))
    ps = jnp.zeros((SUBLANES, LANES), f32)
    ps = ps.at[0, :LORA].set(mu[4 * rw:4 * rw + LORA]).at[1, :LORA].set(mu[4 * rw + LORA:])
    w2p = jnp.pad(rwkv_w2, ((0, LANES - LORA), (0, 0))).astype(bf16)
    a2p = jnp.pad(rwkv_a2, ((0, LANES - LORA), (0, 0))).astype(bf16)
    ob = _rwkv_mix(wide, small, pc, ps, w2p, a2p, batch, seq)

    return oa, ob, wide


def kernel(x, norm_gain, w_in, fox_forget_bias, rwkv_shift_mix, rwkv_w0, rwkv_w2, rwkv_a0, rwkv_a2, rwkv_k_k, rwkv_k_a, rwkv_r_k, rwkv_ln_w, rwkv_ln_b, w_proj_fox, w_proj_rwkv, w_out, final_norm_gain):
    batch, seq, d = x.shape
    depth = norm_gain.shape[0]
    assert depth == 1, "the final rmsnorm is fused into the single layer's output kernel"
    x2d = x.reshape(batch * seq, d)
    oa, ob, wide = _branches(x2d, batch, seq, norm_gain[0], w_in[0], fox_forget_bias[0],
                             rwkv_shift_mix[0], rwkv_w0[0], rwkv_w2[0], rwkv_a0[0], rwkv_a2[0],
                             rwkv_k_k[0], rwkv_k_a[0], rwkv_r_k[0], rwkv_ln_w[0], rwkv_ln_b[0])
    out = _merge_out(oa, ob, wide, x2d, w_proj_fox[0].astype(bf16), w_proj_rwkv[0].astype(bf16),
                     w_out[0].astype(bf16), final_norm_gain.reshape(1, d))
    return out.reshape(batch, seq, d)
```

```python
import functools

import jax
import jax.numpy as jnp
from jax import lax
from jax.experimental import pallas as pl
from jax.experimental.pallas import tpu as pltpu

D_MODEL = 2048
FOX_HEADS = 8
FOX_HEAD_DIM = 128
FOX_WIDTH = FOX_HEADS * FOX_HEAD_DIM
RWKV_HEADS = 16
RWKV_HEAD_DIM = 64
RWKV_WIDTH = RWKV_HEADS * RWKV_HEAD_DIM
LORA = 96
RMS_EPS = 1e-6
GN_EPS = 64e-5
L2_EPS = 1e-12

LANES = 128
SUBLANES = 8
VMEM_LIMIT = 56 * 1024 * 1024

COL_GA = 0
COL_GB = D_MODEL // LANES
COL_ZA = 2 * D_MODEL // LANES
COL_R = COL_ZA + FOX_WIDTH // LANES
COL_K = COL_R + RWKV_WIDTH // LANES
COL_V = COL_K + RWKV_WIDTH // LANES
COL_ZB = COL_V + RWKV_WIDTH // LANES
WIDE_COLS = (COL_ZB + RWKV_WIDTH // LANES) * LANES
SMALL_COLS = 2 * LANES
F_LANE = LORA

RWKV_CHUNK = 64
RWKV_BLOCK = 512
RWKV_TAIL_START = 0.3
RWKV_CHAIN_END = 0.8
HEADS_PER_GROUP = LANES // RWKV_HEAD_DIM
GROUP_ROWS = HEADS_PER_GROUP * RWKV_CHUNK

f32 = jnp.float32
bf16 = jnp.bfloat16


def _dot(a, b):
    return jnp.dot(a.astype(bf16), b.astype(bf16), preferred_element_type=f32)


def _dot_exact_ones(a, ones_bf16, nt=False, terms=3):
    if nt:
        mm = lambda x: lax.dot_general(ones_bf16, x, (((1,), (0,)), ((), ())),
                                       preferred_element_type=f32)
    else:
        mm = lambda x: jnp.dot(x, ones_bf16, preferred_element_type=f32)
    part = a.astype(bf16)
    out = mm(part)
    rest = a
    for _ in range(terms - 1):
        rest = rest - part.astype(f32)
        part = rest.astype(bf16)
        out = out + mm(part)
    return out


def _softplus(x):
    return jnp.maximum(x, 0.0) + jnp.log(1.0 + jnp.exp(-jnp.abs(x)))


def _sigmoid(x):
    return 1.0 / (1.0 + jnp.exp(-x))


PROJ_TM = 1024
PROJ_TN = 1024


def _rmsnorm_kernel(x_ref, g_ref, h_ref):
    xf = x_ref[...]
    ms = jnp.mean(xf * xf, axis=-1, keepdims=True)
    h_ref[...] = (xf * lax.rsqrt(ms + RMS_EPS) * g_ref[...]).astype(h_ref.dtype)


def _rmsnorm_bf16(x2d, gain):
    m, d = x2d.shape
    tm = 512
    return pl.pallas_call(
        _rmsnorm_kernel,
        grid=(m // tm,),
        in_specs=[pl.BlockSpec((tm, d), lambda i: (i, 0)), pl.BlockSpec((1, d), lambda i: (0, 0))],
        out_specs=pl.BlockSpec((tm, d), lambda i: (i, 0)),
        out_shape=jax.ShapeDtypeStruct((m, d), bf16),
        compiler_params=pltpu.CompilerParams(dimension_semantics=("parallel",), vmem_limit_bytes=VMEM_LIMIT),
        name="rmsnorm",
    )(x2d, gain)


def _in_proj_kernel(h_ref, w_ref, wt_ref, qkv_ref, wide_ref, small_ref, w16_ref, *, n_qkv, n_wide, scaled_tiles, scale):
    j = pl.program_id(0)
    i = pl.program_id(1)
    n_main = n_qkv + n_wide

    @pl.when(i == 0)
    def _():
        @pl.when(j < n_main)
        def _():
            factor = jnp.where(j < scaled_tiles, scale, 1.0)
            w16_ref[...] = (w_ref[...] * factor).astype(bf16)

        @pl.when(j == n_main)
        def _():
            w16_ref[0:SMALL_COLS, :] = wt_ref[...].astype(bf16)

    project = lambda w: lax.dot_general(h_ref[...], w, (((1,), (1,)), ((), ())), preferred_element_type=f32)

    @pl.when(j < n_qkv)
    def _():
        qkv_ref[...] = project(w16_ref[...]).astype(qkv_ref.dtype)

    @pl.when(jnp.logical_and(j >= n_qkv, j < n_main))
    def _():
        wide_ref[...] = project(w16_ref[...])

    @pl.when(j == n_main)
    def _():
        small_ref[...] = project(w16_ref[0:SMALL_COLS, :])


def _in_proj(h2d, w_t, pieces, scaled_rows, scale, w_tail):
    m, d = h2d.shape
    tm, tn = PROJ_TM, PROJ_TN
    n_qkv = 3 * FOX_WIDTH // tn
    n_wide = WIDE_COLS // tn
    n_main = n_qkv + n_wide
    n_i = m // tm
    assert n_qkv * tn == 3 * FOX_WIDTH and n_wide * tn == WIDE_COLS and w_tail.shape[0] == SMALL_COLS
    assert scaled_rows % tn == 0 and all(n % tn == 0 and src % SUBLANES == 0 for src, n in pieces)
    assert sum(n for _, n in pieces) == n_main * tn
    starts, first_tile = [], 0
    for src, n in pieces:
        starts.append((first_tile, src))
        first_tile += n // tn

    def src_row(j):
        tile = jnp.int32(0)
        for blk, src in starts:
            tile = jnp.where(j >= blk, src // SUBLANES + (jnp.minimum(j, n_main - 1) - blk) * (tn // SUBLANES), tile)
        return pl.multiple_of(tile * SUBLANES, SUBLANES)

    qkv_row = lambda j, i: jnp.where(j < n_qkv, i, n_i - 1)
    wide_row = lambda j, i: jnp.where(j < n_qkv, 0, jnp.where(j < n_main, i, n_i - 1))
    small_row = lambda j, i: jnp.where(j == n_main, i, 0)
    return pl.pallas_call(
        functools.partial(_in_proj_kernel, n_qkv=n_qkv, n_wide=n_wide, scaled_tiles=scaled_rows // tn,
                          scale=scale),
        grid=(n_main + 1, n_i),
        in_specs=[
            pl.BlockSpec((tm, d), lambda j, i: (i, 0)),
            pl.BlockSpec((pl.Element(tn), pl.Element(d)), lambda j, i: (src_row(j), 0)),
            pl.BlockSpec((SMALL_COLS, d), lambda j, i: (0, 0)),
        ],
        out_specs=[
            pl.BlockSpec((tm, tn), lambda j, i: (qkv_row(j, i), jnp.minimum(j, n_qkv - 1))),
            pl.BlockSpec((tm, tn), lambda j, i: (wide_row(j, i), jnp.clip(j - n_qkv, 0, n_wide - 1))),
            pl.BlockSpec((tm, SMALL_COLS), lambda j, i: (small_row(j, i), 0)),
        ],
        out_shape=[
            jax.ShapeDtypeStruct((m, 3 * FOX_WIDTH), bf16),
            jax.ShapeDtypeStruct((m, WIDE_COLS), f32),
            jax.ShapeDtypeStruct((m, SMALL_COLS), f32),
        ],
        scratch_shapes=[pltpu.VMEM((tn, d), bf16)],
        compiler_params=pltpu.CompilerParams(
            dimension_semantics=("arbitrary", "arbitrary"),
            vmem_limit_bytes=VMEM_LIMIT),
        name="in_proj",
    )(h2d, w_t, w_tail)


def _gate_kernel(f_ref, bias_ref, c_ref, carry_ref, *, tb):
    @pl.when(pl.program_id(1) == 0)
    def _():
        carry_ref[...] = jnp.zeros_like(carry_ref)

    z = f_ref[...] + bias_ref[...]
    log_f = -_softplus(-z)
    row = lax.broadcasted_iota(jnp.int32, (tb, tb), 0)
    col = lax.broadcasted_iota(jnp.int32, (tb, tb), 1)
    tri = jnp.where(col <= row, 1.0, 0.0).astype(bf16)
    c = _dot_exact_ones(log_f, tri, nt=True) + carry_ref[0:1, :]
    c_ref[...] = c
    carry_ref[0:1, :] = c[tb - 1:tb, :]


def _gate_cumsum(small, bias_row, batch, seq):
    tb = 512
    nt = seq // tb
    return pl.pallas_call(
        functools.partial(_gate_kernel, tb=tb),
        grid=(batch, nt),
        in_specs=[
            pl.BlockSpec((tb, LANES), lambda b, t: (b * nt + t, 1)),
            pl.BlockSpec((1, LANES), lambda b, t: (0, 0)),
        ],
        out_specs=pl.BlockSpec((tb, LANES), lambda b, t: (b * nt + t, 0)),
        out_shape=jax.ShapeDtypeStruct((batch * seq, LANES), f32),
        scratch_shapes=[pltpu.VMEM((SUBLANES, LANES), f32)],
        compiler_params=pltpu.CompilerParams(
            dimension_semantics=("parallel", "arbitrary")),
        name="fox_gate_cumsum",
    )(small, bias_row)


LOG2E = 1.4426950408889634
FOX_TK = 256
FOX_GW = 128
FOX_BATCH = 4


def _fox_kernel(q_ref, k_ref, v_ref, c_ref, z_ref, o_ref, crep_ref, acc_ref, vt_ref, *, seq, tk, gw):
    h = pl.program_id(1)
    ng = seq // gw
    nt = seq // tk
    assert seq % gw == 0 and seq % tk == 0 and (tk % gw == 0 or gw % tk == 0)
    mm = lambda a, b: jnp.dot(a, b, preferred_element_type=f32)
    mm_nt = lambda a, b: lax.dot_general(a, b, (((1,), (1,)), ((), ())), preferred_element_type=f32)

    src_lane = lax.broadcasted_iota(jnp.int32, (LANES, LANES), 0)
    pick = jnp.where(src_lane == F_LANE + h, 1.0, 0.0).astype(bf16)
    crep = _dot_exact_ones(c_ref[...], pick) * LOG2E
    crep_ref[...] = jnp.concatenate([crep] * (gw // LANES), axis=1)
    for t in range(nt):
        vt_ref[t] = v_ref[t * tk:(t + 1) * tk, :].T

    key_rel = lax.broadcasted_iota(jnp.int32, (tk, gw), 0)
    qry_rel = lax.broadcasted_iota(jnp.int32, (tk, gw), 1)

    def visible(t, g):
        return t * tk <= g * gw + gw - 1

    def needs_mask(t, g):
        return t * tk + tk - 1 > g * gw

    last_tile = [max(t for t in range(nt) if visible(t, g)) for g in range(ng)]
    sched = []
    for t in range(nt):
        groups = [g for g in range(ng) if visible(t, g)]
        sched += [[(t, g) for g in groups[k:k + FOX_BATCH]] for k in range(0, len(groups), FOX_BATCH)]

    def qk(batch):
        return [mm_nt(k_ref[t * tk:(t + 1) * tk, :], q_ref[g * gw:(g + 1) * gw, :]) for t, g in batch]

    m = [None] * ng
    l = [None] * ng

    def finish(batch, alpha, pv):
        for (t, g), a, x in zip(batch, alpha, pv):
            acc_ref[g] = x if t == 0 else a * acc_ref[g] + x
            if t == last_tile[g]:
                rows = slice(g * gw, (g + 1) * gw)
                z = z_ref[rows, :]
                o = (acc_ref[g] / l[g]).T
                o_ref[rows, :] = (o * (z * _sigmoid(z))).astype(o_ref.dtype)

    s_next = qk(sched[0])
    pending = None
    for bi, batch in enumerate(sched):
        s_cur = s_next
        if bi + 1 < len(sched):
            s_next = qk(sched[bi + 1])
        alpha, p16 = [], []
        for (t, g), s in zip(batch, s_cur):
            s = s - crep_ref[t * tk:(t + 1) * tk, :]
            if needs_mask(t, g):
                s = jnp.where(key_rel + t * tk > qry_rel + g * gw, -jnp.inf, s)
            m_tile = jnp.max(s, axis=0, keepdims=True)
            if t == 0:
                m_new, a = m_tile, None
            else:
                m_new = jnp.maximum(m[g], m_tile)
                a = jnp.exp2(m[g] - m_new)
            p = jnp.exp2(s - m_new)
            p_sum = jnp.sum(p, axis=0, keepdims=True)
            l[g] = p_sum if t == 0 else a * l[g] + p_sum
            m[g] = m_new
            alpha.append(a)
            p16.append(p.astype(bf16))
        pv = [mm(vt_ref[t], p) for (t, g), p in zip(batch, p16)]
        if pending is not None:
            finish(*pending)
        pending = (batch, alpha, pv)
    finish(*pending)


def _fox_attention(qkv3, c3, wide3, batch, seq):
    tk, gw = FOX_TK, FOX_GW
    h8 = FOX_HEADS
    return pl.pallas_call(
        functools.partial(_fox_kernel, seq=seq, tk=tk, gw=gw),
        grid=(batch, h8),
        in_specs=[
            pl.BlockSpec((None, seq, LANES), lambda b, h: (b, 0, h)),
            pl.BlockSpec((None, seq, LANES), lambda b, h: (b, 0, h8 + h)),
            pl.BlockSpec((None, seq, LANES), lambda b, h: (b, 0, 2 * h8 + h)),
            pl.BlockSpec((None, seq, LANES), lambda b, h: (b, 0, 0)),
            pl.BlockSpec((None, seq, LANES), lambda b, h: (b, 0, COL_ZA + h)),
        ],
        out_specs=pl.BlockSpec((None, seq, LANES), lambda b, h: (b, 0, h)),
        out_shape=jax.ShapeDtypeStruct((batch, seq, FOX_WIDTH), bf16),
        scratch_shapes=[
            pltpu.VMEM((seq, gw), f32),
            pltpu.VMEM((seq // gw, LANES, gw), f32),
            pltpu.VMEM((seq // tk, LANES, tk), bf16),
        ],
        compiler_params=pltpu.CompilerParams(
            dimension_semantics=("parallel", "arbitrary"),
            vmem_limit_bytes=VMEM_LIMIT),
        name="fox_attention",
    )(qkv3, qkv3, qkv3, c3, wide3)


P_MU_R, P_MU_K, P_MU_V, P_MU_Z, P_W0, P_A0, P_KK, P_KA, P_RK, P_LNW, P_LNB = range(11)
P_ROWS = 16
K_XA, K_XR, K_YB, K_YK, K_VS, K_BH, K_KH = range(7)


def _spread(emitters, lo=0.0, hi=1.0):
    n = len(emitters)
    return [(lo + (hi - lo) * (i + 0.5) / n, e) for i, e in enumerate(emitters)]


def _interleave(*segment_lists):
    keyed = [(pos, prio, seg) for prio, segs in enumerate(segment_lists) for pos, seg in segs]
    keyed.sort(key=lambda x: (x[0], x[1]))
    for _, _, seg in keyed:
        seg()


def _rwkv_kernel(r_ref, k_ref, v_ref, z_ref, wd_ref, ad_ref, pc_ref, ps_ref, w2_ref, a2_ref, pc3_ref,
                 o_ref,
                 state_ref, prev_ref, stk_ref, xr32_ref, aux1_ref, dec1_ref, p16_ref, p32_ref,
                 aux2_ref, dec2_ref, *, tb, nt):
    C = RWKV_CHUNK
    G = GROUP_ROWS
    N = RWKV_HEAD_DIM
    NC = tb // C
    s = pl.program_id(0)
    first_of_seq_1 = (s % nt) == 0
    first_of_seq_3 = ((s + 2 * nt - 2) % nt) == 0

    @pl.when(s == 0)
    def _():
        for ref in (state_ref, prev_ref, stk_ref, xr32_ref, aux1_ref, dec1_ref, p16_ref, p32_ref,
                    aux2_ref, dec2_ref):
            ref[...] = jnp.zeros_like(ref)

    mm = lambda a, b: jnp.dot(a, b, preferred_element_type=f32)
    mm_nt = lambda a, b: lax.dot_general(a, b, (((1,), (1,)), ((), ())), preferred_element_type=f32)
    mm_tn = lambda a, b: lax.dot_general(a, b, (((0,), (0,)), ((), ())), preferred_element_type=f32)

    def body(cur, prv):
        li = lax.broadcasted_iota(jnp.int32, (LANES, LANES), 0)
        lj = lax.broadcasted_iota(jnp.int32, (LANES, LANES), 1)
        head_ones = jnp.where(li // N == lj // N, 1.0, 0.0).astype(bf16)
        head_sum = lambda x: _dot_exact_ones(x, head_ones, terms=2)
        lane = lax.broadcasted_iota(jnp.int32, (1, LANES), 1)
        head_masks = [jnp.where(lane // N == h, 1.0, 0.0) for h in range(HEADS_PER_GROUP)]
        stack = lambda x: jnp.concatenate([x * hm for hm in head_masks], axis=0)
        lane_c = lax.broadcasted_iota(jnp.int32, (C, LANES), 1)
        head_sel = [lane_c // N == h for h in range(HEADS_PER_GROUP)]

        def stack16(x):
            xb = x.astype(bf16)
            return jnp.concatenate([jnp.where(m, xb, jnp.zeros_like(xb)) for m in head_sel], axis=0)

        gi = lax.broadcasted_iota(jnp.int32, (G, G), 0)
        gj = lax.broadcasted_iota(jnp.int32, (G, G), 1)
        same_head = gi // C == gj // C
        strict = jnp.logical_and(same_head, gj < gi)
        incl = jnp.logical_and(same_head, gj <= gi)
        eye = jnp.where(gi == gj, 1.0, 0.0)
        chunks = range(NC)

        v1 = {}
        prm = lambda idx: pc_ref[idx:idx + 1, :]
        row_in_block = lax.broadcasted_iota(jnp.int32, (tb, LANES), 0)

        def shifted(ref, slot, mu):
            u = ref[...]
            carry = jnp.where(first_of_seq_1, 0.0, prev_ref[slot, 0:1, :])
            prev = jnp.where(row_in_block == 0, carry, pltpu.roll(u, 1, 0))
            prev_ref[slot, 0:1, :] = u[tb - 1:tb, :]
            return u + (prev - u) * mu

        def s1_lora():
            v1["wd"] = shifted(wd_ref, 4, ps_ref[0:1, :])
            v1["ad"] = shifted(ad_ref, 5, ps_ref[1:2, :])
            v1["w_lin"] = _dot(jnp.tanh(v1["wd"]), w2_ref[...])
            v1["a_lin"] = _dot(v1["ad"], a2_ref[...])

        def s1_key_norm():
            v1["kr"] = shifted(k_ref, 1, prm(P_MU_K))
            kk = v1["kr"] * prm(P_KK)
            v1["kk_raw"] = kk
            v1["kk_ss"] = head_sum(kk * kk)

        def s1_decay():
            w = -_softplus(-(prm(P_W0) + v1["w_lin"])) - 0.5
            v1["log_decay"] = -jnp.exp(w)
            ti = lax.broadcasted_iota(jnp.int32, (2 * C, 2 * C), 0)
            tj = lax.broadcasted_iota(jnp.int32, (2 * C, 2 * C), 1)
            pair_tri = jnp.where(jnp.logical_and(tj <= ti, ti // C == tj // C), 1.0, 0.0).astype(bf16)
            v1["ci"] = jnp.concatenate(
                [_dot_exact_ones(v1["log_decay"][k:k + 2 * C], pair_tri, nt=True)
                 for k in range(0, tb, 2 * C)], axis=0)

        def s1_bonus():
            v1["rate"] = _sigmoid(prm(P_A0) + v1["a_lin"])
            v1["r"] = shifted(r_ref, 0, prm(P_MU_R))
            v1["vr"] = shifted(v_ref, 2, prm(P_MU_V))
            v1["kp"] = v1["kr"] * (1.0 + (v1["rate"] - 1.0) * prm(P_KA))
            aux1_ref[cur, 0] = head_sum(v1["r"] * v1["kp"] * prm(P_RK)) * v1["vr"]
            zb = shifted(z_ref, 3, prm(P_MU_Z))
            aux1_ref[cur, 1] = zb * _sigmoid(zb)

        def s1_scale():
            kk = v1["kk_raw"] / jnp.maximum(jnp.sqrt(v1["kk_ss"]), L2_EPS)
            v1["bb"] = kk * v1["rate"]
            ci = v1["ci"]
            v1["a_t"] = -kk * jnp.exp(ci - v1["log_decay"])
            v1["r_t"] = v1["r"] * jnp.exp(ci)
            inv = jnp.exp(-ci)
            v1["b_t"] = v1["bb"] * inv
            v1["k_t"] = v1["kp"] * inv

        def s1_stack(c):
            def emit():
                sl = slice(c * C, (c + 1) * C)
                ci_c = v1["ci"][sl]
                c_last = ci_c[C - 1:C, :]
                to_end = jnp.exp(c_last - ci_c)
                dec1_ref[cur, c] = jnp.broadcast_to(jnp.exp(c_last), (SUBLANES, LANES))
                xr = stack(v1["r_t"][sl])
                xr32_ref[cur, c] = xr
                stk_ref[cur, K_XR, c] = xr.astype(bf16)
                stk_ref[cur, K_XA, c] = stack16(v1["a_t"][sl])
                stk_ref[cur, K_YB, c] = stack16(v1["b_t"][sl])
                stk_ref[cur, K_YK, c] = stack16(v1["k_t"][sl])
                stk_ref[cur, K_VS, c] = stack16(v1["vr"][sl])
                stk_ref[cur, K_BH, c] = stack16(v1["bb"][sl] * to_end)
                stk_ref[cur, K_KH, c] = stack16(v1["kp"][sl] * to_end)
            return emit

        segs1 = ([(0.0, s1_lora), (0.02, s1_key_norm), (0.05, s1_decay), (0.08, s1_bonus)]
                 + _spread([s1_scale] + [s1_stack(c) for c in chunks], RWKV_TAIL_START, 1.0))

        v2 = {}
        ld = lambda kind, c: stk_ref[prv, kind, c]

        def s2_big():
            big = [mm_nt(jnp.concatenate([ld(K_XA, c), ld(K_XR, c)], axis=0),
                         jnp.concatenate([ld(K_YB, c), ld(K_YK, c)], axis=0)) for c in chunks]
            v2["a_ab"] = [jnp.where(strict, big[c][0:G, 0:G], 0.0) for c in chunks]
            v2["a_ak"] = [jnp.where(strict, big[c][0:G, G:2 * G], 0.0).astype(bf16) for c in chunks]
            v2["a_rb"] = [jnp.where(incl, big[c][G:2 * G, 0:G], 0.0).astype(bf16) for c in chunks]
            v2["a_rk"] = [jnp.where(incl, big[c][G:2 * G, G:2 * G], 0.0).astype(bf16) for c in chunks]
            v2["tinv"] = [eye + a for a in v2["a_ab"]]
            v2["pw"] = [a.astype(bf16) for a in v2["a_ab"]]

        def s2_first_square():
            v2["pw"] = [mm(p, p).astype(bf16) for p in v2["pw"]]

        def s2_level(last):
            def emit():
                for c in chunks:
                    pw = v2["pw"][c]
                    t16 = v2["tinv"][c].astype(bf16)
                    if last:
                        v2["tinv"][c] = v2["tinv"][c] + mm(t16, pw)
                    else:
                        both = mm(jnp.concatenate([t16, pw], axis=0), pw)
                        v2["tinv"][c] = v2["tinv"][c] + both[0:G]
                        v2["pw"][c] = both[G:2 * G].astype(bf16)
            return emit

        def s2_av():
            v2["tinv"] = [t.astype(bf16) for t in v2["tinv"]]
            av = [mm(jnp.concatenate([v2["a_ak"][c], v2["a_rk"][c]], axis=0), ld(K_VS, c)) for c in chunks]
            v2["akv"] = [x[0:G].astype(bf16) for x in av]
            v2["arkv"] = [x[G:2 * G] for x in av]

        def s2_wu():
            v2["wu"] = [mm(v2["tinv"][c], jnp.concatenate([ld(K_XA, c), v2["akv"][c]], axis=1)).astype(bf16)
                        for c in chunks]

        def s2_ry():
            for c in chunks:
                e = mm(v2["a_rb"][c], v2["wu"][c])
                p16_ref[cur, 0, c] = (xr32_ref[prv, c] + e[:, 0:LANES]).astype(bf16)
                p32_ref[cur, 0, c] = e[:, LANES:] + v2["arkv"][c]

        def s2_pm():
            for c in chunks:
                p16_ref[cur, 1, c] = mm_tn(v2["wu"][c][:, 0:LANES], ld(K_BH, c)).astype(bf16)

        def s2_q():
            for c in chunks:
                p32_ref[cur, 1, c] = mm_tn(jnp.concatenate([v2["wu"][c][:, LANES:], ld(K_VS, c)], axis=0),
                                           jnp.concatenate([ld(K_BH, c), ld(K_KH, c)], axis=0))
            aux2_ref[cur] = aux1_ref[prv]
            dec2_ref[cur] = dec1_ref[prv]

        segs2 = _spread([s2_big, s2_first_square] + [s2_level(False)] * 4 + [s2_level(True)]
                        + [s2_av, s2_wu, s2_ry, s2_pm, s2_q])

        v3 = {"ys": []}
        prm3 = lambda idx: pc3_ref[idx:idx + 1, :]

        def s3_start():
            v3["state"] = jnp.where(first_of_seq_3, 0.0, state_ref[...])

        def s3_chunk(c):
            def emit():
                state = v3["state"]
                s16 = state.astype(bf16)
                y_st = mm_nt(p16_ref[prv, 0, c], s16) + p32_ref[prv, 0, c]
                v3["ys"].append(y_st[0:C] + y_st[C:2 * C])
                v3["state"] = (state * dec2_ref[prv, c, 0:1, :] + mm(s16, p16_ref[prv, 1, c])
                               + p32_ref[prv, 1, c])
            return emit

        def s3_finish():
            state_ref[...] = v3["state"]
            y = jnp.concatenate(v3["ys"], axis=0)
            mean = head_sum(y) * (1.0 / N)
            yc = y - mean
            var = head_sum(yc * yc) * (1.0 / N)
            yn = yc * lax.rsqrt(var + GN_EPS) * prm3(P_LNW) + prm3(P_LNB)
            o_ref[...] = ((yn + aux2_ref[prv, 0]) * aux2_ref[prv, 1]).astype(o_ref.dtype)

        segs3 = _spread([s3_start] + [s3_chunk(c) for c in chunks] + [s3_finish], 0.0, RWKV_CHAIN_END)

        _interleave(segs3, segs2, segs1)

    for parity in (0, 1):
        @pl.when(s % 2 == parity)
        def _(parity=parity):
            body(cur=parity, prv=1 - parity)


def _rwkv_mix(wide, small, pc, ps, w2p, a2p, batch, seq):
    tb = RWKV_BLOCK
    nt = seq // tb
    groups = RWKV_WIDTH // LANES
    nc = tb // RWKV_CHUNK
    n_blocks = batch * groups * nt

    def where(s):
        s = jnp.clip(s, 0, n_blocks - 1)
        return (s // (nt * groups)) * nt + s % nt, (s // nt) % groups

    col = lambda c0: (lambda s: (where(s)[0], c0 + where(s)[1]))
    fixed_col = lambda c: (lambda s: (where(s)[0], c))
    grp = lambda s: (0, where(s)[1])
    return pl.pallas_call(
        functools.partial(_rwkv_kernel, tb=tb, nt=nt),
        grid=(n_blocks + 2,),
        in_specs=[
            pl.BlockSpec((tb, LANES), col(COL_R)),
            pl.BlockSpec((tb, LANES), col(COL_K)),
            pl.BlockSpec((tb, LANES), col(COL_V)),
            pl.BlockSpec((tb, LANES), col(COL_ZB)),
            pl.BlockSpec((tb, LANES), fixed_col(0)),
            pl.BlockSpec((tb, LANES), fixed_col(1)),
            pl.BlockSpec((P_ROWS, LANES), grp),
            pl.BlockSpec((SUBLANES, LANES), lambda s: (0, 0)),
            pl.BlockSpec((LANES, LANES), grp),
            pl.BlockSpec((LANES, LANES), grp),
            pl.BlockSpec((P_ROWS, LANES), lambda s: grp(s - 2)),
        ],
        out_specs=pl.BlockSpec((tb, LANES), lambda s: (where(s - 2)[0], where(s - 2)[1])),
        out_shape=jax.ShapeDtypeStruct((batch * seq, RWKV_WIDTH), bf16),
        scratch_shapes=[
            pltpu.VMEM((LANES, LANES), f32),
            pltpu.VMEM((6, SUBLANES, LANES), f32),
            pltpu.VMEM((2, 7, nc, GROUP_ROWS, LANES), bf16),
            pltpu.VMEM((2, nc, GROUP_ROWS, LANES), f32),
            pltpu.VMEM((2, 2, tb, LANES), f32),
            pltpu.VMEM((2, nc, SUBLANES, LANES), f32),
            pltpu.VMEM((2, 2, nc, GROUP_ROWS, LANES), bf16),
            pltpu.VMEM((2, 2, nc, GROUP_ROWS, LANES), f32),
            pltpu.VMEM((2, 2, tb, LANES), f32),
            pltpu.VMEM((2, nc, SUBLANES, LANES), f32),
        ],
        compiler_params=pltpu.CompilerParams(
            dimension_semantics=("arbitrary",),
            vmem_limit_bytes=VMEM_LIMIT),
        name="rwkv_mix",
    )(wide, wide, wide, wide, small, small, pc, ps, w2p, a2p, pc)


def _out_kernel(oa_ref, ob_ref, ga_ref, gb_ref, x_ref, wpf_ref, wpr_ref, wo_ref, g_ref, o_ref):
    pa = jnp.dot(oa_ref[...], wpf_ref[...], preferred_element_type=f32)
    pb = jnp.dot(ob_ref[...], wpr_ref[...], preferred_element_type=f32)
    m = _sigmoid(ga_ref[...]) * pa + _sigmoid(gb_ref[...]) * pb
    z = x_ref[...] + jnp.dot(m.astype(bf16), wo_ref[...], preferred_element_type=f32)
    ms = jnp.mean(z * z, axis=-1, keepdims=True)
    o_ref[...] = z * lax.rsqrt(ms + RMS_EPS) * g_ref[...]


def _merge_out(oa, ob, wide, x2d, wpf, wpr, wo, gain):
    m, d = x2d.shape
    tm = 256
    resident = lambda shape: pl.BlockSpec(shape, lambda i: (0, 0), pipeline_mode=pl.Buffered(1))
    return pl.pallas_call(
        _out_kernel,
        grid=(m // tm,),
        in_specs=[
            pl.BlockSpec((tm, FOX_WIDTH), lambda i: (i, 0)),
            pl.BlockSpec((tm, RWKV_WIDTH), lambda i: (i, 0)),
            pl.BlockSpec((tm, d), lambda i: (i, COL_GA * LANES // d)),
            pl.BlockSpec((tm, d), lambda i: (i, COL_GB * LANES // d)),
            pl.BlockSpec((tm, d), lambda i: (i, 0)),
            resident((FOX_WIDTH, d)),
            resident((RWKV_WIDTH, d)),
            resident((d, d)),
            resident((1, d)),
        ],
        out_specs=pl.BlockSpec((tm, d), lambda i: (i, 0)),
        out_shape=jax.ShapeDtypeStruct((m, d), f32),
        compiler_params=pltpu.CompilerParams(
            dimension_semantics=("parallel",),
            vmem_limit_bytes=VMEM_LIMIT),
        name="merge_out",
    )(oa, ob, wide, wide, x2d, wpf, wpr, wo, gain)


def _branches(x2d, batch, seq, norm_gain, w_in, fox_forget_bias, rwkv_shift_mix, rwkv_w0, rwkv_w2,
              rwkv_a0, rwkv_a2, rwkv_k_k, rwkv_k_a, rwkv_r_k, rwkv_ln_w, rwkv_ln_b):
    fw, rw = FOX_WIDTH, RWKV_WIDTH
    w_t = w_in.T
    r_qkv = 0
    r_za = r_qkv + 3 * fw
    r_f = r_za + fw
    r_rkvz = r_f + FOX_HEADS
    r_wd = r_rkvz + 4 * rw
    r_ad = r_wd + LORA
    r_g = r_ad + LORA
    pieces = [(r_qkv, 3 * fw), (r_g, 2 * D_MODEL), (r_za, fw), (r_rkvz, 4 * rw)]
    pad_rows = lambda a: jnp.pad(a, ((0, LANES - a.shape[0]), (0, 0)))
    w_tail = jnp.concatenate(
        [pad_rows(w_t[r_wd:r_wd + LORA]),
         pad_rows(jnp.concatenate([w_t[r_ad:r_ad + LORA], w_t[r_f:r_f + FOX_HEADS]], axis=0))], axis=0)
    h2d = _rmsnorm_bf16(x2d, norm_gain.reshape(1, D_MODEL))
    qkv, wide, small = _in_proj(h2d, w_t, pieces, fw, FOX_HEAD_DIM ** -0.5 * LOG2E, w_tail)

    bias_row = jnp.pad(fox_forget_bias.reshape(1, FOX_HEADS), ((0, 0), (F_LANE, LANES - F_LANE - FOX_HEADS)))
    c = _gate_cumsum(small, bias_row, batch, seq)
    qkv3 = qkv.reshape(batch, seq, 3 * fw)
    oa = _fox_attention(qkv3, c.reshape(batch, seq, LANES), wide.reshape(batch, seq, WIDE_COLS),
                        batch, seq).reshape(batch * seq, fw)

    mu = rwkv_shift_mix
    pc = jnp.zeros((P_ROWS, rw), f32)
    rows = [mu[0:rw], mu[rw:2 * rw], mu[2 * rw:3 * rw], mu[3 * rw:4 * rw], rwkv_w0, rwkv_a0,
            rwkv_k_k, rwkv_k_a, rwkv_r_k.reshape(rw), rwkv_ln_w, rwkv_ln_b]
    pc = pc.at[:len(rows)].set(jnp.stack(rows))
    ps = jnp.zeros((SUBLANES, LANES), f32)
    ps = ps.at[0, :LORA].set(mu[4 * rw:4 * rw + LORA]).at[1, :LORA].set(mu[4 * rw + LORA:])
    w2p = jnp.pad(rwkv_w2, ((0, LANES - LORA), (0, 0))).astype(bf16)
    a2p = jnp.pad(rwkv_a2, ((0, LANES - LORA), (0, 0))).astype(bf16)
    ob = _rwkv_mix(wide, small, pc, ps, w2p, a2p, batch, seq)

    return oa, ob, wide


def kernel(x, norm_gain, w_in, fox_forget_bias, rwkv_shift_mix, rwkv_w0, rwkv_w2, rwkv_a0, rwkv_a2, rwkv_k_k, rwkv_k_a, rwkv_r_k, rwkv_ln_w, rwkv_ln_b, w_proj_fox, w_proj_rwkv, w_out, final_norm_gain):
    batch, seq, d = x.shape
    depth = norm_gain.shape[0]
    assert depth == 1, "the final rmsnorm is fused into the single layer's output kernel"
    x2d = x.reshape(batch * seq, d)
    oa, ob, wide = _branches(x2d, batch, seq, norm_gain[0], w_in[0], fox_forget_bias[0],
                             rwkv_shift_mix[0], rwkv_w0[0], rwkv_w2[0], rwkv_a0[0], rwkv_a2[0],
                             rwkv_k_k[0], rwkv_k_a[0], rwkv_r_k[0], rwkv_ln_w[0], rwkv_ln_b[0])
    out = _merge_out(oa, ob, wide, x2d, w_proj_fox[0].astype(bf16), w_proj_rwkv[0].astype(bf16),
                     w_out[0].astype(bf16), final_norm_gain.reshape(1, d))
    return out.reshape(batch, seq, d)
```

```python
import functools

import jax
import jax.numpy as jnp
from jax import lax
from jax.experimental import pallas as pl
from jax.experimental.pallas import tpu as pltpu

D_MODEL = 2048
FOX_HEADS = 8
FOX_HEAD_DIM = 128
FOX_WIDTH = FOX_HEADS * FOX_HEAD_DIM
RWKV_HEADS = 16
RWKV_HEAD_DIM = 64
RWKV_WIDTH = RWKV_HEADS * RWKV_HEAD_DIM
LORA = 96
RMS_EPS = 1e-6
GN_EPS = 64e-5
L2_EPS = 1e-12

LANES = 128
SUBLANES = 8
VMEM_LIMIT = 56 * 1024 * 1024

COL_GA = 0
COL_GB = D_MODEL // LANES
COL_ZA = 2 * D_MODEL // LANES
COL_R = COL_ZA + FOX_WIDTH // LANES
COL_K = COL_R + RWKV_WIDTH // LANES
COL_V = COL_K + RWKV_WIDTH // LANES
COL_ZB = COL_V + RWKV_WIDTH // LANES
WIDE_COLS = (COL_ZB + RWKV_WIDTH // LANES) * LANES
SMALL_COLS = 2 * LANES
F_LANE = LORA

RWKV_CHUNK = 64
RWKV_BLOCK = 512
RWKV_TAIL_START = 0.3
RWKV_CHAIN_END = 0.8
HEADS_PER_GROUP = LANES // RWKV_HEAD_DIM
GROUP_ROWS = HEADS_PER_GROUP * RWKV_CHUNK

f32 = jnp.float32
bf16 = jnp.bfloat16


def _dot_exact_ones(a, ones_bf16, nt=False, terms=3):
    if nt:
        mm = lambda x: lax.dot_general(ones_bf16, x, (((1,), (0,)), ((), ())),
                                       preferred_element_type=f32)
    else:
        mm = lambda x: jnp.dot(x, ones_bf16, preferred_element_type=f32)
    part = a.astype(bf16)
    out = mm(part)
    rest = a
    for _ in range(terms - 1):
        rest = rest - part.astype(f32)
        part = rest.astype(bf16)
        out = out + mm(part)
    return out


def _softplus(x):
    return jnp.maximum(x, 0.0) + jnp.log(1.0 + jnp.exp(-jnp.abs(x)))


def _sigmoid(x):
    return 1.0 / (1.0 + jnp.exp(-x))


PROJ_TM = 1024
PROJ_TN = 1024


def _rmsnorm_kernel(x_ref, g_ref, h_ref):
    xf = x_ref[...]
    ms = jnp.mean(xf * xf, axis=-1, keepdims=True)
    h_ref[...] = (xf * lax.rsqrt(ms + RMS_EPS) * g_ref[...]).astype(h_ref.dtype)


def _rmsnorm_bf16(x2d, gain):
    m, d = x2d.shape
    tm = 512
    return pl.pallas_call(
        _rmsnorm_kernel,
        grid=(m // tm,),
        in_specs=[pl.BlockSpec((tm, d), lambda i: (i, 0)), pl.BlockSpec((1, d), lambda i: (0, 0))],
        out_specs=pl.BlockSpec((tm, d), lambda i: (i, 0)),
        out_shape=jax.ShapeDtypeStruct((m, d), bf16),
        compiler_params=pltpu.CompilerParams(dimension_semantics=("parallel",), vmem_limit_bytes=VMEM_LIMIT),
        name="rmsnorm",
    )(x2d, gain)


def _in_proj_kernel(h_ref, w_ref, wt_ref, qkv_ref, wide_ref, small_ref, w16_ref, *, n_qkv, n_wide, scaled_tiles, scale):
    j = pl.program_id(0)
    i = pl.program_id(1)
    n_main = n_qkv + n_wide

    @pl.when(i == 0)
    def _():
        @pl.when(j < n_main)
        def _():
            factor = jnp.where(j < scaled_tiles, scale, 1.0)
            w16_ref[...] = (w_ref[...] * factor).astype(bf16)

        @pl.when(j == n_main)
        def _():
            w16_ref[0:SMALL_COLS, :] = wt_ref[...].astype(bf16)

    project = lambda w: lax.dot_general(h_ref[...], w, (((1,), (1,)), ((), ())), preferred_element_type=f32)

    @pl.when(j < n_qkv)
    def _():
        qkv_ref[...] = project(w16_ref[...]).astype(qkv_ref.dtype)

    @pl.when(jnp.logical_and(j >= n_qkv, j < n_main))
    def _():
        wide_ref[...] = project(w16_ref[...])

    @pl.when(j == n_main)
    def _():
        small_ref[...] = project(w16_ref[0:SMALL_COLS, :])


def _in_proj(h2d, w_t, pieces, scaled_rows, scale, w_tail):
    m, d = h2d.shape
    tm, tn = PROJ_TM, PROJ_TN
    n_qkv = 3 * FOX_WIDTH // tn
    n_wide = WIDE_COLS // tn
    n_main = n_qkv + n_wide
    n_i = m // tm
    assert n_qkv * tn == 3 * FOX_WIDTH and n_wide * tn == WIDE_COLS and w_tail.shape[0] == SMALL_COLS
    assert scaled_rows % tn == 0 and all(n % tn == 0 and src % SUBLANES == 0 for src, n in pieces)
    assert sum(n for _, n in pieces) == n_main * tn
    starts, first_tile = [], 0
    for src, n in pieces:
        starts.append((first_tile, src))
        first_tile += n // tn

    def src_row(j):
        tile = jnp.int32(0)
        for blk, src in starts:
            tile = jnp.where(j >= blk, src // SUBLANES + (jnp.minimum(j, n_main - 1) - blk) * (tn // SUBLANES), tile)
        return pl.multiple_of(tile * SUBLANES, SUBLANES)

    qkv_row = lambda j, i: jnp.where(j < n_qkv, i, n_i - 1)
    wide_row = lambda j, i: jnp.where(j < n_qkv, 0, jnp.where(j < n_main, i, n_i - 1))
    small_row = lambda j, i: jnp.where(j == n_main, i, 0)
    return pl.pallas_call(
        functools.partial(_in_proj_kernel, n_qkv=n_qkv, n_wide=n_wide, scaled_tiles=scaled_rows // tn,
                          scale=scale),
        grid=(n_main + 1, n_i),
        in_specs=[
            pl.BlockSpec((tm, d), lambda j, i: (i, 0)),
            pl.BlockSpec((pl.Element(tn), pl.Element(d)), lambda j, i: (src_row(j), 0)),
            pl.BlockSpec((SMALL_COLS, d), lambda j, i: (0, 0)),
        ],
        out_specs=[
            pl.BlockSpec((tm, tn), lambda j, i: (qkv_row(j, i), jnp.minimum(j, n_qkv - 1))),
            pl.BlockSpec((tm, tn), lambda j, i: (wide_row(j, i), jnp.clip(j - n_qkv, 0, n_wide - 1))),
            pl.BlockSpec((tm, SMALL_COLS), lambda j, i: (small_row(j, i), 0)),
        ],
        out_shape=[
            jax.ShapeDtypeStruct((m, 3 * FOX_WIDTH), bf16),
            jax.ShapeDtypeStruct((m, WIDE_COLS), f32),
            jax.ShapeDtypeStruct((m, SMALL_COLS), f32),
        ],
        scratch_shapes=[pltpu.VMEM((tn, d), bf16)],
        compiler_params=pltpu.CompilerParams(
            dimension_semantics=("arbitrary", "arbitrary"),
            vmem_limit_bytes=VMEM_LIMIT),
        name="in_proj",
    )(h2d, w_t, w_tail)


def _gate_kernel(s_ref, bias_ref, mu_ref, c_ref, lora_ref, carry_ref, prev_ref, *, tb):
    @pl.when(pl.program_id(1) == 0)
    def _():
        carry_ref[...] = jnp.zeros_like(carry_ref)
        prev_ref[...] = jnp.zeros_like(prev_ref)

    u = s_ref[...]
    row = lax.broadcasted_iota(jnp.int32, u.shape, 0)
    prev = jnp.where(row == 0, prev_ref[0:1, :], pltpu.roll(u, 1, 0))
    prev_ref[0:1, :] = u[tb - 1:tb, :]
    shifted = u + (prev - u) * mu_ref[...]
    lora_ref[:, 0:LANES] = jnp.tanh(shifted[:, 0:LANES]).astype(lora_ref.dtype)
    lora_ref[:, LANES:2 * LANES] = shifted[:, LANES:2 * LANES].astype(lora_ref.dtype)

    z = u[:, LANES:2 * LANES] + bias_ref[...]
    log_f = -_softplus(-z)
    ti = lax.broadcasted_iota(jnp.int32, (tb, tb), 0)
    tj = lax.broadcasted_iota(jnp.int32, (tb, tb), 1)
    tri = jnp.where(tj <= ti, 1.0, 0.0).astype(bf16)
    c = _dot_exact_ones(log_f, tri, nt=True) + carry_ref[0:1, :]
    c_ref[...] = c
    carry_ref[0:1, :] = c[tb - 1:tb, :]


def _gate_cumsum(small, bias_row, mu_row, batch, seq):
    tb = 512
    nt = seq // tb
    return pl.pallas_call(
        functools.partial(_gate_kernel, tb=tb),
        grid=(batch, nt),
        in_specs=[
            pl.BlockSpec((tb, SMALL_COLS), lambda b, t: (b * nt + t, 0)),
            pl.BlockSpec((1, LANES), lambda b, t: (0, 0)),
            pl.BlockSpec((1, SMALL_COLS), lambda b, t: (0, 0)),
        ],
        out_specs=[
            pl.BlockSpec((tb, LANES), lambda b, t: (b * nt + t, 0)),
            pl.BlockSpec((tb, SMALL_COLS), lambda b, t: (b * nt + t, 0)),
        ],
        out_shape=[
            jax.ShapeDtypeStruct((batch * seq, LANES), f32),
            jax.ShapeDtypeStruct((batch * seq, SMALL_COLS), bf16),
        ],
        scratch_shapes=[pltpu.VMEM((SUBLANES, LANES), f32), pltpu.VMEM((SUBLANES, SMALL_COLS), f32)],
        compiler_params=pltpu.CompilerParams(
            dimension_semantics=("parallel", "arbitrary")),
        name="fox_gate_cumsum",
    )(small, bias_row, mu_row)


LOG2E = 1.4426950408889634
FOX_TK = 256
FOX_GW = 128
FOX_BATCH = 4


def _fox_kernel(q_ref, k_ref, v_ref, c_ref, z_ref, o_ref, crep_ref, acc_ref, vt_ref, *, seq, tk, gw):
    h = pl.program_id(1)
    ng = seq // gw
    nt = seq // tk
    assert seq % gw == 0 and seq % tk == 0 and (tk % gw == 0 or gw % tk == 0)
    mm = lambda a, b: jnp.dot(a, b, preferred_element_type=f32)
    mm_nt = lambda a, b: lax.dot_general(a, b, (((1,), (1,)), ((), ())), preferred_element_type=f32)

    src_lane = lax.broadcasted_iota(jnp.int32, (LANES, LANES), 0)
    pick = jnp.where(src_lane == F_LANE + h, 1.0, 0.0).astype(bf16)
    crep = _dot_exact_ones(c_ref[...], pick) * LOG2E
    crep_ref[...] = jnp.concatenate([crep] * (gw // LANES), axis=1)
    for t in range(nt):
        vt_ref[t] = v_ref[t * tk:(t + 1) * tk, :].T

    key_rel = lax.broadcasted_iota(jnp.int32, (tk, gw), 0)
    qry_rel = lax.broadcasted_iota(jnp.int32, (tk, gw), 1)

    def visible(t, g):
        return t * tk <= g * gw + gw - 1

    def needs_mask(t, g):
        return t * tk + tk - 1 > g * gw

    last_tile = [max(t for t in range(nt) if visible(t, g)) for g in range(ng)]
    sched = []
    for t in range(nt):
        groups = [g for g in range(ng) if visible(t, g)]
        sched += [[(t, g) for g in groups[k:k + FOX_BATCH]] for k in range(0, len(groups), FOX_BATCH)]

    def qk(batch):
        return [mm_nt(k_ref[t * tk:(t + 1) * tk, :], q_ref[g * gw:(g + 1) * gw, :]) for t, g in batch]

    m = [None] * ng
    l = [None] * ng

    def finish(batch, alpha, pv):
        for (t, g), a, x in zip(batch, alpha, pv):
            acc_ref[g] = x if t == 0 else a * acc_ref[g] + x
            if t == last_tile[g]:
                rows = slice(g * gw, (g + 1) * gw)
                z = z_ref[rows, :]
                o = (acc_ref[g] / l[g]).T
                o_ref[rows, :] = (o * (z * _sigmoid(z))).astype(o_ref.dtype)

    s_next = qk(sched[0])
    pending = None
    for bi, batch in enumerate(sched):
        s_cur = s_next
        if bi + 1 < len(sched):
            s_next = qk(sched[bi + 1])
        alpha, p16 = [], []
        for (t, g), s in zip(batch, s_cur):
            s = s - crep_ref[t * tk:(t + 1) * tk, :]
            if needs_mask(t, g):
                s = jnp.where(key_rel + t * tk > qry_rel + g * gw, -jnp.inf, s)
            m_tile = jnp.max(s, axis=0, keepdims=True)
            if t == 0:
                m_new, a = m_tile, None
            else:
                m_new = jnp.maximum(m[g], m_tile)
                a = jnp.exp2(m[g] - m_new)
            p = jnp.exp2(s - m_new)
            p_sum = jnp.sum(p, axis=0, keepdims=True)
            l[g] = p_sum if t == 0 else a * l[g] + p_sum
            m[g] = m_new
            alpha.append(a)
            p16.append(p.astype(bf16))
        pv = [mm(vt_ref[t], p) for (t, g), p in zip(batch, p16)]
        if pending is not None:
            finish(*pending)
        pending = (batch, alpha, pv)
    finish(*pending)


def _fox_attention(qkv3, c3, wide3, batch, seq):
    tk, gw = FOX_TK, FOX_GW
    h8 = FOX_HEADS
    return pl.pallas_call(
        functools.partial(_fox_kernel, seq=seq, tk=tk, gw=gw),
        grid=(batch, h8),
        in_specs=[
            pl.BlockSpec((None, seq, LANES), lambda b, h: (b, 0, h)),
            pl.BlockSpec((None, seq, LANES), lambda b, h: (b, 0, h8 + h)),
            pl.BlockSpec((None, seq, LANES), lambda b, h: (b, 0, 2 * h8 + h)),
            pl.BlockSpec((None, seq, LANES), lambda b, h: (b, 0, 0)),
            pl.BlockSpec((None, seq, LANES), lambda b, h: (b, 0, COL_ZA + h)),
        ],
        out_specs=pl.BlockSpec((None, seq, LANES), lambda b, h: (b, 0, h)),
        out_shape=jax.ShapeDtypeStruct((batch, seq, FOX_WIDTH), bf16),
        scratch_shapes=[
            pltpu.VMEM((seq, gw), f32),
            pltpu.VMEM((seq // gw, LANES, gw), f32),
            pltpu.VMEM((seq // tk, LANES, tk), bf16),
        ],
        compiler_params=pltpu.CompilerParams(
            dimension_semantics=("parallel", "arbitrary"),
            vmem_limit_bytes=VMEM_LIMIT),
        name="fox_attention",
    )(qkv3, qkv3, qkv3, c3, wide3)


P_MU_R, P_MU_K, P_MU_V, P_MU_Z, P_W0, P_A0, P_KK, P_KA, P_RK, P_LNW, P_LNB = range(11)
P_ROWS = 16
K_XA, K_XR, K_YB, K_YK, K_VS, K_BH, K_KH = range(7)


def _spread(emitters, lo=0.0, hi=1.0):
    n = len(emitters)
    return [(lo + (hi - lo) * (i + 0.5) / n, e) for i, e in enumerate(emitters)]


def _interleave(*segment_lists):
    keyed = [(pos, prio, seg) for prio, segs in enumerate(segment_lists) for pos, seg in segs]
    keyed.sort(key=lambda x: (x[0], x[1]))
    for _, _, seg in keyed:
        seg()


def _rwkv_kernel(r_ref, k_ref, v_ref, z_ref, wd_ref, ad_ref, pc_ref, w2_ref, a2_ref, pc3_ref,
                 o_ref,
                 state_ref, prev_ref, stk_ref, xr32_ref, aux1_ref, dec1_ref, p16_ref, p32_ref,
                 aux2_ref, dec2_ref, *, tb, nt):
    C = RWKV_CHUNK
    G = GROUP_ROWS
    N = RWKV_HEAD_DIM
    NC = tb // C
    s = pl.program_id(0)
    first_of_seq_1 = (s % nt) == 0
    first_of_seq_3 = ((s + 2 * nt - 2) % nt) == 0

    @pl.when(s == 0)
    def _():
        for ref in (state_ref, prev_ref, stk_ref, xr32_ref, aux1_ref, dec1_ref, p16_ref, p32_ref,
                    aux2_ref, dec2_ref):
            ref[...] = jnp.zeros_like(ref)

    mm = lambda a, b: jnp.dot(a, b, preferred_element_type=f32)
    mm_nt = lambda a, b: lax.dot_general(a, b, (((1,), (1,)), ((), ())), preferred_element_type=f32)
    mm_tn = lambda a, b: lax.dot_general(a, b, (((0,), (0,)), ((), ())), preferred_element_type=f32)

    def body(cur, prv):
        li = lax.broadcasted_iota(jnp.int32, (LANES, LANES), 0)
        lj = lax.broadcasted_iota(jnp.int32, (LANES, LANES), 1)
        head_ones = jnp.where(li // N == lj // N, 1.0, 0.0).astype(bf16)
        head_sum = lambda x: _dot_exact_ones(x, head_ones, terms=2)
        lane = lax.broadcasted_iota(jnp.int32, (1, LANES), 1)
        head_masks = [jnp.where(lane // N == h, 1.0, 0.0) for h in range(HEADS_PER_GROUP)]
        stack = lambda x: jnp.concatenate([x * hm for hm in head_masks], axis=0)
        lane_c = lax.broadcasted_iota(jnp.int32, (C, LANES), 1)
        head_sel = [lane_c // N == h for h in range(HEADS_PER_GROUP)]

        def stack16(x):
            xb = x.astype(bf16)
            return jnp.concatenate([jnp.where(m, xb, jnp.zeros_like(xb)) for m in head_sel], axis=0)

        gi = lax.broadcasted_iota(jnp.int32, (G, G), 0)
        gj = lax.broadcasted_iota(jnp.int32, (G, G), 1)
        same_head = gi // C == gj // C
        strict = jnp.logical_and(same_head, gj < gi)
        incl = jnp.logical_and(same_head, gj <= gi)
        eye = jnp.where(gi == gj, 1.0, 0.0)
        chunks = range(NC)

        v1 = {}
        prm = lambda idx: pc_ref[idx:idx + 1, :]
        row_in_block = lax.broadcasted_iota(jnp.int32, (tb, LANES), 0)

        def shifted(ref, slot, mu):
            u = ref[...]
            carry = jnp.where(first_of_seq_1, 0.0, prev_ref[slot, 0:1, :])
            prev = jnp.where(row_in_block == 0, carry, pltpu.roll(u, 1, 0))
            prev_ref[slot, 0:1, :] = u[tb - 1:tb, :]
            return u + (prev - u) * mu

        def s1_lora():
            v1["w_lin"] = mm(wd_ref[...], w2_ref[...])
            v1["a_lin"] = mm(ad_ref[...], a2_ref[...])

        def s1_key_norm():
            v1["kr"] = shifted(k_ref, 1, prm(P_MU_K))
            kk = v1["kr"] * prm(P_KK)
            v1["kk_raw"] = kk
            v1["kk_ss"] = head_sum(kk * kk)

        def s1_decay():
            w = -_softplus(-(prm(P_W0) + v1["w_lin"])) - 0.5
            v1["log_decay"] = -jnp.exp(w)
            ti = lax.broadcasted_iota(jnp.int32, (2 * C, 2 * C), 0)
            tj = lax.broadcasted_iota(jnp.int32, (2 * C, 2 * C), 1)
            pair_tri = jnp.where(jnp.logical_and(tj <= ti, ti // C == tj // C), 1.0, 0.0).astype(bf16)
            v1["ci"] = jnp.concatenate(
                [_dot_exact_ones(v1["log_decay"][k:k + 2 * C], pair_tri, nt=True)
                 for k in range(0, tb, 2 * C)], axis=0)

        def s1_bonus():
            v1["rate"] = _sigmoid(prm(P_A0) + v1["a_lin"])
            v1["r"] = shifted(r_ref, 0, prm(P_MU_R))
            v1["vr"] = shifted(v_ref, 2, prm(P_MU_V))
            v1["kp"] = v1["kr"] * (1.0 + (v1["rate"] - 1.0) * prm(P_KA))
            aux1_ref[cur, 0] = head_sum(v1["r"] * v1["kp"] * prm(P_RK)) * v1["vr"]
            zb = shifted(z_ref, 3, prm(P_MU_Z))
            aux1_ref[cur, 1] = zb * _sigmoid(zb)

        def s1_scale():
            kk = v1["kk_raw"] / jnp.maximum(jnp.sqrt(v1["kk_ss"]), L2_EPS)
            v1["bb"] = kk * v1["rate"]
            ci = v1["ci"]
            v1["a_t"] = -kk * jnp.exp(ci - v1["log_decay"])
            v1["r_t"] = v1["r"] * jnp.exp(ci)
            inv = jnp.exp(-ci)
            v1["b_t"] = v1["bb"] * inv
            v1["k_t"] = v1["kp"] * inv

        def s1_stack(c):
            def emit():
                sl = slice(c * C, (c + 1) * C)
                ci_c = v1["ci"][sl]
                c_last = ci_c[C - 1:C, :]
                to_end = jnp.exp(c_last - ci_c)
                dec1_ref[cur, c] = jnp.broadcast_to(jnp.exp(c_last), (SUBLANES, LANES))
                xr = stack(v1["r_t"][sl])
                xr32_ref[cur, c] = xr
                stk_ref[cur, K_XR, c] = xr.astype(bf16)
                stk_ref[cur, K_XA, c] = stack16(v1["a_t"][sl])
                stk_ref[cur, K_YB, c] = stack16(v1["b_t"][sl])
                stk_ref[cur, K_YK, c] = stack16(v1["k_t"][sl])
                stk_ref[cur, K_VS, c] = stack16(v1["vr"][sl])
                stk_ref[cur, K_BH, c] = stack16(v1["bb"][sl] * to_end)
                stk_ref[cur, K_KH, c] = stack16(v1["kp"][sl] * to_end)
            return emit

        segs1 = ([(0.0, s1_lora), (0.02, s1_key_norm), (0.05, s1_decay), (0.08, s1_bonus)]
                 + _spread([s1_scale] + [s1_stack(c) for c in chunks], RWKV_TAIL_START, 1.0))

        v2 = {}
        ld = lambda kind, c: stk_ref[prv, kind, c]

        def s2_big():
            big = [mm_nt(jnp.concatenate([ld(K_XA, c), ld(K_XR, c)], axis=0),
                         jnp.concatenate([ld(K_YB, c), ld(K_YK, c)], axis=0)) for c in chunks]
            v2["a_ab"] = [jnp.where(strict, big[c][0:G, 0:G], 0.0) for c in chunks]
            v2["a_ak"] = [jnp.where(strict, big[c][0:G, G:2 * G], 0.0).astype(bf16) for c in chunks]
            v2["a_rb"] = [jnp.where(incl, big[c][G:2 * G, 0:G], 0.0).astype(bf16) for c in chunks]
            v2["a_rk"] = [jnp.where(incl, big[c][G:2 * G, G:2 * G], 0.0).astype(bf16) for c in chunks]
            v2["tinv"] = [eye + a for a in v2["a_ab"]]
            v2["pw"] = [a.astype(bf16) for a in v2["a_ab"]]

        def s2_first_square():
            v2["pw"] = [mm(p, p).astype(bf16) for p in v2["pw"]]

        def s2_level(last):
            def emit():
                for c in chunks:
                    pw = v2["pw"][c]
                    t16 = v2["tinv"][c].astype(bf16)
                    if last:
                        v2["tinv"][c] = v2["tinv"][c] + mm(t16, pw)
                    else:
                        both = mm(jnp.concatenate([t16, pw], axis=0), pw)
                        v2["tinv"][c] = v2["tinv"][c] + both[0:G]
                        v2["pw"][c] = both[G:2 * G].astype(bf16)
            return emit

        def s2_av():
            v2["tinv"] = [t.astype(bf16) for t in v2["tinv"]]
            av = [mm(jnp.concatenate([v2["a_ak"][c], v2["a_rk"][c]], axis=0), ld(K_VS, c)) for c in chunks]
            v2["akv"] = [x[0:G].astype(bf16) for x in av]
            v2["arkv"] = [x[G:2 * G] for x in av]

        def s2_wu():
            v2["wu"] = [mm(v2["tinv"][c], jnp.concatenate([ld(K_XA, c), v2["akv"][c]], axis=1)).astype(bf16)
                        for c in chunks]

        def s2_ry():
            for c in chunks:
                e = mm(v2["a_rb"][c], v2["wu"][c])
                p16_ref[cur, 0, c] = (xr32_ref[prv, c] + e[:, 0:LANES]).astype(bf16)
                p32_ref[cur, 0, c] = e[:, LANES:] + v2["arkv"][c]

        def s2_pm():
            for c in chunks:
                p16_ref[cur, 1, c] = mm_tn(v2["wu"][c][:, 0:LANES], ld(K_BH, c)).astype(bf16)

        def s2_q():
            for c in chunks:
                p32_ref[cur, 1, c] = mm_tn(jnp.concatenate([v2["wu"][c][:, LANES:], ld(K_VS, c)], axis=0),
                                           jnp.concatenate([ld(K_BH, c), ld(K_KH, c)], axis=0))
            aux2_ref[cur] = aux1_ref[prv]
            dec2_ref[cur] = dec1_ref[prv]

        segs2 = _spread([s2_big, s2_first_square] + [s2_level(False)] * 4 + [s2_level(True)]
                        + [s2_av, s2_wu, s2_ry, s2_pm, s2_q])

        v3 = {"ys": []}
        prm3 = lambda idx: pc3_ref[idx:idx + 1, :]

        def s3_start():
            v3["state"] = jnp.where(first_of_seq_3, 0.0, state_ref[...])

        def s3_chunk(c):
            def emit():
                state = v3["state"]
                s16 = state.astype(bf16)
                y_st = mm_nt(p16_ref[prv, 0, c], s16) + p32_ref[prv, 0, c]
                v3["ys"].append(y_st[0:C] + y_st[C:2 * C])
                v3["state"] = (state * dec2_ref[prv, c, 0:1, :] + mm(s16, p16_ref[prv, 1, c])
                               + p32_ref[prv, 1, c])
            return emit

        def s3_finish():
            state_ref[...] = v3["state"]
            y = jnp.concatenate(v3["ys"], axis=0)
            mean = head_sum(y) * (1.0 / N)
            yc = y - mean
            var = head_sum(yc * yc) * (1.0 / N)
            yn = yc * lax.rsqrt(var + GN_EPS) * prm3(P_LNW) + prm3(P_LNB)
            o_ref[...] = ((yn + aux2_ref[prv, 0]) * aux2_ref[prv, 1]).astype(o_ref.dtype)

        segs3 = _spread([s3_start] + [s3_chunk(c) for c in chunks] + [s3_finish], 0.0, RWKV_CHAIN_END)

        _interleave(segs3, segs2, segs1)

    for parity in (0, 1):
        @pl.when(s % 2 == parity)
        def _(parity=parity):
            body(cur=parity, prv=1 - parity)


def _rwkv_mix(wide, lora_in, pc, w2p, a2p, batch, seq):
    tb = RWKV_BLOCK
    nt = seq // tb
    groups = RWKV_WIDTH // LANES
    nc = tb // RWKV_CHUNK
    n_blocks = batch * groups * nt

    def where(s):
        s = jnp.clip(s, 0, n_blocks - 1)
        return (s // (nt * groups)) * nt + s % nt, (s // nt) % groups

    col = lambda c0: (lambda s: (where(s)[0], c0 + where(s)[1]))
    fixed_col = lambda c: (lambda s: (where(s)[0], c))
    grp = lambda s: (0, where(s)[1])
    return pl.pallas_call(
        functools.partial(_rwkv_kernel, tb=tb, nt=nt),
        grid=(n_blocks + 2,),
        in_specs=[
            pl.BlockSpec((tb, LANES), col(COL_R)),
            pl.BlockSpec((tb, LANES), col(COL_K)),
            pl.BlockSpec((tb, LANES), col(COL_V)),
            pl.BlockSpec((tb, LANES), col(COL_ZB)),
            pl.BlockSpec((tb, LANES), fixed_col(0)),
            pl.BlockSpec((tb, LANES), fixed_col(1)),
            pl.BlockSpec((P_ROWS, LANES), grp),
            pl.BlockSpec((LANES, LANES), grp),
            pl.BlockSpec((LANES, LANES), grp),
            pl.BlockSpec((P_ROWS, LANES), lambda s: grp(s - 2)),
        ],
        out_specs=pl.BlockSpec((tb, LANES), lambda s: (where(s - 2)[0], where(s - 2)[1])),
        out_shape=jax.ShapeDtypeStruct((batch * seq, RWKV_WIDTH), bf16),
        scratch_shapes=[
            pltpu.VMEM((LANES, LANES), f32),
            pltpu.VMEM((4, SUBLANES, LANES), f32),
            pltpu.VMEM((2, 7, nc, GROUP_ROWS, LANES), bf16),
            pltpu.VMEM((2, nc, GROUP_ROWS, LANES), f32),
            pltpu.VMEM((2, 2, tb, LANES), f32),
            pltpu.VMEM((2, nc, SUBLANES, LANES), f32),
            pltpu.VMEM((2, 2, nc, GROUP_ROWS, LANES), bf16),
            pltpu.VMEM((2, 2, nc, GROUP_ROWS, LANES), f32),
            pltpu.VMEM((2, 2, tb, LANES), f32),
            pltpu.VMEM((2, nc, SUBLANES, LANES), f32),
        ],
        compiler_params=pltpu.CompilerParams(
            dimension_semantics=("arbitrary",),
            vmem_limit_bytes=VMEM_LIMIT),
        name="rwkv_mix",
    )(wide, wide, wide, wide, lora_in, lora_in, pc, w2p, a2p, pc)


def _out_kernel(oa_ref, ob_ref, ga_ref, gb_ref, x_ref, wpf_ref, wpr_ref, wo_ref, g_ref, o_ref):
    pa = jnp.dot(oa_ref[...], wpf_ref[...], preferred_element_type=f32)
    pb = jnp.dot(ob_ref[...], wpr_ref[...], preferred_element_type=f32)
    m = _sigmoid(ga_ref[...]) * pa + _sigmoid(gb_ref[...]) * pb
    z = x_ref[...] + jnp.dot(m.astype(bf16), wo_ref[...], preferred_element_type=f32)
    ms = jnp.mean(z * z, axis=-1, keepdims=True)
    o_ref[...] = z * lax.rsqrt(ms + RMS_EPS) * g_ref[...]


def _merge_out(oa, ob, wide, x2d, wpf, wpr, wo, gain):
    m, d = x2d.shape
    tm = 256
    resident = lambda shape: pl.BlockSpec(shape, lambda i: (0, 0), pipeline_mode=pl.Buffered(1))
    return pl.pallas_call(
        _out_kernel,
        grid=(m // tm,),
        in_specs=[
            pl.BlockSpec((tm, FOX_WIDTH), lambda i: (i, 0)),
            pl.BlockSpec((tm, RWKV_WIDTH), lambda i: (i, 0)),
            pl.BlockSpec((tm, d), lambda i: (i, COL_GA * LANES // d)),
            pl.BlockSpec((tm, d), lambda i: (i, COL_GB * LANES // d)),
            pl.BlockSpec((tm, d), lambda i: (i, 0)),
            resident((FOX_WIDTH, d)),
            resident((RWKV_WIDTH, d)),
            resident((d, d)),
            resident((1, d)),
        ],
        out_specs=pl.BlockSpec((tm, d), lambda i: (i, 0)),
        out_shape=jax.ShapeDtypeStruct((m, d), f32),
        compiler_params=pltpu.CompilerParams(
            dimension_semantics=("parallel",),
            vmem_limit_bytes=VMEM_LIMIT),
        name="merge_out",
    )(oa, ob, wide, wide, x2d, wpf, wpr, wo, gain)


def _branches(x2d, batch, seq, norm_gain, w_in, fox_forget_bias, rwkv_shift_mix, rwkv_w0, rwkv_w2,
              rwkv_a0, rwkv_a2, rwkv_k_k, rwkv_k_a, rwkv_r_k, rwkv_ln_w, rwkv_ln_b):
    fw, rw = FOX_WIDTH, RWKV_WIDTH
    w_t = w_in.T
    r_qkv = 0
    r_za = r_qkv + 3 * fw
    r_f = r_za + fw
    r_rkvz = r_f + FOX_HEADS
    r_wd = r_rkvz + 4 * rw
    r_ad = r_wd + LORA
    r_g = r_ad + LORA
    pieces = [(r_qkv, 3 * fw), (r_g, 2 * D_MODEL), (r_za, fw), (r_rkvz, 4 * rw)]
    pad_rows = lambda a: jnp.pad(a, ((0, LANES - a.shape[0]), (0, 0)))
    w_tail = jnp.concatenate(
        [pad_rows(w_t[r_wd:r_wd + LORA]),
         pad_rows(jnp.concatenate([w_t[r_ad:r_ad + LORA], w_t[r_f:r_f + FOX_HEADS]], axis=0))], axis=0)
    h2d = _rmsnorm_bf16(x2d, norm_gain.reshape(1, D_MODEL))
    qkv, wide, small = _in_proj(h2d, w_t, pieces, fw, FOX_HEAD_DIM ** -0.5 * LOG2E, w_tail)

    bias_row = jnp.pad(fox_forget_bias.reshape(1, FOX_HEADS), ((0, 0), (F_LANE, LANES - F_LANE - FOX_HEADS)))
    mu = rwkv_shift_mix
    pad_lanes = lambda a: jnp.pad(a, (0, LANES - a.shape[0]))
    mu_row = jnp.concatenate([pad_lanes(mu[4 * rw:4 * rw + LORA]),
                              pad_lanes(mu[4 * rw + LORA:])]).reshape(1, SMALL_COLS)
    c, lora_in = _gate_cumsum(small, bias_row, mu_row, batch, seq)
    qkv3 = qkv.reshape(batch, seq, 3 * fw)
    oa = _fox_attention(qkv3, c.reshape(batch, seq, LANES), wide.reshape(batch, seq, WIDE_COLS),
                        batch, seq).reshape(batch * seq, fw)

    pc = jnp.zeros((P_ROWS, rw), f32)
    rows = [mu[0:rw], mu[rw:2 * rw], mu[2 * rw:3 * rw], mu[3 * rw:4 * rw], rwkv_w0, rwkv_a0,
            rwkv_k_k, rwkv_k_a, rwkv_r_k.reshape(rw), rwkv_ln_w, rwkv_ln_b]
    pc = pc.at[:len(rows)].set(jnp.stack(rows))
    w2p = jnp.pad(rwkv_w2, ((0, LANES - LORA), (0, 0))).astype(bf16)
    a2p = jnp.pad(rwkv_a2, ((0, LANES - LORA), (0, 0))).astype(bf16)
    ob = _rwkv_mix(wide, lora_in, pc, w2p, a2p, batch, seq)

    return oa, ob, wide


def kernel(x, norm_gain, w_in, fox_forget_bias, rwkv_shift_mix, rwkv_w0, rwkv_w2, rwkv_a0, rwkv_a2, rwkv_k_k, rwkv_k_a, rwkv_r_k, rwkv_ln_w, rwkv_ln_b, w_proj_fox, w_proj_rwkv, w_out, final_norm_gain):
    batch, seq, d = x.shape
    depth = norm_gain.shape[0]
    assert depth == 1, "the final rmsnorm is fused into the single layer's output kernel"
    x2d = x.reshape(batch * seq, d)
    oa, ob, wide = _branches(x2d, batch, seq, norm_gain[0], w_in[0], fox_forget_bias[0],
                             rwkv_shift_mix[0], rwkv_w0[0], rwkv_w2[0], rwkv_a0[0], rwkv_a2[0],
                             rwkv_k_k[0], rwkv_k_a[0], rwkv_r_k[0], rwkv_ln_w[0], rwkv_ln_b[0])
    out = _merge_out(oa, ob, wide, x2d, w_proj_fox[0].astype(bf16), w_proj_rwkv[0].astype(bf16),
                     w_out[0].astype(bf16), final_norm_gain.reshape(1, d))
    return out.reshape(batch, seq, d)
```

```python
import functools

import jax
import jax.numpy as jnp
from jax import lax
from jax.experimental import pallas as pl
from jax.experimental.pallas import tpu as pltpu

D_MODEL = 2048
FOX_HEADS = 8
FOX_HEAD_DIM = 128
FOX_WIDTH = FOX_HEADS * FOX_HEAD_DIM
RWKV_HEADS = 16
RWKV_HEAD_DIM = 64
RWKV_WIDTH = RWKV_HEADS * RWKV_HEAD_DIM
LORA = 96
RMS_EPS = 1e-6
GN_EPS = 64e-5
L2_EPS = 1e-12

LANES = 128
SUBLANES = 8
VMEM_LIMIT = 56 * 1024 * 1024

COL_GA = 0
COL_GB = D_MODEL // LANES
COL_ZA = 2 * D_MODEL // LANES
COL_R = COL_ZA + FOX_WIDTH // LANES
COL_K = COL_R + RWKV_WIDTH // LANES
COL_V = COL_K + RWKV_WIDTH // LANES
COL_ZB = COL_V + RWKV_WIDTH // LANES
WIDE_COLS = (COL_ZB + RWKV_WIDTH // LANES) * LANES
SMALL_COLS = 2 * LANES
F_LANE = LORA

RWKV_CHUNK = 64
RWKV_BLOCK = 1024
RWKV_TAIL_START = 0.3
RWKV_CHAIN_END = 0.8
HEADS_PER_GROUP = LANES // RWKV_HEAD_DIM
GROUP_ROWS = HEADS_PER_GROUP * RWKV_CHUNK

f32 = jnp.float32
bf16 = jnp.bfloat16


def _dot(a, b):
    return jnp.dot(a.astype(bf16), b.astype(bf16), preferred_element_type=f32)


def _dot_exact_ones(a, ones_bf16, nt=False, terms=3):
    if nt:
        mm = lambda x: lax.dot_general(ones_bf16, x, (((1,), (0,)), ((), ())),
                                       preferred_element_type=f32)
    else:
        mm = lambda x: jnp.dot(x, ones_bf16, preferred_element_type=f32)
    part = a.astype(bf16)
    out = mm(part)
    rest = a
    for _ in range(terms - 1):
        rest = rest - part.astype(f32)
        part = rest.astype(bf16)
        out = out + mm(part)
    return out


def _softplus(x):
    return jnp.maximum(x, 0.0) + jnp.log(1.0 + jnp.exp(-jnp.abs(x)))


def _sigmoid(x):
    return 1.0 / (1.0 + jnp.exp(-x))


PROJ_TM = 1024
PROJ_TN = 1024


def _rmsnorm_kernel(x_ref, g_ref, h_ref):
    xf = x_ref[...]
    ms = jnp.mean(xf * xf, axis=-1, keepdims=True)
    h_ref[...] = (xf * lax.rsqrt(ms + RMS_EPS) * g_ref[...]).astype(h_ref.dtype)


def _rmsnorm_bf16(x2d, gain):
    m, d = x2d.shape
    tm = 512
    return pl.pallas_call(
        _rmsnorm_kernel,
        grid=(m // tm,),
        in_specs=[pl.BlockSpec((tm, d), lambda i: (i, 0)), pl.BlockSpec((1, d), lambda i: (0, 0))],
        out_specs=pl.BlockSpec((tm, d), lambda i: (i, 0)),
        out_shape=jax.ShapeDtypeStruct((m, d), bf16),
        compiler_params=pltpu.CompilerParams(dimension_semantics=("parallel",), vmem_limit_bytes=VMEM_LIMIT),
        name="rmsnorm",
    )(x2d, gain)


def _in_proj_kernel(h_ref, w_ref, wt_ref, qkv_ref, wide_ref, small_ref, w16_ref, *, n_qkv, n_wide, scaled_tiles, scale):
    j = pl.program_id(0)
    i = pl.program_id(1)
    n_main = n_qkv + n_wide

    @pl.when(i == 0)
    def _():
        @pl.when(j < n_main)
        def _():
            factor = jnp.where(j < scaled_tiles, scale, 1.0)
            w16_ref[...] = (w_ref[...] * factor).astype(bf16)

        @pl.when(j == n_main)
        def _():
            w16_ref[0:SMALL_COLS, :] = wt_ref[...].astype(bf16)

    project = lambda w: lax.dot_general(h_ref[...], w, (((1,), (1,)), ((), ())), preferred_element_type=f32)

    @pl.when(j < n_qkv)
    def _():
        qkv_ref[...] = project(w16_ref[...]).astype(qkv_ref.dtype)

    @pl.when(jnp.logical_and(j >= n_qkv, j < n_main))
    def _():
        wide_ref[...] = project(w16_ref[...])

    @pl.when(j == n_main)
    def _():
        small_ref[...] = project(w16_ref[0:SMALL_COLS, :])


def _in_proj(h2d, w_t, pieces, scaled_rows, scale, w_tail):
    m, d = h2d.shape
    tm, tn = PROJ_TM, PROJ_TN
    n_qkv = 3 * FOX_WIDTH // tn
    n_wide = WIDE_COLS // tn
    n_main = n_qkv + n_wide
    n_i = m // tm
    assert n_qkv * tn == 3 * FOX_WIDTH and n_wide * tn == WIDE_COLS and w_tail.shape[0] == SMALL_COLS
    assert scaled_rows % tn == 0 and all(n % tn == 0 and src % SUBLANES == 0 for src, n in pieces)
    assert sum(n for _, n in pieces) == n_main * tn
    starts, first_tile = [], 0
    for src, n in pieces:
        starts.append((first_tile, src))
        first_tile += n // tn

    def src_row(j):
        tile = jnp.int32(0)
        for blk, src in starts:
            tile = jnp.where(j >= blk, src // SUBLANES + (jnp.minimum(j, n_main - 1) - blk) * (tn // SUBLANES), tile)
        return pl.multiple_of(tile * SUBLANES, SUBLANES)

    qkv_row = lambda j, i: jnp.where(j < n_qkv, i, n_i - 1)
    wide_row = lambda j, i: jnp.where(j < n_qkv, 0, jnp.where(j < n_main, i, n_i - 1))
    small_row = lambda j, i: jnp.where(j == n_main, i, 0)
    return pl.pallas_call(
        functools.partial(_in_proj_kernel, n_qkv=n_qkv, n_wide=n_wide, scaled_tiles=scaled_rows // tn,
                          scale=scale),
        grid=(n_main + 1, n_i),
        in_specs=[
            pl.BlockSpec((tm, d), lambda j, i: (i, 0)),
            pl.BlockSpec((pl.Element(tn), pl.Element(d)), lambda j, i: (src_row(j), 0)),
            pl.BlockSpec((SMALL_COLS, d), lambda j, i: (0, 0)),
        ],
        out_specs=[
            pl.BlockSpec((tm, tn), lambda j, i: (qkv_row(j, i), jnp.minimum(j, n_qkv - 1))),
            pl.BlockSpec((tm, tn), lambda j, i: (wide_row(j, i), jnp.clip(j - n_qkv, 0, n_wide - 1))),
            pl.BlockSpec((tm, SMALL_COLS), lambda j, i: (small_row(j, i), 0)),
        ],
        out_shape=[
            jax.ShapeDtypeStruct((m, 3 * FOX_WIDTH), bf16),
            jax.ShapeDtypeStruct((m, WIDE_COLS), f32),
            jax.ShapeDtypeStruct((m, SMALL_COLS), f32),
        ],
        scratch_shapes=[pltpu.VMEM((tn, d), bf16)],
        compiler_params=pltpu.CompilerParams(
            dimension_semantics=("arbitrary", "arbitrary"),
            vmem_limit_bytes=VMEM_LIMIT),
        name="in_proj",
    )(h2d, w_t, w_tail)


def _gate_kernel(f_ref, bias_ref, c_ref, carry_ref, *, tb):
    @pl.when(pl.program_id(1) == 0)
    def _():
        carry_ref[...] = jnp.zeros_like(carry_ref)

    z = f_ref[...] + bias_ref[...]
    log_f = -_softplus(-z)
    row = lax.broadcasted_iota(jnp.int32, (tb, tb), 0)
    col = lax.broadcasted_iota(jnp.int32, (tb, tb), 1)
    tri = jnp.where(col <= row, 1.0, 0.0).astype(bf16)
    c = _dot_exact_ones(log_f, tri, nt=True) + carry_ref[0:1, :]
    c_ref[...] = c
    carry_ref[0:1, :] = c[tb - 1:tb, :]


def _gate_cumsum(small, bias_row, batch, seq):
    tb = 512
    nt = seq // tb
    return pl.pallas_call(
        functools.partial(_gate_kernel, tb=tb),
        grid=(batch, nt),
        in_specs=[
            pl.BlockSpec((tb, LANES), lambda b, t: (b * nt + t, 1)),
            pl.BlockSpec((1, LANES), lambda b, t: (0, 0)),
        ],
        out_specs=pl.BlockSpec((tb, LANES), lambda b, t: (b * nt + t, 0)),
        out_shape=jax.ShapeDtypeStruct((batch * seq, LANES), f32),
        scratch_shapes=[pltpu.VMEM((SUBLANES, LANES), f32)],
        compiler_params=pltpu.CompilerParams(
            dimension_semantics=("parallel", "arbitrary")),
        name="fox_gate_cumsum",
    )(small, bias_row)


LOG2E = 1.4426950408889634
FOX_TK = 256
FOX_GW = 128
FOX_BATCH = 4


def _fox_kernel(q_ref, k_ref, v_ref, c_ref, z_ref, o_ref, crep_ref, acc_ref, vt_ref, *, seq, tk, gw):
    h = pl.program_id(1)
    ng = seq // gw
    nt = seq // tk
    assert seq % gw == 0 and seq % tk == 0 and (tk % gw == 0 or gw % tk == 0)
    mm = lambda a, b: jnp.dot(a, b, preferred_element_type=f32)
    mm_nt = lambda a, b: lax.dot_general(a, b, (((1,), (1,)), ((), ())), preferred_element_type=f32)

    src_lane = lax.broadcasted_iota(jnp.int32, (LANES, LANES), 0)
    pick = jnp.where(src_lane == F_LANE + h, 1.0, 0.0).astype(bf16)
    crep = _dot_exact_ones(c_ref[...], pick) * LOG2E
    crep_ref[...] = jnp.concatenate([crep] * (gw // LANES), axis=1)
    for t in range(nt):
        vt_ref[t] = v_ref[t * tk:(t + 1) * tk, :].T

    key_rel = lax.broadcasted_iota(jnp.int32, (tk, gw), 0)
    qry_rel = lax.broadcasted_iota(jnp.int32, (tk, gw), 1)

    def visible(t, g):
        return t * tk <= g * gw + gw - 1

    def needs_mask(t, g):
        return t * tk + tk - 1 > g * gw

    last_tile = [max(t for t in range(nt) if visible(t, g)) for g in range(ng)]
    sched = []
    for t in range(nt):
        groups = [g for g in range(ng) if visible(t, g)]
        sched += [[(t, g) for g in groups[k:k + FOX_BATCH]] for k in range(0, len(groups), FOX_BATCH)]

    def qk(batch):
        return [mm_nt(k_ref[t * tk:(t + 1) * tk, :], q_ref[g * gw:(g + 1) * gw, :]) for t, g in batch]

    m = [None] * ng
    l = [None] * ng

    def finish(batch, alpha, pv):
        for (t, g), a, x in zip(batch, alpha, pv):
            acc_ref[g] = x if t == 0 else a * acc_ref[g] + x
            if t == last_tile[g]:
                rows = slice(g * gw, (g + 1) * gw)
                z = z_ref[rows, :]
                o = (acc_ref[g] / l[g]).T
                o_ref[rows, :] = (o * (z * _sigmoid(z))).astype(o_ref.dtype)

    s_next = qk(sched[0])
    pending = None
    for bi, batch in enumerate(sched):
        s_cur = s_next
        if bi + 1 < len(sched):
            s_next = qk(sched[bi + 1])
        alpha, p16 = [], []
        for (t, g), s in zip(batch, s_cur):
            s = s - crep_ref[t * tk:(t + 1) * tk, :]
            if needs_mask(t, g):
                s = jnp.where(key_rel + t * tk > qry_rel + g * gw, -jnp.inf, s)
            m_tile = jnp.max(s, axis=0, keepdims=True)
            if t == 0:
                m_new, a = m_tile, None
            else:
                m_new = jnp.maximum(m[g], m_tile)
                a = jnp.exp2(m[g] - m_new)
            p = jnp.exp2(s - m_new)
            p_sum = jnp.sum(p, axis=0, keepdims=True)
            l[g] = p_sum if t == 0 else a * l[g] + p_sum
            m[g] = m_new
            alpha.append(a)
            p16.append(p.astype(bf16))
        pv = [mm(vt_ref[t], p) for (t, g), p in zip(batch, p16)]
        if pending is not None:
            finish(*pending)
        pending = (batch, alpha, pv)
    finish(*pending)


def _fox_attention(qkv3, c3, wide3, batch, seq):
    tk, gw = FOX_TK, FOX_GW
    h8 = FOX_HEADS
    return pl.pallas_call(
        functools.partial(_fox_kernel, seq=seq, tk=tk, gw=gw),
        grid=(batch, h8),
        in_specs=[
            pl.BlockSpec((None, seq, LANES), lambda b, h: (b, 0, h)),
            pl.BlockSpec((None, seq, LANES), lambda b, h: (b, 0, h8 + h)),
            pl.BlockSpec((None, seq, LANES), lambda b, h: (b, 0, 2 * h8 + h)),
            pl.BlockSpec((None, seq, LANES), lambda b, h: (b, 0, 0)),
            pl.BlockSpec((None, seq, LANES), lambda b, h: (b, 0, COL_ZA + h)),
        ],
        out_specs=pl.BlockSpec((None, seq, LANES), lambda b, h: (b, 0, h)),
        out_shape=jax.ShapeDtypeStruct((batch, seq, FOX_WIDTH), bf16),
        scratch_shapes=[
            pltpu.VMEM((seq, gw), f32),
            pltpu.VMEM((seq // gw, LANES, gw), f32),
            pltpu.VMEM((seq // tk, LANES, tk), bf16),
        ],
        compiler_params=pltpu.CompilerParams(
            dimension_semantics=("parallel", "arbitrary"),
            vmem_limit_bytes=VMEM_LIMIT),
        name="fox_attention",
    )(qkv3, qkv3, qkv3, c3, wide3)


P_MU_R, P_MU_K, P_MU_V, P_MU_Z, P_W0, P_A0, P_KK, P_KA, P_RK, P_LNW, P_LNB = range(11)
P_ROWS = 16
K_XA, K_XR, K_YB, K_YK, K_VS, K_BH, K_KH = range(7)


def _spread(emitters, lo=0.0, hi=1.0):
    n = len(emitters)
    return [(lo + (hi - lo) * (i + 0.5) / n, e) for i, e in enumerate(emitters)]


def _interleave(*segment_lists):
    keyed = [(pos, prio, seg) for prio, segs in enumerate(segment_lists) for pos, seg in segs]
    keyed.sort(key=lambda x: (x[0], x[1]))
    for _, _, seg in keyed:
        seg()


def _rwkv_kernel(r_ref, k_ref, v_ref, z_ref, wd_ref, ad_ref, pc_ref, ps_ref, w2_ref, a2_ref, pc3_ref,
                 o_ref,
                 state_ref, prev_ref, stk_ref, xr32_ref, aux1_ref, dec1_ref, p16_ref, p32_ref,
                 aux2_ref, dec2_ref, *, tb, nt):
    C = RWKV_CHUNK
    G = GROUP_ROWS
    N = RWKV_HEAD_DIM
    NC = tb // C
    s = pl.program_id(0)
    first_of_seq_1 = (s % nt) == 0
    first_of_seq_3 = ((s + 2 * nt - 2) % nt) == 0

    @pl.when(s == 0)
    def _():
        for ref in (state_ref, prev_ref, stk_ref, xr32_ref, aux1_ref, dec1_ref, p16_ref, p32_ref,
                    aux2_ref, dec2_ref):
            ref[...] = jnp.zeros_like(ref)

    mm = lambda a, b: jnp.dot(a, b, preferred_element_type=f32)
    mm_nt = lambda a, b: lax.dot_general(a, b, (((1,), (1,)), ((), ())), preferred_element_type=f32)
    mm_tn = lambda a, b: lax.dot_general(a, b, (((0,), (0,)), ((), ())), preferred_element_type=f32)

    def body(cur, prv):
        li = lax.broadcasted_iota(jnp.int32, (LANES, LANES), 0)
        lj = lax.broadcasted_iota(jnp.int32, (LANES, LANES), 1)
        head_ones = jnp.where(li // N == lj // N, 1.0, 0.0).astype(bf16)
        head_sum = lambda x: _dot_exact_ones(x, head_ones, terms=2)
        lane = lax.broadcasted_iota(jnp.int32, (1, LANES), 1)
        head_masks = [jnp.where(lane // N == h, 1.0, 0.0) for h in range(HEADS_PER_GROUP)]
        stack = lambda x: jnp.concatenate([x * hm for hm in head_masks], axis=0)
        lane_c = lax.broadcasted_iota(jnp.int32, (C, LANES), 1)
        head_sel = [lane_c // N == h for h in range(HEADS_PER_GROUP)]

        def stack16(x):
            xb = x.astype(bf16)
            return jnp.concatenate([jnp.where(m, xb, jnp.zeros_like(xb)) for m in head_sel], axis=0)

        gi = lax.broadcasted_iota(jnp.int32, (G, G), 0)
        gj = lax.broadcasted_iota(jnp.int32, (G, G), 1)
        same_head = gi // C == gj // C
        strict = jnp.logical_and(same_head, gj < gi)
        incl = jnp.logical_and(same_head, gj <= gi)
        eye = jnp.where(gi == gj, 1.0, 0.0)
        chunks = range(NC)

        v1 = {}
        prm = lambda idx: pc_ref[idx:idx + 1, :]
        row_in_block = lax.broadcasted_iota(jnp.int32, (tb, LANES), 0)

        def shifted(ref, slot, mu):
            u = ref[...]
            carry = jnp.where(first_of_seq_1, 0.0, prev_ref[slot, 0:1, :])
            prev = jnp.where(row_in_block == 0, carry, pltpu.roll(u, 1, 0))
            prev_ref[slot, 0:1, :] = u[tb - 1:tb, :]
            return u + (prev - u) * mu

        def s1_lora():
            v1["wd"] = shifted(wd_ref, 4, ps_ref[0:1, :])
            v1["ad"] = shifted(ad_ref, 5, ps_ref[1:2, :])
            v1["w_lin"] = _dot(jnp.tanh(v1["wd"]), w2_ref[...])
            v1["a_lin"] = _dot(v1["ad"], a2_ref[...])

        def s1_key_norm():
            v1["kr"] = shifted(k_ref, 1, prm(P_MU_K))
            kk = v1["kr"] * prm(P_KK)
            v1["kk_raw"] = kk
            v1["kk_ss"] = head_sum(kk * kk)

        def s1_decay():
            w = -_softplus(-(prm(P_W0) + v1["w_lin"])) - 0.5
            v1["log_decay"] = -jnp.exp(w)
            ti = lax.broadcasted_iota(jnp.int32, (2 * C, 2 * C), 0)
            tj = lax.broadcasted_iota(jnp.int32, (2 * C, 2 * C), 1)
            pair_tri = jnp.where(jnp.logical_and(tj <= ti, ti // C == tj // C), 1.0, 0.0).astype(bf16)
            v1["ci"] = jnp.concatenate(
                [_dot_exact_ones(v1["log_decay"][k:k + 2 * C], pair_tri, nt=True)
                 for k in range(0, tb, 2 * C)], axis=0)

        def s1_bonus():
            v1["rate"] = _sigmoid(prm(P_A0) + v1["a_lin"])
            v1["r"] = shifted(r_ref, 0, prm(P_MU_R))
            v1["vr"] = shifted(v_ref, 2, prm(P_MU_V))
            v1["kp"] = v1["kr"] * (1.0 + (v1["rate"] - 1.0) * prm(P_KA))
            aux1_ref[cur, 0] = head_sum(v1["r"] * v1["kp"] * prm(P_RK)) * v1["vr"]
            zb = shifted(z_ref, 3, prm(P_MU_Z))
            aux1_ref[cur, 1] = zb * _sigmoid(zb)

        def s1_scale():
            kk = v1["kk_raw"] / jnp.maximum(jnp.sqrt(v1["kk_ss"]), L2_EPS)
            v1["bb"] = kk * v1["rate"]
            ci = v1["ci"]
            v1["a_t"] = -kk * jnp.exp(ci - v1["log_decay"])
            v1["r_t"] = v1["r"] * jnp.exp(ci)
            inv = jnp.exp(-ci)
            v1["b_t"] = v1["bb"] * inv
            v1["k_t"] = v1["kp"] * inv

        def s1_stack(c):
            def emit():
                sl = slice(c * C, (c + 1) * C)
                ci_c = v1["ci"][sl]
                c_last = ci_c[C - 1:C, :]
                to_end = jnp.exp(c_last - ci_c)
                dec1_ref[cur, c] = jnp.broadcast_to(jnp.exp(c_last), (SUBLANES, LANES))
                xr = stack(v1["r_t"][sl])
                xr32_ref[cur, c] = xr
                stk_ref[cur, K_XR, c] = xr.astype(bf16)
                stk_ref[cur, K_XA, c] = stack16(v1["a_t"][sl])
                stk_ref[cur, K_YB, c] = stack16(v1["b_t"][sl])
                stk_ref[cur, K_YK, c] = stack16(v1["k_t"][sl])
                stk_ref[cur, K_VS, c] = stack16(v1["vr"][sl])
                stk_ref[cur, K_BH, c] = stack16(v1["bb"][sl] * to_end)
                stk_ref[cur, K_KH, c] = stack16(v1["kp"][sl] * to_end)
            return emit

        segs1 = ([(0.0, s1_lora), (0.02, s1_key_norm), (0.05, s1_decay), (0.08, s1_bonus)]
                 + _spread([s1_scale] + [s1_stack(c) for c in chunks], RWKV_TAIL_START, 1.0))

        v2 = {}
        ld = lambda kind, c: stk_ref[prv, kind, c]

        def s2_big():
            big = [mm_nt(jnp.concatenate([ld(K_XA, c), ld(K_XR, c)], axis=0),
                         jnp.concatenate([ld(K_YB, c), ld(K_YK, c)], axis=0)) for c in chunks]
            v2["a_ab"] = [jnp.where(strict, big[c][0:G, 0:G], 0.0) for c in chunks]
            v2["a_ak"] = [jnp.where(strict, big[c][0:G, G:2 * G], 0.0).astype(bf16) for c in chunks]
            v2["a_rb"] = [jnp.where(incl, big[c][G:2 * G, 0:G], 0.0).astype(bf16) for c in chunks]
            v2["a_rk"] = [jnp.where(incl, big[c][G:2 * G, G:2 * G], 0.0).astype(bf16) for c in chunks]
            v2["tinv"] = [eye + a for a in v2["a_ab"]]
            v2["pw"] = [a.astype(bf16) for a in v2["a_ab"]]

        def s2_first_square():
            v2["pw"] = [mm(p, p).astype(bf16) for p in v2["pw"]]

        def s2_level(last):
            def emit():
                for c in chunks:
                    pw = v2["pw"][c]
                    t16 = v2["tinv"][c].astype(bf16)
                    if last:
                        v2["tinv"][c] = v2["tinv"][c] + mm(t16, pw)
                    else:
                        both = mm(jnp.concatenate([t16, pw], axis=0), pw)
                        v2["tinv"][c] = v2["tinv"][c] + both[0:G]
                        v2["pw"][c] = both[G:2 * G].astype(bf16)
            return emit

        def s2_av():
            v2["tinv"] = [t.astype(bf16) for t in v2["tinv"]]
            av = [mm(jnp.concatenate([v2["a_ak"][c], v2["a_rk"][c]], axis=0), ld(K_VS, c)) for c in chunks]
            v2["akv"] = [x[0:G].astype(bf16) for x in av]
            v2["arkv"] = [x[G:2 * G] for x in av]

        def s2_wu():
            v2["wu"] = [mm(v2["tinv"][c], jnp.concatenate([ld(K_XA, c), v2["akv"][c]], axis=1)).astype(bf16)
                        for c in chunks]

        def s2_ry():
            for c in chunks:
                e = mm(v2["a_rb"][c], v2["wu"][c])
                p16_ref[cur, 0, c] = (xr32_ref[prv, c] + e[:, 0:LANES]).astype(bf16)
                p32_ref[cur, 0, c] = e[:, LANES:] + v2["arkv"][c]

        def s2_pm():
            for c in chunks:
                p16_ref[cur, 1, c] = mm_tn(v2["wu"][c][:, 0:LANES], ld(K_BH, c)).astype(bf16)

        def s2_q():
            for c in chunks:
                p32_ref[cur, 1, c] = mm_tn(jnp.concatenate([v2["wu"][c][:, LANES:], ld(K_VS, c)], axis=0),
                                           jnp.concatenate([ld(K_BH, c), ld(K_KH, c)], axis=0))
            aux2_ref[cur] = aux1_ref[prv]
            dec2_ref[cur] = dec1_ref[prv]

        segs2 = _spread([s2_big, s2_first_square] + [s2_level(False)] * 4 + [s2_level(True)]
                        + [s2_av, s2_wu, s2_ry, s2_pm, s2_q])

        v3 = {"ys": []}
        prm3 = lambda idx: pc3_ref[idx:idx + 1, :]

        def s3_start():
            v3["state"] = jnp.where(first_of_seq_3, 0.0, state_ref[...])

        def s3_chunk(c):
            def emit():
                state = v3["state"]
                s16 = state.astype(bf16)
                y_st = mm_nt(p16_ref[prv, 0, c], s16) + p32_ref[prv, 0, c]
                v3["ys"].append(y_st[0:C] + y_st[C:2 * C])
                v3["state"] = (state * dec2_ref[prv, c, 0:1, :] + mm(s16, p16_ref[prv, 1, c])
                               + p32_ref[prv, 1, c])
            return emit

        def s3_finish():
            state_ref[...] = v3["state"]
            y = jnp.concatenate(v3["ys"], axis=0)
            mean = head_sum(y) * (1.0 / N)
            yc = y - mean
            var = head_sum(yc * yc) * (1.0 / N)
            yn = yc * lax.rsqrt(var + GN_EPS) * prm3(P_LNW) + prm3(P_LNB)
            o_ref[...] = ((yn + aux2_ref[prv, 0]) * aux2_ref[prv, 1]).astype(o_ref.dtype)

        segs3 = _spread([s3_start] + [s3_chunk(c) for c in chunks] + [s3_finish], 0.0, RWKV_CHAIN_END)

        _interleave(segs3, segs2, segs1)

    for parity in (0, 1):
        @pl.when(s % 2 == parity)
        def _(parity=parity):
            body(cur=parity, prv=1 - parity)


def _rwkv_mix(wide, small, pc, ps, w2p, a2p, batch, seq):
    tb = RWKV_BLOCK
    nt = seq // tb
    groups = RWKV_WIDTH // LANES
    nc = tb // RWKV_CHUNK
    n_blocks = batch * groups * nt

    def where(s):
        s = jnp.clip(s, 0, n_blocks - 1)
        return (s // (nt * groups)) * nt + s % nt, (s // nt) % groups

    col = lambda c0: (lambda s: (where(s)[0], c0 + where(s)[1]))
    fixed_col = lambda c: (lambda s: (where(s)[0], c))
    grp = lambda s: (0, where(s)[1])
    return pl.pallas_call(
        functools.partial(_rwkv_kernel, tb=tb, nt=nt),
        grid=(n_blocks + 2,),
        in_specs=[
            pl.BlockSpec((tb, LANES), col(COL_R)),
            pl.BlockSpec((tb, LANES), col(COL_K)),
            pl.BlockSpec((tb, LANES), col(COL_V)),
            pl.BlockSpec((tb, LANES), col(COL_ZB)),
            pl.BlockSpec((tb, LANES), fixed_col(0)),
            pl.BlockSpec((tb, LANES), fixed_col(1)),
            pl.BlockSpec((P_ROWS, LANES), grp),
            pl.BlockSpec((SUBLANES, LANES), lambda s: (0, 0)),
            pl.BlockSpec((LANES, LANES), grp),
            pl.BlockSpec((LANES, LANES), grp),
            pl.BlockSpec((P_ROWS, LANES), lambda s: grp(s - 2)),
        ],
        out_specs=pl.BlockSpec((tb, LANES), lambda s: (where(s - 2)[0], where(s - 2)[1])),
        out_shape=jax.ShapeDtypeStruct((batch * seq, RWKV_WIDTH), bf16),
        scratch_shapes=[
            pltpu.VMEM((LANES, LANES), f32),
            pltpu.VMEM((6, SUBLANES, LANES), f32),
            pltpu.VMEM((2, 7, nc, GROUP_ROWS, LANES), bf16),
            pltpu.VMEM((2, nc, GROUP_ROWS, LANES), f32),
            pltpu.VMEM((2, 2, tb, LANES), f32),
            pltpu.VMEM((2, nc, SUBLANES, LANES), f32),
            pltpu.VMEM((2, 2, nc, GROUP_ROWS, LANES), bf16),
            pltpu.VMEM((2, 2, nc, GROUP_ROWS, LANES), f32),
            pltpu.VMEM((2, 2, tb, LANES), f32),
            pltpu.VMEM((2, nc, SUBLANES, LANES), f32),
        ],
        compiler_params=pltpu.CompilerParams(
            dimension_semantics=("arbitrary",),
            vmem_limit_bytes=VMEM_LIMIT),
        name="rwkv_mix",
    )(wide, wide, wide, wide, small, small, pc, ps, w2p, a2p, pc)


def _out_kernel(oa_ref, ob_ref, ga_ref, gb_ref, x_ref, wpf_ref, wpr_ref, wo_ref, g_ref, o_ref):
    pa = jnp.dot(oa_ref[...], wpf_ref[...], preferred_element_type=f32)
    pb = jnp.dot(ob_ref[...], wpr_ref[...], preferred_element_type=f32)
    m = _sigmoid(ga_ref[...]) * pa + _sigmoid(gb_ref[...]) * pb
    z = x_ref[...] + jnp.dot(m.astype(bf16), wo_ref[...], preferred_element_type=f32)
    ms = jnp.mean(z * z, axis=-1, keepdims=True)
    o_ref[...] = z * lax.rsqrt(ms + RMS_EPS) * g_ref[...]


def _merge_out(oa, ob, wide, x2d, wpf, wpr, wo, gain):
    m, d = x2d.shape
    tm = 256
    resident = lambda shape: pl.BlockSpec(shape, lambda i: (0, 0), pipeline_mode=pl.Buffered(1))
    return pl.pallas_call(
        _out_kernel,
        grid=(m // tm,),
        in_specs=[
            pl.BlockSpec((tm, FOX_WIDTH), lambda i: (i, 0)),
            pl.BlockSpec((tm, RWKV_WIDTH), lambda i: (i, 0)),
            pl.BlockSpec((tm, d), lambda i: (i, COL_GA * LANES // d)),
            pl.BlockSpec((tm, d), lambda i: (i, COL_GB * LANES // d)),
            pl.BlockSpec((tm, d), lambda i: (i, 0)),
            resident((FOX_WIDTH, d)),
            resident((RWKV_WIDTH, d)),
            resident((d, d)),
            resident((1, d)),
        ],
        out_specs=pl.BlockSpec((tm, d), lambda i: (i, 0)),
        out_shape=jax.ShapeDtypeStruct((m, d), f32),
        compiler_params=pltpu.CompilerParams(
            dimension_semantics=("parallel",),
            vmem_limit_bytes=VMEM_LIMIT),
        name="merge_out",
    )(oa, ob, wide, wide, x2d, wpf, wpr, wo, gain)


def _branches(x2d, batch, seq, norm_gain, w_in, fox_forget_bias, rwkv_shift_mix, rwkv_w0, rwkv_w2,
              rwkv_a0, rwkv_a2, rwkv_k_k, rwkv_k_a, rwkv_r_k, rwkv_ln_w, rwkv_ln_b):
    fw, rw = FOX_WIDTH, RWKV_WIDTH
    w_t = w_in.T
    r_qkv = 0
    r_za = r_qkv + 3 * fw
    r_f = r_za + fw
    r_rkvz = r_f + FOX_HEADS
    r_wd = r_rkvz + 4 * rw
    r_ad = r_wd + LORA
    r_g = r_ad + LORA
    pieces = [(r_qkv, 3 * fw), (r_g, 2 * D_MODEL), (r_za, fw), (r_rkvz, 4 * rw)]
    pad_rows = lambda a: jnp.pad(a, ((0, LANES - a.shape[0]), (0, 0)))
    w_tail = jnp.concatenate(
        [pad_rows(w_t[r_wd:r_wd + LORA]),
         pad_rows(jnp.concatenate([w_t[r_ad:r_ad + LORA], w_t[r_f:r_f + FOX_HEADS]], axis=0))], axis=0)
    h2d = _rmsnorm_bf16(x2d, norm_gain.reshape(1, D_MODEL))
    qkv, wide, small = _in_proj(h2d, w_t, pieces, fw, FOX_HEAD_DIM ** -0.5 * LOG2E, w_tail)

    bias_row = jnp.pad(fox_forget_bias.reshape(1, FOX_HEADS), ((0, 0), (F_LANE, LANES - F_LANE - FOX_HEADS)))
    c = _gate_cumsum(small, bias_row, batch, seq)
    qkv3 = qkv.reshape(batch, seq, 3 * fw)
    oa = _fox_attention(qkv3, c.reshape(batch, seq, LANES), wide.reshape(batch, seq, WIDE_COLS),
                        batch, seq).reshape(batch * seq, fw)

    mu = rwkv_shift_mix
    pc = jnp.zeros((P_ROWS, rw), f32)
    rows = [mu[0:rw], mu[rw:2 * rw], mu[2 * rw:3 * rw], mu[3 * rw:4 * rw], rwkv_w0, rwkv_a0,
            rwkv_k_k, rwkv_k_a, rwkv_r_k.reshape(rw), rwkv_ln_w, rwkv_ln_b]
    pc = pc.at[:len(rows)].set(jnp.stack(rows))
    ps = jnp.zeros((SUBLANES, LANES), f32)
    ps = ps.at[0, :LORA].set(mu[4 * rw:4 * rw + LORA]).at[1, :LORA].set(mu[4 * rw + LORA:])
    w2p = jnp.pad(rwkv_w2, ((0, LANES - LORA), (0, 0))).astype(bf16)
    a2p = jnp.pad(rwkv_a2, ((0, LANES - LORA), (0, 0))).astype(bf16)
    ob = _rwkv_mix(wide, small, pc, ps, w2p, a2p, batch, seq)

    return oa, ob, wide


def kernel(x, norm_gain, w_in, fox_forget_bias, rwkv_shift_mix, rwkv_w0, rwkv_w2, rwkv_a0, rwkv_a2, rwkv_k_k, rwkv_k_a, rwkv_r_k, rwkv_ln_w, rwkv_ln_b, w_proj_fox, w_proj_rwkv, w_out, final_norm_gain):
    batch, seq, d = x.shape
    depth = norm_gain.shape[0]
    assert depth == 1, "the final rmsnorm is fused into the single layer's output kernel"
    x2d = x.reshape(batch * seq, d)
    oa, ob, wide = _branches(x2d, batch, seq, norm_gain[0], w_in[0], fox_forget_bias[0],
                             rwkv_shift_mix[0], rwkv_w0[0], rwkv_w2[0], rwkv_a0[0], rwkv_a2[0],
                             rwkv_k_k[0], rwkv_k_a[0], rwkv_r_k[0], rwkv_ln_w[0], rwkv_ln_b[0])
    out = _merge_out(oa, ob, wide, x2d, w_proj_fox[0].astype(bf16), w_proj_rwkv[0].astype(bf16),
                     w_out[0].astype(bf16), final_norm_gain.reshape(1, d))
    return out.reshape(batch, seq, d)
```

```python
import functools

import jax
import jax.numpy as jnp
from jax import lax
from jax.experimental import pallas as pl
from jax.experimental.pallas import tpu as pltpu

D_MODEL = 2048
FOX_HEADS = 8
FOX_HEAD_DIM = 128
FOX_WIDTH = FOX_HEADS * FOX_HEAD_DIM
RWKV_HEADS = 16
RWKV_HEAD_DIM = 64
RWKV_WIDTH = RWKV_HEADS * RWKV_HEAD_DIM
LORA = 96
RMS_EPS = 1e-6
GN_EPS = 64e-5
L2_EPS = 1e-12

LANES = 128
SUBLANES = 8
VMEM_LIMIT = 56 * 1024 * 1024

COL_GA = 0
COL_GB = D_MODEL // LANES
COL_ZA = 2 * D_MODEL // LANES
COL_R = COL_ZA + FOX_WIDTH // LANES
COL_K = COL_R + RWKV_WIDTH // LANES
COL_V = COL_K + RWKV_WIDTH // LANES
COL_ZB = COL_V + RWKV_WIDTH // LANES
WIDE_COLS = (COL_ZB + RWKV_WIDTH // LANES) * LANES
SMALL_COLS = 2 * LANES
F_LANE = LORA

RWKV_CHUNK = 64
RWKV_BLOCK = 512
RWKV_TAIL_START = 0.3
RWKV_CHAIN_END = 0.8
HEADS_PER_GROUP = LANES // RWKV_HEAD_DIM
GROUP_ROWS = HEADS_PER_GROUP * RWKV_CHUNK

f32 = jnp.float32
bf16 = jnp.bfloat16


def _dot(a, b):
    return jnp.dot(a.astype(bf16), b.astype(bf16), preferred_element_type=f32)


def _dot_exact_ones(a, ones_bf16, nt=False, terms=3):
    if nt:
        mm = lambda x: lax.dot_general(ones_bf16, x, (((1,), (0,)), ((), ())),
                                       preferred_element_type=f32)
    else:
        mm = lambda x: jnp.dot(x, ones_bf16, preferred_element_type=f32)
    part = a.astype(bf16)
    out = mm(part)
    rest = a
    for _ in range(terms - 1):
        rest = rest - part.astype(f32)
        part = rest.astype(bf16)
        out = out + mm(part)
    return out


def _softplus(x):
    return jnp.maximum(x, 0.0) + jnp.log(1.0 + jnp.exp(-jnp.abs(x)))


def _sigmoid(x):
    return 1.0 / (1.0 + jnp.exp(-x))


PROJ_TM = 1024
PROJ_TN = 1024


def _rmsnorm_kernel(x_ref, g_ref, h_ref):
    xf = x_ref[...]
    ms = jnp.mean(xf * xf, axis=-1, keepdims=True)
    h_ref[...] = (xf * lax.rsqrt(ms + RMS_EPS) * g_ref[...]).astype(h_ref.dtype)


def _rmsnorm_bf16(x2d, gain):
    m, d = x2d.shape
    tm = 512
    return pl.pallas_call(
        _rmsnorm_kernel,
        grid=(m // tm,),
        in_specs=[pl.BlockSpec((tm, d), lambda i: (i, 0)), pl.BlockSpec((1, d), lambda i: (0, 0))],
        out_specs=pl.BlockSpec((tm, d), lambda i: (i, 0)),
        out_shape=jax.ShapeDtypeStruct((m, d), bf16),
        compiler_params=pltpu.CompilerParams(dimension_semantics=("parallel",), vmem_limit_bytes=VMEM_LIMIT),
        name="rmsnorm",
    )(x2d, gain)


def _in_proj_kernel(h_ref, w_ref, wt_ref, qkv_ref, wide_ref, small_ref, w16_ref, *, n_qkv, n_wide, scaled_tiles, scale):
    j = pl.program_id(0)
    i = pl.program_id(1)
    n_main = n_qkv + n_wide

    @pl.when(i == 0)
    def _():
        @pl.when(j < n_main)
        def _():
            factor = jnp.where(j < scaled_tiles, scale, 1.0)
            w16_ref[...] = (w_ref[...] * factor).astype(bf16)

        @pl.when(j == n_main)
        def _():
            w16_ref[0:SMALL_COLS, :] = wt_ref[...].astype(bf16)

    project = lambda w: lax.dot_general(h_ref[...], w, (((1,), (1,)), ((), ())), preferred_element_type=f32)

    @pl.when(j < n_qkv)
    def _():
        qkv_ref[...] = project(w16_ref[...]).astype(qkv_ref.dtype)

    @pl.when(jnp.logical_and(j >= n_qkv, j < n_main))
    def _():
        wide_ref[...] = project(w16_ref[...])

    @pl.when(j == n_main)
    def _():
        small_ref[...] = project(w16_ref[0:SMALL_COLS, :])


def _in_proj(h2d, w_t, pieces, scaled_rows, scale, w_tail):
    m, d = h2d.shape
    tm, tn = PROJ_TM, PROJ_TN
    n_qkv = 3 * FOX_WIDTH // tn
    n_wide = WIDE_COLS // tn
    n_main = n_qkv + n_wide
    n_i = m // tm
    assert n_qkv * tn == 3 * FOX_WIDTH and n_wide * tn == WIDE_COLS and w_tail.shape[0] == SMALL_COLS
    assert scaled_rows % tn == 0 and all(n % tn == 0 and src % SUBLANES == 0 for src, n in pieces)
    assert sum(n for _, n in pieces) == n_main * tn
    starts, first_tile = [], 0
    for src, n in pieces:
        starts.append((first_tile, src))
        first_tile += n // tn

    def src_row(j):
        tile = jnp.int32(0)
        for blk, src in starts:
            tile = jnp.where(j >= blk, src // SUBLANES + (jnp.minimum(j, n_main - 1) - blk) * (tn // SUBLANES), tile)
        return pl.multiple_of(tile * SUBLANES, SUBLANES)

    qkv_row = lambda j, i: jnp.where(j < n_qkv, i, n_i - 1)
    wide_row = lambda j, i: jnp.where(j < n_qkv, 0, jnp.where(j < n_main, i, n_i - 1))
    small_row = lambda j, i: jnp.where(j == n_main, i, 0)
    return pl.pallas_call(
        functools.partial(_in_proj_kernel, n_qkv=n_qkv, n_wide=n_wide, scaled_tiles=scaled_rows // tn,
                          scale=scale),
        grid=(n_main + 1, n_i),
        in_specs=[
            pl.BlockSpec((tm, d), lambda j, i: (i, 0)),
            pl.BlockSpec((pl.Element(tn), pl.Element(d)), lambda j, i: (src_row(j), 0)),
            pl.BlockSpec((SMALL_COLS, d), lambda j, i: (0, 0)),
        ],
        out_specs=[
            pl.BlockSpec((tm, tn), lambda j, i: (qkv_row(j, i), jnp.minimum(j, n_qkv - 1))),
            pl.BlockSpec((tm, tn), lambda j, i: (wide_row(j, i), jnp.clip(j - n_qkv, 0, n_wide - 1))),
            pl.BlockSpec((tm, SMALL_COLS), lambda j, i: (small_row(j, i), 0)),
        ],
        out_shape=[
            jax.ShapeDtypeStruct((m, 3 * FOX_WIDTH), bf16),
            jax.ShapeDtypeStruct((m, WIDE_COLS), f32),
            jax.ShapeDtypeStruct((m, SMALL_COLS), f32),
        ],
        scratch_shapes=[pltpu.VMEM((tn, d), bf16)],
        compiler_params=pltpu.CompilerParams(
            dimension_semantics=("arbitrary", "arbitrary"),
            vmem_limit_bytes=VMEM_LIMIT),
        name="in_proj",
    )(h2d, w_t, w_tail)


def _gate_kernel(f_ref, bias_ref, c_ref, carry_ref, *, tb):
    @pl.when(pl.program_id(1) == 0)
    def _():
        carry_ref[...] = jnp.zeros_like(carry_ref)

    z = f_ref[...] + bias_ref[...]
    log_f = -_softplus(-z)
    row = lax.broadcasted_iota(jnp.int32, (tb, tb), 0)
    col = lax.broadcasted_iota(jnp.int32, (tb, tb), 1)
    tri = jnp.where(col <= row, 1.0, 0.0).astype(bf16)
    c = _dot_exact_ones(log_f, tri, nt=True) + carry_ref[0:1, :]
    c_ref[...] = c
    carry_ref[0:1, :] = c[tb - 1:tb, :]


def _gate_cumsum(small, bias_row, batch, seq):
    tb = 512
    nt = seq // tb
    return pl.pallas_call(
        functools.partial(_gate_kernel, tb=tb),
        grid=(batch, nt),
        in_specs=[
            pl.BlockSpec((tb, LANES), lambda b, t: (b * nt + t, 1)),
            pl.BlockSpec((1, LANES), lambda b, t: (0, 0)),
        ],
        out_specs=pl.BlockSpec((tb, LANES), lambda b, t: (b * nt + t, 0)),
        out_shape=jax.ShapeDtypeStruct((batch * seq, LANES), f32),
        scratch_shapes=[pltpu.VMEM((SUBLANES, LANES), f32)],
        compiler_params=pltpu.CompilerParams(
            dimension_semantics=("parallel", "arbitrary")),
        name="fox_gate_cumsum",
    )(small, bias_row)


LOG2E = 1.4426950408889634
FOX_TK = 256
FOX_GW = 256
FOX_BATCH = 4


def _fox_kernel(q_ref, k_ref, v_ref, c_ref, z_ref, o_ref, crep_ref, acc_ref, vt_ref, *, seq, tk, gw):
    h = pl.program_id(1)
    ng = seq // gw
    nt = seq // tk
    assert seq % gw == 0 and seq % tk == 0 and (tk % gw == 0 or gw % tk == 0)
    mm = lambda a, b: jnp.dot(a, b, preferred_element_type=f32)
    mm_nt = lambda a, b: lax.dot_general(a, b, (((1,), (1,)), ((), ())), preferred_element_type=f32)

    src_lane = lax.broadcasted_iota(jnp.int32, (LANES, LANES), 0)
    pick = jnp.where(src_lane == F_LANE + h, 1.0, 0.0).astype(bf16)
    crep = _dot_exact_ones(c_ref[...], pick) * LOG2E
    crep_ref[...] = jnp.concatenate([crep] * (gw // LANES), axis=1)
    for t in range(nt):
        vt_ref[t, 0:LANES, :] = v_ref[t * tk:(t + 1) * tk, :].T
        vt_ref[t, LANES:LANES + SUBLANES, :] = jnp.ones((SUBLANES, tk), bf16)

    key_rel = lax.broadcasted_iota(jnp.int32, (tk, gw), 0)
    qry_rel = lax.broadcasted_iota(jnp.int32, (tk, gw), 1)

    def visible(t, g):
        return t * tk <= g * gw + gw - 1

    def needs_mask(t, g):
        return t * tk + tk - 1 > g * gw

    last_tile = [max(t for t in range(nt) if visible(t, g)) for g in range(ng)]
    sched = []
    for t in range(nt):
        groups = [g for g in range(ng) if visible(t, g)]
        sched += [[(t, g) for g in groups[k:k + FOX_BATCH]] for k in range(0, len(groups), FOX_BATCH)]

    def qk(batch):
        return [mm_nt(k_ref[t * tk:(t + 1) * tk, :], q_ref[g * gw:(g + 1) * gw, :]) for t, g in batch]

    m = [None] * ng
    l = [None] * ng

    def finish(batch, alpha, pv):
        for (t, g), a, xs in zip(batch, alpha, pv):
            x, p_sum = xs[0:LANES], xs[LANES:LANES + 1]
            acc_ref[g] = x if t == 0 else a * acc_ref[g] + x
            l[g] = p_sum if t == 0 else a * l[g] + p_sum
            if t == last_tile[g]:
                rows = slice(g * gw, (g + 1) * gw)
                z = z_ref[rows, :]
                o = (acc_ref[g] / l[g]).T
                o_ref[rows, :] = (o * (z * _sigmoid(z))).astype(o_ref.dtype)

    s_next = qk(sched[0])
    pending = None
    for bi, batch in enumerate(sched):
        s_cur = s_next
        if bi + 1 < len(sched):
            s_next = qk(sched[bi + 1])
        alpha, p16 = [], []
        for (t, g), s in zip(batch, s_cur):
            s = s - crep_ref[t * tk:(t + 1) * tk, :]
            if needs_mask(t, g):
                s = jnp.where(key_rel + t * tk > qry_rel + g * gw, -jnp.inf, s)
            m_tile = jnp.max(s, axis=0, keepdims=True)
            if t == 0:
                m_new, a = m_tile, None
            else:
                m_new = jnp.maximum(m[g], m_tile)
                a = jnp.exp2(m[g] - m_new)
            m[g] = m_new
            alpha.append(a)
            p16.append(jnp.exp2((s - m_new).astype(bf16)))
        pv = [mm(vt_ref[t], p) for (t, g), p in zip(batch, p16)]
        if pending is not None:
            finish(*pending)
        pending = (batch, alpha, pv)
    finish(*pending)


def _fox_attention(qkv3, c3, wide3, batch, seq):
    tk, gw = FOX_TK, FOX_GW
    h8 = FOX_HEADS
    return pl.pallas_call(
        functools.partial(_fox_kernel, seq=seq, tk=tk, gw=gw),
        grid=(batch, h8),
        in_specs=[
            pl.BlockSpec((None, seq, LANES), lambda b, h: (b, 0, h)),
            pl.BlockSpec((None, seq, LANES), lambda b, h: (b, 0, h8 + h)),
            pl.BlockSpec((None, seq, LANES), lambda b, h: (b, 0, 2 * h8 + h)),
            pl.BlockSpec((None, seq, LANES), lambda b, h: (b, 0, 0)),
            pl.BlockSpec((None, seq, LANES), lambda b, h: (b, 0, COL_ZA + h)),
        ],
        out_specs=pl.BlockSpec((None, seq, LANES), lambda b, h: (b, 0, h)),
        out_shape=jax.ShapeDtypeStruct((batch, seq, FOX_WIDTH), bf16),
        scratch_shapes=[
            pltpu.VMEM((seq, gw), f32),
            pltpu.VMEM((seq // gw, LANES, gw), f32),
            pltpu.VMEM((seq // tk, LANES + SUBLANES, tk), bf16),
        ],
        compiler_params=pltpu.CompilerParams(
            dimension_semantics=("parallel", "arbitrary"),
            vmem_limit_bytes=VMEM_LIMIT),
        name="fox_attention",
    )(qkv3, qkv3, qkv3, c3, wide3)


P_MU_R, P_MU_K, P_MU_V, P_MU_Z, P_W0, P_A0, P_KK, P_KA, P_RK, P_LNW, P_LNB = range(11)
P_ROWS = 16
K_XA, K_XR, K_YB, K_YK, K_VS, K_BH, K_KH = range(7)


def _spread(emitters, lo=0.0, hi=1.0):
    n = len(emitters)
    return [(lo + (hi - lo) * (i + 0.5) / n, e) for i, e in enumerate(emitters)]


def _interleave(*segment_lists):
    keyed = [(pos, prio, seg) for prio, segs in enumerate(segment_lists) for pos, seg in segs]
    keyed.sort(key=lambda x: (x[0], x[1]))
    for _, _, seg in keyed:
        seg()


def _rwkv_kernel(r_ref, k_ref, v_ref, z_ref, wd_ref, ad_ref, pc_ref, ps_ref, w2_ref, a2_ref, pc3_ref,
                 o_ref,
                 state_ref, prev_ref, stk_ref, xr32_ref, aux1_ref, dec1_ref, p16_ref, p32_ref,
                 aux2_ref, dec2_ref, *, tb, nt):
    C = RWKV_CHUNK
    G = GROUP_ROWS
    N = RWKV_HEAD_DIM
    NC = tb // C
    s = pl.program_id(0)
    first_of_seq_1 = (s % nt) == 0
    first_of_seq_3 = ((s + 2 * nt - 2) % nt) == 0

    @pl.when(s == 0)
    def _():
        for ref in (state_ref, prev_ref, stk_ref, xr32_ref, aux1_ref, dec1_ref, p16_ref, p32_ref,
                    aux2_ref, dec2_ref):
            ref[...] = jnp.zeros_like(ref)

    mm = lambda a, b: jnp.dot(a, b, preferred_element_type=f32)
    mm_nt = lambda a, b: lax.dot_general(a, b, (((1,), (1,)), ((), ())), preferred_element_type=f32)
    mm_tn = lambda a, b: lax.dot_general(a, b, (((0,), (0,)), ((), ())), preferred_element_type=f32)

    def body(cur, prv):
        li = lax.broadcasted_iota(jnp.int32, (LANES, LANES), 0)
        lj = lax.broadcasted_iota(jnp.int32, (LANES, LANES), 1)
        head_ones = jnp.where(li // N == lj // N, 1.0, 0.0).astype(bf16)
        head_sum = lambda x: _dot_exact_ones(x, head_ones, terms=2)
        lane = lax.broadcasted_iota(jnp.int32, (1, LANES), 1)
        head_masks = [jnp.where(lane // N == h, 1.0, 0.0) for h in range(HEADS_PER_GROUP)]
        stack = lambda x: jnp.concatenate([x * hm for hm in head_masks], axis=0)
        lane_c = lax.broadcasted_iota(jnp.int32, (C, LANES), 1)
        head_sel = [lane_c // N == h for h in range(HEADS_PER_GROUP)]

        def stack16(x):
            xb = x.astype(bf16)
            return jnp.concatenate([jnp.where(m, xb, jnp.zeros_like(xb)) for m in head_sel], axis=0)

        gi = lax.broadcasted_iota(jnp.int32, (G, G), 0)
        gj = lax.broadcasted_iota(jnp.int32, (G, G), 1)
        same_head = gi // C == gj // C
        strict = jnp.logical_and(same_head, gj < gi)
        incl = jnp.logical_and(same_head, gj <= gi)
        eye = jnp.where(gi == gj, 1.0, 0.0)
        chunks = range(NC)

        v1 = {}
        prm = lambda idx: pc_ref[idx:idx + 1, :]
        row_in_block = lax.broadcasted_iota(jnp.int32, (tb, LANES), 0)

        def shifted(ref, slot, mu):
            u = ref[...]
            carry = jnp.where(first_of_seq_1, 0.0, prev_ref[slot, 0:1, :])
            prev = jnp.where(row_in_block == 0, carry, pltpu.roll(u, 1, 0))
            prev_ref[slot, 0:1, :] = u[tb - 1:tb, :]
            return u + (prev - u) * mu

        def s1_lora():
            v1["wd"] = shifted(wd_ref, 4, ps_ref[0:1, :])
            v1["ad"] = shifted(ad_ref, 5, ps_ref[1:2, :])
            v1["w_lin"] = _dot(jnp.tanh(v1["wd"]), w2_ref[...])
            v1["a_lin"] = _dot(v1["ad"], a2_ref[...])

        def s1_key_norm():
            v1["kr"] = shifted(k_ref, 1, prm(P_MU_K))
            kk = v1["kr"] * prm(P_KK)
            v1["kk_raw"] = kk
            v1["kk_ss"] = head_sum(kk * kk)

        def s1_decay():
            w = -_softplus(-(prm(P_W0) + v1["w_lin"])) - 0.5
            v1["log_decay"] = -jnp.exp(w)
            ti = lax.broadcasted_iota(jnp.int32, (2 * C, 2 * C), 0)
            tj = lax.broadcasted_iota(jnp.int32, (2 * C, 2 * C), 1)
            pair_tri = jnp.where(jnp.logical_and(tj <= ti, ti // C == tj // C), 1.0, 0.0).astype(bf16)
            v1["ci"] = jnp.concatenate(
                [_dot_exact_ones(v1["log_decay"][k:k + 2 * C], pair_tri, nt=True)
                 for k in range(0, tb, 2 * C)], axis=0)

        def s1_bonus():
            v1["rate"] = _sigmoid(prm(P_A0) + v1["a_lin"])
            v1["r"] = shifted(r_ref, 0, prm(P_MU_R))
            v1["vr"] = shifted(v_ref, 2, prm(P_MU_V))
            v1["kp"] = v1["kr"] * (1.0 + (v1["rate"] - 1.0) * prm(P_KA))
            aux1_ref[cur, 0] = head_sum(v1["r"] * v1["kp"] * prm(P_RK)) * v1["vr"]
            zb = shifted(z_ref, 3, prm(P_MU_Z))
            aux1_ref[cur, 1] = zb * _sigmoid(zb)

        def s1_scale():
            kk = v1["kk_raw"] / jnp.maximum(jnp.sqrt(v1["kk_ss"]), L2_EPS)
            v1["bb"] = kk * v1["rate"]
            ci = v1["ci"]
            v1["a_t"] = -kk * jnp.exp(ci - v1["log_decay"])
            v1["r_t"] = v1["r"] * jnp.exp(ci)
            inv = jnp.exp(-ci)
            v1["b_t"] = v1["bb"] * inv
            v1["k_t"] = v1["kp"] * inv

        def s1_stack(c):
            def emit():
                sl = slice(c * C, (c + 1) * C)
                ci_c = v1["ci"][sl]
                c_last = ci_c[C - 1:C, :]
                to_end = jnp.exp(c_last - ci_c)
                dec1_ref[cur, c] = jnp.broadcast_to(jnp.exp(c_last), (SUBLANES, LANES))
                xr = stack(v1["r_t"][sl])
                xr32_ref[cur, c] = xr
                stk_ref[cur, K_XR, c] = xr.astype(bf16)
                stk_ref[cur, K_XA, c] = stack16(v1["a_t"][sl])
                stk_ref[cur, K_YB, c] = stack16(v1["b_t"][sl])
                stk_ref[cur, K_YK, c] = stack16(v1["k_t"][sl])
                stk_ref[cur, K_VS, c] = stack16(v1["vr"][sl])
                stk_ref[cur, K_BH, c] = stack16(v1["bb"][sl] * to_end)
                stk_ref[cur, K_KH, c] = stack16(v1["kp"][sl] * to_end)
            return emit

        segs1 = ([(0.0, s1_lora), (0.02, s1_key_norm), (0.05, s1_decay), (0.08, s1_bonus)]
                 + _spread([s1_scale] + [s1_stack(c) for c in chunks], RWKV_TAIL_START, 1.0))

        v2 = {}
        ld = lambda kind, c: stk_ref[prv, kind, c]

        def s2_big():
            big = [mm_nt(jnp.concatenate([ld(K_XA, c), ld(K_XR, c)], axis=0),
                         jnp.concatenate([ld(K_YB, c), ld(K_YK, c)], axis=0)) for c in chunks]
            v2["a_ab"] = [jnp.where(strict, big[c][0:G, 0:G], 0.0) for c in chunks]
            v2["a_ak"] = [jnp.where(strict, big[c][0:G, G:2 * G], 0.0).astype(bf16) for c in chunks]
            v2["a_rb"] = [jnp.where(incl, big[c][G:2 * G, 0:G], 0.0).astype(bf16) for c in chunks]
            v2["a_rk"] = [jnp.where(incl, big[c][G:2 * G, G:2 * G], 0.0).astype(bf16) for c in chunks]
            v2["tinv"] = [eye + a for a in v2["a_ab"]]
            v2["pw"] = [a.astype(bf16) for a in v2["a_ab"]]

        def s2_first_square():
            v2["pw"] = [mm(p, p).astype(bf16) for p in v2["pw"]]

        def s2_level(last):
            def emit():
                for c in chunks:
                    pw = v2["pw"][c]
                    t16 = v2["tinv"][c].astype(bf16)
                    if last:
                        v2["tinv"][c] = v2["tinv"][c] + mm(t16, pw)
                    else:
                        both = mm(jnp.concatenate([t16, pw], axis=0), pw)
                        v2["tinv"][c] = v2["tinv"][c] + both[0:G]
                        v2["pw"][c] = both[G:2 * G].astype(bf16)
            return emit

        def s2_av():
            v2["tinv"] = [t.astype(bf16) for t in v2["tinv"]]
            av = [mm(jnp.concatenate([v2["a_ak"][c], v2["a_rk"][c]], axis=0), ld(K_VS, c)) for c in chunks]
            v2["akv"] = [x[0:G].astype(bf16) for x in av]
            v2["arkv"] = [x[G:2 * G] for x in av]

        def s2_wu():
            v2["wu"] = [mm(v2["tinv"][c], jnp.concatenate([ld(K_XA, c), v2["akv"][c]], axis=1)).astype(bf16)
                        for c in chunks]

        def s2_ry():
            for c in chunks:
                e = mm(v2["a_rb"][c], v2["wu"][c])
                p16_ref[cur, 0, c] = (xr32_ref[prv, c] + e[:, 0:LANES]).astype(bf16)
                p32_ref[cur, 0, c] = e[:, LANES:] + v2["arkv"][c]

        def s2_pm():
            for c in chunks:
                p16_ref[cur, 1, c] = mm_tn(v2["wu"][c][:, 0:LANES], ld(K_BH, c)).astype(bf16)

        def s2_q():
            for c in chunks:
                p32_ref[cur, 1, c] = mm_tn(jnp.concatenate([v2["wu"][c][:, LANES:], ld(K_VS, c)], axis=0),
                                           jnp.concatenate([ld(K_BH, c), ld(K_KH, c)], axis=0))
            aux2_ref[cur] = aux1_ref[prv]
            dec2_ref[cur] = dec1_ref[prv]

        segs2 = _spread([s2_big, s2_first_square] + [s2_level(False)] * 4 + [s2_level(True)]
                        + [s2_av, s2_wu, s2_ry, s2_pm, s2_q])

        v3 = {"ys": []}
        prm3 = lambda idx: pc3_ref[idx:idx + 1, :]

        def s3_start():
            v3["state"] = jnp.where(first_of_seq_3, 0.0, state_ref[...])

        def s3_chunk(c):
            def emit():
                state = v3["state"]
                s16 = state.astype(bf16)
                y_st = mm_nt(p16_ref[prv, 0, c], s16) + p32_ref[prv, 0, c]
                v3["ys"].append(y_st[0:C] + y_st[C:2 * C])
                v3["state"] = (state * dec2_ref[prv, c, 0:1, :] + mm(s16, p16_ref[prv, 1, c])
                               + p32_ref[prv, 1, c])
            return emit

        def s3_finish():
            state_ref[...] = v3["state"]
            y = jnp.concatenate(v3["ys"], axis=0)
            mean = head_sum(y) * (1.0 / N)
            yc = y - mean
            var = head_sum(yc * yc) * (1.0 / N)
            yn = yc * lax.rsqrt(var + GN_EPS) * prm3(P_LNW) + prm3(P_LNB)
            o_ref[...] = ((yn + aux2_ref[prv, 0]) * aux2_ref[prv, 1]).astype(o_ref.dtype)

        segs3 = _spread([s3_start] + [s3_chunk(c) for c in chunks] + [s3_finish], 0.0, RWKV_CHAIN_END)

        _interleave(segs3, segs2, segs1)

    for parity in (0, 1):
        @pl.when(s % 2 == parity)
        def _(parity=parity):
            body(cur=parity, prv=1 - parity)


def _rwkv_mix(wide, small, pc, ps, w2p, a2p, batch, seq):
    tb = RWKV_BLOCK
    nt = seq // tb
    groups = RWKV_WIDTH // LANES
    nc = tb // RWKV_CHUNK
    n_blocks = batch * groups * nt

    def where(s):
        s = jnp.clip(s, 0, n_blocks - 1)
        return (s // (nt * groups)) * nt + s % nt, (s // nt) % groups

    col = lambda c0: (lambda s: (where(s)[0], c0 + where(s)[1]))
    fixed_col = lambda c: (lambda s: (where(s)[0], c))
    grp = lambda s: (0, where(s)[1])
    return pl.pallas_call(
        functools.partial(_rwkv_kernel, tb=tb, nt=nt),
        grid=(n_blocks + 2,),
        in_specs=[
            pl.BlockSpec((tb, LANES), col(COL_R)),
            pl.BlockSpec((tb, LANES), col(COL_K)),
            pl.BlockSpec((tb, LANES), col(COL_V)),
            pl.BlockSpec((tb, LANES), col(COL_ZB)),
            pl.BlockSpec((tb, LANES), fixed_col(0)),
            pl.BlockSpec((tb, LANES), fixed_col(1)),
            pl.BlockSpec((P_ROWS, LANES), grp),
            pl.BlockSpec((SUBLANES, LANES), lambda s: (0, 0)),
            pl.BlockSpec((LANES, LANES), grp),
            pl.BlockSpec((LANES, LANES), grp),
            pl.BlockSpec((P_ROWS, LANES), lambda s: grp(s - 2)),
        ],
        out_specs=pl.BlockSpec((tb, LANES), lambda s: (where(s - 2)[0], where(s - 2)[1])),
        out_shape=jax.ShapeDtypeStruct((batch * seq, RWKV_WIDTH), bf16),
        scratch_shapes=[
            pltpu.VMEM((LANES, LANES), f32),
            pltpu.VMEM((6, SUBLANES, LANES), f32),
            pltpu.VMEM((2, 7, nc, GROUP_ROWS, LANES), bf16),
            pltpu.VMEM((2, nc, GROUP_ROWS, LANES), f32),
            pltpu.VMEM((2, 2, tb, LANES), f32),
            pltpu.VMEM((2, nc, SUBLANES, LANES), f32),
            pltpu.VMEM((2, 2, nc, GROUP_ROWS, LANES), bf16),
            pltpu.VMEM((2, 2, nc, GROUP_ROWS, LANES), f32),
            pltpu.VMEM((2, 2, tb, LANES), f32),
            pltpu.VMEM((2, nc, SUBLANES, LANES), f32),
        ],
        compiler_params=pltpu.CompilerParams(
            dimension_semantics=("arbitrary",),
            vmem_limit_bytes=VMEM_LIMIT),
        name="rwkv_mix",
    )(wide, wide, wide, wide, small, small, pc, ps, w2p, a2p, pc)


def _out_kernel(oa_ref, ob_ref, ga_ref, gb_ref, x_ref, wpf_ref, wpr_ref, wo_ref, g_ref, o_ref):
    pa = jnp.dot(oa_ref[...], wpf_ref[...], preferred_element_type=f32)
    pb = jnp.dot(ob_ref[...], wpr_ref[...], preferred_element_type=f32)
    m = _sigmoid(ga_ref[...]) * pa + _sigmoid(gb_ref[...]) * pb
    z = x_ref[...] + jnp.dot(m.astype(bf16), wo_ref[...], preferred_element_type=f32)
    ms = jnp.mean(z * z, axis=-1, keepdims=True)
    o_ref[...] = z * lax.rsqrt(ms + RMS_EPS) * g_ref[...]


def _merge_out(oa, ob, wide, x2d, wpf, wpr, wo, gain):
    m, d = x2d.shape
    tm = 256
    resident = lambda shape: pl.BlockSpec(shape, lambda i: (0, 0), pipeline_mode=pl.Buffered(1))
    return pl.pallas_call(
        _out_kernel,
        grid=(m // tm,),
        in_specs=[
            pl.BlockSpec((tm, FOX_WIDTH), lambda i: (i, 0)),
            pl.BlockSpec((tm, RWKV_WIDTH), lambda i: (i, 0)),
            pl.BlockSpec((tm, d), lambda i: (i, COL_GA * LANES // d)),
            pl.BlockSpec((tm, d), lambda i: (i, COL_GB * LANES // d)),
            pl.BlockSpec((tm, d), lambda i: (i, 0)),
            resident((FOX_WIDTH, d)),
            resident((RWKV_WIDTH, d)),
            resident((d, d)),
            resident((1, d)),
        ],
        out_specs=pl.BlockSpec((tm, d), lambda i: (i, 0)),
        out_shape=jax.ShapeDtypeStruct((m, d), f32),
        compiler_params=pltpu.CompilerParams(
            dimension_semantics=("parallel",),
            vmem_limit_bytes=VMEM_LIMIT),
        name="merge_out",
    )(oa, ob, wide, wide, x2d, wpf, wpr, wo, gain)


def _branches(x2d, batch, seq, norm_gain, w_in, fox_forget_bias, rwkv_shift_mix, rwkv_w0, rwkv_w2,
              rwkv_a0, rwkv_a2, rwkv_k_k, rwkv_k_a, rwkv_r_k, rwkv_ln_w, rwkv_ln_b):
    fw, rw = FOX_WIDTH, RWKV_WIDTH
    w_t = w_in.T
    r_qkv = 0
    r_za = r_qkv + 3 * fw
    r_f = r_za + fw
    r_rkvz = r_f + FOX_HEADS
    r_wd = r_rkvz + 4 * rw
    r_ad = r_wd + LORA
    r_g = r_ad + LORA
    pieces = [(r_qkv, 3 * fw), (r_g, 2 * D_MODEL), (r_za, fw), (r_rkvz, 4 * rw)]
    pad_rows = lambda a: jnp.pad(a, ((0, LANES - a.shape[0]), (0, 0)))
    w_tail = jnp.concatenate(
        [pad_rows(w_t[r_wd:r_wd + LORA]),
         pad_rows(jnp.concatenate([w_t[r_ad:r_ad + LORA], w_t[r_f:r_f + FOX_HEADS]], axis=0))], axis=0)
    h2d = _rmsnorm_bf16(x2d, norm_gain.reshape(1, D_MODEL))
    qkv, wide, small = _in_proj(h2d, w_t, pieces, fw, FOX_HEAD_DIM ** -0.5 * LOG2E, w_tail)

    bias_row = jnp.pad(fox_forget_bias.reshape(1, FOX_HEADS), ((0, 0), (F_LANE, LANES - F_LANE - FOX_HEADS)))
    c = _gate_cumsum(small, bias_row, batch, seq)
    qkv3 = qkv.reshape(batch, seq, 3 * fw)
    oa = _fox_attention(qkv3, c.reshape(batch, seq, LANES), wide.reshape(batch, seq, WIDE_COLS),
                        batch, seq).reshape(batch * seq, fw)

    mu = rwkv_shift_mix
    pc = jnp.zeros((P_ROWS, rw), f32)
    rows = [mu[0:rw], mu[rw:2 * rw], mu[2 * rw:3 * rw], mu[3 * rw:4 * rw], rwkv_w0, rwkv_a0,
            rwkv_k_k, rwkv_k_a, rwkv_r_k.reshape(rw), rwkv_ln_w, rwkv_ln_b]
    pc = pc.at[:len(rows)].set(jnp.stack(rows))
    ps = jnp.zeros((SUBLANES, LANES), f32)
    ps = ps.at[0, :LORA].set(mu[4 * rw:4 * rw + LORA]).at[1, :LORA].set(mu[4 * rw + LORA:])
    w2p = jnp.pad(rwkv_w2, ((0, LANES - LORA), (0, 0))).astype(bf16)
    a2p = jnp.pad(rwkv_a2, ((0, LANES - LORA), (0, 0))).astype(bf16)
    ob = _rwkv_mix(wide, small, pc, ps, w2p, a2p, batch, seq)

    return oa, ob, wide


def kernel(x, norm_gain, w_in, fox_forget_bias, rwkv_shift_mix, rwkv_w0, rwkv_w2, rwkv_a0, rwkv_a2, rwkv_k_k, rwkv_k_a, rwkv_r_k, rwkv_ln_w, rwkv_ln_b, w_proj_fox, w_proj_rwkv, w_out, final_norm_gain):
    batch, seq, d = x.shape
    depth = norm_gain.shape[0]
    assert depth == 1, "the final rmsnorm is fused into the single layer's output kernel"
    x2d = x.reshape(batch * seq, d)
    oa, ob, wide = _branches(x2d, batch, seq, norm_gain[0], w_in[0], fox_forget_bias[0],
                             rwkv_shift_mix[0], rwkv_w0[0], rwkv_w2[0], rwkv_a0[0], rwkv_a2[0],
                             rwkv_k_k[0], rwkv_k_a[0], rwkv_r_k[0], rwkv_ln_w[0], rwkv_ln_b[0])
    out = _merge_out(oa, ob, wide, x2d, w_proj_fox[0].astype(bf16), w_proj_rwkv[0].astype(bf16),
                     w_out[0].astype(bf16), final_norm_gain.reshape(1, d))
    return out.reshape(batch, seq, d)
```

```python
import functools

import jax
import jax.numpy as jnp
from jax import lax
from jax.experimental import pallas as pl
from jax.experimental.pallas import tpu as pltpu

D_MODEL = 2048
FOX_HEADS = 8
FOX_HEAD_DIM = 128
FOX_WIDTH = FOX_HEADS * FOX_HEAD_DIM
RWKV_HEADS = 16
RWKV_HEAD_DIM = 64
RWKV_WIDTH = RWKV_HEADS * RWKV_HEAD_DIM
LORA = 96
RMS_EPS = 1e-6
GN_EPS = 64e-5
L2_EPS = 1e-12

LANES = 128
SUBLANES = 8
VMEM_LIMIT = 56 * 1024 * 1024

COL_GA = 0
COL_GB = D_MODEL // LANES
COL_ZA = 2 * D_MODEL // LANES
COL_R = COL_ZA + FOX_WIDTH // LANES
COL_K = COL_R + RWKV_WIDTH // LANES
COL_V = COL_K + RWKV_WIDTH // LANES
COL_ZB = COL_V + RWKV_WIDTH // LANES
WIDE_COLS = (COL_ZB + RWKV_WIDTH // LANES) * LANES
SMALL_COLS = 2 * LANES
F_LANE = LORA

RWKV_CHUNK = 64
RWKV_BLOCK = 512
RWKV_TAIL_START = 0.3
RWKV_CHAIN_END = 0.8
HEADS_PER_GROUP = LANES // RWKV_HEAD_DIM
GROUP_ROWS = HEADS_PER_GROUP * RWKV_CHUNK

f32 = jnp.float32
bf16 = jnp.bfloat16


def _dot(a, b):
    return jnp.dot(a.astype(bf16), b.astype(bf16), preferred_element_type=f32)


def _dot_exact_ones(a, ones_bf16, nt=False, terms=3):
    if nt:
        mm = lambda x: lax.dot_general(ones_bf16, x, (((1,), (0,)), ((), ())),
                                       preferred_element_type=f32)
    else:
        mm = lambda x: jnp.dot(x, ones_bf16, preferred_element_type=f32)
    part = a.astype(bf16)
    out = mm(part)
    rest = a
    for _ in range(terms - 1):
        rest = rest - part.astype(f32)
        part = rest.astype(bf16)
        out = out + mm(part)
    return out


def _softplus(x):
    return jnp.maximum(x, 0.0) + jnp.log(1.0 + jnp.exp(-jnp.abs(x)))


def _sigmoid(x):
    return 1.0 / (1.0 + jnp.exp(-x))


PROJ_TM = 1024
PROJ_TN = 1024


def _rmsnorm_kernel(x_ref, g_ref, h_ref):
    xf = x_ref[...]
    ms = jnp.mean(xf * xf, axis=-1, keepdims=True)
    h_ref[...] = (xf * lax.rsqrt(ms + RMS_EPS) * g_ref[...]).astype(h_ref.dtype)


def _rmsnorm_bf16(x2d, gain):
    m, d = x2d.shape
    tm = 512
    return pl.pallas_call(
        _rmsnorm_kernel,
        grid=(m // tm,),
        in_specs=[pl.BlockSpec((tm, d), lambda i: (i, 0)), pl.BlockSpec((1, d), lambda i: (0, 0))],
        out_specs=pl.BlockSpec((tm, d), lambda i: (i, 0)),
        out_shape=jax.ShapeDtypeStruct((m, d), bf16),
        compiler_params=pltpu.CompilerParams(dimension_semantics=("parallel",), vmem_limit_bytes=VMEM_LIMIT),
        name="rmsnorm",
    )(x2d, gain)


def _in_proj_kernel(h_ref, w_ref, wt_ref, qkv_ref, wide_ref, small_ref, w16_ref, *, n_qkv, n_wide, scaled_tiles, scale):
    j = pl.program_id(0)
    i = pl.program_id(1)
    n_main = n_qkv + n_wide

    @pl.when(i == 0)
    def _():
        @pl.when(j < n_main)
        def _():
            factor = jnp.where(j < scaled_tiles, scale, 1.0)
            w16_ref[...] = (w_ref[...] * factor).astype(bf16)

        @pl.when(j == n_main)
        def _():
            w16_ref[0:SMALL_COLS, :] = wt_ref[...].astype(bf16)

    project = lambda w: lax.dot_general(h_ref[...], w, (((1,), (1,)), ((), ())), preferred_element_type=f32)

    @pl.when(j < n_qkv)
    def _():
        qkv_ref[...] = project(w16_ref[...]).astype(qkv_ref.dtype)

    @pl.when(jnp.logical_and(j >= n_qkv, j < n_main))
    def _():
        wide_ref[...] = project(w16_ref[...])

    @pl.when(j == n_main)
    def _():
        small_ref[...] = project(w16_ref[0:SMALL_COLS, :])


def _in_proj(h2d, w_t, pieces, scaled_rows, scale, w_tail):
    m, d = h2d.shape
    tm, tn = PROJ_TM, PROJ_TN
    n_qkv = 3 * FOX_WIDTH // tn
    n_wide = WIDE_COLS // tn
    n_main = n_qkv + n_wide
    n_i = m // tm
    assert n_qkv * tn == 3 * FOX_WIDTH and n_wide * tn == WIDE_COLS and w_tail.shape[0] == SMALL_COLS
    assert scaled_rows % tn == 0 and all(n % tn == 0 and src % SUBLANES == 0 for src, n in pieces)
    assert sum(n for _, n in pieces) == n_main * tn
    starts, first_tile = [], 0
    for src, n in pieces:
        starts.append((first_tile, src))
        first_tile += n // tn

    def src_row(j):
        tile = jnp.int32(0)
        for blk, src in starts:
            tile = jnp.where(j >= blk, src // SUBLANES + (jnp.minimum(j, n_main - 1) - blk) * (tn // SUBLANES), tile)
        return pl.multiple_of(tile * SUBLANES, SUBLANES)

    qkv_row = lambda j, i: jnp.where(j < n_qkv, i, n_i - 1)
    wide_row = lambda j, i: jnp.where(j < n_qkv, 0, jnp.where(j < n_main, i, n_i - 1))
    small_row = lambda j, i: jnp.where(j == n_main, i, 0)
    return pl.pallas_call(
        functools.partial(_in_proj_kernel, n_qkv=n_qkv, n_wide=n_wide, scaled_tiles=scaled_rows // tn,
                          scale=scale),
        grid=(n_main + 1, n_i),
        in_specs=[
            pl.BlockSpec((tm, d), lambda j, i: (i, 0)),
            pl.BlockSpec((pl.Element(tn), pl.Element(d)), lambda j, i: (src_row(j), 0)),
            pl.BlockSpec((SMALL_COLS, d), lambda j, i: (0, 0)),
        ],
        out_specs=[
            pl.BlockSpec((tm, tn), lambda j, i: (qkv_row(j, i), jnp.minimum(j, n_qkv - 1))),
            pl.BlockSpec((tm, tn), lambda j, i: (wide_row(j, i), jnp.clip(j - n_qkv, 0, n_wide - 1))),
            pl.BlockSpec((tm, SMALL_COLS), lambda j, i: (small_row(j, i), 0)),
        ],
        out_shape=[
            jax.ShapeDtypeStruct((m, 3 * FOX_WIDTH), bf16),
            jax.ShapeDtypeStruct((m, WIDE_COLS), f32),
            jax.ShapeDtypeStruct((m, SMALL_COLS), f32),
        ],
        scratch_shapes=[pltpu.VMEM((tn, d), bf16)],
        compiler_params=pltpu.CompilerParams(
            dimension_semantics=("arbitrary", "arbitrary"),
            vmem_limit_bytes=VMEM_LIMIT),
        name="in_proj",
    )(h2d, w_t, w_tail)


def _gate_kernel(f_ref, bias_ref, c_ref, carry_ref, *, tb):
    @pl.when(pl.program_id(1) == 0)
    def _():
        carry_ref[...] = jnp.zeros_like(carry_ref)

    z = f_ref[...] + bias_ref[...]
    log_f = -_softplus(-z)
    row = lax.broadcasted_iota(jnp.int32, (tb, tb), 0)
    col = lax.broadcasted_iota(jnp.int32, (tb, tb), 1)
    tri = jnp.where(col <= row, 1.0, 0.0).astype(bf16)
    c = _dot_exact_ones(log_f, tri, nt=True) + carry_ref[0:1, :]
    c_ref[...] = c
    carry_ref[0:1, :] = c[tb - 1:tb, :]


def _gate_cumsum(small, bias_row, batch, seq):
    tb = 512
    nt = seq // tb
    return pl.pallas_call(
        functools.partial(_gate_kernel, tb=tb),
        grid=(batch, nt),
        in_specs=[
            pl.BlockSpec((tb, LANES), lambda b, t: (b * nt + t, 1)),
            pl.BlockSpec((1, LANES), lambda b, t: (0, 0)),
        ],
        out_specs=pl.BlockSpec((tb, LANES), lambda b, t: (b * nt + t, 0)),
        out_shape=jax.ShapeDtypeStruct((batch * seq, LANES), f32),
        scratch_shapes=[pltpu.VMEM((SUBLANES, LANES), f32)],
        compiler_params=pltpu.CompilerParams(
            dimension_semantics=("parallel", "arbitrary")),
        name="fox_gate_cumsum",
    )(small, bias_row)


LOG2E = 1.4426950408889634
FOX_TK = 256
FOX_GW = 256
FOX_BATCH = 4


def _fox_kernel(q_ref, k_ref, v_ref, c_ref, z_ref, o_ref, crep_ref, acc_ref, vt_ref, *, seq, tk, gw):
    h = pl.program_id(1)
    ng = seq // gw
    nt = seq // tk
    assert seq % gw == 0 and seq % tk == 0 and (tk % gw == 0 or gw % tk == 0)
    mm = lambda a, b: jnp.dot(a, b, preferred_element_type=f32)
    mm_nt = lambda a, b: lax.dot_general(a, b, (((1,), (1,)), ((), ())), preferred_element_type=f32)

    lane = lax.broadcasted_iota(jnp.int32, (1, LANES), 1)
    onehot = jnp.where(lane == F_LANE + h, LOG2E, 0.0)
    crep_ref[...] = jnp.sum(c_ref[...] * onehot, axis=1, keepdims=True)
    for t in range(nt):
        vt_ref[t, 0:LANES, :] = v_ref[t * tk:(t + 1) * tk, :].T
        vt_ref[t, LANES:LANES + SUBLANES, :] = jnp.ones((SUBLANES, tk), bf16)

    key_rel = lax.broadcasted_iota(jnp.int32, (tk, gw), 0)
    qry_rel = lax.broadcasted_iota(jnp.int32, (tk, gw), 1)

    def visible(t, g):
        return t * tk <= g * gw + gw - 1

    def needs_mask(t, g):
        return t * tk + tk - 1 > g * gw

    last_tile = [max(t for t in range(nt) if visible(t, g)) for g in range(ng)]
    sched = []
    for t in range(nt):
        groups = [g for g in range(ng) if visible(t, g)]
        sched += [[(t, g) for g in groups[k:k + FOX_BATCH]] for k in range(0, len(groups), FOX_BATCH)]

    def qk(batch):
        return [mm_nt(k_ref[t * tk:(t + 1) * tk, :], q_ref[g * gw:(g + 1) * gw, :]) for t, g in batch]

    m = [None] * ng
    l = [None] * ng

    def finish(batch, alpha, pv):
        for (t, g), a, xs in zip(batch, alpha, pv):
            x, p_sum = xs[0:LANES], xs[LANES:LANES + 1]
            acc_ref[g] = x if t == 0 else a * acc_ref[g] + x
            l[g] = p_sum if t == 0 else a * l[g] + p_sum
            if t == last_tile[g]:
                rows = slice(g * gw, (g + 1) * gw)
                z = z_ref[rows, :]
                o = (acc_ref[g] / l[g]).T
                o_ref[rows, :] = (o * (z * _sigmoid(z))).astype(o_ref.dtype)

    s_next = qk(sched[0])
    pending = None
    for bi, batch in enumerate(sched):
        s_cur = s_next
        if bi + 1 < len(sched):
            s_next = qk(sched[bi + 1])
        alpha, p16 = [], []
        for (t, g), s in zip(batch, s_cur):
            s = s - crep_ref[t * tk:(t + 1) * tk, :]
            if needs_mask(t, g):
                s = jnp.where(key_rel + t * tk > qry_rel + g * gw, -jnp.inf, s)
            m_tile = jnp.max(s, axis=0, keepdims=True)
            if t == 0:
                m_new, a = m_tile, None
            else:
                m_new = jnp.maximum(m[g], m_tile)
                a = jnp.exp2(m[g] - m_new)
            m[g] = m_new
            alpha.append(a)
            p16.append(jnp.exp2((s - m_new).astype(bf16)))
        pv = [mm(vt_ref[t], p) for (t, g), p in zip(batch, p16)]
        if pending is not None:
            finish(*pending)
        pending = (batch, alpha, pv)
    finish(*pending)


def _fox_attention(qkv3, c3, wide3, batch, seq):
    tk, gw = FOX_TK, FOX_GW
    h8 = FOX_HEADS
    return pl.pallas_call(
        functools.partial(_fox_kernel, seq=seq, tk=tk, gw=gw),
        grid=(batch, h8),
        in_specs=[
            pl.BlockSpec((None, seq, LANES), lambda b, h: (b, 0, h)),
            pl.BlockSpec((None, seq, LANES), lambda b, h: (b, 0, h8 + h)),
            pl.BlockSpec((None, seq, LANES), lambda b, h: (b, 0, 2 * h8 + h)),
            pl.BlockSpec((None, seq, LANES), lambda b, h: (b, 0, 0)),
            pl.BlockSpec((None, seq, LANES), lambda b, h: (b, 0, COL_ZA + h)),
        ],
        out_specs=pl.BlockSpec((None, seq, LANES), lambda b, h: (b, 0, h)),
        out_shape=jax.ShapeDtypeStruct((batch, seq, FOX_WIDTH), bf16),
        scratch_shapes=[
            pltpu.VMEM((seq, 1), f32),
            pltpu.VMEM((seq // gw, LANES, gw), f32),
            pltpu.VMEM((seq // tk, LANES + SUBLANES, tk), bf16),
        ],
        compiler_params=pltpu.CompilerParams(
            dimension_semantics=("parallel", "arbitrary"),
            vmem_limit_bytes=VMEM_LIMIT),
        name="fox_attention",
    )(qkv3, qkv3, qkv3, c3, wide3)


P_MU_R, P_MU_K, P_MU_V, P_MU_Z, P_W0, P_A0, P_KK, P_KA, P_RK, P_LNW, P_LNB = range(11)
P_ROWS = 16
K_XA, K_XR, K_YB, K_YK, K_VS, K_BH, K_KH = range(7)


def _spread(emitters, lo=0.0, hi=1.0):
    n = len(emitters)
    return [(lo + (hi - lo) * (i + 0.5) / n, e) for i, e in enumerate(emitters)]


def _interleave(*segment_lists):
    keyed = [(pos, prio, seg) for prio, segs in enumerate(segment_lists) for pos, seg in segs]
    keyed.sort(key=lambda x: (x[0], x[1]))
    for _, _, seg in keyed:
        seg()


def _rwkv_kernel(r_ref, k_ref, v_ref, z_ref, wd_ref, ad_ref, pc_ref, ps_ref, w2_ref, a2_ref, pc3_ref,
                 o_ref,
                 state_ref, prev_ref, stk_ref, xr32_ref, aux1_ref, dec1_ref, p16_ref, p32_ref,
                 aux2_ref, dec2_ref, *, tb, nt):
    C = RWKV_CHUNK
    G = GROUP_ROWS
    N = RWKV_HEAD_DIM
    NC = tb // C
    s = pl.program_id(0)
    first_of_seq_1 = (s % nt) == 0
    first_of_seq_3 = ((s + 2 * nt - 2) % nt) == 0

    @pl.when(s == 0)
    def _():
        for ref in (state_ref, prev_ref, stk_ref, xr32_ref, aux1_ref, dec1_ref, p16_ref, p32_ref,
                    aux2_ref, dec2_ref):
            ref[...] = jnp.zeros_like(ref)

    mm = lambda a, b: jnp.dot(a, b, preferred_element_type=f32)
    mm_nt = lambda a, b: lax.dot_general(a, b, (((1,), (1,)), ((), ())), preferred_element_type=f32)
    mm_tn = lambda a, b: lax.dot_general(a, b, (((0,), (0,)), ((), ())), preferred_element_type=f32)

    def body(cur, prv):
        li = lax.broadcasted_iota(jnp.int32, (LANES, LANES), 0)
        lj = lax.broadcasted_iota(jnp.int32, (LANES, LANES), 1)
        head_ones = jnp.where(li // N == lj // N, 1.0, 0.0).astype(bf16)
        head_sum = lambda x: _dot_exact_ones(x, head_ones, terms=2)
        lane = lax.broadcasted_iota(jnp.int32, (1, LANES), 1)
        head_masks = [jnp.where(lane // N == h, 1.0, 0.0) for h in range(HEADS_PER_GROUP)]
        stack = lambda x: jnp.concatenate([x * hm for hm in head_masks], axis=0)
        lane_c = lax.broadcasted_iota(jnp.int32, (C, LANES), 1)
        head_sel = [lane_c // N == h for h in range(HEADS_PER_GROUP)]

        def stack16(x):
            xb = x.astype(bf16)
            return jnp.concatenate([jnp.where(m, xb, jnp.zeros_like(xb)) for m in head_sel], axis=0)

        gi = lax.broadcasted_iota(jnp.int32, (G, G), 0)
        gj = lax.broadcasted_iota(jnp.int32, (G, G), 1)
        same_head = gi // C == gj // C
        strict = jnp.logical_and(same_head, gj < gi)
        incl = jnp.logical_and(same_head, gj <= gi)
        eye = jnp.where(gi == gj, 1.0, 0.0)
        chunks = range(NC)

        v1 = {}
        prm = lambda idx: pc_ref[idx:idx + 1, :]
        row_in_block = lax.broadcasted_iota(jnp.int32, (tb, LANES), 0)

        def shifted(ref, slot, mu):
            u = ref[...]
            carry = jnp.where(first_of_seq_1, 0.0, prev_ref[slot, 0:1, :])
            prev = jnp.where(row_in_block == 0, carry, pltpu.roll(u, 1, 0))
            prev_ref[slot, 0:1, :] = u[tb - 1:tb, :]
            return u + (prev - u) * mu

        def s1_lora():
            v1["wd"] = shifted(wd_ref, 4, ps_ref[0:1, :])
            v1["ad"] = shifted(ad_ref, 5, ps_ref[1:2, :])
            v1["w_lin"] = _dot(jnp.tanh(v1["wd"]), w2_ref[...])
            v1["a_lin"] = _dot(v1["ad"], a2_ref[...])

        def s1_key_norm():
            v1["kr"] = shifted(k_ref, 1, prm(P_MU_K))
            kk = v1["kr"] * prm(P_KK)
            v1["kk_raw"] = kk
            v1["kk_ss"] = head_sum(kk * kk)

        def s1_decay():
            w = -_softplus(-(prm(P_W0) + v1["w_lin"])) - 0.5
            v1["log_decay"] = -jnp.exp(w)
            ti = lax.broadcasted_iota(jnp.int32, (2 * C, 2 * C), 0)
            tj = lax.broadcasted_iota(jnp.int32, (2 * C, 2 * C), 1)
            pair_tri = jnp.where(jnp.logical_and(tj <= ti, ti // C == tj // C), 1.0, 0.0).astype(bf16)
            v1["ci"] = jnp.concatenate(
                [_dot_exact_ones(v1["log_decay"][k:k + 2 * C], pair_tri, nt=True)
                 for k in range(0, tb, 2 * C)], axis=0)

        def s1_bonus():
            v1["rate"] = _sigmoid(prm(P_A0) + v1["a_lin"])
            v1["r"] = shifted(r_ref, 0, prm(P_MU_R))
            v1["vr"] = shifted(v_ref, 2, prm(P_MU_V))
            v1["kp"] = v1["kr"] * (1.0 + (v1["rate"] - 1.0) * prm(P_KA))
            aux1_ref[cur, 0] = head_sum(v1["r"] * v1["kp"] * prm(P_RK)) * v1["vr"]
            zb = shifted(z_ref, 3, prm(P_MU_Z))
            aux1_ref[cur, 1] = zb * _sigmoid(zb)

        def s1_scale():
            kk = v1["kk_raw"] / jnp.maximum(jnp.sqrt(v1["kk_ss"]), L2_EPS)
            v1["bb"] = kk * v1["rate"]
            ci = v1["ci"]
            v1["a_t"] = -kk * jnp.exp(ci - v1["log_decay"])
            v1["r_t"] = v1["r"] * jnp.exp(ci)
            inv = jnp.exp(-ci)
            v1["b_t"] = v1["bb"] * inv
            v1["k_t"] = v1["kp"] * inv

        def s1_stack(c):
            def emit():
                sl = slice(c * C, (c + 1) * C)
                ci_c = v1["ci"][sl]
                c_last = ci_c[C - 1:C, :]
                to_end = jnp.exp(c_last - ci_c)
                dec1_ref[cur, c] = jnp.broadcast_to(jnp.exp(c_last), (SUBLANES, LANES))
                xr = stack(v1["r_t"][sl])
                xr32_ref[cur, c] = xr
                stk_ref[cur, K_XR, c] = xr.astype(bf16)
                stk_ref[cur, K_XA, c] = stack16(v1["a_t"][sl])
                stk_ref[cur, K_YB, c] = stack16(v1["b_t"][sl])
                stk_ref[cur, K_YK, c] = stack16(v1["k_t"][sl])
                stk_ref[cur, K_VS, c] = stack16(v1["vr"][sl])
                stk_ref[cur, K_BH, c] = stack16(v1["bb"][sl] * to_end)
                stk_ref[cur, K_KH, c] = stack16(v1["kp"][sl] * to_end)
            return emit

        segs1 = ([(0.0, s1_lora), (0.02, s1_key_norm), (0.05, s1_decay), (0.08, s1_bonus)]
                 + _spread([s1_scale] + [s1_stack(c) for c in chunks], RWKV_TAIL_START, 1.0))

        v2 = {}
        ld = lambda kind, c: stk_ref[prv, kind, c]

        def s2_big():
            big = [mm_nt(jnp.concatenate([ld(K_XA, c), ld(K_XR, c)], axis=0),
                         jnp.concatenate([ld(K_YB, c), ld(K_YK, c)], axis=0)) for c in chunks]
            v2["a_ab"] = [jnp.where(strict, big[c][0:G, 0:G], 0.0) for c in chunks]
            v2["a_ak"] = [jnp.where(strict, big[c][0:G, G:2 * G], 0.0).astype(bf16) for c in chunks]
            v2["a_rb"] = [jnp.where(incl, big[c][G:2 * G, 0:G], 0.0).astype(bf16) for c in chunks]
            v2["a_rk"] = [jnp.where(incl, big[c][G:2 * G, G:2 * G], 0.0).astype(bf16) for c in chunks]
            v2["tinv"] = [eye + a for a in v2["a_ab"]]
            v2["pw"] = [a.astype(bf16) for a in v2["a_ab"]]

        def s2_first_square():
            v2["pw"] = [mm(p, p).astype(bf16) for p in v2["pw"]]

        def s2_level(last):
            def emit():
                for c in chunks:
                    pw = v2["pw"][c]
                    t16 = v2["tinv"][c].astype(bf16)
                    if last:
                        v2["tinv"][c] = v2["tinv"][c] + mm(t16, pw)
                    else:
                        both = mm(jnp.concatenate([t16, pw], axis=0), pw)
                        v2["tinv"][c] = v2["tinv"][c] + both[0:G]
                        v2["pw"][c] = both[G:2 * G].astype(bf16)
            return emit

        def s2_av():
            v2["tinv"] = [t.astype(bf16) for t in v2["tinv"]]
            av = [mm(jnp.concatenate([v2["a_ak"][c], v2["a_rk"][c]], axis=0), ld(K_VS, c)) for c in chunks]
            v2["akv"] = [x[0:G].astype(bf16) for x in av]
            v2["arkv"] = [x[G:2 * G] for x in av]

        def s2_wu():
            v2["wu"] = [mm(v2["tinv"][c], jnp.concatenate([ld(K_XA, c), v2["akv"][c]], axis=1)).astype(bf16)
                        for c in chunks]

        def s2_ry():
            for c in chunks:
                e = mm(v2["a_rb"][c], v2["wu"][c])
                p16_ref[cur, 0, c] = (xr32_ref[prv, c] + e[:, 0:LANES]).astype(bf16)
                p32_ref[cur, 0, c] = e[:, LANES:] + v2["arkv"][c]

        def s2_pm():
            for c in chunks:
                p16_ref[cur, 1, c] = mm_tn(v2["wu"][c][:, 0:LANES], ld(K_BH, c)).astype(bf16)

        def s2_q():
            for c in chunks:
                p32_ref[cur, 1, c] = mm_tn(jnp.concatenate([v2["wu"][c][:, LANES:], ld(K_VS, c)], axis=0),
                                           jnp.concatenate([ld(K_BH, c), ld(K_KH, c)], axis=0))
            aux2_ref[cur] = aux1_ref[prv]
            dec2_ref[cur] = dec1_ref[prv]

        segs2 = _spread([s2_big, s2_first_square] + [s2_level(False)] * 4 + [s2_level(True)]
                        + [s2_av, s2_wu, s2_ry, s2_pm, s2_q])

        v3 = {"ys": []}
        prm3 = lambda idx: pc3_ref[idx:idx + 1, :]

        def s3_start():
            v3["state"] = jnp.where(first_of_seq_3, 0.0, state_ref[...])

        def s3_chunk(c):
            def emit():
                state = v3["state"]
                s16 = state.astype(bf16)
                y_st = mm_nt(p16_ref[prv, 0, c], s16) + p32_ref[prv, 0, c]
                v3["ys"].append(y_st[0:C] + y_st[C:2 * C])
                v3["state"] = (state * dec2_ref[prv, c, 0:1, :] + mm(s16, p16_ref[prv, 1, c])
                               + p32_ref[prv, 1, c])
            return emit

        def s3_finish():
            state_ref[...] = v3["state"]
            y = jnp.concatenate(v3["ys"], axis=0)
            mean = head_sum(y) * (1.0 / N)
            yc = y - mean
            var = head_sum(yc * yc) * (1.0 / N)
            yn = yc * lax.rsqrt(var + GN_EPS) * prm3(P_LNW) + prm3(P_LNB)
            o_ref[...] = ((yn + aux2_ref[prv, 0]) * aux2_ref[prv, 1]).astype(o_ref.dtype)

        segs3 = _spread([s3_start] + [s3_chunk(c) for c in chunks] + [s3_finish], 0.0, RWKV_CHAIN_END)

        _interleave(segs3, segs2, segs1)

    for parity in (0, 1):
        @pl.when(s % 2 == parity)
        def _(parity=parity):
            body(cur=parity, prv=1 - parity)


def _rwkv_mix(wide, small, pc, ps, w2p, a2p, batch, seq):
    tb = RWKV_BLOCK
    nt = seq // tb
    groups = RWKV_WIDTH // LANES
    nc = tb // RWKV_CHUNK
    n_blocks = batch * groups * nt

    def where(s):
        s = jnp.clip(s, 0, n_blocks - 1)
        return (s // (nt * groups)) * nt + s % nt, (s // nt) % groups

    col = lambda c0: (lambda s: (where(s)[0], c0 + where(s)[1]))
    fixed_col = lambda c: (lambda s: (where(s)[0], c))
    grp = lambda s: (0, where(s)[1])
    return pl.pallas_call(
        functools.partial(_rwkv_kernel, tb=tb, nt=nt),
        grid=(n_blocks + 2,),
        in_specs=[
            pl.BlockSpec((tb, LANES), col(COL_R)),
            pl.BlockSpec((tb, LANES), col(COL_K)),
            pl.BlockSpec((tb, LANES), col(COL_V)),
            pl.BlockSpec((tb, LANES), col(COL_ZB)),
            pl.BlockSpec((tb, LANES), fixed_col(0)),
            pl.BlockSpec((tb, LANES), fixed_col(1)),
            pl.BlockSpec((P_ROWS, LANES), grp),
            pl.BlockSpec((SUBLANES, LANES), lambda s: (0, 0)),
            pl.BlockSpec((LANES, LANES), grp),
            pl.BlockSpec((LANES, LANES), grp),
            pl.BlockSpec((P_ROWS, LANES), lambda s: grp(s - 2)),
        ],
        out_specs=pl.BlockSpec((tb, LANES), lambda s: (where(s - 2)[0], where(s - 2)[1])),
        out_shape=jax.ShapeDtypeStruct((batch * seq, RWKV_WIDTH), bf16),
        scratch_shapes=[
            pltpu.VMEM((LANES, LANES), f32),
            pltpu.VMEM((6, SUBLANES, LANES), f32),
            pltpu.VMEM((2, 7, nc, GROUP_ROWS, LANES), bf16),
            pltpu.VMEM((2, nc, GROUP_ROWS, LANES), f32),
            pltpu.VMEM((2, 2, tb, LANES), f32),
            pltpu.VMEM((2, nc, SUBLANES, LANES), f32),
            pltpu.VMEM((2, 2, nc, GROUP_ROWS, LANES), bf16),
            pltpu.VMEM((2, 2, nc, GROUP_ROWS, LANES), f32),
            pltpu.VMEM((2, 2, tb, LANES), f32),
            pltpu.VMEM((2, nc, SUBLANES, LANES), f32),
        ],
        compiler_params=pltpu.CompilerParams(
            dimension_semantics=("arbitrary",),
            vmem_limit_bytes=VMEM_LIMIT),
        name="rwkv_mix",
    )(wide, wide, wide, wide, small, small, pc, ps, w2p, a2p, pc)


def _out_kernel(oa_ref, ob_ref, ga_ref, gb_ref, x_ref, wpf_ref, wpr_ref, wo_ref, g_ref, o_ref):
    pa = jnp.dot(oa_ref[...], wpf_ref[...], preferred_element_type=f32)
    pb = jnp.dot(ob_ref[...], wpr_ref[...], preferred_element_type=f32)
    m = _sigmoid(ga_ref[...]) * pa + _sigmoid(gb_ref[...]) * pb
    z = x_ref[...] + jnp.dot(m.astype(bf16), wo_ref[...], preferred_element_type=f32)
    ms = jnp.mean(z * z, axis=-1, keepdims=True)
    o_ref[...] = z * lax.rsqrt(ms + RMS_EPS) * g_ref[...]


def _merge_out(oa, ob, wide, x2d, wpf, wpr, wo, gain):
    m, d = x2d.shape
    tm = 256
    resident = lambda shape: pl.BlockSpec(shape, lambda i: (0, 0), pipeline_mode=pl.Buffered(1))
    return pl.pallas_call(
        _out_kernel,
        grid=(m // tm,),
        in_specs=[
            pl.BlockSpec((tm, FOX_WIDTH), lambda i: (i, 0)),
            pl.BlockSpec((tm, RWKV_WIDTH), lambda i: (i, 0)),
            pl.BlockSpec((tm, d), lambda i: (i, COL_GA * LANES // d)),
            pl.BlockSpec((tm, d), lambda i: (i, COL_GB * LANES // d)),
            pl.BlockSpec((tm, d), lambda i: (i, 0)),
            resident((FOX_WIDTH, d)),
            resident((RWKV_WIDTH, d)),
            resident((d, d)),
            resident((1, d)),
        ],
        out_specs=pl.BlockSpec((tm, d), lambda i: (i, 0)),
        out_shape=jax.ShapeDtypeStruct((m, d), f32),
        compiler_params=pltpu.CompilerParams(
            dimension_semantics=("parallel",),
            vmem_limit_bytes=VMEM_LIMIT),
        name="merge_out",
    )(oa, ob, wide, wide, x2d, wpf, wpr, wo, gain)


def _branches(x2d, batch, seq, norm_gain, w_in, fox_forget_bias, rwkv_shift_mix, rwkv_w0, rwkv_w2,
              rwkv_a0, rwkv_a2, rwkv_k_k, rwkv_k_a, rwkv_r_k, rwkv_ln_w, rwkv_ln_b):
    fw, rw = FOX_WIDTH, RWKV_WIDTH
    w_t = w_in.T
    r_qkv = 0
    r_za = r_qkv + 3 * fw
    r_f = r_za + fw
    r_rkvz = r_f + FOX_HEADS
    r_wd = r_rkvz + 4 * rw
    r_ad = r_wd + LORA
    r_g = r_ad + LORA
    pieces = [(r_qkv, 3 * fw), (r_g, 2 * D_MODEL), (r_za, fw), (r_rkvz, 4 * rw)]
    pad_rows = lambda a: jnp.pad(a, ((0, LANES - a.shape[0]), (0, 0)))
    w_tail = jnp.concatenate(
        [pad_rows(w_t[r_wd:r_wd + LORA]),
         pad_rows(jnp.concatenate([w_t[r_ad:r_ad + LORA], w_t[r_f:r_f + FOX_HEADS]], axis=0))], axis=0)
    h2d = _rmsnorm_bf16(x2d, norm_gain.reshape(1, D_MODEL))
    qkv, wide, small = _in_proj(h2d, w_t, pieces, fw, FOX_HEAD_DIM ** -0.5 * LOG2E, w_tail)

    bias_row = jnp.pad(fox_forget_bias.reshape(1, FOX_HEADS), ((0, 0), (F_LANE, LANES - F_LANE - FOX_HEADS)))
    c = _gate_cumsum(small, bias_row, batch, seq)
    qkv3 = qkv.reshape(batch, seq, 3 * fw)
    oa = _fox_attention(qkv3, c.reshape(batch, seq, LANES), wide.reshape(batch, seq, WIDE_COLS),
                        batch, seq).reshape(batch * seq, fw)

    mu = rwkv_shift_mix
    pc = jnp.zeros((P_ROWS, rw), f32)
    rows = [mu[0:rw], mu[rw:2 * rw], mu[2 * rw:3 * rw], mu[3 * rw:4 * rw], rwkv_w0, rwkv_a0,
            rwkv_k_k, rwkv_k_a, rwkv_r_k.reshape(rw), rwkv_ln_w, rwkv_ln_b]
    pc = pc.at[:len(rows)].set(jnp.stack(rows))
    ps = jnp.zeros((SUBLANES, LANES), f32)
    ps = ps.at[0, :LORA].set(mu[4 * rw:4 * rw + LORA]).at[1, :LORA].set(mu[4 * rw + LORA:])
    w2p = jnp.pad(rwkv_w2, ((0, LANES - LORA), (0, 0))).astype(bf16)
    a2p = jnp.pad(rwkv_a2, ((0, LANES - LORA), (0, 0))).astype(bf16)
    ob = _rwkv_mix(wide, small, pc, ps, w2p, a2p, batch, seq)

    return oa, ob, wide


def kernel(x, norm_gain, w_in, fox_forget_bias, rwkv_shift_mix, rwkv_w0, rwkv_w2, rwkv_a0, rwkv_a2, rwkv_k_k, rwkv_k_a, rwkv_r_k, rwkv_ln_w, rwkv_ln_b, w_proj_fox, w_proj_rwkv, w_out, final_norm_gain):
    batch, seq, d = x.shape
    depth = norm_gain.shape[0]
    assert depth == 1, "the final rmsnorm is fused into the single layer's output kernel"
    x2d = x.reshape(batch * seq, d)
    oa, ob, wide = _branches(x2d, batch, seq, norm_gain[0], w_in[0], fox_forget_bias[0],
                             rwkv_shift_mix[0], rwkv_w0[0], rwkv_w2[0], rwkv_a0[0], rwkv_a2[0],
                             rwkv_k_k[0], rwkv_k_a[0], rwkv_r_k[0], rwkv_ln_w[0], rwkv_ln_b[0])
    out = _merge_out(oa, ob, wide, x2d, w_proj_fox[0].astype(bf16), w_proj_rwkv[0].astype(bf16),
                     w_out[0].astype(bf16), final_norm_gain.reshape(1, d))
    return out.reshape(batch, seq, d)
```

```python
import functools

import jax
import jax.numpy as jnp
from jax import lax
from jax.experimental import pallas as pl
from jax.experimental.pallas import tpu as pltpu

D_MODEL = 2048
FOX_HEADS = 8
FOX_HEAD_DIM = 128
FOX_WIDTH = FOX_HEADS * FOX_HEAD_DIM
RWKV_HEADS = 16
RWKV_HEAD_DIM = 64
RWKV_WIDTH = RWKV_HEADS * RWKV_HEAD_DIM
LORA = 96
RMS_EPS = 1e-6
GN_EPS = 64e-5
L2_EPS = 1e-12

LANES = 128
SUBLANES = 8
VMEM_LIMIT = 56 * 1024 * 1024

COL_GA = 0
COL_GB = D_MODEL // LANES
COL_ZA = 2 * D_MODEL // LANES
COL_R = COL_ZA + FOX_WIDTH // LANES
COL_K = COL_R + RWKV_WIDTH // LANES
COL_V = COL_K + RWKV_WIDTH // LANES
COL_ZB = COL_V + RWKV_WIDTH // LANES
WIDE_COLS = (COL_ZB + RWKV_WIDTH // LANES) * LANES
SMALL_COLS = 2 * LANES
F_LANE = LORA

RWKV_CHUNK = 64
RWKV_BLOCK = 512
RWKV_TAIL_START = 0.3
RWKV_CHAIN_END = 0.8
HEADS_PER_GROUP = LANES // RWKV_HEAD_DIM
GROUP_ROWS = HEADS_PER_GROUP * RWKV_CHUNK

f32 = jnp.float32
bf16 = jnp.bfloat16


def _dot(a, b):
    return jnp.dot(a.astype(bf16), b.astype(bf16), preferred_element_type=f32)


def _dot_exact_ones(a, ones_bf16, nt=False, terms=3):
    if nt:
        mm = lambda x: lax.dot_general(ones_bf16, x, (((1,), (0,)), ((), ())),
                                       preferred_element_type=f32)
    else:
        mm = lambda x: jnp.dot(x, ones_bf16, preferred_element_type=f32)
    part = a.astype(bf16)
    out = mm(part)
    rest = a
    for _ in range(terms - 1):
        rest = rest - part.astype(f32)
        part = rest.astype(bf16)
        out = out + mm(part)
    return out


def _softplus(x):
    return jnp.maximum(x, 0.0) + jnp.log(1.0 + jnp.exp(-jnp.abs(x)))


def _sigmoid(x):
    return 1.0 / (1.0 + jnp.exp(-x))


PROJ_TM = 1024
PROJ_TN = 1024


def _rmsnorm_kernel(x_ref, g_ref, h_ref):
    xf = x_ref[...]
    ms = jnp.mean(xf * xf, axis=-1, keepdims=True)
    h_ref[...] = (xf * lax.rsqrt(ms + RMS_EPS) * g_ref[...]).astype(h_ref.dtype)


def _rmsnorm_bf16(x2d, gain):
    m, d = x2d.shape
    tm = 512
    return pl.pallas_call(
        _rmsnorm_kernel,
        grid=(m // tm,),
        in_specs=[pl.BlockSpec((tm, d), lambda i: (i, 0)), pl.BlockSpec((1, d), lambda i: (0, 0))],
        out_specs=pl.BlockSpec((tm, d), lambda i: (i, 0)),
        out_shape=jax.ShapeDtypeStruct((m, d), bf16),
        compiler_params=pltpu.CompilerParams(dimension_semantics=("parallel",), vmem_limit_bytes=VMEM_LIMIT),
        name="rmsnorm",
    )(x2d, gain)


def _in_proj_kernel(h_ref, w_ref, wt_ref, qkv_ref, wide_ref, small_ref, w16_ref, *, n_qkv, n_wide, scaled_tiles, scale):
    j = pl.program_id(0)
    i = pl.program_id(1)
    n_main = n_qkv + n_wide

    @pl.when(i == 0)
    def _():
        @pl.when(j < n_main)
        def _():
            factor = jnp.where(j < scaled_tiles, scale, 1.0)
            w16_ref[...] = (w_ref[...] * factor).astype(bf16)

        @pl.when(j == n_main)
        def _():
            w16_ref[0:SMALL_COLS, :] = wt_ref[...].astype(bf16)

    project = lambda w: lax.dot_general(h_ref[...], w, (((1,), (1,)), ((), ())), preferred_element_type=f32)

    @pl.when(j < n_qkv)
    def _():
        qkv_ref[...] = project(w16_ref[...]).astype(qkv_ref.dtype)

    @pl.when(jnp.logical_and(j >= n_qkv, j < n_main))
    def _():
        wide_ref[...] = project(w16_ref[...])

    @pl.when(j == n_main)
    def _():
        small_ref[...] = project(w16_ref[0:SMALL_COLS, :])


def _in_proj(h2d, w_t, pieces, scaled_rows, scale, w_tail):
    m, d = h2d.shape
    tm, tn = PROJ_TM, PROJ_TN
    n_qkv = 3 * FOX_WIDTH // tn
    n_wide = WIDE_COLS // tn
    n_main = n_qkv + n_wide
    n_i = m // tm
    assert n_qkv * tn == 3 * FOX_WIDTH and n_wide * tn == WIDE_COLS and w_tail.shape[0] == SMALL_COLS
    assert scaled_rows % tn == 0 and all(n % tn == 0 and src % SUBLANES == 0 for src, n in pieces)
    assert sum(n for _, n in pieces) == n_main * tn
    starts, first_tile = [], 0
    for src, n in pieces:
        starts.append((first_tile, src))
        first_tile += n // tn

    def src_row(j):
        tile = jnp.int32(0)
        for blk, src in starts:
            tile = jnp.where(j >= blk, src // SUBLANES + (jnp.minimum(j, n_main - 1) - blk) * (tn // SUBLANES), tile)
        return pl.multiple_of(tile * SUBLANES, SUBLANES)

    qkv_row = lambda j, i: jnp.where(j < n_qkv, i, n_i - 1)
    wide_row = lambda j, i: jnp.where(j < n_qkv, 0, jnp.where(j < n_main, i, n_i - 1))
    small_row = lambda j, i: jnp.where(j == n_main, i, 0)
    return pl.pallas_call(
        functools.partial(_in_proj_kernel, n_qkv=n_qkv, n_wide=n_wide, scaled_tiles=scaled_rows // tn,
                          scale=scale),
        grid=(n_main + 1, n_i),
        in_specs=[
            pl.BlockSpec((tm, d), lambda j, i: (i, 0)),
            pl.BlockSpec((pl.Element(tn), pl.Element(d)), lambda j, i: (src_row(j), 0)),
            pl.BlockSpec((SMALL_COLS, d), lambda j, i: (0, 0)),
        ],
        out_specs=[
            pl.BlockSpec((tm, tn), lambda j, i: (qkv_row(j, i), jnp.minimum(j, n_qkv - 1))),
            pl.BlockSpec((tm, tn), lambda j, i: (wide_row(j, i), jnp.clip(j - n_qkv, 0, n_wide - 1))),
            pl.BlockSpec((tm, SMALL_COLS), lambda j, i: (small_row(j, i), 0)),
        ],
        out_shape=[
            jax.ShapeDtypeStruct((m, 3 * FOX_WIDTH), bf16),
            jax.ShapeDtypeStruct((m, WIDE_COLS), f32),
            jax.ShapeDtypeStruct((m, SMALL_COLS), f32),
        ],
        scratch_shapes=[pltpu.VMEM((tn, d), bf16)],
        compiler_params=pltpu.CompilerParams(
            dimension_semantics=("arbitrary", "arbitrary"),
            vmem_limit_bytes=VMEM_LIMIT),
        name="in_proj",
    )(h2d, w_t, w_tail)


def _gate_kernel(f_ref, bias_ref, c_ref, carry_ref, *, tb):
    @pl.when(pl.program_id(1) == 0)
    def _():
        carry_ref[...] = jnp.zeros_like(carry_ref)

    z = f_ref[...] + bias_ref[...]
    log_f = -_softplus(-z)
    row = lax.broadcasted_iota(jnp.int32, (tb, tb), 0)
    col = lax.broadcasted_iota(jnp.int32, (tb, tb), 1)
    tri = jnp.where(col <= row, 1.0, 0.0).astype(bf16)
    c = _dot_exact_ones(log_f, tri, nt=True) + carry_ref[0:1, :]
    c_ref[...] = c
    carry_ref[0:1, :] = c[tb - 1:tb, :]


def _gate_cumsum(small, bias_row, batch, seq):
    tb = 512
    nt = seq // tb
    return pl.pallas_call(
        functools.partial(_gate_kernel, tb=tb),
        grid=(batch, nt),
        in_specs=[
            pl.BlockSpec((tb, LANES), lambda b, t: (b * nt + t, 1)),
            pl.BlockSpec((1, LANES), lambda b, t: (0, 0)),
        ],
        out_specs=pl.BlockSpec((tb, LANES), lambda b, t: (b * nt + t, 0)),
        out_shape=jax.ShapeDtypeStruct((batch * seq, LANES), f32),
        scratch_shapes=[pltpu.VMEM((SUBLANES, LANES), f32)],
        compiler_params=pltpu.CompilerParams(
            dimension_semantics=("parallel", "arbitrary")),
        name="fox_gate_cumsum",
    )(small, bias_row)


LOG2E = 1.4426950408889634
FOX_TK = 256
FOX_GW = 256
FOX_BATCH = 4


def _fox_kernel(q_ref, k_ref, v_ref, c_ref, z_ref, o_ref, crep_ref, acc_ref, vt_ref, *, seq, tk, gw):
    h = pl.program_id(1)
    ng = seq // gw
    nt = seq // tk
    assert seq % gw == 0 and seq % tk == 0 and (tk % gw == 0 or gw % tk == 0)
    mm = lambda a, b: jnp.dot(a, b, preferred_element_type=f32)
    mm_nt = lambda a, b: lax.dot_general(a, b, (((1,), (1,)), ((), ())), preferred_element_type=f32)

    lane = lax.broadcasted_iota(jnp.int32, (1, LANES), 1)
    onehot = jnp.where(lane == F_LANE + h, LOG2E, 0.0)
    crep_ref[...] = jnp.sum(c_ref[...] * onehot, axis=1, keepdims=True)
    for t in range(nt):
        vt_ref[t, 0:LANES, :] = v_ref[t * tk:(t + 1) * tk, :].T
        vt_ref[t, LANES:LANES + SUBLANES, :] = jnp.ones((SUBLANES, tk), bf16)

    key_rel = lax.broadcasted_iota(jnp.int32, (tk, gw), 0)
    qry_rel = lax.broadcasted_iota(jnp.int32, (tk, gw), 1)

    def visible(t, g):
        return t * tk <= g * gw + gw - 1

    def needs_mask(t, g):
        return t * tk + tk - 1 > g * gw

    last_tile = [max(t for t in range(nt) if visible(t, g)) for g in range(ng)]
    sched = []
    for t in range(nt):
        groups = [g for g in range(ng) if visible(t, g)]
        sched += [[(t, g) for g in groups[k:k + FOX_BATCH]] for k in range(0, len(groups), FOX_BATCH)]

    def qk(batch):
        return [mm_nt(k_ref[t * tk:(t + 1) * tk, :], q_ref[g * gw:(g + 1) * gw, :]) for t, g in batch]

    m = [None] * ng
    l = [None] * ng

    def finish(batch, alpha, pv):
        for (t, g), a, xs in zip(batch, alpha, pv):
            x, p_sum = xs[0:LANES], xs[LANES:LANES + 1]
            acc_ref[g] = x if t == 0 else a * acc_ref[g] + x
            l[g] = p_sum if t == 0 else a * l[g] + p_sum
            if t == last_tile[g]:
                rows = slice(g * gw, (g + 1) * gw)
                z = z_ref[rows, :]
                o = (acc_ref[g] / l[g]).T
                o_ref[rows, :] = (o * (z * _sigmoid(z))).astype(o_ref.dtype)

    s_next = qk(sched[0])
    pending = None
    for bi, batch in enumerate(sched):
        s_cur = s_next
        if bi + 1 < len(sched):
            s_next = qk(sched[bi + 1])
        alpha, p16 = [], []
        for (t, g), s in zip(batch, s_cur):
            s = s - crep_ref[t * tk:(t + 1) * tk, :]
            if needs_mask(t, g):
                s = jnp.where(key_rel + t * tk > qry_rel + g * gw, -jnp.inf, s)
            m_tile = jnp.max(s, axis=0, keepdims=True)
            if t == 0:
                m_new, a = m_tile, None
            else:
                m_new = jnp.maximum(m[g], m_tile)
                a = jnp.exp2(m[g] - m_new)
            m[g] = m_new
            alpha.append(a)
            p16.append(jnp.exp2((s - m_new).astype(bf16)))
        pv = [mm(vt_ref[t], p) for (t, g), p in zip(batch, p16)]
        if pending is not None:
            finish(*pending)
        pending = (batch, alpha, pv)
    finish(*pending)


def _fox_attention(qkv3, c3, wide3, batch, seq):
    tk, gw = FOX_TK, FOX_GW
    h8 = FOX_HEADS
    return pl.pallas_call(
        functools.partial(_fox_kernel, seq=seq, tk=tk, gw=gw),
        grid=(batch, h8),
        in_specs=[
            pl.BlockSpec((None, seq, LANES), lambda b, h: (b, 0, h)),
            pl.BlockSpec((None, seq, LANES), lambda b, h: (b, 0, h8 + h)),
            pl.BlockSpec((None, seq, LANES), lambda b, h: (b, 0, 2 * h8 + h)),
            pl.BlockSpec((None, seq, LANES), lambda b, h: (b, 0, 0)),
            pl.BlockSpec((None, seq, LANES), lambda b, h: (b, 0, COL_ZA + h)),
        ],
        out_specs=pl.BlockSpec((None, seq, LANES), lambda b, h: (b, 0, h)),
        out_shape=jax.ShapeDtypeStruct((batch, seq, FOX_WIDTH), bf16),
        scratch_shapes=[
            pltpu.VMEM((seq, 1), f32),
            pltpu.VMEM((seq // gw, LANES, gw), f32),
            pltpu.VMEM((seq // tk, LANES + SUBLANES, tk), bf16),
        ],
        compiler_params=pltpu.CompilerParams(
            dimension_semantics=("parallel", "arbitrary"),
            vmem_limit_bytes=VMEM_LIMIT),
        name="fox_attention",
    )(qkv3, qkv3, qkv3, c3, wide3)


P_MU_R, P_MU_K, P_MU_V, P_MU_Z, P_W0, P_A0, P_KK, P_KA, P_RK, P_LNW, P_LNB = range(11)
P_ROWS = 16
K_XA, K_XR, K_YB, K_YK, K_VS, K_BH, K_KH = range(7)


def _spread(emitters, lo=0.0, hi=1.0):
    n = len(emitters)
    return [(lo + (hi - lo) * (i + 0.5) / n, e) for i, e in enumerate(emitters)]


def _interleave(*segment_lists):
    keyed = [(pos, prio, seg) for prio, segs in enumerate(segment_lists) for pos, seg in segs]
    keyed.sort(key=lambda x: (x[0], x[1]))
    for _, _, seg in keyed:
        seg()


def _rwkv_kernel(r_ref, k_ref, v_ref, z_ref, wd_ref, ad_ref, pc_ref, ps_ref, w2_ref, a2_ref, pc3_ref,
                 o_ref,
                 state_ref, prev_ref, stk_ref, xr32_ref, aux1_ref, dec1_ref, p16_ref, p32_ref,
                 aux2_ref, dec2_ref, *, tb, nt):
    C = RWKV_CHUNK
    G = GROUP_ROWS
    N = RWKV_HEAD_DIM
    NC = tb // C
    s = pl.program_id(0)
    first_of_seq_1 = (s % nt) == 0
    first_of_seq_3 = ((s + 2 * nt - 2) % nt) == 0

    @pl.when(s == 0)
    def _():
        for ref in (state_ref, prev_ref, stk_ref, xr32_ref, aux1_ref, dec1_ref, p16_ref, p32_ref,
                    aux2_ref, dec2_ref):
            ref[...] = jnp.zeros_like(ref)

    mm = lambda a, b: jnp.dot(a, b, preferred_element_type=f32)
    mm_nt = lambda a, b: lax.dot_general(a, b, (((1,), (1,)), ((), ())), preferred_element_type=f32)
    mm_tn = lambda a, b: lax.dot_general(a, b, (((0,), (0,)), ((), ())), preferred_element_type=f32)

    def body(cur, prv):
        lane_t = lax.broadcasted_iota(jnp.int32, (1, LANES), 1)
        head_masks = [jnp.where(lane_t // N == h, 1.0, 0.0) for h in range(HEADS_PER_GROUP)]

        def head_sum(x):
            sums = [jnp.sum(x * hm, axis=1, keepdims=True) for hm in head_masks]
            out = sums[HEADS_PER_GROUP - 1]
            for hh in range(HEADS_PER_GROUP - 2, -1, -1):
                out = jnp.where(lane_t // N == hh, sums[hh], out)
            return out

        stack = lambda x: jnp.concatenate([x * hm for hm in head_masks], axis=0)
        lane_c = lax.broadcasted_iota(jnp.int32, (C, LANES), 1)
        head_sel = [lane_c // N == h for h in range(HEADS_PER_GROUP)]

        def stack16(x):
            xb = x.astype(bf16)
            return jnp.concatenate([jnp.where(m, xb, jnp.zeros_like(xb)) for m in head_sel], axis=0)

        gi = lax.broadcasted_iota(jnp.int32, (G, G), 0)
        gj = lax.broadcasted_iota(jnp.int32, (G, G), 1)
        same_head = gi // C == gj // C
        strict = jnp.logical_and(same_head, gj < gi)
        incl = jnp.logical_and(same_head, gj <= gi)
        eye = jnp.where(gi == gj, 1.0, 0.0)
        chunks = range(NC)

        v1 = {}
        prm = lambda idx: pc_ref[idx:idx + 1, :]
        row_in_block = lax.broadcasted_iota(jnp.int32, (tb, LANES), 0)

        def shifted(ref, slot, mu):
            u = ref[...]
            carry = jnp.where(first_of_seq_1, 0.0, prev_ref[slot, 0:1, :])
            prev = jnp.where(row_in_block == 0, carry, pltpu.roll(u, 1, 0))
            prev_ref[slot, 0:1, :] = u[tb - 1:tb, :]
            return u + (prev - u) * mu

        def s1_lora():
            v1["wd"] = shifted(wd_ref, 4, ps_ref[0:1, :])
            v1["ad"] = shifted(ad_ref, 5, ps_ref[1:2, :])
            v1["w_lin"] = _dot(jnp.tanh(v1["wd"]), w2_ref[...])
            v1["a_lin"] = _dot(v1["ad"], a2_ref[...])

        def s1_key_norm():
            v1["kr"] = shifted(k_ref, 1, prm(P_MU_K))
            kk = v1["kr"] * prm(P_KK)
            v1["kk_raw"] = kk
            v1["kk_ss"] = head_sum(kk * kk)

        def s1_decay():
            w = -_softplus(-(prm(P_W0) + v1["w_lin"])) - 0.5
            v1["log_decay"] = -jnp.exp(w)
            ti = lax.broadcasted_iota(jnp.int32, (2 * C, 2 * C), 0)
            tj = lax.broadcasted_iota(jnp.int32, (2 * C, 2 * C), 1)
            pair_tri = jnp.where(jnp.logical_and(tj <= ti, ti // C == tj // C), 1.0, 0.0).astype(bf16)
            v1["ci"] = jnp.concatenate(
                [_dot_exact_ones(v1["log_decay"][k:k + 2 * C], pair_tri, nt=True)
                 for k in range(0, tb, 2 * C)], axis=0)

        def s1_bonus():
            v1["rate"] = _sigmoid(prm(P_A0) + v1["a_lin"])
            v1["r"] = shifted(r_ref, 0, prm(P_MU_R))
            v1["vr"] = shifted(v_ref, 2, prm(P_MU_V))
            v1["kp"] = v1["kr"] * (1.0 + (v1["rate"] - 1.0) * prm(P_KA))
            aux1_ref[cur, 0] = head_sum(v1["r"] * v1["kp"] * prm(P_RK)) * v1["vr"]
            zb = shifted(z_ref, 3, prm(P_MU_Z))
            aux1_ref[cur, 1] = zb * _sigmoid(zb)

        def s1_scale():
            kk = v1["kk_raw"] / jnp.maximum(jnp.sqrt(v1["kk_ss"]), L2_EPS)
            v1["bb"] = kk * v1["rate"]
            ci = v1["ci"]
            v1["a_t"] = -kk * jnp.exp(ci - v1["log_decay"])
            v1["r_t"] = v1["r"] * jnp.exp(ci)
            inv = jnp.exp(-ci)
            v1["b_t"] = v1["bb"] * inv
            v1["k_t"] = v1["kp"] * inv

        def s1_stack(c):
            def emit():
                sl = slice(c * C, (c + 1) * C)
                ci_c = v1["ci"][sl]
                c_last = ci_c[C - 1:C, :]
                to_end = jnp.exp(c_last - ci_c)
                dec1_ref[cur, c] = jnp.broadcast_to(jnp.exp(c_last), (SUBLANES, LANES))
                xr = stack(v1["r_t"][sl])
                xr32_ref[cur, c] = xr
                stk_ref[cur, K_XR, c] = xr.astype(bf16)
                stk_ref[cur, K_XA, c] = stack16(v1["a_t"][sl])
                stk_ref[cur, K_YB, c] = stack16(v1["b_t"][sl])
                stk_ref[cur, K_YK, c] = stack16(v1["k_t"][sl])
                stk_ref[cur, K_VS, c] = stack16(v1["vr"][sl])
                stk_ref[cur, K_BH, c] = stack16(v1["bb"][sl] * to_end)
                stk_ref[cur, K_KH, c] = stack16(v1["kp"][sl] * to_end)
            return emit

        segs1 = ([(0.0, s1_lora), (0.02, s1_key_norm), (0.05, s1_decay), (0.08, s1_bonus)]
                 + _spread([s1_scale] + [s1_stack(c) for c in chunks], RWKV_TAIL_START, 1.0))

        v2 = {}
        ld = lambda kind, c: stk_ref[prv, kind, c]

        def s2_big():
            big = [mm_nt(jnp.concatenate([ld(K_XA, c), ld(K_XR, c)], axis=0),
                         jnp.concatenate([ld(K_YB, c), ld(K_YK, c)], axis=0)) for c in chunks]
            v2["a_ab"] = [jnp.where(strict, big[c][0:G, 0:G], 0.0) for c in chunks]
            v2["a_ak"] = [jnp.where(strict, big[c][0:G, G:2 * G], 0.0).astype(bf16) for c in chunks]
            v2["a_rb"] = [jnp.where(incl, big[c][G:2 * G, 0:G], 0.0).astype(bf16) for c in chunks]
            v2["a_rk"] = [jnp.where(incl, big[c][G:2 * G, G:2 * G], 0.0).astype(bf16) for c in chunks]
            v2["tinv"] = [eye + a for a in v2["a_ab"]]
            v2["pw"] = [a.astype(bf16) for a in v2["a_ab"]]

        def s2_first_square():
            v2["pw"] = [mm(p, p).astype(bf16) for p in v2["pw"]]

        def s2_level(last):
            def emit():
                for c in chunks:
                    pw = v2["pw"][c]
                    t16 = v2["tinv"][c].astype(bf16)
                    if last:
                        v2["tinv"][c] = v2["tinv"][c] + mm(t16, pw)
                    else:
                        both = mm(jnp.concatenate([t16, pw], axis=0), pw)
                        v2["tinv"][c] = v2["tinv"][c] + both[0:G]
                        v2["pw"][c] = both[G:2 * G].astype(bf16)
            return emit

        def s2_av():
            v2["tinv"] = [t.astype(bf16) for t in v2["tinv"]]
            av = [mm(jnp.concatenate([v2["a_ak"][c], v2["a_rk"][c]], axis=0), ld(K_VS, c)) for c in chunks]
            v2["akv"] = [x[0:G].astype(bf16) for x in av]
            v2["arkv"] = [x[G:2 * G] for x in av]

        def s2_wu():
            v2["wu"] = [mm(v2["tinv"][c], jnp.concatenate([ld(K_XA, c), v2["akv"][c]], axis=1)).astype(bf16)
                        for c in chunks]

        def s2_ry():
            for c in chunks:
                e = mm(v2["a_rb"][c], v2["wu"][c])
                p16_ref[cur, 0, c] = (xr32_ref[prv, c] + e[:, 0:LANES]).astype(bf16)
                p32_ref[cur, 0, c] = e[:, LANES:] + v2["arkv"][c]

        def s2_pm():
            for c in chunks:
                p16_ref[cur, 1, c] = mm_tn(v2["wu"][c][:, 0:LANES], ld(K_BH, c)).astype(bf16)

        def s2_q():
            for c in chunks:
                p32_ref[cur, 1, c] = mm_tn(jnp.concatenate([v2["wu"][c][:, LANES:], ld(K_VS, c)], axis=0),
                                           jnp.concatenate([ld(K_BH, c), ld(K_KH, c)], axis=0))
            aux2_ref[cur] = aux1_ref[prv]
            dec2_ref[cur] = dec1_ref[prv]

        segs2 = _spread([s2_big, s2_first_square] + [s2_level(False)] * 4 + [s2_level(True)]
                        + [s2_av, s2_wu, s2_ry, s2_pm, s2_q])

        v3 = {"ys": []}
        prm3 = lambda idx: pc3_ref[idx:idx + 1, :]

        def s3_start():
            v3["state"] = jnp.where(first_of_seq_3, 0.0, state_ref[...])

        def s3_chunk(c):
            def emit():
                state = v3["state"]
                s16 = state.astype(bf16)
                y_st = mm_nt(p16_ref[prv, 0, c], s16) + p32_ref[prv, 0, c]
                v3["ys"].append(y_st[0:C] + y_st[C:2 * C])
                v3["state"] = (state * dec2_ref[prv, c, 0:1, :] + mm(s16, p16_ref[prv, 1, c])
                               + p32_ref[prv, 1, c])
            return emit

        def s3_finish():
            state_ref[...] = v3["state"]
            y = jnp.concatenate(v3["ys"], axis=0)
            mean = head_sum(y) * (1.0 / N)
            yc = y - mean
            var = head_sum(yc * yc) * (1.0 / N)
            yn = yc * lax.rsqrt(var + GN_EPS) * prm3(P_LNW) + prm3(P_LNB)
            o_ref[...] = ((yn + aux2_ref[prv, 0]) * aux2_ref[prv, 1]).astype(o_ref.dtype)

        segs3 = _spread([s3_start] + [s3_chunk(c) for c in chunks] + [s3_finish], 0.0, RWKV_CHAIN_END)

        _interleave(segs3, segs2, segs1)

    for parity in (0, 1):
        @pl.when(s % 2 == parity)
        def _(parity=parity):
            body(cur=parity, prv=1 - parity)


def _rwkv_mix(wide, small, pc, ps, w2p, a2p, batch, seq):
    tb = RWKV_BLOCK
    nt = seq // tb
    groups = RWKV_WIDTH // LANES
    nc = tb // RWKV_CHUNK
    n_blocks = batch * groups * nt

    def where(s):
        s = jnp.clip(s, 0, n_blocks - 1)
        return (s // (nt * groups)) * nt + s % nt, (s // nt) % groups

    col = lambda c0: (lambda s: (where(s)[0], c0 + where(s)[1]))
    fixed_col = lambda c: (lambda s: (where(s)[0], c))
    grp = lambda s: (0, where(s)[1])
    return pl.pallas_call(
        functools.partial(_rwkv_kernel, tb=tb, nt=nt),
        grid=(n_blocks + 2,),
        in_specs=[
            pl.BlockSpec((tb, LANES), col(COL_R)),
            pl.BlockSpec((tb, LANES), col(COL_K)),
            pl.BlockSpec((tb, LANES), col(COL_V)),
            pl.BlockSpec((tb, LANES), col(COL_ZB)),
            pl.BlockSpec((tb, LANES), fixed_col(0)),
            pl.BlockSpec((tb, LANES), fixed_col(1)),
            pl.BlockSpec((P_ROWS, LANES), grp),
            pl.BlockSpec((SUBLANES, LANES), lambda s: (0, 0)),
            pl.BlockSpec((LANES, LANES), grp),
            pl.BlockSpec((LANES, LANES), grp),
            pl.BlockSpec((P_ROWS, LANES), lambda s: grp(s - 2)),
        ],
        out_specs=pl.BlockSpec((tb, LANES), lambda s: (where(s - 2)[0], where(s - 2)[1])),
        out_shape=jax.ShapeDtypeStruct((batch * seq, RWKV_WIDTH), bf16),
        scratch_shapes=[
            pltpu.VMEM((LANES, LANES), f32),
            pltpu.VMEM((6, SUBLANES, LANES), f32),
            pltpu.VMEM((2, 7, nc, GROUP_ROWS, LANES), bf16),
            pltpu.VMEM((2, nc, GROUP_ROWS, LANES), f32),
            pltpu.VMEM((2, 2, tb, LANES), f32),
            pltpu.VMEM((2, nc, SUBLANES, LANES), f32),
            pltpu.VMEM((2, 2, nc, GROUP_ROWS, LANES), bf16),
            pltpu.VMEM((2, 2, nc, GROUP_ROWS, LANES), f32),
            pltpu.VMEM((2, 2, tb, LANES), f32),
            pltpu.VMEM((2, nc, SUBLANES, LANES), f32),
        ],
        compiler_params=pltpu.CompilerParams(
            dimension_semantics=("arbitrary",),
            vmem_limit_bytes=VMEM_LIMIT),
        name="rwkv_mix",
    )(wide, wide, wide, wide, small, small, pc, ps, w2p, a2p, pc)


def _out_kernel(oa_ref, ob_ref, ga_ref, gb_ref, x_ref, wpf_ref, wpr_ref, wo_ref, g_ref, o_ref):
    pa = jnp.dot(oa_ref[...], wpf_ref[...], preferred_element_type=f32)
    pb = jnp.dot(ob_ref[...], wpr_ref[...], preferred_element_type=f32)
    m = _sigmoid(ga_ref[...]) * pa + _sigmoid(gb_ref[...]) * pb
    z = x_ref[...] + jnp.dot(m.astype(bf16), wo_ref[...], preferred_element_type=f32)
    ms = jnp.mean(z * z, axis=-1, keepdims=True)
    o_ref[...] = z * lax.rsqrt(ms + RMS_EPS) * g_ref[...]


def _merge_out(oa, ob, wide, x2d, wpf, wpr, wo, gain):
    m, d = x2d.shape
    tm = 256
    resident = lambda shape: pl.BlockSpec(shape, lambda i: (0, 0), pipeline_mode=pl.Buffered(1))
    return pl.pallas_call(
        _out_kernel,
        grid=(m // tm,),
        in_specs=[
            pl.BlockSpec((tm, FOX_WIDTH), lambda i: (i, 0)),
            pl.BlockSpec((tm, RWKV_WIDTH), lambda i: (i, 0)),
            pl.BlockSpec((tm, d), lambda i: (i, COL_GA * LANES // d)),
            pl.BlockSpec((tm, d), lambda i: (i, COL_GB * LANES // d)),
            pl.BlockSpec((tm, d), lambda i: (i, 0)),
            resident((FOX_WIDTH, d)),
            resident((RWKV_WIDTH, d)),
            resident((d, d)),
            resident((1, d)),
        ],
        out_specs=pl.BlockSpec((tm, d), lambda i: (i, 0)),
        out_shape=jax.ShapeDtypeStruct((m, d), f32),
        compiler_params=pltpu.CompilerParams(
            dimension_semantics=("parallel",),
            vmem_limit_bytes=VMEM_LIMIT),
        name="merge_out",
    )(oa, ob, wide, wide, x2d, wpf, wpr, wo, gain)


def _branches(x2d, batch, seq, norm_gain, w_in, fox_forget_bias, rwkv_shift_mix, rwkv_w0, rwkv_w2,
              rwkv_a0, rwkv_a2, rwkv_k_k, rwkv_k_a, rwkv_r_k, rwkv_ln_w, rwkv_ln_b):
    fw, rw = FOX_WIDTH, RWKV_WIDTH
    w_t = w_in.T
    r_qkv = 0
    r_za = r_qkv + 3 * fw
    r_f = r_za + fw
    r_rkvz = r_f + FOX_HEADS
    r_wd = r_rkvz + 4 * rw
    r_ad = r_wd + LORA
    r_g = r_ad + LORA
    pieces = [(r_qkv, 3 * fw), (r_g, 2 * D_MODEL), (r_za, fw), (r_rkvz, 4 * rw)]
    pad_rows = lambda a: jnp.pad(a, ((0, LANES - a.shape[0]), (0, 0)))
    w_tail = jnp.concatenate(
        [pad_rows(w_t[r_wd:r_wd + LORA]),
         pad_rows(jnp.concatenate([w_t[r_ad:r_ad + LORA], w_t[r_f:r_f + FOX_HEADS]], axis=0))], axis=0)
    h2d = _rmsnorm_bf16(x2d, norm_gain.reshape(1, D_MODEL))
    qkv, wide, small = _in_proj(h2d, w_t, pieces, fw, FOX_HEAD_DIM ** -0.5 * LOG2E, w_tail)

    bias_row = jnp.pad(fox_forget_bias.reshape(1, FOX_HEADS), ((0, 0), (F_LANE, LANES - F_LANE - FOX_HEADS)))
    c = _gate_cumsum(small, bias_row, batch, seq)
    qkv3 = qkv.reshape(batch, seq, 3 * fw)
    oa = _fox_attention(qkv3, c.reshape(batch, seq, LANES), wide.reshape(batch, seq, WIDE_COLS),
                        batch, seq).reshape(batch * seq, fw)

    mu = rwkv_shift_mix
    pc = jnp.zeros((P_ROWS, rw), f32)
    rows = [mu[0:rw], mu[rw:2 * rw], mu[2 * rw:3 * rw], mu[3 * rw:4 * rw], rwkv_w0, rwkv_a0,
            rwkv_k_k, rwkv_k_a, rwkv_r_k.reshape(rw), rwkv_ln_w, rwkv_ln_b]
    pc = pc.at[:len(rows)].set(jnp.stack(rows))
    ps = jnp.zeros((SUBLANES, LANES), f32)
    ps = ps.at[0, :LORA].set(mu[4 * rw:4 * rw + LORA]).at[1, :LORA].set(mu[4 * rw + LORA:])
    w2p = jnp.pad(rwkv_w2, ((0, LANES - LORA), (0, 0))).astype(bf16)
    a2p = jnp.pad(rwkv_a2, ((0, LANES - LORA), (0, 0))).astype(bf16)
    ob = _rwkv_mix(wide, small, pc, ps, w2p, a2p, batch, seq)

    return oa, ob, wide


def kernel(x, norm_gain, w_in, fox_forget_bias, rwkv_shift_mix, rwkv_w0, rwkv_w2, rwkv_a0, rwkv_a2, rwkv_k_k, rwkv_k_a, rwkv_r_k, rwkv_ln_w, rwkv_ln_b, w_proj_fox, w_proj_rwkv, w_out, final_norm_gain):
    batch, seq, d = x.shape
    depth = norm_gain.shape[0]
    assert depth == 1, "the final rmsnorm is fused into the single layer's output kernel"
    x2d = x.reshape(batch * seq, d)
    oa, ob, wide = _branches(x2d, batch, seq, norm_gain[0], w_in[0], fox_forget_bias[0],
                             rwkv_shift_mix[0], rwkv_w0[0], rwkv_w2[0], rwkv_a0[0], rwkv_a2[0],
                             rwkv_k_k[0], rwkv_k_a[0], rwkv_r_k[0], rwkv_ln_w[0], rwkv_ln_b[0])
    out = _merge_out(oa, ob, wide, x2d, w_proj_fox[0].astype(bf16), w_proj_rwkv[0].astype(bf16),
                     w_out[0].astype(bf16), final_norm_gain.reshape(1, d))
    return out.reshape(batch, seq, d)
```

```python
import functools

import jax
import jax.numpy as jnp
from jax import lax
from jax.experimental import pallas as pl
from jax.experimental.pallas import tpu as pltpu

D_MODEL = 2048
FOX_HEADS = 8
FOX_HEAD_DIM = 128
FOX_WIDTH = FOX_HEADS * FOX_HEAD_DIM
RWKV_HEADS = 16
RWKV_HEAD_DIM = 64
RWKV_WIDTH = RWKV_HEADS * RWKV_HEAD_DIM
LORA = 96
RMS_EPS = 1e-6
GN_EPS = 64e-5
L2_EPS = 1e-12

LANES = 128
SUBLANES = 8
VMEM_LIMIT = 56 * 1024 * 1024

COL_GA = 0
COL_GB = D_MODEL // LANES
COL_ZA = 2 * D_MODEL // LANES
COL_R = COL_ZA + FOX_WIDTH // LANES
COL_K = COL_R + RWKV_WIDTH // LANES
COL_V = COL_K + RWKV_WIDTH // LANES
COL_ZB = COL_V + RWKV_WIDTH // LANES
WIDE_COLS = (COL_ZB + RWKV_WIDTH // LANES) * LANES
SMALL_COLS = 2 * LANES
F_LANE = LORA

RWKV_CHUNK = 64
RWKV_BLOCK = 512
RWKV_TAIL_START = 0.3
RWKV_CHAIN_END = 0.8
HEADS_PER_GROUP = LANES // RWKV_HEAD_DIM
GROUP_ROWS = HEADS_PER_GROUP * RWKV_CHUNK

f32 = jnp.float32
bf16 = jnp.bfloat16


def _dot(a, b):
    return jnp.dot(a.astype(bf16), b.astype(bf16), preferred_element_type=f32)


def _dot_exact_ones(a, ones_bf16, nt=False, terms=3):
    if nt:
        mm = lambda x: lax.dot_general(ones_bf16, x, (((1,), (0,)), ((), ())),
                                       preferred_element_type=f32)
    else:
        mm = lambda x: jnp.dot(x, ones_bf16, preferred_element_type=f32)
    part = a.astype(bf16)
    out = mm(part)
    rest = a
    for _ in range(terms - 1):
        rest = rest - part.astype(f32)
        part = rest.astype(bf16)
        out = out + mm(part)
    return out


def _softplus(x):
    return jnp.maximum(x, 0.0) + jnp.log(1.0 + jnp.exp(-jnp.abs(x)))


def _sigmoid(x):
    return 1.0 / (1.0 + jnp.exp(-x))


PROJ_TM = 1024
PROJ_TN = 1024


def _norm_q_kernel(x_ref, g_ref, w_ref, h_ref, q_ref, w16_ref, *, scale):
    @pl.when(pl.program_id(0) == 0)
    def _():
        w16_ref[...] = (w_ref[...] * scale).astype(bf16)

    xf = x_ref[...]
    ms = jnp.mean(xf * xf, axis=-1, keepdims=True)
    h = (xf * lax.rsqrt(ms + RMS_EPS) * g_ref[...]).astype(bf16)
    h_ref[...] = h
    q_ref[...] = lax.dot_general(h, w16_ref[...], (((1,), (1,)), ((), ())),
                                 preferred_element_type=f32).astype(q_ref.dtype)


def _norm_q(x2d, gain, w_t, scale):
    m, d = x2d.shape
    tm = PROJ_TM
    return pl.pallas_call(
        functools.partial(_norm_q_kernel, scale=scale),
        grid=(m // tm,),
        in_specs=[
            pl.BlockSpec((tm, d), lambda i: (i, 0)),
            pl.BlockSpec((1, d), lambda i: (0, 0)),
            pl.BlockSpec((FOX_WIDTH, d), lambda i: (0, 0)),
        ],
        out_specs=[pl.BlockSpec((tm, d), lambda i: (i, 0)), pl.BlockSpec((tm, FOX_WIDTH), lambda i: (i, 0))],
        out_shape=[jax.ShapeDtypeStruct((m, d), bf16), jax.ShapeDtypeStruct((m, FOX_WIDTH), bf16)],
        scratch_shapes=[pltpu.VMEM((FOX_WIDTH, d), bf16)],
        compiler_params=pltpu.CompilerParams(dimension_semantics=("arbitrary",), vmem_limit_bytes=VMEM_LIMIT),
        name="norm_q",
    )(x2d, gain, w_t)


def _in_proj_kernel(h_ref, w_ref, wt_ref, kv_ref, wide_ref, small_ref, w16_ref, *, n_kv, n_wide):
    j = pl.program_id(0)
    i = pl.program_id(1)
    n_main = n_kv + n_wide

    @pl.when(i == 0)
    def _():
        @pl.when(j < n_main)
        def _():
            w16_ref[...] = w_ref[...].astype(bf16)

        @pl.when(j == n_main)
        def _():
            w16_ref[0:SMALL_COLS, :] = wt_ref[...].astype(bf16)

    project = lambda w: lax.dot_general(h_ref[...], w, (((1,), (1,)), ((), ())), preferred_element_type=f32)

    @pl.when(j < n_kv)
    def _():
        kv_ref[...] = project(w16_ref[...]).astype(kv_ref.dtype)

    @pl.when(jnp.logical_and(j >= n_kv, j < n_main))
    def _():
        wide_ref[...] = project(w16_ref[...])

    @pl.when(j == n_main)
    def _():
        small_ref[...] = project(w16_ref[0:SMALL_COLS, :])


def _in_proj(h2d, w_t, pieces, w_tail):
    m, d = h2d.shape
    tm, tn = PROJ_TM, PROJ_TN
    n_kv = 2 * FOX_WIDTH // tn
    n_wide = WIDE_COLS // tn
    n_main = n_kv + n_wide
    n_i = m // tm
    assert n_kv * tn == 2 * FOX_WIDTH and n_wide * tn == WIDE_COLS and w_tail.shape[0] == SMALL_COLS
    assert all(n % tn == 0 and src % SUBLANES == 0 for src, n in pieces)
    assert sum(n for _, n in pieces) == n_main * tn
    starts, first_tile = [], 0
    for src, n in pieces:
        starts.append((first_tile, src))
        first_tile += n // tn

    def src_row(j):
        tile = jnp.int32(0)
        for blk, src in starts:
            tile = jnp.where(j >= blk, src // SUBLANES + (jnp.minimum(j, n_main - 1) - blk) * (tn // SUBLANES), tile)
        return pl.multiple_of(tile * SUBLANES, SUBLANES)

    kv_row = lambda j, i: jnp.where(j < n_kv, i, n_i - 1)
    wide_row = lambda j, i: jnp.where(j < n_kv, 0, jnp.where(j < n_main, i, n_i - 1))
    small_row = lambda j, i: jnp.where(j == n_main, i, 0)
    return pl.pallas_call(
        functools.partial(_in_proj_kernel, n_kv=n_kv, n_wide=n_wide),
        grid=(n_main + 1, n_i),
        in_specs=[
            pl.BlockSpec((tm, d), lambda j, i: (i, 0)),
            pl.BlockSpec((pl.Element(tn), pl.Element(d)), lambda j, i: (src_row(j), 0)),
            pl.BlockSpec((SMALL_COLS, d), lambda j, i: (0, 0)),
        ],
        out_specs=[
            pl.BlockSpec((tm, tn), lambda j, i: (kv_row(j, i), jnp.minimum(j, n_kv - 1))),
            pl.BlockSpec((tm, tn), lambda j, i: (wide_row(j, i), jnp.clip(j - n_kv, 0, n_wide - 1))),
            pl.BlockSpec((tm, SMALL_COLS), lambda j, i: (small_row(j, i), 0)),
        ],
        out_shape=[
            jax.ShapeDtypeStruct((m, 2 * FOX_WIDTH), bf16),
            jax.ShapeDtypeStruct((m, WIDE_COLS), f32),
            jax.ShapeDtypeStruct((m, SMALL_COLS), f32),
        ],
        scratch_shapes=[pltpu.VMEM((tn, d), bf16)],
        compiler_params=pltpu.CompilerParams(
            dimension_semantics=("arbitrary", "arbitrary"),
            vmem_limit_bytes=VMEM_LIMIT),
        name="in_proj",
    )(h2d, w_t, w_tail)


def _gate_kernel(f_ref, bias_ref, c_ref, carry_ref, *, tb):
    @pl.when(pl.program_id(1) == 0)
    def _():
        carry_ref[...] = jnp.zeros_like(carry_ref)

    z = f_ref[...] + bias_ref[...]
    log_f = -_softplus(-z)
    row = lax.broadcasted_iota(jnp.int32, (tb, tb), 0)
    col = lax.broadcasted_iota(jnp.int32, (tb, tb), 1)
    tri = jnp.where(col <= row, 1.0, 0.0).astype(bf16)
    c = _dot_exact_ones(log_f, tri, nt=True) + carry_ref[0:1, :]
    c_ref[...] = c
    carry_ref[0:1, :] = c[tb - 1:tb, :]


def _gate_cumsum(small, bias_row, batch, seq):
    tb = 512
    nt = seq // tb
    return pl.pallas_call(
        functools.partial(_gate_kernel, tb=tb),
        grid=(batch, nt),
        in_specs=[
            pl.BlockSpec((tb, LANES), lambda b, t: (b * nt + t, 1)),
            pl.BlockSpec((1, LANES), lambda b, t: (0, 0)),
        ],
        out_specs=pl.BlockSpec((tb, LANES), lambda b, t: (b * nt + t, 0)),
        out_shape=jax.ShapeDtypeStruct((batch * seq, LANES), f32),
        scratch_shapes=[pltpu.VMEM((SUBLANES, LANES), f32)],
        compiler_params=pltpu.CompilerParams(
            dimension_semantics=("parallel", "arbitrary")),
        name="fox_gate_cumsum",
    )(small, bias_row)


LOG2E = 1.4426950408889634
FOX_TK = 256
FOX_GW = 256
FOX_BATCH = 4


def _fox_kernel(q_ref, k_ref, v_ref, c_ref, z_ref, o_ref, crep_ref, acc_ref, vt_ref, *, seq, tk, gw):
    h = pl.program_id(1)
    ng = seq // gw
    nt = seq // tk
    assert seq % gw == 0 and seq % tk == 0 and (tk % gw == 0 or gw % tk == 0)
    mm = lambda a, b: jnp.dot(a, b, preferred_element_type=f32)
    mm_nt = lambda a, b: lax.dot_general(a, b, (((1,), (1,)), ((), ())), preferred_element_type=f32)

    lane = lax.broadcasted_iota(jnp.int32, (1, LANES), 1)
    onehot = jnp.where(lane == F_LANE + h, LOG2E, 0.0)
    crep_ref[...] = jnp.sum(c_ref[...] * onehot, axis=1, keepdims=True)
    for t in range(nt):
        vt_ref[t, 0:LANES, :] = v_ref[t * tk:(t + 1) * tk, :].T
        vt_ref[t, LANES:LANES + SUBLANES, :] = jnp.ones((SUBLANES, tk), bf16)

    key_rel = lax.broadcasted_iota(jnp.int32, (tk, gw), 0)
    qry_rel = lax.broadcasted_iota(jnp.int32, (tk, gw), 1)

    def visible(t, g):
        return t * tk <= g * gw + gw - 1

    def needs_mask(t, g):
        return t * tk + tk - 1 > g * gw

    last_tile = [max(t for t in range(nt) if visible(t, g)) for g in range(ng)]
    sched = []
    for t in range(nt):
        groups = [g for g in range(ng) if visible(t, g)]
        sched += [[(t, g) for g in groups[k:k + FOX_BATCH]] for k in range(0, len(groups), FOX_BATCH)]

    def qk(batch):
        return [mm_nt(k_ref[t * tk:(t + 1) * tk, :], q_ref[g * gw:(g + 1) * gw, :]) for t, g in batch]

    m = [None] * ng
    l = [None] * ng

    def finish(batch, alpha, pv):
        for (t, g), a, xs in zip(batch, alpha, pv):
            x, p_sum = xs[0:LANES], xs[LANES:LANES + 1]
            acc_ref[g] = x if t == 0 else a * acc_ref[g] + x
            l[g] = p_sum if t == 0 else a * l[g] + p_sum
            if t == last_tile[g]:
                rows = slice(g * gw, (g + 1) * gw)
                z = z_ref[rows, :]
                o = (acc_ref[g] / l[g]).T
                o_ref[rows, :] = (o * (z * _sigmoid(z))).astype(o_ref.dtype)

    s_next = qk(sched[0])
    pending = None
    for bi, batch in enumerate(sched):
        s_cur = s_next
        if bi + 1 < len(sched):
            s_next = qk(sched[bi + 1])
        alpha, p16 = [], []
        for (t, g), s in zip(batch, s_cur):
            s = s - crep_ref[t * tk:(t + 1) * tk, :]
            if needs_mask(t, g):
                s = jnp.where(key_rel + t * tk > qry_rel + g * gw, -jnp.inf, s)
            m_tile = jnp.max(s, axis=0, keepdims=True)
            if t == 0:
                m_new, a = m_tile, None
            else:
                m_new = jnp.maximum(m[g], m_tile)
                a = jnp.exp2(m[g] - m_new)
            m[g] = m_new
            alpha.append(a)
            p16.append(jnp.exp2((s - m_new).astype(bf16)))
        pv = [mm(vt_ref[t], p) for (t, g), p in zip(batch, p16)]
        if pending is not None:
            finish(*pending)
        pending = (batch, alpha, pv)
    finish(*pending)


def _fox_attention(q3, kv3, c3, wide3, batch, seq):
    tk, gw = FOX_TK, FOX_GW
    h8 = FOX_HEADS
    return pl.pallas_call(
        functools.partial(_fox_kernel, seq=seq, tk=tk, gw=gw),
        grid=(batch, h8),
        in_specs=[
            pl.BlockSpec((None, seq, LANES), lambda b, h: (b, 0, h)),
            pl.BlockSpec((None, seq, LANES), lambda b, h: (b, 0, h)),
            pl.BlockSpec((None, seq, LANES), lambda b, h: (b, 0, h8 + h)),
            pl.BlockSpec((None, seq, LANES), lambda b, h: (b, 0, 0)),
            pl.BlockSpec((None, seq, LANES), lambda b, h: (b, 0, COL_ZA + h)),
        ],
        out_specs=pl.BlockSpec((None, seq, LANES), lambda b, h: (b, 0, h)),
        out_shape=jax.ShapeDtypeStruct((batch, seq, FOX_WIDTH), bf16),
        scratch_shapes=[
            pltpu.VMEM((seq, 1), f32),
            pltpu.VMEM((seq // gw, LANES, gw), f32),
            pltpu.VMEM((seq // tk, LANES + SUBLANES, tk), bf16),
        ],
        compiler_params=pltpu.CompilerParams(
            dimension_semantics=("parallel", "arbitrary"),
            vmem_limit_bytes=VMEM_LIMIT),
        name="fox_attention",
    )(q3, kv3, kv3, c3, wide3)


P_MU_R, P_MU_K, P_MU_V, P_MU_Z, P_W0, P_A0, P_KK, P_KA, P_RK, P_LNW, P_LNB = range(11)
P_ROWS = 16
K_XA, K_XR, K_YB, K_YK, K_VS, K_BH, K_KH = range(7)


def _spread(emitters, lo=0.0, hi=1.0):
    n = len(emitters)
    return [(lo + (hi - lo) * (i + 0.5) / n, e) for i, e in enumerate(emitters)]


def _interleave(*segment_lists):
    keyed = [(pos, prio, seg) for prio, segs in enumerate(segment_lists) for pos, seg in segs]
    keyed.sort(key=lambda x: (x[0], x[1]))
    for _, _, seg in keyed:
        seg()


def _rwkv_kernel(r_ref, k_ref, v_ref, z_ref, wd_ref, ad_ref, pc_ref, ps_ref, w2_ref, a2_ref, pc3_ref,
                 o_ref,
                 state_ref, prev_ref, stk_ref, xr32_ref, aux1_ref, dec1_ref, p16_ref, p32_ref,
                 aux2_ref, dec2_ref, *, tb, nt):
    C = RWKV_CHUNK
    G = GROUP_ROWS
    N = RWKV_HEAD_DIM
    NC = tb // C
    s = pl.program_id(0)
    first_of_seq_1 = (s % nt) == 0
    first_of_seq_3 = ((s + 2 * nt - 2) % nt) == 0

    @pl.when(s == 0)
    def _():
        for ref in (state_ref, prev_ref, stk_ref, xr32_ref, aux1_ref, dec1_ref, p16_ref, p32_ref,
                    aux2_ref, dec2_ref):
            ref[...] = jnp.zeros_like(ref)

    mm = lambda a, b: jnp.dot(a, b, preferred_element_type=f32)
    mm_nt = lambda a, b: lax.dot_general(a, b, (((1,), (1,)), ((), ())), preferred_element_type=f32)
    mm_tn = lambda a, b: lax.dot_general(a, b, (((0,), (0,)), ((), ())), preferred_element_type=f32)

    def body(cur, prv):
        lane_t = lax.broadcasted_iota(jnp.int32, (1, LANES), 1)
        head_masks = [jnp.where(lane_t // N == h, 1.0, 0.0) for h in range(HEADS_PER_GROUP)]

        def head_sum(x):
            sums = [jnp.sum(x * hm, axis=1, keepdims=True) for hm in head_masks]
            out = sums[HEADS_PER_GROUP - 1]
            for hh in range(HEADS_PER_GROUP - 2, -1, -1):
                out = jnp.where(lane_t // N == hh, sums[hh], out)
            return out

        stack = lambda x: jnp.concatenate([x * hm for hm in head_masks], axis=0)
        lane_c = lax.broadcasted_iota(jnp.int32, (C, LANES), 1)
        head_sel = [lane_c // N == h for h in range(HEADS_PER_GROUP)]

        def stack16(x):
            xb = x.astype(bf16)
            return jnp.concatenate([jnp.where(m, xb, jnp.zeros_like(xb)) for m in head_sel], axis=0)

        gi = lax.broadcasted_iota(jnp.int32, (G, G), 0)
        gj = lax.broadcasted_iota(jnp.int32, (G, G), 1)
        same_head = gi // C == gj // C
        strict = jnp.logical_and(same_head, gj < gi)
        incl = jnp.logical_and(same_head, gj <= gi)
        eye = jnp.where(gi == gj, 1.0, 0.0)
        chunks = range(NC)

        v1 = {}
        prm = lambda idx: pc_ref[idx:idx + 1, :]
        row_in_block = lax.broadcasted_iota(jnp.int32, (tb, LANES), 0)

        def shifted(ref, slot, mu):
            u = ref[...]
            carry = jnp.where(first_of_seq_1, 0.0, prev_ref[slot, 0:1, :])
            prev = jnp.where(row_in_block == 0, carry, pltpu.roll(u, 1, 0))
            prev_ref[slot, 0:1, :] = u[tb - 1:tb, :]
            return u + (prev - u) * mu

        def s1_lora():
            v1["wd"] = shifted(wd_ref, 4, ps_ref[0:1, :])
            v1["ad"] = shifted(ad_ref, 5, ps_ref[1:2, :])
            v1["w_lin"] = _dot(jnp.tanh(v1["wd"]), w2_ref[...])
            v1["a_lin"] = _dot(v1["ad"], a2_ref[...])

        def s1_key_norm():
            v1["kr"] = shifted(k_ref, 1, prm(P_MU_K))
            kk = v1["kr"] * prm(P_KK)
            v1["kk_raw"] = kk
            v1["kk_ss"] = head_sum(kk * kk)

        def s1_decay():
            w = -_softplus(-(prm(P_W0) + v1["w_lin"])) - 0.5
            v1["log_decay"] = -jnp.exp(w)
            ti = lax.broadcasted_iota(jnp.int32, (2 * C, 2 * C), 0)
            tj = lax.broadcasted_iota(jnp.int32, (2 * C, 2 * C), 1)
            pair_tri = jnp.where(jnp.logical_and(tj <= ti, ti // C == tj // C), 1.0, 0.0).astype(bf16)
            v1["ci"] = jnp.concatenate(
                [_dot_exact_ones(v1["log_decay"][k:k + 2 * C], pair_tri, nt=True)
                 for k in range(0, tb, 2 * C)], axis=0)

        def s1_bonus():
            v1["rate"] = _sigmoid(prm(P_A0) + v1["a_lin"])
            v1["r"] = shifted(r_ref, 0, prm(P_MU_R))
            v1["vr"] = shifted(v_ref, 2, prm(P_MU_V))
            v1["kp"] = v1["kr"] * (1.0 + (v1["rate"] - 1.0) * prm(P_KA))
            aux1_ref[cur, 0] = head_sum(v1["r"] * v1["kp"] * prm(P_RK)) * v1["vr"]
            zb = shifted(z_ref, 3, prm(P_MU_Z))
            aux1_ref[cur, 1] = zb * _sigmoid(zb)

        def s1_scale():
            kk = v1["kk_raw"] / jnp.maximum(jnp.sqrt(v1["kk_ss"]), L2_EPS)
            v1["bb"] = kk * v1["rate"]
            ci = v1["ci"]
            v1["a_t"] = -kk * jnp.exp(ci - v1["log_decay"])
            v1["r_t"] = v1["r"] * jnp.exp(ci)
            inv = jnp.exp(-ci)
            v1["b_t"] = v1["bb"] * inv
            v1["k_t"] = v1["kp"] * inv

        def s1_stack(c):
            def emit():
                sl = slice(c * C, (c + 1) * C)
                ci_c = v1["ci"][sl]
                c_last = ci_c[C - 1:C, :]
                to_end = jnp.exp(c_last - ci_c)
                dec1_ref[cur, c] = jnp.broadcast_to(jnp.exp(c_last), (SUBLANES, LANES))
                xr = stack(v1["r_t"][sl])
                xr32_ref[cur, c] = xr
                stk_ref[cur, K_XR, c] = xr.astype(bf16)
                stk_ref[cur, K_XA, c] = stack16(v1["a_t"][sl])
                stk_ref[cur, K_YB, c] = stack16(v1["b_t"][sl])
                stk_ref[cur, K_YK, c] = stack16(v1["k_t"][sl])
                stk_ref[cur, K_VS, c] = stack16(v1["vr"][sl])
                stk_ref[cur, K_BH, c] = stack16(v1["bb"][sl] * to_end)
                stk_ref[cur, K_KH, c] = stack16(v1["kp"][sl] * to_end)
            return emit

        segs1 = ([(0.0, s1_lora), (0.02, s1_key_norm), (0.05, s1_decay), (0.08, s1_bonus)]
                 + _spread([s1_scale] + [s1_stack(c) for c in chunks], RWKV_TAIL_START, 1.0))

        v2 = {}
        ld = lambda kind, c: stk_ref[prv, kind, c]

        def s2_big():
            big = [mm_nt(jnp.concatenate([ld(K_XA, c), ld(K_XR, c)], axis=0),
                         jnp.concatenate([ld(K_YB, c), ld(K_YK, c)], axis=0)) for c in chunks]
            v2["a_ab"] = [jnp.where(strict, big[c][0:G, 0:G], 0.0) for c in chunks]
            v2["a_ak"] = [jnp.where(strict, big[c][0:G, G:2 * G], 0.0).astype(bf16) for c in chunks]
            v2["a_rb"] = [jnp.where(incl, big[c][G:2 * G, 0:G], 0.0).astype(bf16) for c in chunks]
            v2["a_rk"] = [jnp.where(incl, big[c][G:2 * G, G:2 * G], 0.0).astype(bf16) for c in chunks]
            v2["tinv"] = [eye + a for a in v2["a_ab"]]
            v2["pw"] = [a.astype(bf16) for a in v2["a_ab"]]

        def s2_first_square():
            v2["pw"] = [mm(p, p).astype(bf16) for p in v2["pw"]]

        def s2_level(last):
            def emit():
                for c in chunks:
                    pw = v2["pw"][c]
                    t16 = v2["tinv"][c].astype(bf16)
                    if last:
                        v2["tinv"][c] = v2["tinv"][c] + mm(t16, pw)
                    else:
                        both = mm(jnp.concatenate([t16, pw], axis=0), pw)
                        v2["tinv"][c] = v2["tinv"][c] + both[0:G]
                        v2["pw"][c] = both[G:2 * G].astype(bf16)
            return emit

        def s2_av():
            v2["tinv"] = [t.astype(bf16) for t in v2["tinv"]]
            av = [mm(jnp.concatenate([v2["a_ak"][c], v2["a_rk"][c]], axis=0), ld(K_VS, c)) for c in chunks]
            v2["akv"] = [x[0:G].astype(bf16) for x in av]
            v2["arkv"] = [x[G:2 * G] for x in av]

        def s2_wu():
            v2["wu"] = [mm(v2["tinv"][c], jnp.concatenate([ld(K_XA, c), v2["akv"][c]], axis=1)).astype(bf16)
                        for c in chunks]

        def s2_ry():
            for c in chunks:
                e = mm(v2["a_rb"][c], v2["wu"][c])
                p16_ref[cur, 0, c] = (xr32_ref[prv, c] + e[:, 0:LANES]).astype(bf16)
                p32_ref[cur, 0, c] = e[:, LANES:] + v2["arkv"][c]

        def s2_pm():
            for c in chunks:
                p16_ref[cur, 1, c] = mm_tn(v2["wu"][c][:, 0:LANES], ld(K_BH, c)).astype(bf16)

        def s2_q():
            for c in chunks:
                p32_ref[cur, 1, c] = mm_tn(jnp.concatenate([v2["wu"][c][:, LANES:], ld(K_VS, c)], axis=0),
                                           jnp.concatenate([ld(K_BH, c), ld(K_KH, c)], axis=0))
            aux2_ref[cur] = aux1_ref[prv]
            dec2_ref[cur] = dec1_ref[prv]

        segs2 = _spread([s2_big, s2_first_square] + [s2_level(False)] * 4 + [s2_level(True)]
                        + [s2_av, s2_wu, s2_ry, s2_pm, s2_q])

        v3 = {"ys": []}
        prm3 = lambda idx: pc3_ref[idx:idx + 1, :]

        def s3_start():
            v3["state"] = jnp.where(first_of_seq_3, 0.0, state_ref[...])

        def s3_chunk(c):
            def emit():
                state = v3["state"]
                s16 = state.astype(bf16)
                y_st = mm_nt(p16_ref[prv, 0, c], s16) + p32_ref[prv, 0, c]
                v3["ys"].append(y_st[0:C] + y_st[C:2 * C])
                v3["state"] = (state * dec2_ref[prv, c, 0:1, :] + mm(s16, p16_ref[prv, 1, c])
                               + p32_ref[prv, 1, c])
            return emit

        def s3_finish():
            state_ref[...] = v3["state"]
            y = jnp.concatenate(v3["ys"], axis=0)
            mean = head_sum(y) * (1.0 / N)
            yc = y - mean
            var = head_sum(yc * yc) * (1.0 / N)
            yn = yc * lax.rsqrt(var + GN_EPS) * prm3(P_LNW) + prm3(P_LNB)
            o_ref[...] = ((yn + aux2_ref[prv, 0]) * aux2_ref[prv, 1]).astype(o_ref.dtype)

        segs3 = _spread([s3_start] + [s3_chunk(c) for c in chunks] + [s3_finish], 0.0, RWKV_CHAIN_END)

        _interleave(segs3, segs2, segs1)

    for parity in (0, 1):
        @pl.when(s % 2 == parity)
        def _(parity=parity):
            body(cur=parity, prv=1 - parity)


def _rwkv_mix(wide, small, pc, ps, w2p, a2p, batch, seq):
    tb = RWKV_BLOCK
    nt = seq // tb
    groups = RWKV_WIDTH // LANES
    nc = tb // RWKV_CHUNK
    n_blocks = batch * groups * nt

    def where(s):
        s = jnp.clip(s, 0, n_blocks - 1)
        return (s // (nt * groups)) * nt + s % nt, (s // nt) % groups

    col = lambda c0: (lambda s: (where(s)[0], c0 + where(s)[1]))
    fixed_col = lambda c: (lambda s: (where(s)[0], c))
    grp = lambda s: (0, where(s)[1])
    return pl.pallas_call(
        functools.partial(_rwkv_kernel, tb=tb, nt=nt),
        grid=(n_blocks + 2,),
        in_specs=[
            pl.BlockSpec((tb, LANES), col(COL_R)),
            pl.BlockSpec((tb, LANES), col(COL_K)),
            pl.BlockSpec((tb, LANES), col(COL_V)),
            pl.BlockSpec((tb, LANES), col(COL_ZB)),
            pl.BlockSpec((tb, LANES), fixed_col(0)),
            pl.BlockSpec((tb, LANES), fixed_col(1)),
            pl.BlockSpec((P_ROWS, LANES), grp),
            pl.BlockSpec((SUBLANES, LANES), lambda s: (0, 0)),
            pl.BlockSpec((LANES, LANES), grp),
            pl.BlockSpec((LANES, LANES), grp),
            pl.BlockSpec((P_ROWS, LANES), lambda s: grp(s - 2)),
        ],
        out_specs=pl.BlockSpec((tb, LANES), lambda s: (where(s - 2)[0], where(s - 2)[1])),
        out_shape=jax.ShapeDtypeStruct((batch * seq, RWKV_WIDTH), bf16),
        scratch_shapes=[
            pltpu.VMEM((LANES, LANES), f32),
            pltpu.VMEM((6, SUBLANES, LANES), f32),
            pltpu.VMEM((2, 7, nc, GROUP_ROWS, LANES), bf16),
            pltpu.VMEM((2, nc, GROUP_ROWS, LANES), f32),
            pltpu.VMEM((2, 2, tb, LANES), f32),
            pltpu.VMEM((2, nc, SUBLANES, LANES), f32),
            pltpu.VMEM((2, 2, nc, GROUP_ROWS, LANES), bf16),
            pltpu.VMEM((2, 2, nc, GROUP_ROWS, LANES), f32),
            pltpu.VMEM((2, 2, tb, LANES), f32),
            pltpu.VMEM((2, nc, SUBLANES, LANES), f32),
        ],
        compiler_params=pltpu.CompilerParams(
            dimension_semantics=("arbitrary",),
            vmem_limit_bytes=VMEM_LIMIT),
        name="rwkv_mix",
    )(wide, wide, wide, wide, small, small, pc, ps, w2p, a2p, pc)


def _out_kernel(oa_ref, ob_ref, ga_ref, gb_ref, x_ref, wpf_ref, wpr_ref, wo_ref, g_ref, o_ref):
    pa = jnp.dot(oa_ref[...], wpf_ref[...], preferred_element_type=f32)
    pb = jnp.dot(ob_ref[...], wpr_ref[...], preferred_element_type=f32)
    m = _sigmoid(ga_ref[...]) * pa + _sigmoid(gb_ref[...]) * pb
    z = x_ref[...] + jnp.dot(m.astype(bf16), wo_ref[...], preferred_element_type=f32)
    ms = jnp.mean(z * z, axis=-1, keepdims=True)
    o_ref[...] = z * lax.rsqrt(ms + RMS_EPS) * g_ref[...]


def _merge_out(oa, ob, wide, x2d, wpf, wpr, wo, gain):
    m, d = x2d.shape
    tm = 256
    resident = lambda shape: pl.BlockSpec(shape, lambda i: (0, 0), pipeline_mode=pl.Buffered(1))
    return pl.pallas_call(
        _out_kernel,
        grid=(m // tm,),
        in_specs=[
            pl.BlockSpec((tm, FOX_WIDTH), lambda i: (i, 0)),
            pl.BlockSpec((tm, RWKV_WIDTH), lambda i: (i, 0)),
            pl.BlockSpec((tm, d), lambda i: (i, COL_GA * LANES // d)),
            pl.BlockSpec((tm, d), lambda i: (i, COL_GB * LANES // d)),
            pl.BlockSpec((tm, d), lambda i: (i, 0)),
            resident((FOX_WIDTH, d)),
            resident((RWKV_WIDTH, d)),
            resident((d, d)),
            resident((1, d)),
        ],
        out_specs=pl.BlockSpec((tm, d), lambda i: (i, 0)),
        out_shape=jax.ShapeDtypeStruct((m, d), f32),
        compiler_params=pltpu.CompilerParams(
            dimension_semantics=("parallel",),
            vmem_limit_bytes=VMEM_LIMIT),
        name="merge_out",
    )(oa, ob, wide, wide, x2d, wpf, wpr, wo, gain)


def _branches(x2d, batch, seq, norm_gain, w_in, fox_forget_bias, rwkv_shift_mix, rwkv_w0, rwkv_w2,
              rwkv_a0, rwkv_a2, rwkv_k_k, rwkv_k_a, rwkv_r_k, rwkv_ln_w, rwkv_ln_b):
    fw, rw = FOX_WIDTH, RWKV_WIDTH
    w_t = w_in.T
    r_qkv = 0
    r_za = r_qkv + 3 * fw
    r_f = r_za + fw
    r_rkvz = r_f + FOX_HEADS
    r_wd = r_rkvz + 4 * rw
    r_ad = r_wd + LORA
    r_g = r_ad + LORA
    pieces = [(r_qkv + fw, 2 * fw), (r_g, 2 * D_MODEL), (r_za, fw), (r_rkvz, 4 * rw)]
    pad_rows = lambda a: jnp.pad(a, ((0, LANES - a.shape[0]), (0, 0)))
    w_tail = jnp.concatenate(
        [pad_rows(w_t[r_wd:r_wd + LORA]),
         pad_rows(jnp.concatenate([w_t[r_ad:r_ad + LORA], w_t[r_f:r_f + FOX_HEADS]], axis=0))], axis=0)
    h2d, q = _norm_q(x2d, norm_gain.reshape(1, D_MODEL), w_t, FOX_HEAD_DIM ** -0.5 * LOG2E)
    kv, wide, small = _in_proj(h2d, w_t, pieces, w_tail)

    bias_row = jnp.pad(fox_forget_bias.reshape(1, FOX_HEADS), ((0, 0), (F_LANE, LANES - F_LANE - FOX_HEADS)))
    c = _gate_cumsum(small, bias_row, batch, seq)
    oa = _fox_attention(q.reshape(batch, seq, fw), kv.reshape(batch, seq, 2 * fw), c.reshape(batch, seq, LANES), wide.reshape(batch, seq, WIDE_COLS),
                        batch, seq).reshape(batch * seq, fw)

    mu = rwkv_shift_mix
    pc = jnp.zeros((P_ROWS, rw), f32)
    rows = [mu[0:rw], mu[rw:2 * rw], mu[2 * rw:3 * rw], mu[3 * rw:4 * rw], rwkv_w0, rwkv_a0,
            rwkv_k_k, rwkv_k_a, rwkv_r_k.reshape(rw), rwkv_ln_w, rwkv_ln_b]
    pc = pc.at[:len(rows)].set(jnp.stack(rows))
    ps = jnp.zeros((SUBLANES, LANES), f32)
    ps = ps.at[0, :LORA].set(mu[4 * rw:4 * rw + LORA]).at[1, :LORA].set(mu[4 * rw + LORA:])
    w2p = jnp.pad(rwkv_w2, ((0, LANES - LORA), (0, 0))).astype(bf16)
    a2p = jnp.pad(rwkv_a2, ((0, LANES - LORA), (0, 0))).astype(bf16)
    ob = _rwkv_mix(wide, small, pc, ps, w2p, a2p, batch, seq)

    return oa, ob, wide


def kernel(x, norm_gain, w_in, fox_forget_bias, rwkv_shift_mix, rwkv_w0, rwkv_w2, rwkv_a0, rwkv_a2, rwkv_k_k, rwkv_k_a, rwkv_r_k, rwkv_ln_w, rwkv_ln_b, w_proj_fox, w_proj_rwkv, w_out, final_norm_gain):
    batch, seq, d = x.shape
    depth = norm_gain.shape[0]
    assert depth == 1, "the final rmsnorm is fused into the single layer's output kernel"
    x2d = x.reshape(batch * seq, d)
    oa, ob, wide = _branches(x2d, batch, seq, norm_gain[0], w_in[0], fox_forget_bias[0],
                             rwkv_shift_mix[0], rwkv_w0[0], rwkv_w2[0], rwkv_a0[0], rwkv_a2[0],
                             rwkv_k_k[0], rwkv_k_a[0], rwkv_r_k[0], rwkv_ln_w[0], rwkv_ln_b[0])
    out = _merge_out(oa, ob, wide, x2d, w_proj_fox[0].astype(bf16), w_proj_rwkv[0].astype(bf16),
                     w_out[0].astype(bf16), final_norm_gain.reshape(1, d))
    return out.reshape(batch, seq, d)
```

```python
import functools

import jax
import jax.numpy as jnp
from jax import lax
from jax.experimental import pallas as pl
from jax.experimental.pallas import tpu as pltpu

D_MODEL = 2048
FOX_HEADS = 8
FOX_HEAD_DIM = 128
FOX_WIDTH = FOX_HEADS * FOX_HEAD_DIM
RWKV_HEADS = 16
RWKV_HEAD_DIM = 64
RWKV_WIDTH = RWKV_HEADS * RWKV_HEAD_DIM
LORA = 96
RMS_EPS = 1e-6
GN_EPS = 64e-5
L2_EPS = 1e-12

LANES = 128
SUBLANES = 8
VMEM_LIMIT = 56 * 1024 * 1024

COL_GA = 0
COL_GB = D_MODEL // LANES
COL_ZA = 2 * D_MODEL // LANES
COL_R = COL_ZA + FOX_WIDTH // LANES
COL_K = COL_R + RWKV_WIDTH // LANES
COL_V = COL_K + RWKV_WIDTH // LANES
COL_ZB = COL_V + RWKV_WIDTH // LANES
WIDE_COLS = (COL_ZB + RWKV_WIDTH // LANES) * LANES
SMALL_COLS = 2 * LANES
F_LANE = LORA

RWKV_CHUNK = 64
RWKV_BLOCK = 512
RWKV_TAIL_START = 0.3
RWKV_CHAIN_END = 0.8
HEADS_PER_GROUP = LANES // RWKV_HEAD_DIM
GROUP_ROWS = HEADS_PER_GROUP * RWKV_CHUNK

f32 = jnp.float32
bf16 = jnp.bfloat16


def _dot(a, b):
    return jnp.dot(a.astype(bf16), b.astype(bf16), preferred_element_type=f32)


def _dot_exact_ones(a, ones_bf16, nt=False, terms=3):
    if nt:
        mm = lambda x: lax.dot_general(ones_bf16, x, (((1,), (0,)), ((), ())),
                                       preferred_element_type=f32)
    else:
        mm = lambda x: jnp.dot(x, ones_bf16, preferred_element_type=f32)
    part = a.astype(bf16)
    out = mm(part)
    rest = a
    for _ in range(terms - 1):
        rest = rest - part.astype(f32)
        part = rest.astype(bf16)
        out = out + mm(part)
    return out


def _softplus(x):
    return jnp.maximum(x, 0.0) + jnp.log(1.0 + jnp.exp(-jnp.abs(x)))


def _sigmoid(x):
    return 1.0 / (1.0 + jnp.exp(-x))


PROJ_TM = 1024
PROJ_TN = 1024


def _norm_q_kernel(x_ref, g_ref, w_ref, h_ref, q_ref, w16_ref, *, scale):
    @pl.when(pl.program_id(0) == 0)
    def _():
        w16_ref[...] = (w_ref[...] * scale).astype(bf16)

    xf = x_ref[...]
    ms = jnp.mean(xf * xf, axis=-1, keepdims=True)
    h = (xf * lax.rsqrt(ms + RMS_EPS) * g_ref[...]).astype(bf16)
    h_ref[...] = h
    q_ref[...] = lax.dot_general(h, w16_ref[...], (((1,), (1,)), ((), ())),
                                 preferred_element_type=f32).astype(q_ref.dtype)


def _norm_q(x2d, gain, w_t, scale):
    m, d = x2d.shape
    tm = PROJ_TM
    return pl.pallas_call(
        functools.partial(_norm_q_kernel, scale=scale),
        grid=(m // tm,),
        in_specs=[
            pl.BlockSpec((tm, d), lambda i: (i, 0)),
            pl.BlockSpec((1, d), lambda i: (0, 0)),
            pl.BlockSpec((FOX_WIDTH, d), lambda i: (0, 0)),
        ],
        out_specs=[pl.BlockSpec((tm, d), lambda i: (i, 0)), pl.BlockSpec((tm, FOX_WIDTH), lambda i: (i, 0))],
        out_shape=[jax.ShapeDtypeStruct((m, d), bf16), jax.ShapeDtypeStruct((m, FOX_WIDTH), bf16)],
        scratch_shapes=[pltpu.VMEM((FOX_WIDTH, d), bf16)],
        compiler_params=pltpu.CompilerParams(dimension_semantics=("arbitrary",), vmem_limit_bytes=VMEM_LIMIT),
        name="norm_q",
    )(x2d, gain, w_t)


def _in_proj_kernel(h_ref, w_ref, wt_ref, kv_ref, wide_ref, small_ref, w16_ref, *, n_kv, n_wide):
    j = pl.program_id(0)
    i = pl.program_id(1)
    n_main = n_kv + n_wide

    @pl.when(i == 0)
    def _():
        @pl.when(j < n_main)
        def _():
            w16_ref[...] = w_ref[...].astype(bf16)

        @pl.when(j == n_main)
        def _():
            w16_ref[0:SMALL_COLS, :] = wt_ref[...].astype(bf16)

    project = lambda w: lax.dot_general(h_ref[...], w, (((1,), (1,)), ((), ())), preferred_element_type=f32)

    @pl.when(j < n_kv)
    def _():
        kv_ref[...] = project(w16_ref[...]).astype(kv_ref.dtype)

    @pl.when(jnp.logical_and(j >= n_kv, j < n_main))
    def _():
        wide_ref[...] = project(w16_ref[...])

    @pl.when(j == n_main)
    def _():
        small_ref[...] = project(w16_ref[0:SMALL_COLS, :])


def _in_proj(h2d, w_t, pieces, w_tail):
    m, d = h2d.shape
    tm, tn = PROJ_TM, PROJ_TN
    n_kv = 2 * FOX_WIDTH // tn
    n_wide = WIDE_COLS // tn
    n_main = n_kv + n_wide
    n_i = m // tm
    assert n_kv * tn == 2 * FOX_WIDTH and n_wide * tn == WIDE_COLS and w_tail.shape[0] == SMALL_COLS
    assert all(n % tn == 0 and src % SUBLANES == 0 for src, n in pieces)
    assert sum(n for _, n in pieces) == n_main * tn
    starts, first_tile = [], 0
    for src, n in pieces:
        starts.append((first_tile, src))
        first_tile += n // tn

    def src_row(j):
        tile = jnp.int32(0)
        for blk, src in starts:
            tile = jnp.where(j >= blk, src // SUBLANES + (jnp.minimum(j, n_main - 1) - blk) * (tn // SUBLANES), tile)
        return pl.multiple_of(tile * SUBLANES, SUBLANES)

    kv_row = lambda j, i: jnp.where(j < n_kv, i, n_i - 1)
    wide_row = lambda j, i: jnp.where(j < n_kv, 0, jnp.where(j < n_main, i, n_i - 1))
    small_row = lambda j, i: jnp.where(j == n_main, i, 0)
    return pl.pallas_call(
        functools.partial(_in_proj_kernel, n_kv=n_kv, n_wide=n_wide),
        grid=(n_main + 1, n_i),
        in_specs=[
            pl.BlockSpec((tm, d), lambda j, i: (i, 0)),
            pl.BlockSpec((pl.Element(tn), pl.Element(d)), lambda j, i: (src_row(j), 0)),
            pl.BlockSpec((SMALL_COLS, d), lambda j, i: (0, 0)),
        ],
        out_specs=[
            pl.BlockSpec((tm, tn), lambda j, i: (kv_row(j, i), jnp.minimum(j, n_kv - 1))),
            pl.BlockSpec((tm, tn), lambda j, i: (wide_row(j, i), jnp.clip(j - n_kv, 0, n_wide - 1))),
            pl.BlockSpec((tm, SMALL_COLS), lambda j, i: (small_row(j, i), 0)),
        ],
        out_shape=[
            jax.ShapeDtypeStruct((m, 2 * FOX_WIDTH), bf16),
            jax.ShapeDtypeStruct((m, WIDE_COLS), f32),
            jax.ShapeDtypeStruct((m, SMALL_COLS), f32),
        ],
        scratch_shapes=[pltpu.VMEM((tn, d), bf16)],
        compiler_params=pltpu.CompilerParams(
            dimension_semantics=("arbitrary", "arbitrary"),
            vmem_limit_bytes=VMEM_LIMIT),
        name="in_proj",
    )(h2d, w_t, w_tail)


def _gate_kernel(f_ref, bias_ref, c_ref, carry_ref, *, tb):
    @pl.when(pl.program_id(1) == 0)
    def _():
        carry_ref[...] = jnp.zeros_like(carry_ref)

    z = f_ref[...] + bias_ref[...]
    log_f = -_softplus(-z)
    row = lax.broadcasted_iota(jnp.int32, (tb, tb), 0)
    col = lax.broadcasted_iota(jnp.int32, (tb, tb), 1)
    tri = jnp.where(col <= row, 1.0, 0.0).astype(bf16)
    c = _dot_exact_ones(log_f, tri, nt=True) + carry_ref[0:1, :]
    c_ref[...] = c
    carry_ref[0:1, :] = c[tb - 1:tb, :]


def _gate_cumsum(small, bias_row, batch, seq):
    tb = 512
    nt = seq // tb
    return pl.pallas_call(
        functools.partial(_gate_kernel, tb=tb),
        grid=(batch, nt),
        in_specs=[
            pl.BlockSpec((tb, LANES), lambda b, t: (b * nt + t, 1)),
            pl.BlockSpec((1, LANES), lambda b, t: (0, 0)),
        ],
        out_specs=pl.BlockSpec((tb, LANES), lambda b, t: (b * nt + t, 0)),
        out_shape=jax.ShapeDtypeStruct((batch * seq, LANES), f32),
        scratch_shapes=[pltpu.VMEM((SUBLANES, LANES), f32)],
        compiler_params=pltpu.CompilerParams(
            dimension_semantics=("parallel", "arbitrary")),
        name="fox_gate_cumsum",
    )(small, bias_row)


LOG2E = 1.4426950408889634
FOX_TK = 256
FOX_GW = 256
FOX_BATCH = 4


def _fox_kernel(q_ref, k_ref, v_ref, c_ref, z_ref, wf_ref, wr_ref, wo_ref, o_ref, wf16_ref, wr16_ref, wo16_ref,
                crep_ref, acc_ref, vt_ref, *, seq, tk, gw):
    wf16_ref[...] = wf_ref[...].astype(bf16)
    wr16_ref[...] = wr_ref[...].astype(bf16)
    wo16_ref[...] = wo_ref[...].astype(bf16)

    h = pl.program_id(1)
    ng = seq // gw
    nt = seq // tk
    assert seq % gw == 0 and seq % tk == 0 and (tk % gw == 0 or gw % tk == 0)
    mm = lambda a, b: jnp.dot(a, b, preferred_element_type=f32)
    mm_nt = lambda a, b: lax.dot_general(a, b, (((1,), (1,)), ((), ())), preferred_element_type=f32)

    lane = lax.broadcasted_iota(jnp.int32, (1, LANES), 1)
    onehot = jnp.where(lane == F_LANE + h, LOG2E, 0.0)
    crep_ref[...] = jnp.sum(c_ref[...] * onehot, axis=1, keepdims=True)
    for t in range(nt):
        vt_ref[t, 0:LANES, :] = v_ref[t * tk:(t + 1) * tk, :].T
        vt_ref[t, LANES:LANES + SUBLANES, :] = jnp.ones((SUBLANES, tk), bf16)

    key_rel = lax.broadcasted_iota(jnp.int32, (tk, gw), 0)
    qry_rel = lax.broadcasted_iota(jnp.int32, (tk, gw), 1)

    def visible(t, g):
        return t * tk <= g * gw + gw - 1

    def needs_mask(t, g):
        return t * tk + tk - 1 > g * gw

    last_tile = [max(t for t in range(nt) if visible(t, g)) for g in range(ng)]
    sched = []
    for t in range(nt):
        groups = [g for g in range(ng) if visible(t, g)]
        sched += [[(t, g) for g in groups[k:k + FOX_BATCH]] for k in range(0, len(groups), FOX_BATCH)]

    def qk(batch):
        return [mm_nt(k_ref[t * tk:(t + 1) * tk, :], q_ref[g * gw:(g + 1) * gw, :]) for t, g in batch]

    m = [None] * ng
    l = [None] * ng

    def finish(batch, alpha, pv):
        for (t, g), a, xs in zip(batch, alpha, pv):
            x, p_sum = xs[0:LANES], xs[LANES:LANES + 1]
            acc_ref[g] = x if t == 0 else a * acc_ref[g] + x
            l[g] = p_sum if t == 0 else a * l[g] + p_sum
            if t == last_tile[g]:
                rows = slice(g * gw, (g + 1) * gw)
                z = z_ref[rows, :]
                o = (acc_ref[g] / l[g]).T
                o_ref[rows, :] = (o * (z * _sigmoid(z))).astype(o_ref.dtype)

    s_next = qk(sched[0])
    pending = None
    for bi, batch in enumerate(sched):
        s_cur = s_next
        if bi + 1 < len(sched):
            s_next = qk(sched[bi + 1])
        alpha, p16 = [], []
        for (t, g), s in zip(batch, s_cur):
            s = s - crep_ref[t * tk:(t + 1) * tk, :]
            if needs_mask(t, g):
                s = jnp.where(key_rel + t * tk > qry_rel + g * gw, -jnp.inf, s)
            m_tile = jnp.max(s, axis=0, keepdims=True)
            if t == 0:
                m_new, a = m_tile, None
            else:
                m_new = jnp.maximum(m[g], m_tile)
                a = jnp.exp2(m[g] - m_new)
            m[g] = m_new
            alpha.append(a)
            p16.append(jnp.exp2((s - m_new).astype(bf16)))
        pv = [mm(vt_ref[t], p) for (t, g), p in zip(batch, p16)]
        if pending is not None:
            finish(*pending)
        pending = (batch, alpha, pv)
    finish(*pending)


def _fox_attention(q3, kv3, c3, wide3, w_proj_fox, w_proj_rwkv, w_out, batch, seq):
    tk, gw = FOX_TK, FOX_GW
    h8 = FOX_HEADS
    steps = batch * h8
    assert all(w.shape[0] % (steps * 2 * SUBLANES) == 0 for w in (w_proj_fox, w_proj_rwkv, w_out))
    slab = lambda w: pl.BlockSpec((w.shape[0] // steps, w.shape[1]), lambda b, h: (b * h8 + h, 0))
    return pl.pallas_call(
        functools.partial(_fox_kernel, seq=seq, tk=tk, gw=gw),
        grid=(batch, h8),
        in_specs=[
            pl.BlockSpec((None, seq, LANES), lambda b, h: (b, 0, h)),
            pl.BlockSpec((None, seq, LANES), lambda b, h: (b, 0, h)),
            pl.BlockSpec((None, seq, LANES), lambda b, h: (b, 0, h8 + h)),
            pl.BlockSpec((None, seq, LANES), lambda b, h: (b, 0, 0)),
            pl.BlockSpec((None, seq, LANES), lambda b, h: (b, 0, COL_ZA + h)),
            slab(w_proj_fox), slab(w_proj_rwkv), slab(w_out),
        ],
        out_specs=[pl.BlockSpec((None, seq, LANES), lambda b, h: (b, 0, h)),
                   slab(w_proj_fox), slab(w_proj_rwkv), slab(w_out)],
        out_shape=[jax.ShapeDtypeStruct((batch, seq, FOX_WIDTH), bf16),
                   jax.ShapeDtypeStruct(w_proj_fox.shape, bf16),
                   jax.ShapeDtypeStruct(w_proj_rwkv.shape, bf16),
                   jax.ShapeDtypeStruct(w_out.shape, bf16)],
        scratch_shapes=[
            pltpu.VMEM((seq, 1), f32),
            pltpu.VMEM((seq // gw, LANES, gw), f32),
            pltpu.VMEM((seq // tk, LANES + SUBLANES, tk), bf16),
        ],
        compiler_params=pltpu.CompilerParams(
            dimension_semantics=("parallel", "arbitrary"),
            vmem_limit_bytes=VMEM_LIMIT),
        name="fox_attention",
    )(q3, kv3, kv3, c3, wide3, w_proj_fox, w_proj_rwkv, w_out)


P_MU_R, P_MU_K, P_MU_V, P_MU_Z, P_W0, P_A0, P_KK, P_KA, P_RK, P_LNW, P_LNB = range(11)
P_ROWS = 16
K_XA, K_XR, K_YB, K_YK, K_VS, K_BH, K_KH = range(7)


def _spread(emitters, lo=0.0, hi=1.0):
    n = len(emitters)
    return [(lo + (hi - lo) * (i + 0.5) / n, e) for i, e in enumerate(emitters)]


def _interleave(*segment_lists):
    keyed = [(pos, prio, seg) for prio, segs in enumerate(segment_lists) for pos, seg in segs]
    keyed.sort(key=lambda x: (x[0], x[1]))
    for _, _, seg in keyed:
        seg()


def _rwkv_kernel(r_ref, k_ref, v_ref, z_ref, wd_ref, ad_ref, pc_ref, ps_ref, w2_ref, a2_ref, pc3_ref,
                 o_ref,
                 state_ref, prev_ref, stk_ref, xr32_ref, aux1_ref, dec1_ref, p16_ref, p32_ref,
                 aux2_ref, dec2_ref, *, tb, nt):
    C = RWKV_CHUNK
    G = GROUP_ROWS
    N = RWKV_HEAD_DIM
    NC = tb // C
    s = pl.program_id(0)
    first_of_seq_1 = (s % nt) == 0
    first_of_seq_3 = ((s + 2 * nt - 2) % nt) == 0

    @pl.when(s == 0)
    def _():
        for ref in (state_ref, prev_ref, stk_ref, xr32_ref, aux1_ref, dec1_ref, p16_ref, p32_ref,
                    aux2_ref, dec2_ref):
            ref[...] = jnp.zeros_like(ref)

    mm = lambda a, b: jnp.dot(a, b, preferred_element_type=f32)
    mm_nt = lambda a, b: lax.dot_general(a, b, (((1,), (1,)), ((), ())), preferred_element_type=f32)
    mm_tn = lambda a, b: lax.dot_general(a, b, (((0,), (0,)), ((), ())), preferred_element_type=f32)

    def body(cur, prv):
        lane_t = lax.broadcasted_iota(jnp.int32, (1, LANES), 1)
        head_masks = [jnp.where(lane_t // N == h, 1.0, 0.0) for h in range(HEADS_PER_GROUP)]

        def head_sum(x):
            sums = [jnp.sum(x * hm, axis=1, keepdims=True) for hm in head_masks]
            out = sums[HEADS_PER_GROUP - 1]
            for hh in range(HEADS_PER_GROUP - 2, -1, -1):
                out = jnp.where(lane_t // N == hh, sums[hh], out)
            return out

        stack = lambda x: jnp.concatenate([x * hm for hm in head_masks], axis=0)
        lane_c = lax.broadcasted_iota(jnp.int32, (C, LANES), 1)
        head_sel = [lane_c // N == h for h in range(HEADS_PER_GROUP)]

        def stack16(x):
            xb = x.astype(bf16)
            return jnp.concatenate([jnp.where(m, xb, jnp.zeros_like(xb)) for m in head_sel], axis=0)

        gi = lax.broadcasted_iota(jnp.int32, (G, G), 0)
        gj = lax.broadcasted_iota(jnp.int32, (G, G), 1)
        same_head = gi // C == gj // C
        strict = jnp.logical_and(same_head, gj < gi)
        incl = jnp.logical_and(same_head, gj <= gi)
        eye = jnp.where(gi == gj, 1.0, 0.0)
        chunks = range(NC)

        v1 = {}
        prm = lambda idx: pc_ref[idx:idx + 1, :]
        row_in_block = lax.broadcasted_iota(jnp.int32, (tb, LANES), 0)

        def shifted(ref, slot, mu):
            u = ref[...]
            carry = jnp.where(first_of_seq_1, 0.0, prev_ref[slot, 0:1, :])
            prev = jnp.where(row_in_block == 0, carry, pltpu.roll(u, 1, 0))
            prev_ref[slot, 0:1, :] = u[tb - 1:tb, :]
            return u + (prev - u) * mu

        def s1_lora():
            v1["wd"] = shifted(wd_ref, 4, ps_ref[0:1, :])
            v1["ad"] = shifted(ad_ref, 5, ps_ref[1:2, :])
            v1["w_lin"] = _dot(jnp.tanh(v1["wd"]), w2_ref[...])
            v1["a_lin"] = _dot(v1["ad"], a2_ref[...])

        def s1_key_norm():
            v1["kr"] = shifted(k_ref, 1, prm(P_MU_K))
            kk = v1["kr"] * prm(P_KK)
            v1["kk_raw"] = kk
            v1["kk_ss"] = head_sum(kk * kk)

        def s1_decay():
            w = -_softplus(-(prm(P_W0) + v1["w_lin"])) - 0.5
            v1["log_decay"] = -jnp.exp(w)
            ti = lax.broadcasted_iota(jnp.int32, (2 * C, 2 * C), 0)
            tj = lax.broadcasted_iota(jnp.int32, (2 * C, 2 * C), 1)
            pair_tri = jnp.where(jnp.logical_and(tj <= ti, ti // C == tj // C), 1.0, 0.0).astype(bf16)
            v1["ci"] = jnp.concatenate(
                [_dot_exact_ones(v1["log_decay"][k:k + 2 * C], pair_tri, nt=True)
                 for k in range(0, tb, 2 * C)], axis=0)

        def s1_bonus():
            v1["rate"] = _sigmoid(prm(P_A0) + v1["a_lin"])
            v1["r"] = shifted(r_ref, 0, prm(P_MU_R))
            v1["vr"] = shifted(v_ref, 2, prm(P_MU_V))
            v1["kp"] = v1["kr"] * (1.0 + (v1["rate"] - 1.0) * prm(P_KA))
            aux1_ref[cur, 0] = head_sum(v1["r"] * v1["kp"] * prm(P_RK)) * v1["vr"]
            zb = shifted(z_ref, 3, prm(P_MU_Z))
            aux1_ref[cur, 1] = zb * _sigmoid(zb)

        def s1_scale():
            kk = v1["kk_raw"] / jnp.maximum(jnp.sqrt(v1["kk_ss"]), L2_EPS)
            v1["bb"] = kk * v1["rate"]
            ci = v1["ci"]
            v1["a_t"] = -kk * jnp.exp(ci - v1["log_decay"])
            v1["r_t"] = v1["r"] * jnp.exp(ci)
            inv = jnp.exp(-ci)
            v1["b_t"] = v1["bb"] * inv
            v1["k_t"] = v1["kp"] * inv

        def s1_stack(c):
            def emit():
                sl = slice(c * C, (c + 1) * C)
                ci_c = v1["ci"][sl]
                c_last = ci_c[C - 1:C, :]
                to_end = jnp.exp(c_last - ci_c)
                dec1_ref[cur, c] = jnp.broadcast_to(jnp.exp(c_last), (SUBLANES, LANES))
                xr = stack(v1["r_t"][sl])
                xr32_ref[cur, c] = xr
                stk_ref[cur, K_XR, c] = xr.astype(bf16)
                stk_ref[cur, K_XA, c] = stack16(v1["a_t"][sl])
                stk_ref[cur, K_YB, c] = stack16(v1["b_t"][sl])
                stk_ref[cur, K_YK, c] = stack16(v1["k_t"][sl])
                stk_ref[cur, K_VS, c] = stack16(v1["vr"][sl])
                stk_ref[cur, K_BH, c] = stack16(v1["bb"][sl] * to_end)
                stk_ref[cur, K_KH, c] = stack16(v1["kp"][sl] * to_end)
            return emit

        segs1 = ([(0.0, s1_lora), (0.02, s1_key_norm), (0.05, s1_decay), (0.08, s1_bonus)]
                 + _spread([s1_scale] + [s1_stack(c) for c in chunks], RWKV_TAIL_START, 1.0))

        v2 = {}
        ld = lambda kind, c: stk_ref[prv, kind, c]

        def s2_big():
            big = [mm_nt(jnp.concatenate([ld(K_XA, c), ld(K_XR, c)], axis=0),
                         jnp.concatenate([ld(K_YB, c), ld(K_YK, c)], axis=0)) for c in chunks]
            v2["a_ab"] = [jnp.where(strict, big[c][0:G, 0:G], 0.0) for c in chunks]
            v2["a_ak"] = [jnp.where(strict, big[c][0:G, G:2 * G], 0.0).astype(bf16) for c in chunks]
            v2["a_rb"] = [jnp.where(incl, big[c][G:2 * G, 0:G], 0.0).astype(bf16) for c in chunks]
            v2["a_rk"] = [jnp.where(incl, big[c][G:2 * G, G:2 * G], 0.0).astype(bf16) for c in chunks]
            v2["tinv"] = [eye + a for a in v2["a_ab"]]
            v2["pw"] = [a.astype(bf16) for a in v2["a_ab"]]

        def s2_first_square():
            v2["pw"] = [mm(p, p).astype(bf16) for p in v2["pw"]]

        def s2_level(last):
            def emit():
                for c in chunks:
                    pw = v2["pw"][c]
                    t16 = v2["tinv"][c].astype(bf16)
                    if last:
                        v2["tinv"][c] = v2["tinv"][c] + mm(t16, pw)
                    else:
                        both = mm(jnp.concatenate([t16, pw], axis=0), pw)
                        v2["tinv"][c] = v2["tinv"][c] + both[0:G]
                        v2["pw"][c] = both[G:2 * G].astype(bf16)
            return emit

        def s2_av():
            v2["tinv"] = [t.astype(bf16) for t in v2["tinv"]]
            av = [mm(jnp.concatenate([v2["a_ak"][c], v2["a_rk"][c]], axis=0), ld(K_VS, c)) for c in chunks]
            v2["akv"] = [x[0:G].astype(bf16) for x in av]
            v2["arkv"] = [x[G:2 * G] for x in av]

        def s2_wu():
            v2["wu"] = [mm(v2["tinv"][c], jnp.concatenate([ld(K_XA, c), v2["akv"][c]], axis=1)).astype(bf16)
                        for c in chunks]

        def s2_ry():
            for c in chunks:
                e = mm(v2["a_rb"][c], v2["wu"][c])
                p16_ref[cur, 0, c] = (xr32_ref[prv, c] + e[:, 0:LANES]).astype(bf16)
                p32_ref[cur, 0, c] = e[:, LANES:] + v2["arkv"][c]

        def s2_pm():
            for c in chunks:
                p16_ref[cur, 1, c] = mm_tn(v2["wu"][c][:, 0:LANES], ld(K_BH, c)).astype(bf16)

        def s2_q():
            for c in chunks:
                p32_ref[cur, 1, c] = mm_tn(jnp.concatenate([v2["wu"][c][:, LANES:], ld(K_VS, c)], axis=0),
                                           jnp.concatenate([ld(K_BH, c), ld(K_KH, c)], axis=0))
            aux2_ref[cur] = aux1_ref[prv]
            dec2_ref[cur] = dec1_ref[prv]

        segs2 = _spread([s2_big, s2_first_square] + [s2_level(False)] * 4 + [s2_level(True)]
                        + [s2_av, s2_wu, s2_ry, s2_pm, s2_q])

        v3 = {"ys": []}
        prm3 = lambda idx: pc3_ref[idx:idx + 1, :]

        def s3_start():
            v3["state"] = jnp.where(first_of_seq_3, 0.0, state_ref[...])

        def s3_chunk(c):
            def emit():
                state = v3["state"]
                s16 = state.astype(bf16)
                y_st = mm_nt(p16_ref[prv, 0, c], s16) + p32_ref[prv, 0, c]
                v3["ys"].append(y_st[0:C] + y_st[C:2 * C])
                v3["state"] = (state * dec2_ref[prv, c, 0:1, :] + mm(s16, p16_ref[prv, 1, c])
                               + p32_ref[prv, 1, c])
            return emit

        def s3_finish():
            state_ref[...] = v3["state"]
            y = jnp.concatenate(v3["ys"], axis=0)
            mean = head_sum(y) * (1.0 / N)
            yc = y - mean
            var = head_sum(yc * yc) * (1.0 / N)
            yn = yc * lax.rsqrt(var + GN_EPS) * prm3(P_LNW) + prm3(P_LNB)
            o_ref[...] = ((yn + aux2_ref[prv, 0]) * aux2_ref[prv, 1]).astype(o_ref.dtype)

        segs3 = _spread([s3_start] + [s3_chunk(c) for c in chunks] + [s3_finish], 0.0, RWKV_CHAIN_END)

        _interleave(segs3, segs2, segs1)

    for parity in (0, 1):
        @pl.when(s % 2 == parity)
        def _(parity=parity):
            body(cur=parity, prv=1 - parity)


def _rwkv_mix(wide, small, pc, ps, w2p, a2p, batch, seq):
    tb = RWKV_BLOCK
    nt = seq // tb
    groups = RWKV_WIDTH // LANES
    nc = tb // RWKV_CHUNK
    n_blocks = batch * groups * nt

    def where(s):
        s = jnp.clip(s, 0, n_blocks - 1)
        return (s // (nt * groups)) * nt + s % nt, (s // nt) % groups

    col = lambda c0: (lambda s: (where(s)[0], c0 + where(s)[1]))
    fixed_col = lambda c: (lambda s: (where(s)[0], c))
    grp = lambda s: (0, where(s)[1])
    return pl.pallas_call(
        functools.partial(_rwkv_kernel, tb=tb, nt=nt),
        grid=(n_blocks + 2,),
        in_specs=[
            pl.BlockSpec((tb, LANES), col(COL_R)),
            pl.BlockSpec((tb, LANES), col(COL_K)),
            pl.BlockSpec((tb, LANES), col(COL_V)),
            pl.BlockSpec((tb, LANES), col(COL_ZB)),
            pl.BlockSpec((tb, LANES), fixed_col(0)),
            pl.BlockSpec((tb, LANES), fixed_col(1)),
            pl.BlockSpec((P_ROWS, LANES), grp),
            pl.BlockSpec((SUBLANES, LANES), lambda s: (0, 0)),
            pl.BlockSpec((LANES, LANES), grp),
            pl.BlockSpec((LANES, LANES), grp),
            pl.BlockSpec((P_ROWS, LANES), lambda s: grp(s - 2)),
        ],
        out_specs=pl.BlockSpec((tb, LANES), lambda s: (where(s - 2)[0], where(s - 2)[1])),
        out_shape=jax.ShapeDtypeStruct((batch * seq, RWKV_WIDTH), bf16),
        scratch_shapes=[
            pltpu.VMEM((LANES, LANES), f32),
            pltpu.VMEM((6, SUBLANES, LANES), f32),
            pltpu.VMEM((2, 7, nc, GROUP_ROWS, LANES), bf16),
            pltpu.VMEM((2, nc, GROUP_ROWS, LANES), f32),
            pltpu.VMEM((2, 2, tb, LANES), f32),
            pltpu.VMEM((2, nc, SUBLANES, LANES), f32),
            pltpu.VMEM((2, 2, nc, GROUP_ROWS, LANES), bf16),
            pltpu.VMEM((2, 2, nc, GROUP_ROWS, LANES), f32),
            pltpu.VMEM((2, 2, tb, LANES), f32),
            pltpu.VMEM((2, nc, SUBLANES, LANES), f32),
        ],
        compiler_params=pltpu.CompilerParams(
            dimension_semantics=("arbitrary",),
            vmem_limit_bytes=VMEM_LIMIT),
        name="rwkv_mix",
    )(wide, wide, wide, wide, small, small, pc, ps, w2p, a2p, pc)


def _out_kernel(oa_ref, ob_ref, ga_ref, gb_ref, x_ref, wpf_ref, wpr_ref, wo_ref, g_ref, o_ref):
    pa = jnp.dot(oa_ref[...], wpf_ref[...], preferred_element_type=f32)
    pb = jnp.dot(ob_ref[...], wpr_ref[...], preferred_element_type=f32)
    m = _sigmoid(ga_ref[...]) * pa + _sigmoid(gb_ref[...]) * pb
    z = x_ref[...] + jnp.dot(m.astype(bf16), wo_ref[...], preferred_element_type=f32)
    ms = jnp.mean(z * z, axis=-1, keepdims=True)
    o_ref[...] = z * lax.rsqrt(ms + RMS_EPS) * g_ref[...]


def _merge_out(oa, ob, wide, x2d, wpf, wpr, wo, gain):
    m, d = x2d.shape
    tm = 256
    resident = lambda shape: pl.BlockSpec(shape, lambda i: (0, 0), pipeline_mode=pl.Buffered(1))
    return pl.pallas_call(
        _out_kernel,
        grid=(m // tm,),
        in_specs=[
            pl.BlockSpec((tm, FOX_WIDTH), lambda i: (i, 0)),
            pl.BlockSpec((tm, RWKV_WIDTH), lambda i: (i, 0)),
            pl.BlockSpec((tm, d), lambda i: (i, COL_GA * LANES // d)),
            pl.BlockSpec((tm, d), lambda i: (i, COL_GB * LANES // d)),
            pl.BlockSpec((tm, d), lambda i: (i, 0)),
            resident((FOX_WIDTH, d)),
            resident((RWKV_WIDTH, d)),
            resident((d, d)),
            resident((1, d)),
        ],
        out_specs=pl.BlockSpec((tm, d), lambda i: (i, 0)),
        out_shape=jax.ShapeDtypeStruct((m, d), f32),
        compiler_params=pltpu.CompilerParams(
            dimension_semantics=("parallel",),
            vmem_limit_bytes=VMEM_LIMIT),
        name="merge_out",
    )(oa, ob, wide, wide, x2d, wpf, wpr, wo, gain)


def _branches(x2d, batch, seq, norm_gain, w_in, fox_forget_bias, rwkv_shift_mix, rwkv_w0, rwkv_w2,
              rwkv_a0, rwkv_a2, rwkv_k_k, rwkv_k_a, rwkv_r_k, rwkv_ln_w, rwkv_ln_b,
              w_proj_fox, w_proj_rwkv, w_out):
    fw, rw = FOX_WIDTH, RWKV_WIDTH
    w_t = w_in.T
    r_qkv = 0
    r_za = r_qkv + 3 * fw
    r_f = r_za + fw
    r_rkvz = r_f + FOX_HEADS
    r_wd = r_rkvz + 4 * rw
    r_ad = r_wd + LORA
    r_g = r_ad + LORA
    pieces = [(r_qkv + fw, 2 * fw), (r_g, 2 * D_MODEL), (r_za, fw), (r_rkvz, 4 * rw)]
    pad_rows = lambda a: jnp.pad(a, ((0, LANES - a.shape[0]), (0, 0)))
    w_tail = jnp.concatenate(
        [pad_rows(w_t[r_wd:r_wd + LORA]),
         pad_rows(jnp.concatenate([w_t[r_ad:r_ad + LORA], w_t[r_f:r_f + FOX_HEADS]], axis=0))], axis=0)
    h2d, q = _norm_q(x2d, norm_gain.reshape(1, D_MODEL), w_t, FOX_HEAD_DIM ** -0.5 * LOG2E)
    kv, wide, small = _in_proj(h2d, w_t, pieces, w_tail)

    bias_row = jnp.pad(fox_forget_bias.reshape(1, FOX_HEADS), ((0, 0), (F_LANE, LANES - F_LANE - FOX_HEADS)))
    c = _gate_cumsum(small, bias_row, batch, seq)
    oa, *w16 = _fox_attention(q.reshape(batch, seq, fw), kv.reshape(batch, seq, 2 * fw),
                              c.reshape(batch, seq, LANES), wide.reshape(batch, seq, WIDE_COLS),
                              w_proj_fox, w_proj_rwkv, w_out, batch, seq)
    oa = oa.reshape(batch * seq, fw)

    mu = rwkv_shift_mix
    pc = jnp.zeros((P_ROWS, rw), f32)
    rows = [mu[0:rw], mu[rw:2 * rw], mu[2 * rw:3 * rw], mu[3 * rw:4 * rw], rwkv_w0, rwkv_a0,
            rwkv_k_k, rwkv_k_a, rwkv_r_k.reshape(rw), rwkv_ln_w, rwkv_ln_b]
    pc = pc.at[:len(rows)].set(jnp.stack(rows))
    ps = jnp.zeros((SUBLANES, LANES), f32)
    ps = ps.at[0, :LORA].set(mu[4 * rw:4 * rw + LORA]).at[1, :LORA].set(mu[4 * rw + LORA:])
    w2p = jnp.pad(rwkv_w2, ((0, LANES - LORA), (0, 0))).astype(bf16)
    a2p = jnp.pad(rwkv_a2, ((0, LANES - LORA), (0, 0))).astype(bf16)
    ob = _rwkv_mix(wide, small, pc, ps, w2p, a2p, batch, seq)

    return oa, ob, wide, w16


def kernel(x, norm_gain, w_in, fox_forget_bias, rwkv_shift_mix, rwkv_w0, rwkv_w2, rwkv_a0, rwkv_a2, rwkv_k_k, rwkv_k_a, rwkv_r_k, rwkv_ln_w, rwkv_ln_b, w_proj_fox, w_proj_rwkv, w_out, final_norm_gain):
    batch, seq, d = x.shape
    depth = norm_gain.shape[0]
    assert depth == 1, "the final rmsnorm is fused into the single layer's output kernel"
    x2d = x.reshape(batch * seq, d)
    oa, ob, wide, (wpf16, wpr16, wo16) = _branches(
        x2d, batch, seq, norm_gain[0], w_in[0], fox_forget_bias[0], rwkv_shift_mix[0], rwkv_w0[0], rwkv_w2[0],
        rwkv_a0[0], rwkv_a2[0], rwkv_k_k[0], rwkv_k_a[0], rwkv_r_k[0], rwkv_ln_w[0], rwkv_ln_b[0],
        w_proj_fox[0], w_proj_rwkv[0], w_out[0])
    out = _merge_out(oa, ob, wide, x2d, wpf16, wpr16, wo16, final_norm_gain.reshape(1, d))
    return out.reshape(batch, seq, d)
```

```python
import functools

import jax
import jax.numpy as jnp
from jax import lax
from jax.experimental import pallas as pl
from jax.experimental.pallas import tpu as pltpu

D_MODEL = 2048
FOX_HEADS = 8
FOX_HEAD_DIM = 128
FOX_WIDTH = FOX_HEADS * FOX_HEAD_DIM
RWKV_HEADS = 16
RWKV_HEAD_DIM = 64
RWKV_WIDTH = RWKV_HEADS * RWKV_HEAD_DIM
LORA = 96
RMS_EPS = 1e-6
GN_EPS = 64e-5
L2_EPS = 1e-12

LANES = 128
SUBLANES = 8
VMEM_LIMIT = 56 * 1024 * 1024

COL_GA = 0
COL_GB = D_MODEL // LANES
COL_ZA = 2 * D_MODEL // LANES
COL_R = COL_ZA + FOX_WIDTH // LANES
COL_K = COL_R + RWKV_WIDTH // LANES
COL_V = COL_K + RWKV_WIDTH // LANES
COL_ZB = COL_V + RWKV_WIDTH // LANES
WIDE_COLS = (COL_ZB + RWKV_WIDTH // LANES) * LANES
SMALL_COLS = 2 * LANES
F_LANE = LORA

RWKV_CHUNK = 64
RWKV_BLOCK = 512
RWKV_TAIL_START = 0.3
RWKV_CHAIN_END = 0.8
HEADS_PER_GROUP = LANES // RWKV_HEAD_DIM
GROUP_ROWS = HEADS_PER_GROUP * RWKV_CHUNK

f32 = jnp.float32
bf16 = jnp.bfloat16


def _dot(a, b):
    return jnp.dot(a.astype(bf16), b.astype(bf16), preferred_element_type=f32)


def _dot_exact_ones(a, ones_bf16, nt=False, terms=3):
    if nt:
        mm = lambda x: lax.dot_general(ones_bf16, x, (((1,), (0,)), ((), ())),
                                       preferred_element_type=f32)
    else:
        mm = lambda x: jnp.dot(x, ones_bf16, preferred_element_type=f32)
    part = a.astype(bf16)
    out = mm(part)
    rest = a
    for _ in range(terms - 1):
        rest = rest - part.astype(f32)
        part = rest.astype(bf16)
        out = out + mm(part)
    return out


def _softplus(x):
    return jnp.maximum(x, 0.0) + jnp.log(1.0 + jnp.exp(-jnp.abs(x)))


def _sigmoid(x):
    return 1.0 / (1.0 + jnp.exp(-x))


PROJ_TM = 1024
PROJ_TN = 1024


def _norm_q_kernel(x_ref, g_ref, w_ref, h_ref, q_ref, w16_ref, *, scale):
    @pl.when(pl.program_id(0) == 0)
    def _():
        w16_ref[...] = (w_ref[...] * scale).astype(bf16)

    xf = x_ref[...]
    ms = jnp.mean(xf * xf, axis=-1, keepdims=True)
    h = (xf * lax.rsqrt(ms + RMS_EPS) * g_ref[...]).astype(bf16)
    h_ref[...] = h
    q_ref[...] = lax.dot_general(h, w16_ref[...], (((1,), (1,)), ((), ())),
                                 preferred_element_type=f32).astype(q_ref.dtype)


def _norm_q(x2d, gain, w_t, scale):
    m, d = x2d.shape
    tm = PROJ_TM
    return pl.pallas_call(
        functools.partial(_norm_q_kernel, scale=scale),
        grid=(m // tm,),
        in_specs=[
            pl.BlockSpec((tm, d), lambda i: (i, 0)),
            pl.BlockSpec((1, d), lambda i: (0, 0)),
            pl.BlockSpec((FOX_WIDTH, d), lambda i: (0, 0)),
        ],
        out_specs=[pl.BlockSpec((tm, d), lambda i: (i, 0)), pl.BlockSpec((tm, FOX_WIDTH), lambda i: (i, 0))],
        out_shape=[jax.ShapeDtypeStruct((m, d), bf16), jax.ShapeDtypeStruct((m, FOX_WIDTH), bf16)],
        scratch_shapes=[pltpu.VMEM((FOX_WIDTH, d), bf16)],
        compiler_params=pltpu.CompilerParams(dimension_semantics=("arbitrary",), vmem_limit_bytes=VMEM_LIMIT),
        name="norm_q",
    )(x2d, gain, w_t)


def _in_proj_kernel(h_ref, w_ref, wt_ref, kv_ref, wide_ref, small_ref, w16_ref, *, n_kv, n_wide):
    j = pl.program_id(0)
    i = pl.program_id(1)
    n_main = n_kv + n_wide

    @pl.when(i == 0)
    def _():
        @pl.when(j < n_main)
        def _():
            w16_ref[...] = w_ref[...].astype(bf16)

        @pl.when(j == n_main)
        def _():
            w16_ref[0:SMALL_COLS, :] = wt_ref[...].astype(bf16)

    project = lambda w: lax.dot_general(h_ref[...], w, (((1,), (1,)), ((), ())), preferred_element_type=f32)

    @pl.when(j < n_kv)
    def _():
        kv_ref[...] = project(w16_ref[...]).astype(kv_ref.dtype)

    @pl.when(jnp.logical_and(j >= n_kv, j < n_main))
    def _():
        wide_ref[...] = project(w16_ref[...])

    @pl.when(j == n_main)
    def _():
        small_ref[...] = project(w16_ref[0:SMALL_COLS, :])


def _in_proj(h2d, w_t, pieces, w_tail):
    m, d = h2d.shape
    tm, tn = PROJ_TM, PROJ_TN
    n_kv = 2 * FOX_WIDTH // tn
    n_wide = WIDE_COLS // tn
    n_main = n_kv + n_wide
    n_i = m // tm
    assert n_kv * tn == 2 * FOX_WIDTH and n_wide * tn == WIDE_COLS and w_tail.shape[0] == SMALL_COLS
    assert all(n % tn == 0 and src % SUBLANES == 0 for src, n in pieces)
    assert sum(n for _, n in pieces) == n_main * tn
    starts, first_tile = [], 0
    for src, n in pieces:
        starts.append((first_tile, src))
        first_tile += n // tn

    def src_row(j):
        tile = jnp.int32(0)
        for blk, src in starts:
            tile = jnp.where(j >= blk, src // SUBLANES + (jnp.minimum(j, n_main - 1) - blk) * (tn // SUBLANES), tile)
        return pl.multiple_of(tile * SUBLANES, SUBLANES)

    kv_row = lambda j, i: jnp.where(j < n_kv, i, n_i - 1)
    wide_row = lambda j, i: jnp.where(j < n_kv, 0, jnp.where(j < n_main, i, n_i - 1))
    small_row = lambda j, i: jnp.where(j == n_main, i, 0)
    return pl.pallas_call(
        functools.partial(_in_proj_kernel, n_kv=n_kv, n_wide=n_wide),
        grid=(n_main + 1, n_i),
        in_specs=[
            pl.BlockSpec((tm, d), lambda j, i: (i, 0)),
            pl.BlockSpec((pl.Element(tn), pl.Element(d)), lambda j, i: (src_row(j), 0)),
            pl.BlockSpec((SMALL_COLS, d), lambda j, i: (0, 0)),
        ],
        out_specs=[
            pl.BlockSpec((tm, tn), lambda j, i: (kv_row(j, i), jnp.minimum(j, n_kv - 1))),
            pl.BlockSpec((tm, tn), lambda j, i: (wide_row(j, i), jnp.clip(j - n_kv, 0, n_wide - 1))),
            pl.BlockSpec((tm, SMALL_COLS), lambda j, i: (small_row(j, i), 0)),
        ],
        out_shape=[
            jax.ShapeDtypeStruct((m, 2 * FOX_WIDTH), bf16),
            jax.ShapeDtypeStruct((m, WIDE_COLS), f32),
            jax.ShapeDtypeStruct((m, SMALL_COLS), f32),
        ],
        scratch_shapes=[pltpu.VMEM((tn, d), bf16)],
        compiler_params=pltpu.CompilerParams(
            dimension_semantics=("arbitrary", "arbitrary"),
            vmem_limit_bytes=VMEM_LIMIT),
        name="in_proj",
    )(h2d, w_t, w_tail)


GATE_SUB_BLOCK = 256


def _gate_kernel(f_ref, bias_ref, c_ref, carry_ref, *, tb):
    @pl.when(pl.program_id(1) == 0)
    def _():
        carry_ref[...] = jnp.zeros_like(carry_ref)

    z = f_ref[...] + bias_ref[...]
    log_f = -_softplus(-z)
    sb = GATE_SUB_BLOCK
    row = lax.broadcasted_iota(jnp.int32, (sb, sb), 0)
    col = lax.broadcasted_iota(jnp.int32, (sb, sb), 1)
    tri = jnp.where(col <= row, 1.0, 0.0).astype(bf16)
    carry = carry_ref[0:1, :]
    for k in range(0, tb, sb):
        c = _dot_exact_ones(log_f[k:k + sb], tri, nt=True) + carry
        c_ref[k:k + sb, :] = c
        carry = c[sb - 1:sb, :]
    carry_ref[0:1, :] = carry


def _gate_cumsum(small, bias_row, batch, seq):
    tb = 1024
    nt = seq // tb
    return pl.pallas_call(
        functools.partial(_gate_kernel, tb=tb),
        grid=(batch, nt),
        in_specs=[
            pl.BlockSpec((tb, LANES), lambda b, t: (b * nt + t, 1)),
            pl.BlockSpec((1, LANES), lambda b, t: (0, 0)),
        ],
        out_specs=pl.BlockSpec((tb, LANES), lambda b, t: (b * nt + t, 0)),
        out_shape=jax.ShapeDtypeStruct((batch * seq, LANES), f32),
        scratch_shapes=[pltpu.VMEM((SUBLANES, LANES), f32)],
        compiler_params=pltpu.CompilerParams(
            dimension_semantics=("parallel", "arbitrary")),
        name="fox_gate_cumsum",
    )(small, bias_row)


LOG2E = 1.4426950408889634
FOX_TK = 256
FOX_GW = 256
FOX_BATCH = 4


def _fox_kernel(q_ref, k_ref, v_ref, c_ref, z_ref, wf_ref, wr_ref, wo_ref, o_ref, wf16_ref, wr16_ref, wo16_ref,
                crep_ref, acc_ref, vt_ref, *, seq, tk, gw):
    wf16_ref[...] = wf_ref[...].astype(bf16)
    wr16_ref[...] = wr_ref[...].astype(bf16)
    wo16_ref[...] = wo_ref[...].astype(bf16)

    h = pl.program_id(1)
    ng = seq // gw
    nt = seq // tk
    assert seq % gw == 0 and seq % tk == 0 and (tk % gw == 0 or gw % tk == 0)
    mm = lambda a, b: jnp.dot(a, b, preferred_element_type=f32)
    mm_nt = lambda a, b: lax.dot_general(a, b, (((1,), (1,)), ((), ())), preferred_element_type=f32)

    lane = lax.broadcasted_iota(jnp.int32, (1, LANES), 1)
    onehot = jnp.where(lane == F_LANE + h, LOG2E, 0.0)
    crep_ref[...] = jnp.sum(c_ref[...] * onehot, axis=1, keepdims=True)
    for t in range(nt):
        vt_ref[t, 0:LANES, :] = v_ref[t * tk:(t + 1) * tk, :].T
        vt_ref[t, LANES:LANES + SUBLANES, :] = jnp.ones((SUBLANES, tk), bf16)

    key_rel = lax.broadcasted_iota(jnp.int32, (tk, gw), 0)
    qry_rel = lax.broadcasted_iota(jnp.int32, (tk, gw), 1)

    def visible(t, g):
        return t * tk <= g * gw + gw - 1

    def needs_mask(t, g):
        return t * tk + tk - 1 > g * gw

    last_tile = [max(t for t in range(nt) if visible(t, g)) for g in range(ng)]
    sched = []
    for t in range(nt):
        groups = [g for g in range(ng) if visible(t, g)]
        sched += [[(t, g) for g in groups[k:k + FOX_BATCH]] for k in range(0, len(groups), FOX_BATCH)]

    def qk(batch):
        return [mm_nt(k_ref[t * tk:(t + 1) * tk, :], q_ref[g * gw:(g + 1) * gw, :]) for t, g in batch]

    m = [None] * ng
    l = [None] * ng

    def finish(batch, alpha, pv):
        for (t, g), a, xs in zip(batch, alpha, pv):
            x, p_sum = xs[0:LANES], xs[LANES:LANES + 1]
            acc_ref[g] = x if t == 0 else a * acc_ref[g] + x
            l[g] = p_sum if t == 0 else a * l[g] + p_sum
            if t == last_tile[g]:
                rows = slice(g * gw, (g + 1) * gw)
                z = z_ref[rows, :]
                o = (acc_ref[g] / l[g]).T
                o_ref[rows, :] = (o * (z * _sigmoid(z))).astype(o_ref.dtype)

    s_next = qk(sched[0])
    pending = None
    for bi, batch in enumerate(sched):
        s_cur = s_next
        if bi + 1 < len(sched):
            s_next = qk(sched[bi + 1])
        alpha, p16 = [], []
        for (t, g), s in zip(batch, s_cur):
            s = s - crep_ref[t * tk:(t + 1) * tk, :]
            if needs_mask(t, g):
                s = jnp.where(key_rel + t * tk > qry_rel + g * gw, -jnp.inf, s)
            m_tile = jnp.max(s, axis=0, keepdims=True)
            if t == 0:
                m_new, a = m_tile, None
            else:
                m_new = jnp.maximum(m[g], m_tile)
                a = jnp.exp2(m[g] - m_new)
            m[g] = m_new
            alpha.append(a)
            p16.append(jnp.exp2((s - m_new).astype(bf16)))
        pv = [mm(vt_ref[t], p) for (t, g), p in zip(batch, p16)]
        if pending is not None:
            finish(*pending)
        pending = (batch, alpha, pv)
    finish(*pending)


def _fox_attention(q3, kv3, c3, wide3, w_proj_fox, w_proj_rwkv, w_out, batch, seq):
    tk, gw = FOX_TK, FOX_GW
    h8 = FOX_HEADS
    steps = batch * h8
    assert all(w.shape[0] % (steps * 2 * SUBLANES) == 0 for w in (w_proj_fox, w_proj_rwkv, w_out))
    slab = lambda w: pl.BlockSpec((w.shape[0] // steps, w.shape[1]), lambda b, h: (b * h8 + h, 0))
    return pl.pallas_call(
        functools.partial(_fox_kernel, seq=seq, tk=tk, gw=gw),
        grid=(batch, h8),
        in_specs=[
            pl.BlockSpec((None, seq, LANES), lambda b, h: (b, 0, h)),
            pl.BlockSpec((None, seq, LANES), lambda b, h: (b, 0, h)),
            pl.BlockSpec((None, seq, LANES), lambda b, h: (b, 0, h8 + h)),
            pl.BlockSpec((None, seq, LANES), lambda b, h: (b, 0, 0)),
            pl.BlockSpec((None, seq, LANES), lambda b, h: (b, 0, COL_ZA + h)),
            slab(w_proj_fox), slab(w_proj_rwkv), slab(w_out),
        ],
        out_specs=[pl.BlockSpec((None, seq, LANES), lambda b, h: (b, 0, h)),
                   slab(w_proj_fox), slab(w_proj_rwkv), slab(w_out)],
        out_shape=[jax.ShapeDtypeStruct((batch, seq, FOX_WIDTH), bf16),
                   jax.ShapeDtypeStruct(w_proj_fox.shape, bf16),
                   jax.ShapeDtypeStruct(w_proj_rwkv.shape, bf16),
                   jax.ShapeDtypeStruct(w_out.shape, bf16)],
        scratch_shapes=[
            pltpu.VMEM((seq, 1), f32),
            pltpu.VMEM((seq // gw, LANES, gw), f32),
            pltpu.VMEM((seq // tk, LANES + SUBLANES, tk), bf16),
        ],
        compiler_params=pltpu.CompilerParams(
            dimension_semantics=("parallel", "arbitrary"),
            vmem_limit_bytes=VMEM_LIMIT),
        name="fox_attention",
    )(q3, kv3, kv3, c3, wide3, w_proj_fox, w_proj_rwkv, w_out)


P_MU_R, P_MU_K, P_MU_V, P_MU_Z, P_W0, P_A0, P_KK, P_KA, P_RK, P_LNW, P_LNB = range(11)
P_ROWS = 16
K_XA, K_XR, K_YB, K_YK, K_VS, K_BH, K_KH = range(7)


def _spread(emitters, lo=0.0, hi=1.0):
    n = len(emitters)
    return [(lo + (hi - lo) * (i + 0.5) / n, e) for i, e in enumerate(emitters)]


def _interleave(*segment_lists):
    keyed = [(pos, prio, seg) for prio, segs in enumerate(segment_lists) for pos, seg in segs]
    keyed.sort(key=lambda x: (x[0], x[1]))
    for _, _, seg in keyed:
        seg()


def _rwkv_kernel(r_ref, k_ref, v_ref, z_ref, wd_ref, ad_ref, pc_ref, ps_ref, w2_ref, a2_ref, pc3_ref,
                 o_ref,
                 state_ref, prev_ref, stk_ref, xr32_ref, aux1_ref, dec1_ref, p16_ref, p32_ref,
                 aux2_ref, dec2_ref, *, tb, nt):
    C = RWKV_CHUNK
    G = GROUP_ROWS
    N = RWKV_HEAD_DIM
    NC = tb // C
    s = pl.program_id(0)
    first_of_seq_1 = (s % nt) == 0
    first_of_seq_3 = ((s + 2 * nt - 2) % nt) == 0

    @pl.when(s == 0)
    def _():
        for ref in (state_ref, prev_ref, stk_ref, xr32_ref, aux1_ref, dec1_ref, p16_ref, p32_ref,
                    aux2_ref, dec2_ref):
            ref[...] = jnp.zeros_like(ref)

    mm = lambda a, b: jnp.dot(a, b, preferred_element_type=f32)
    mm_nt = lambda a, b: lax.dot_general(a, b, (((1,), (1,)), ((), ())), preferred_element_type=f32)
    mm_tn = lambda a, b: lax.dot_general(a, b, (((0,), (0,)), ((), ())), preferred_element_type=f32)

    def body(cur, prv):
        lane_t = lax.broadcasted_iota(jnp.int32, (1, LANES), 1)
        head_masks = [jnp.where(lane_t // N == h, 1.0, 0.0) for h in range(HEADS_PER_GROUP)]

        def head_sum(x):
            sums = [jnp.sum(x * hm, axis=1, keepdims=True) for hm in head_masks]
            out = sums[HEADS_PER_GROUP - 1]
            for hh in range(HEADS_PER_GROUP - 2, -1, -1):
                out = jnp.where(lane_t // N == hh, sums[hh], out)
            return out

        stack = lambda x: jnp.concatenate([x * hm for hm in head_masks], axis=0)
        lane_c = lax.broadcasted_iota(jnp.int32, (C, LANES), 1)
        head_sel = [lane_c // N == h for h in range(HEADS_PER_GROUP)]

        def stack16(x):
            xb = x.astype(bf16)
            return jnp.concatenate([jnp.where(m, xb, jnp.zeros_like(xb)) for m in head_sel], axis=0)

        gi = lax.broadcasted_iota(jnp.int32, (G, G), 0)
        gj = lax.broadcasted_iota(jnp.int32, (G, G), 1)
        same_head = gi // C == gj // C
        strict = jnp.logical_and(same_head, gj < gi)
        incl = jnp.logical_and(same_head, gj <= gi)
        eye = jnp.where(gi == gj, 1.0, 0.0)
        chunks = range(NC)

        v1 = {}
        prm = lambda idx: pc_ref[idx:idx + 1, :]
        row_in_block = lax.broadcasted_iota(jnp.int32, (tb, LANES), 0)

        def shifted(ref, slot, mu):
            u = ref[...]
            carry = jnp.where(first_of_seq_1, 0.0, prev_ref[slot, 0:1, :])
            prev = jnp.where(row_in_block == 0, carry, pltpu.roll(u, 1, 0))
            prev_ref[slot, 0:1, :] = u[tb - 1:tb, :]
            return u + (prev - u) * mu

        def s1_lora():
            v1["wd"] = shifted(wd_ref, 4, ps_ref[0:1, :])
            v1["ad"] = shifted(ad_ref, 5, ps_ref[1:2, :])
            v1["w_lin"] = _dot(jnp.tanh(v1["wd"]), w2_ref[...])
            v1["a_lin"] = _dot(v1["ad"], a2_ref[...])

        def s1_key_norm():
            v1["kr"] = shifted(k_ref, 1, prm(P_MU_K))
            kk = v1["kr"] * prm(P_KK)
            v1["kk_raw"] = kk
            v1["kk_ss"] = head_sum(kk * kk)

        def s1_decay():
            w = -_softplus(-(prm(P_W0) + v1["w_lin"])) - 0.5
            v1["log_decay"] = -jnp.exp(w)
            ti = lax.broadcasted_iota(jnp.int32, (2 * C, 2 * C), 0)
            tj = lax.broadcasted_iota(jnp.int32, (2 * C, 2 * C), 1)
            pair_tri = jnp.where(jnp.logical_and(tj <= ti, ti // C == tj // C), 1.0, 0.0).astype(bf16)
            v1["ci"] = jnp.concatenate(
                [_dot_exact_ones(v1["log_decay"][k:k + 2 * C], pair_tri, nt=True)
                 for k in range(0, tb, 2 * C)], axis=0)

        def s1_bonus():
            v1["rate"] = _sigmoid(prm(P_A0) + v1["a_lin"])
            v1["r"] = shifted(r_ref, 0, prm(P_MU_R))
            v1["vr"] = shifted(v_ref, 2, prm(P_MU_V))
            v1["kp"] = v1["kr"] * (1.0 + (v1["rate"] - 1.0) * prm(P_KA))
            aux1_ref[cur, 0] = head_sum(v1["r"] * v1["kp"] * prm(P_RK)) * v1["vr"]
            zb = shifted(z_ref, 3, prm(P_MU_Z))
            aux1_ref[cur, 1] = zb * _sigmoid(zb)

        def s1_scale():
            kk = v1["kk_raw"] / jnp.maximum(jnp.sqrt(v1["kk_ss"]), L2_EPS)
            v1["bb"] = kk * v1["rate"]
            ci = v1["ci"]
            v1["a_t"] = -kk * jnp.exp(ci - v1["log_decay"])
            v1["r_t"] = v1["r"] * jnp.exp(ci)
            inv = jnp.exp(-ci)
            v1["b_t"] = v1["bb"] * inv
            v1["k_t"] = v1["kp"] * inv

        def s1_stack(c):
            def emit():
                sl = slice(c * C, (c + 1) * C)
                ci_c = v1["ci"][sl]
                c_last = ci_c[C - 1:C, :]
                to_end = jnp.exp(c_last - ci_c)
                dec1_ref[cur, c] = jnp.broadcast_to(jnp.exp(c_last), (SUBLANES, LANES))
                xr = stack(v1["r_t"][sl])
                xr32_ref[cur, c] = xr
                stk_ref[cur, K_XR, c] = xr.astype(bf16)
                stk_ref[cur, K_XA, c] = stack16(v1["a_t"][sl])
                stk_ref[cur, K_YB, c] = stack16(v1["b_t"][sl])
                stk_ref[cur, K_YK, c] = stack16(v1["k_t"][sl])
                stk_ref[cur, K_VS, c] = stack16(v1["vr"][sl])
                stk_ref[cur, K_BH, c] = stack16(v1["bb"][sl] * to_end)
                stk_ref[cur, K_KH, c] = stack16(v1["kp"][sl] * to_end)
            return emit

        segs1 = ([(0.0, s1_lora), (0.02, s1_key_norm), (0.05, s1_decay), (0.08, s1_bonus)]
                 + _spread([s1_scale] + [s1_stack(c) for c in chunks], RWKV_TAIL_START, 1.0))

        v2 = {}
        ld = lambda kind, c: stk_ref[prv, kind, c]

        def s2_big():
            big = [mm_nt(jnp.concatenate([ld(K_XA, c), ld(K_XR, c)], axis=0),
                         jnp.concatenate([ld(K_YB, c), ld(K_YK, c)], axis=0)) for c in chunks]
            v2["a_ab"] = [jnp.where(strict, big[c][0:G, 0:G], 0.0) for c in chunks]
            v2["a_ak"] = [jnp.where(strict, big[c][0:G, G:2 * G], 0.0).astype(bf16) for c in chunks]
            v2["a_rb"] = [jnp.where(incl, big[c][G:2 * G, 0:G], 0.0).astype(bf16) for c in chunks]
            v2["a_rk"] = [jnp.where(incl, big[c][G:2 * G, G:2 * G], 0.0).astype(bf16) for c in chunks]
            v2["tinv"] = [eye + a for a in v2["a_ab"]]
            v2["pw"] = [a.astype(bf16) for a in v2["a_ab"]]

        def s2_first_square():
            v2["pw"] = [mm(p, p).astype(bf16) for p in v2["pw"]]

        def s2_level(last):
            def emit():
                for c in chunks:
                    pw = v2["pw"][c]
                    t16 = v2["tinv"][c].astype(bf16)
                    if last:
                        v2["tinv"][c] = v2["tinv"][c] + mm(t16, pw)
                    else:
                        both = mm(jnp.concatenate([t16, pw], axis=0), pw)
                        v2["tinv"][c] = v2["tinv"][c] + both[0:G]
                        v2["pw"][c] = both[G:2 * G].astype(bf16)
            return emit

        def s2_av():
            v2["tinv"] = [t.astype(bf16) for t in v2["tinv"]]
            av = [mm(jnp.concatenate([v2["a_ak"][c], v2["a_rk"][c]], axis=0), ld(K_VS, c)) for c in chunks]
            v2["akv"] = [x[0:G].astype(bf16) for x in av]
            v2["arkv"] = [x[G:2 * G] for x in av]

        def s2_wu():
            v2["wu"] = [mm(v2["tinv"][c], jnp.concatenate([ld(K_XA, c), v2["akv"][c]], axis=1)).astype(bf16)
                        for c in chunks]

        def s2_ry():
            for c in chunks:
                e = mm(v2["a_rb"][c], v2["wu"][c])
                p16_ref[cur, 0, c] = (xr32_ref[prv, c] + e[:, 0:LANES]).astype(bf16)
                p32_ref[cur, 0, c] = e[:, LANES:] + v2["arkv"][c]

        def s2_pm():
            for c in chunks:
                p16_ref[cur, 1, c] = mm_tn(v2["wu"][c][:, 0:LANES], ld(K_BH, c)).astype(bf16)

        def s2_q():
            for c in chunks:
                p32_ref[cur, 1, c] = mm_tn(jnp.concatenate([v2["wu"][c][:, LANES:], ld(K_VS, c)], axis=0),
                                           jnp.concatenate([ld(K_BH, c), ld(K_KH, c)], axis=0))
            aux2_ref[cur] = aux1_ref[prv]
            dec2_ref[cur] = dec1_ref[prv]

        segs2 = _spread([s2_big, s2_first_square] + [s2_level(False)] * 4 + [s2_level(True)]
                        + [s2_av, s2_wu, s2_ry, s2_pm, s2_q])

        v3 = {"ys": []}
        prm3 = lambda idx: pc3_ref[idx:idx + 1, :]

        def s3_start():
            v3["state"] = jnp.where(first_of_seq_3, 0.0, state_ref[...])

        def s3_chunk(c):
            def emit():
                state = v3["state"]
                s16 = state.astype(bf16)
                y_st = mm_nt(p16_ref[prv, 0, c], s16) + p32_ref[prv, 0, c]
                v3["ys"].append(y_st[0:C] + y_st[C:2 * C])
                v3["state"] = (state * dec2_ref[prv, c, 0:1, :] + mm(s16, p16_ref[prv, 1, c])
                               + p32_ref[prv, 1, c])
            return emit

        def s3_finish():
            state_ref[...] = v3["state"]
            y = jnp.concatenate(v3["ys"], axis=0)
            mean = head_sum(y) * (1.0 / N)
            yc = y - mean
            var = head_sum(yc * yc) * (1.0 / N)
            yn = yc * lax.rsqrt(var + GN_EPS) * prm3(P_LNW) + prm3(P_LNB)
            o_ref[...] = ((yn + aux2_ref[prv, 0]) * aux2_ref[prv, 1]).astype(o_ref.dtype)

        segs3 = _spread([s3_start] + [s3_chunk(c) for c in chunks] + [s3_finish], 0.0, RWKV_CHAIN_END)

        _interleave(segs3, segs2, segs1)

    for parity in (0, 1):
        @pl.when(s % 2 == parity)
        def _(parity=parity):
            body(cur=parity, prv=1 - parity)


def _rwkv_mix(wide, small, pc, ps, w2p, a2p, batch, seq):
    tb = RWKV_BLOCK
    nt = seq // tb
    groups = RWKV_WIDTH // LANES
    nc = tb // RWKV_CHUNK
    n_blocks = batch * groups * nt

    def where(s):
        s = jnp.clip(s, 0, n_blocks - 1)
        return (s // (nt * groups)) * nt + s % nt, (s // nt) % groups

    col = lambda c0: (lambda s: (where(s)[0], c0 + where(s)[1]))
    fixed_col = lambda c: (lambda s: (where(s)[0], c))
    grp = lambda s: (0, where(s)[1])
    return pl.pallas_call(
        functools.partial(_rwkv_kernel, tb=tb, nt=nt),
        grid=(n_blocks + 2,),
        in_specs=[
            pl.BlockSpec((tb, LANES), col(COL_R)),
            pl.BlockSpec((tb, LANES), col(COL_K)),
            pl.BlockSpec((tb, LANES), col(COL_V)),
            pl.BlockSpec((tb, LANES), col(COL_ZB)),
            pl.BlockSpec((tb, LANES), fixed_col(0)),
            pl.BlockSpec((tb, LANES), fixed_col(1)),
            pl.BlockSpec((P_ROWS, LANES), grp),
            pl.BlockSpec((SUBLANES, LANES), lambda s: (0, 0)),
            pl.BlockSpec((LANES, LANES), grp),
            pl.BlockSpec((LANES, LANES), grp),
            pl.BlockSpec((P_ROWS, LANES), lambda s: grp(s - 2)),
        ],
        out_specs=pl.BlockSpec((tb, LANES), lambda s: (where(s - 2)[0], where(s - 2)[1])),
        out_shape=jax.ShapeDtypeStruct((batch * seq, RWKV_WIDTH), bf16),
        scratch_shapes=[
            pltpu.VMEM((LANES, LANES), f32),
            pltpu.VMEM((6, SUBLANES, LANES), f32),
            pltpu.VMEM((2, 7, nc, GROUP_ROWS, LANES), bf16),
            pltpu.VMEM((2, nc, GROUP_ROWS, LANES), f32),
            pltpu.VMEM((2, 2, tb, LANES), f32),
            pltpu.VMEM((2, nc, SUBLANES, LANES), f32),
            pltpu.VMEM((2, 2, nc, GROUP_ROWS, LANES), bf16),
            pltpu.VMEM((2, 2, nc, GROUP_ROWS, LANES), f32),
            pltpu.VMEM((2, 2, tb, LANES), f32),
            pltpu.VMEM((2, nc, SUBLANES, LANES), f32),
        ],
        compiler_params=pltpu.CompilerParams(
            dimension_semantics=("arbitrary",),
            vmem_limit_bytes=VMEM_LIMIT),
        name="rwkv_mix",
    )(wide, wide, wide, wide, small, small, pc, ps, w2p, a2p, pc)


def _out_kernel(oa_ref, ob_ref, ga_ref, gb_ref, x_ref, wpf_ref, wpr_ref, wo_ref, g_ref, o_ref):
    pa = jnp.dot(oa_ref[...], wpf_ref[...], preferred_element_type=f32)
    pb = jnp.dot(ob_ref[...], wpr_ref[...], preferred_element_type=f32)
    m = _sigmoid(ga_ref[...]) * pa + _sigmoid(gb_ref[...]) * pb
    z = x_ref[...] + jnp.dot(m.astype(bf16), wo_ref[...], preferred_element_type=f32)
    ms = jnp.mean(z * z, axis=-1, keepdims=True)
    o_ref[...] = z * lax.rsqrt(ms + RMS_EPS) * g_ref[...]


def _merge_out(oa, ob, wide, x2d, wpf, wpr, wo, gain):
    m, d = x2d.shape
    tm = 256
    resident = lambda shape: pl.BlockSpec(shape, lambda i: (0, 0), pipeline_mode=pl.Buffered(1))
    return pl.pallas_call(
        _out_kernel,
        grid=(m // tm,),
        in_specs=[
            pl.BlockSpec((tm, FOX_WIDTH), lambda i: (i, 0)),
            pl.BlockSpec((tm, RWKV_WIDTH), lambda i: (i, 0)),
            pl.BlockSpec((tm, d), lambda i: (i, COL_GA * LANES // d)),
            pl.BlockSpec((tm, d), lambda i: (i, COL_GB * LANES // d)),
            pl.BlockSpec((tm, d), lambda i: (i, 0)),
            resident((FOX_WIDTH, d)),
            resident((RWKV_WIDTH, d)),
            resident((d, d)),
            resident((1, d)),
        ],
        out_specs=pl.BlockSpec((tm, d), lambda i: (i, 0)),
        out_shape=jax.ShapeDtypeStruct((m, d), f32),
        compiler_params=pltpu.CompilerParams(
            dimension_semantics=("parallel",),
            vmem_limit_bytes=VMEM_LIMIT),
        name="merge_out",
    )(oa, ob, wide, wide, x2d, wpf, wpr, wo, gain)


def _branches(x2d, batch, seq, norm_gain, w_in, fox_forget_bias, rwkv_shift_mix, rwkv_w0, rwkv_w2,
              rwkv_a0, rwkv_a2, rwkv_k_k, rwkv_k_a, rwkv_r_k, rwkv_ln_w, rwkv_ln_b,
              w_proj_fox, w_proj_rwkv, w_out):
    fw, rw = FOX_WIDTH, RWKV_WIDTH
    w_t = w_in.T
    r_qkv = 0
    r_za = r_qkv + 3 * fw
    r_f = r_za + fw
    r_rkvz = r_f + FOX_HEADS
    r_wd = r_rkvz + 4 * rw
    r_ad = r_wd + LORA
    r_g = r_ad + LORA
    pieces = [(r_qkv + fw, 2 * fw), (r_g, 2 * D_MODEL), (r_za, fw), (r_rkvz, 4 * rw)]
    pad_rows = lambda a: jnp.pad(a, ((0, LANES - a.shape[0]), (0, 0)))
    w_tail = jnp.concatenate(
        [pad_rows(w_t[r_wd:r_wd + LORA]),
         pad_rows(jnp.concatenate([w_t[r_ad:r_ad + LORA], w_t[r_f:r_f + FOX_HEADS]], axis=0))], axis=0)
    h2d, q = _norm_q(x2d, norm_gain.reshape(1, D_MODEL), w_t, FOX_HEAD_DIM ** -0.5 * LOG2E)
    kv, wide, small = _in_proj(h2d, w_t, pieces, w_tail)

    bias_row = jnp.pad(fox_forget_bias.reshape(1, FOX_HEADS), ((0, 0), (F_LANE, LANES - F_LANE - FOX_HEADS)))
    c = _gate_cumsum(small, bias_row, batch, seq)
    oa, *w16 = _fox_attention(q.reshape(batch, seq, fw), kv.reshape(batch, seq, 2 * fw),
                              c.reshape(batch, seq, LANES), wide.reshape(batch, seq, WIDE_COLS),
                              w_proj_fox, w_proj_rwkv, w_out, batch, seq)
    oa = oa.reshape(batch * seq, fw)

    mu = rwkv_shift_mix
    pc = jnp.zeros((P_ROWS, rw), f32)
    rows = [mu[0:rw], mu[rw:2 * rw], mu[2 * rw:3 * rw], mu[3 * rw:4 * rw], rwkv_w0, rwkv_a0,
            rwkv_k_k, rwkv_k_a, rwkv_r_k.reshape(rw), rwkv_ln_w, rwkv_ln_b]
    pc = pc.at[:len(rows)].set(jnp.stack(rows))
    ps = jnp.zeros((SUBLANES, LANES), f32)
    ps = ps.at[0, :LORA].set(mu[4 * rw:4 * rw + LORA]).at[1, :LORA].set(mu[4 * rw + LORA:])
    w2p = jnp.pad(rwkv_w2, ((0, LANES - LORA), (0, 0))).astype(bf16)
    a2p = jnp.pad(rwkv_a2, ((0, LANES - LORA), (0, 0))).astype(bf16)
    ob = _rwkv_mix(wide, small, pc, ps, w2p, a2p, batch, seq)

    return oa, ob, wide, w16


def kernel(x, norm_gain, w_in, fox_forget_bias, rwkv_shift_mix, rwkv_w0, rwkv_w2, rwkv_a0, rwkv_a2, rwkv_k_k, rwkv_k_a, rwkv_r_k, rwkv_ln_w, rwkv_ln_b, w_proj_fox, w_proj_rwkv, w_out, final_norm_gain):
    batch, seq, d = x.shape
    depth = norm_gain.shape[0]
    assert depth == 1, "the final rmsnorm is fused into the single layer's output kernel"
    x2d = x.reshape(batch * seq, d)
    oa, ob, wide, (wpf16, wpr16, wo16) = _branches(
        x2d, batch, seq, norm_gain[0], w_in[0], fox_forget_bias[0], rwkv_shift_mix[0], rwkv_w0[0], rwkv_w2[0],
        rwkv_a0[0], rwkv_a2[0], rwkv_k_k[0], rwkv_k_a[0], rwkv_r_k[0], rwkv_ln_w[0], rwkv_ln_b[0],
        w_proj_fox[0], w_proj_rwkv[0], w_out[0])
    out = _merge_out(oa, ob, wide, x2d, wpf16, wpr16, wo16, final_norm_gain.reshape(1, d))
    return out.reshape(batch, seq, d)
```

```python
import functools

import jax
import jax.numpy as jnp
from jax import lax
from jax.experimental import pallas as pl
from jax.experimental.pallas import tpu as pltpu

D_MODEL = 2048
FOX_HEADS = 8
FOX_HEAD_DIM = 128
FOX_WIDTH = FOX_HEADS * FOX_HEAD_DIM
RWKV_HEADS = 16
RWKV_HEAD_DIM = 64
RWKV_WIDTH = RWKV_HEADS * RWKV_HEAD_DIM
LORA = 96
RMS_EPS = 1e-6
GN_EPS = 64e-5
L2_EPS = 1e-12

LANES = 128
SUBLANES = 8
VMEM_LIMIT = 56 * 1024 * 1024

COL_GA = 0
COL_GB = D_MODEL // LANES
COL_ZA = 2 * D_MODEL // LANES
COL_R = COL_ZA + FOX_WIDTH // LANES
COL_K = COL_R + RWKV_WIDTH // LANES
COL_V = COL_K + RWKV_WIDTH // LANES
COL_ZB = COL_V + RWKV_WIDTH // LANES
WIDE_COLS = (COL_ZB + RWKV_WIDTH // LANES) * LANES
SMALL_COLS = 2 * LANES
F_LANE = LORA

RWKV_CHUNK = 64
RWKV_BLOCK = 512
RWKV_TAIL_START = 0.3
RWKV_CHAIN_END = 0.8
HEADS_PER_GROUP = LANES // RWKV_HEAD_DIM
GROUP_ROWS = HEADS_PER_GROUP * RWKV_CHUNK

f32 = jnp.float32
bf16 = jnp.bfloat16


def _dot(a, b):
    return jnp.dot(a.astype(bf16), b.astype(bf16), preferred_element_type=f32)


def _dot_exact_ones(a, ones_bf16, nt=False, terms=3):
    if nt:
        mm = lambda x: lax.dot_general(ones_bf16, x, (((1,), (0,)), ((), ())),
                                       preferred_element_type=f32)
    else:
        mm = lambda x: jnp.dot(x, ones_bf16, preferred_element_type=f32)
    part = a.astype(bf16)
    out = mm(part)
    rest = a
    for _ in range(terms - 1):
        rest = rest - part.astype(f32)
        part = rest.astype(bf16)
        out = out + mm(part)
    return out


def _softplus(x):
    return jnp.maximum(x, 0.0) + jnp.log(1.0 + jnp.exp(-jnp.abs(x)))


def _sigmoid(x):
    return 1.0 / (1.0 + jnp.exp(-x))


PROJ_TM = 1024
PROJ_TN = 1024


def _norm_q_kernel(x_ref, g_ref, w_ref, h_ref, q_ref, w16_ref, *, scale):
    @pl.when(pl.program_id(0) == 0)
    def _():
        w16_ref[...] = (w_ref[...] * scale).astype(bf16)

    xf = x_ref[...]
    ms = jnp.mean(xf * xf, axis=-1, keepdims=True)
    h = (xf * lax.rsqrt(ms + RMS_EPS) * g_ref[...]).astype(bf16)
    h_ref[...] = h
    q_ref[...] = lax.dot_general(h, w16_ref[...], (((1,), (1,)), ((), ())),
                                 preferred_element_type=f32).astype(q_ref.dtype)


def _norm_q(x2d, gain, w_t, scale):
    m, d = x2d.shape
    tm = PROJ_TM
    return pl.pallas_call(
        functools.partial(_norm_q_kernel, scale=scale),
        grid=(m // tm,),
        in_specs=[
            pl.BlockSpec((tm, d), lambda i: (i, 0)),
            pl.BlockSpec((1, d), lambda i: (0, 0)),
            pl.BlockSpec((FOX_WIDTH, d), lambda i: (0, 0)),
        ],
        out_specs=[pl.BlockSpec((tm, d), lambda i: (i, 0)), pl.BlockSpec((tm, FOX_WIDTH), lambda i: (i, 0))],
        out_shape=[jax.ShapeDtypeStruct((m, d), bf16), jax.ShapeDtypeStruct((m, FOX_WIDTH), bf16)],
        scratch_shapes=[pltpu.VMEM((FOX_WIDTH, d), bf16)],
        compiler_params=pltpu.CompilerParams(dimension_semantics=("arbitrary",), vmem_limit_bytes=VMEM_LIMIT),
        name="norm_q",
    )(x2d, gain, w_t)


def _in_proj_kernel(h_ref, w_ref, wt_ref, kv_ref, wide_ref, small_ref, w16_ref, *, n_kv, n_wide):
    j = pl.program_id(0)
    i = pl.program_id(1)
    n_main = n_kv + n_wide

    @pl.when(i == 0)
    def _():
        @pl.when(j < n_main)
        def _():
            w16_ref[...] = w_ref[...].astype(bf16)

        @pl.when(j == n_main)
        def _():
            w16_ref[0:SMALL_COLS, :] = wt_ref[...].astype(bf16)

    project = lambda w: lax.dot_general(h_ref[...], w, (((1,), (1,)), ((), ())), preferred_element_type=f32)

    @pl.when(j < n_kv)
    def _():
        kv_ref[...] = project(w16_ref[...]).astype(kv_ref.dtype)

    @pl.when(jnp.logical_and(j >= n_kv, j < n_main))
    def _():
        wide_ref[...] = project(w16_ref[...])

    @pl.when(j == n_main)
    def _():
        small_ref[...] = project(w16_ref[0:SMALL_COLS, :])


def _in_proj(h2d, w_t, pieces, w_tail):
    m, d = h2d.shape
    tm, tn = PROJ_TM, PROJ_TN
    n_kv = 2 * FOX_WIDTH // tn
    n_wide = WIDE_COLS // tn
    n_main = n_kv + n_wide
    n_i = m // tm
    assert n_kv * tn == 2 * FOX_WIDTH and n_wide * tn == WIDE_COLS and w_tail.shape[0] == SMALL_COLS
    assert all(n % tn == 0 and src % SUBLANES == 0 for src, n in pieces)
    assert sum(n for _, n in pieces) == n_main * tn
    starts, first_tile = [], 0
    for src, n in pieces:
        starts.append((first_tile, src))
        first_tile += n // tn

    def src_row(j):
        tile = jnp.int32(0)
        for blk, src in starts:
            tile = jnp.where(j >= blk, src // SUBLANES + (jnp.minimum(j, n_main - 1) - blk) * (tn // SUBLANES), tile)
        return pl.multiple_of(tile * SUBLANES, SUBLANES)

    kv_row = lambda j, i: jnp.where(j < n_kv, i, n_i - 1)
    wide_row = lambda j, i: jnp.where(j < n_kv, 0, jnp.where(j < n_main, i, n_i - 1))
    small_row = lambda j, i: jnp.where(j == n_main, i, 0)
    return pl.pallas_call(
        functools.partial(_in_proj_kernel, n_kv=n_kv, n_wide=n_wide),
        grid=(n_main + 1, n_i),
        in_specs=[
            pl.BlockSpec((tm, d), lambda j, i: (i, 0)),
            pl.BlockSpec((pl.Element(tn), pl.Element(d)), lambda j, i: (src_row(j), 0)),
            pl.BlockSpec((SMALL_COLS, d), lambda j, i: (0, 0)),
        ],
        out_specs=[
            pl.BlockSpec((tm, tn), lambda j, i: (kv_row(j, i), jnp.minimum(j, n_kv - 1))),
            pl.BlockSpec((tm, tn), lambda j, i: (wide_row(j, i), jnp.clip(j - n_kv, 0, n_wide - 1))),
            pl.BlockSpec((tm, SMALL_COLS), lambda j, i: (small_row(j, i), 0)),
        ],
        out_shape=[
            jax.ShapeDtypeStruct((m, 2 * FOX_WIDTH), bf16),
            jax.ShapeDtypeStruct((m, WIDE_COLS), f32),
            jax.ShapeDtypeStruct((m, SMALL_COLS), f32),
        ],
        scratch_shapes=[pltpu.VMEM((tn, d), bf16)],
        compiler_params=pltpu.CompilerParams(
            dimension_semantics=("arbitrary", "arbitrary"),
            vmem_limit_bytes=VMEM_LIMIT),
        name="in_proj",
    )(h2d, w_t, w_tail)


GATE_SUB_BLOCK = 256


def _gate_kernel(f_ref, bias_ref, c_ref, carry_ref, *, tb):
    @pl.when(pl.program_id(1) == 0)
    def _():
        carry_ref[...] = jnp.zeros_like(carry_ref)

    z = f_ref[...] + bias_ref[...]
    log_f = -_softplus(-z)
    sb = GATE_SUB_BLOCK
    row = lax.broadcasted_iota(jnp.int32, (sb, sb), 0)
    col = lax.broadcasted_iota(jnp.int32, (sb, sb), 1)
    tri = jnp.where(col <= row, 1.0, 0.0).astype(bf16)
    carry = carry_ref[0:1, :]
    for k in range(0, tb, sb):
        c = _dot_exact_ones(log_f[k:k + sb], tri, nt=True) + carry
        c_ref[k:k + sb, :] = c
        carry = c[sb - 1:sb, :]
    carry_ref[0:1, :] = carry


def _gate_cumsum(small, bias_row, batch, seq):
    tb = 1024
    nt = seq // tb
    return pl.pallas_call(
        functools.partial(_gate_kernel, tb=tb),
        grid=(batch, nt),
        in_specs=[
            pl.BlockSpec((tb, LANES), lambda b, t: (b * nt + t, 1)),
            pl.BlockSpec((1, LANES), lambda b, t: (0, 0)),
        ],
        out_specs=pl.BlockSpec((tb, LANES), lambda b, t: (b * nt + t, 0)),
        out_shape=jax.ShapeDtypeStruct((batch * seq, LANES), f32),
        scratch_shapes=[pltpu.VMEM((SUBLANES, LANES), f32)],
        compiler_params=pltpu.CompilerParams(
            dimension_semantics=("parallel", "arbitrary")),
        name="fox_gate_cumsum",
    )(small, bias_row)


LOG2E = 1.4426950408889634
FOX_TK = 256
FOX_GW = 256
FOX_BATCH = 4


def _fox_kernel(q_ref, k_ref, v_ref, c_ref, z_ref, wf_ref, wr_ref, wo_ref, o_ref, wf16_ref, wr16_ref, wo16_ref,
                crep_ref, acc_ref, vt_ref, *, seq, tk, gw):
    wf16_ref[...] = wf_ref[...].astype(bf16)
    wr16_ref[...] = wr_ref[...].astype(bf16)
    wo16_ref[...] = wo_ref[...].astype(bf16)

    h = pl.program_id(1)
    ng = seq // gw
    nt = seq // tk
    assert seq % gw == 0 and seq % tk == 0 and (tk % gw == 0 or gw % tk == 0)
    mm = lambda a, b: jnp.dot(a, b, preferred_element_type=f32)
    mm_nt = lambda a, b: lax.dot_general(a, b, (((1,), (1,)), ((), ())), preferred_element_type=f32)

    lane = lax.broadcasted_iota(jnp.int32, (1, LANES), 1)
    onehot = jnp.where(lane == F_LANE + h, LOG2E, 0.0)
    crep_ref[...] = jnp.sum(c_ref[...] * onehot, axis=1, keepdims=True)
    for t in range(nt):
        vt_ref[t, 0:LANES, :] = v_ref[t * tk:(t + 1) * tk, :].T
        vt_ref[t, LANES:LANES + SUBLANES, :] = jnp.ones((SUBLANES, tk), bf16)

    key_rel = lax.broadcasted_iota(jnp.int32, (tk, gw), 0)
    qry_rel = lax.broadcasted_iota(jnp.int32, (tk, gw), 1)

    def visible(t, g):
        return t * tk <= g * gw + gw - 1

    def needs_mask(t, g):
        return t * tk + tk - 1 > g * gw

    last_tile = [max(t for t in range(nt) if visible(t, g)) for g in range(ng)]
    sched = []
    for t in range(nt):
        groups = [g for g in range(ng) if visible(t, g)]
        sched += [[(t, g) for g in groups[k:k + FOX_BATCH]] for k in range(0, len(groups), FOX_BATCH)]

    def qk(batch):
        return [mm_nt(k_ref[t * tk:(t + 1) * tk, :], q_ref[g * gw:(g + 1) * gw, :]) for t, g in batch]

    m = [None] * ng
    l = [None] * ng

    def finish(batch, alpha, pv):
        for (t, g), a, xs in zip(batch, alpha, pv):
            x, p_sum = xs[0:LANES], xs[LANES:LANES + 1]
            acc_ref[g] = x if t == 0 else a * acc_ref[g] + x
            l[g] = p_sum if t == 0 else a * l[g] + p_sum
            if t == last_tile[g]:
                rows = slice(g * gw, (g + 1) * gw)
                z = z_ref[rows, :]
                o = (acc_ref[g] / l[g]).T
                o_ref[rows, :] = (o * (z * _sigmoid(z))).astype(o_ref.dtype)

    s_next = qk(sched[0])
    pending = None
    for bi, batch in enumerate(sched):
        s_cur = s_next
        if bi + 1 < len(sched):
            s_next = qk(sched[bi + 1])
        alpha, p16 = [], []
        for (t, g), s in zip(batch, s_cur):
            s = s - crep_ref[t * tk:(t + 1) * tk, :]
            if needs_mask(t, g):
                s = jnp.where(key_rel + t * tk > qry_rel + g * gw, -jnp.inf, s)
            m_tile = jnp.max(s, axis=0, keepdims=True)
            if t == 0:
                m_new, a = m_tile, None
            else:
                m_new = jnp.maximum(m[g], m_tile)
                a = jnp.exp2(m[g] - m_new)
            m[g] = m_new
            alpha.append(a)
            p16.append(jnp.exp2((s - m_new).astype(bf16)))
        pv = [mm(vt_ref[t], p) for (t, g), p in zip(batch, p16)]
        if pending is not None:
            finish(*pending)
        pending = (batch, alpha, pv)
    finish(*pending)


def _fox_attention(q3, kv3, c3, wide3, w_proj_fox, w_proj_rwkv, w_out, batch, seq):
    tk, gw = FOX_TK, FOX_GW
    h8 = FOX_HEADS
    steps = batch * h8
    assert all(w.shape[0] % (steps * 2 * SUBLANES) == 0 for w in (w_proj_fox, w_proj_rwkv, w_out))
    slab = lambda w: pl.BlockSpec((w.shape[0] // steps, w.shape[1]), lambda b, h: (b * h8 + h, 0))
    return pl.pallas_call(
        functools.partial(_fox_kernel, seq=seq, tk=tk, gw=gw),
        grid=(batch, h8),
        in_specs=[
            pl.BlockSpec((None, seq, LANES), lambda b, h: (b, 0, h)),
            pl.BlockSpec((None, seq, LANES), lambda b, h: (b, 0, h)),
            pl.BlockSpec((None, seq, LANES), lambda b, h: (b, 0, h8 + h)),
            pl.BlockSpec((None, seq, LANES), lambda b, h: (b, 0, 0)),
            pl.BlockSpec((None, seq, LANES), lambda b, h: (b, 0, COL_ZA + h)),
            slab(w_proj_fox), slab(w_proj_rwkv), slab(w_out),
        ],
        out_specs=[pl.BlockSpec((None, seq, LANES), lambda b, h: (b, 0, h)),
                   slab(w_proj_fox), slab(w_proj_rwkv), slab(w_out)],
        out_shape=[jax.ShapeDtypeStruct((batch, seq, FOX_WIDTH), bf16),
                   jax.ShapeDtypeStruct(w_proj_fox.shape, bf16),
                   jax.ShapeDtypeStruct(w_proj_rwkv.shape, bf16),
                   jax.ShapeDtypeStruct(w_out.shape, bf16)],
        scratch_shapes=[
            pltpu.VMEM((seq, 1), f32),
            pltpu.VMEM((seq // gw, LANES, gw), f32),
            pltpu.VMEM((seq // tk, LANES + SUBLANES, tk), bf16),
        ],
        compiler_params=pltpu.CompilerParams(
            dimension_semantics=("parallel", "arbitrary"),
            vmem_limit_bytes=VMEM_LIMIT),
        name="fox_attention",
    )(q3, kv3, kv3, c3, wide3, w_proj_fox, w_proj_rwkv, w_out)


P_MU_R, P_MU_K, P_MU_V, P_MU_Z, P_W0, P_A0, P_KK, P_KA, P_RK, P_LNW, P_LNB = range(11)
P_ROWS = 16
K_XA, K_XR, K_YB, K_YK, K_VS, K_BH, K_KH = range(7)


def _spread(emitters, lo=0.0, hi=1.0):
    n = len(emitters)
    return [(lo + (hi - lo) * (i + 0.5) / n, e) for i, e in enumerate(emitters)]


def _interleave(*segment_lists):
    keyed = [(pos, prio, seg) for prio, segs in enumerate(segment_lists) for pos, seg in segs]
    keyed.sort(key=lambda x: (x[0], x[1]))
    for _, _, seg in keyed:
        seg()


def _rwkv_kernel(r_ref, k_ref, v_ref, z_ref, wd_ref, ad_ref, pc_ref, ps_ref, w2_ref, a2_ref, pc3_ref,
                 o_ref,
                 state_ref, prev_ref, stk_ref, xr32_ref, aux1_ref, dec1_ref, p16_ref, p32_ref,
                 aux2_ref, dec2_ref, *, tb, nt):
    C = RWKV_CHUNK
    G = GROUP_ROWS
    N = RWKV_HEAD_DIM
    NC = tb // C
    s = pl.program_id(0)
    first_of_seq_1 = (s % nt) == 0
    first_of_seq_3 = ((s + 2 * nt - 2) % nt) == 0

    @pl.when(s == 0)
    def _():
        for ref in (state_ref, prev_ref, stk_ref, xr32_ref, aux1_ref, dec1_ref, p16_ref, p32_ref,
                    aux2_ref, dec2_ref):
            ref[...] = jnp.zeros_like(ref)

    mm = lambda a, b: jnp.dot(a, b, preferred_element_type=f32)
    mm_nt = lambda a, b: lax.dot_general(a, b, (((1,), (1,)), ((), ())), preferred_element_type=f32)
    mm_tn = lambda a, b: lax.dot_general(a, b, (((0,), (0,)), ((), ())), preferred_element_type=f32)

    def body(cur, prv):
        lane_t = lax.broadcasted_iota(jnp.int32, (1, LANES), 1)
        head_masks = [jnp.where(lane_t // N == h, 1.0, 0.0) for h in range(HEADS_PER_GROUP)]

        def head_sum(x):
            sums = [jnp.sum(x * hm, axis=1, keepdims=True) for hm in head_masks]
            out = sums[HEADS_PER_GROUP - 1]
            for hh in range(HEADS_PER_GROUP - 2, -1, -1):
                out = jnp.where(lane_t // N == hh, sums[hh], out)
            return out

        stack = lambda x: jnp.concatenate([x * hm for hm in head_masks], axis=0)
        lane_c = lax.broadcasted_iota(jnp.int32, (C, LANES), 1)
        head_sel = [lane_c // N == h for h in range(HEADS_PER_GROUP)]

        def stack16(x):
            xb = x.astype(bf16)
            return jnp.concatenate([jnp.where(m, xb, jnp.zeros_like(xb)) for m in head_sel], axis=0)

        gi = lax.broadcasted_iota(jnp.int32, (G, G), 0)
        gj = lax.broadcasted_iota(jnp.int32, (G, G), 1)
        same_head = gi // C == gj // C
        strict = jnp.logical_and(same_head, gj < gi)
        incl = jnp.logical_and(same_head, gj <= gi)
        eye = jnp.where(gi == gj, 1.0, 0.0)
        chunks = range(NC)

        R = 2 * C
        subs = range(tb // R)
        v1 = [dict() for _ in subs]
        prm = lambda idx: pc_ref[idx:idx + 1, :]
        first_row = lax.broadcasted_iota(jnp.int32, (R, LANES), 0) == 0
        ti = lax.broadcasted_iota(jnp.int32, (R, R), 0)
        tj = lax.broadcasted_iota(jnp.int32, (R, R), 1)
        pair_tri = jnp.where(jnp.logical_and(tj <= ti, ti // C == tj // C), 1.0, 0.0).astype(bf16)

        def shifted(ref, slot, mu, k):
            u = ref[k * R:(k + 1) * R, :]
            if k == 0:
                before = jnp.where(first_of_seq_1, 0.0, prev_ref[slot, 0:1, :])
            else:
                before = ref[k * R - 1:k * R, :]
            prev = jnp.where(first_row, before, pltpu.roll(u, 1, 0))
            if k == len(subs) - 1:
                prev_ref[slot, 0:1, :] = u[R - 1:R, :]
            return u + (prev - u) * mu

        def s1_lora(k):
            def emit():
                d = v1[k]
                d["w_lin"] = _dot(jnp.tanh(shifted(wd_ref, 4, ps_ref[0:1, :], k)), w2_ref[...])
                d["a_lin"] = _dot(shifted(ad_ref, 5, ps_ref[1:2, :], k), a2_ref[...])
            return emit

        def s1_decay(k):
            def emit():
                d = v1[k]
                w = -_softplus(-(prm(P_W0) + d["w_lin"])) - 0.5
                d["log_decay"] = -jnp.exp(w)
                d["ci"] = _dot_exact_ones(d["log_decay"], pair_tri, nt=True)
            return emit

        def s1_rest(k):
            def emit():
                d = v1[k]
                rows = slice(k * R, (k + 1) * R)
                kr = shifted(k_ref, 1, prm(P_MU_K), k)
                kk = kr * prm(P_KK)
                kk = kk / jnp.maximum(jnp.sqrt(head_sum(kk * kk)), L2_EPS)
                rate = _sigmoid(prm(P_A0) + d["a_lin"])
                r = shifted(r_ref, 0, prm(P_MU_R), k)
                d["vr"] = shifted(v_ref, 2, prm(P_MU_V), k)
                d["kp"] = kr * (1.0 + (rate - 1.0) * prm(P_KA))
                aux1_ref[cur, 0, rows, :] = head_sum(r * d["kp"] * prm(P_RK)) * d["vr"]
                zb = shifted(z_ref, 3, prm(P_MU_Z), k)
                aux1_ref[cur, 1, rows, :] = zb * _sigmoid(zb)
                d["bb"] = kk * rate
                ci = d["ci"]
                d["a_t"] = -kk * jnp.exp(ci - d["log_decay"])
                d["r_t"] = r * jnp.exp(ci)
                inv = jnp.exp(-ci)
                d["b_t"] = d["bb"] * inv
                d["k_t"] = d["kp"] * inv
            return emit

        def s1_stack(c):
            def emit():
                d = v1[c // 2]
                sl = slice((c % 2) * C, (c % 2 + 1) * C)
                ci_c = d["ci"][sl]
                c_last = ci_c[C - 1:C, :]
                to_end = jnp.exp(c_last - ci_c)
                dec1_ref[cur, c] = jnp.broadcast_to(jnp.exp(c_last), (SUBLANES, LANES))
                xr = stack(d["r_t"][sl])
                xr32_ref[cur, c] = xr
                stk_ref[cur, K_XR, c] = xr.astype(bf16)
                stk_ref[cur, K_XA, c] = stack16(d["a_t"][sl])
                stk_ref[cur, K_YB, c] = stack16(d["b_t"][sl])
                stk_ref[cur, K_YK, c] = stack16(d["k_t"][sl])
                stk_ref[cur, K_VS, c] = stack16(d["vr"][sl])
                stk_ref[cur, K_BH, c] = stack16(d["bb"][sl] * to_end)
                stk_ref[cur, K_KH, c] = stack16(d["kp"][sl] * to_end)
            return emit

        tail = []
        for k in subs:
            tail += [s1_rest(k), s1_stack(2 * k), s1_stack(2 * k + 1)]
        segs1 = ([(0.0, s1_lora(k)) for k in subs] + [(0.05, s1_decay(k)) for k in subs]
                 + _spread(tail, RWKV_TAIL_START, 1.0))

        v2 = {}
        ld = lambda kind, c: stk_ref[prv, kind, c]

        def s2_big():
            big = [mm_nt(jnp.concatenate([ld(K_XA, c), ld(K_XR, c)], axis=0),
                         jnp.concatenate([ld(K_YB, c), ld(K_YK, c)], axis=0)) for c in chunks]
            v2["a_ab"] = [jnp.where(strict, big[c][0:G, 0:G], 0.0) for c in chunks]
            v2["a_ak"] = [jnp.where(strict, big[c][0:G, G:2 * G], 0.0).astype(bf16) for c in chunks]
            v2["a_rb"] = [jnp.where(incl, big[c][G:2 * G, 0:G], 0.0).astype(bf16) for c in chunks]
            v2["a_rk"] = [jnp.where(incl, big[c][G:2 * G, G:2 * G], 0.0).astype(bf16) for c in chunks]
            v2["tinv"] = [eye + a for a in v2["a_ab"]]
            v2["pw"] = [a.astype(bf16) for a in v2["a_ab"]]

        def s2_first_square():
            v2["pw"] = [mm(p, p).astype(bf16) for p in v2["pw"]]

        def s2_level(last):
            def emit():
                for c in chunks:
                    pw = v2["pw"][c]
                    t16 = v2["tinv"][c].astype(bf16)
                    if last:
                        v2["tinv"][c] = v2["tinv"][c] + mm(t16, pw)
                    else:
                        both = mm(jnp.concatenate([t16, pw], axis=0), pw)
                        v2["tinv"][c] = v2["tinv"][c] + both[0:G]
                        v2["pw"][c] = both[G:2 * G].astype(bf16)
            return emit

        def s2_av():
            v2["tinv"] = [t.astype(bf16) for t in v2["tinv"]]
            av = [mm(jnp.concatenate([v2["a_ak"][c], v2["a_rk"][c]], axis=0), ld(K_VS, c)) for c in chunks]
            v2["akv"] = [x[0:G].astype(bf16) for x in av]
            v2["arkv"] = [x[G:2 * G] for x in av]

        def s2_wu():
            v2["wu"] = [mm(v2["tinv"][c], jnp.concatenate([ld(K_XA, c), v2["akv"][c]], axis=1)).astype(bf16)
                        for c in chunks]

        def s2_ry():
            for c in chunks:
                e = mm(v2["a_rb"][c], v2["wu"][c])
                p16_ref[cur, 0, c] = (xr32_ref[prv, c] + e[:, 0:LANES]).astype(bf16)
                p32_ref[cur, 0, c] = e[:, LANES:] + v2["arkv"][c]

        def s2_pm():
            for c in chunks:
                p16_ref[cur, 1, c] = mm_tn(v2["wu"][c][:, 0:LANES], ld(K_BH, c)).astype(bf16)

        def s2_q():
            for c in chunks:
                p32_ref[cur, 1, c] = mm_tn(jnp.concatenate([v2["wu"][c][:, LANES:], ld(K_VS, c)], axis=0),
                                           jnp.concatenate([ld(K_BH, c), ld(K_KH, c)], axis=0))
            aux2_ref[cur] = aux1_ref[prv]
            dec2_ref[cur] = dec1_ref[prv]

        segs2 = _spread([s2_big, s2_first_square] + [s2_level(False)] * 4 + [s2_level(True)]
                        + [s2_av, s2_wu, s2_ry, s2_pm, s2_q])

        v3 = {"ys": []}
        prm3 = lambda idx: pc3_ref[idx:idx + 1, :]

        def s3_start():
            v3["state"] = jnp.where(first_of_seq_3, 0.0, state_ref[...])

        def s3_chunk(c):
            def emit():
                state = v3["state"]
                s16 = state.astype(bf16)
                y_st = mm_nt(p16_ref[prv, 0, c], s16) + p32_ref[prv, 0, c]
                v3["ys"].append(y_st[0:C] + y_st[C:2 * C])
                v3["state"] = (state * dec2_ref[prv, c, 0:1, :] + mm(s16, p16_ref[prv, 1, c])
                               + p32_ref[prv, 1, c])
            return emit

        def s3_finish():
            state_ref[...] = v3["state"]
            y = jnp.concatenate(v3["ys"], axis=0)
            mean = head_sum(y) * (1.0 / N)
            yc = y - mean
            var = head_sum(yc * yc) * (1.0 / N)
            yn = yc * lax.rsqrt(var + GN_EPS) * prm3(P_LNW) + prm3(P_LNB)
            o_ref[...] = ((yn + aux2_ref[prv, 0]) * aux2_ref[prv, 1]).astype(o_ref.dtype)

        segs3 = _spread([s3_start] + [s3_chunk(c) for c in chunks] + [s3_finish], 0.0, RWKV_CHAIN_END)

        _interleave(segs3, segs2, segs1)

    for parity in (0, 1):
        @pl.when(s % 2 == parity)
        def _(parity=parity):
            body(cur=parity, prv=1 - parity)


def _rwkv_mix(wide, small, pc, ps, w2p, a2p, batch, seq):
    tb = RWKV_BLOCK
    nt = seq // tb
    groups = RWKV_WIDTH // LANES
    nc = tb // RWKV_CHUNK
    n_blocks = batch * groups * nt

    def where(s):
        s = jnp.clip(s, 0, n_blocks - 1)
        return (s // (nt * groups)) * nt + s % nt, (s // nt) % groups

    col = lambda c0: (lambda s: (where(s)[0], c0 + where(s)[1]))
    fixed_col = lambda c: (lambda s: (where(s)[0], c))
    grp = lambda s: (0, where(s)[1])
    return pl.pallas_call(
        functools.partial(_rwkv_kernel, tb=tb, nt=nt),
        grid=(n_blocks + 2,),
        in_specs=[
            pl.BlockSpec((tb, LANES), col(COL_R)),
            pl.BlockSpec((tb, LANES), col(COL_K)),
            pl.BlockSpec((tb, LANES), col(COL_V)),
            pl.BlockSpec((tb, LANES), col(COL_ZB)),
            pl.BlockSpec((tb, LANES), fixed_col(0)),
            pl.BlockSpec((tb, LANES), fixed_col(1)),
            pl.BlockSpec((P_ROWS, LANES), grp),
            pl.BlockSpec((SUBLANES, LANES), lambda s: (0, 0)),
            pl.BlockSpec((LANES, LANES), grp),
            pl.BlockSpec((LANES, LANES), grp),
            pl.BlockSpec((P_ROWS, LANES), lambda s: grp(s - 2)),
        ],
        out_specs=pl.BlockSpec((tb, LANES), lambda s: (where(s - 2)[0], where(s - 2)[1])),
        out_shape=jax.ShapeDtypeStruct((batch * seq, RWKV_WIDTH), bf16),
        scratch_shapes=[
            pltpu.VMEM((LANES, LANES), f32),
            pltpu.VMEM((6, SUBLANES, LANES), f32),
            pltpu.VMEM((2, 7, nc, GROUP_ROWS, LANES), bf16),
            pltpu.VMEM((2, nc, GROUP_ROWS, LANES), f32),
            pltpu.VMEM((2, 2, tb, LANES), f32),
            pltpu.VMEM((2, nc, SUBLANES, LANES), f32),
            pltpu.VMEM((2, 2, nc, GROUP_ROWS, LANES), bf16),
            pltpu.VMEM((2, 2, nc, GROUP_ROWS, LANES), f32),
            pltpu.VMEM((2, 2, tb, LANES), f32),
            pltpu.VMEM((2, nc, SUBLANES, LANES), f32),
        ],
        compiler_params=pltpu.CompilerParams(
            dimension_semantics=("arbitrary",),
            vmem_limit_bytes=VMEM_LIMIT),
        name="rwkv_mix",
    )(wide, wide, wide, wide, small, small, pc, ps, w2p, a2p, pc)


def _out_kernel(oa_ref, ob_ref, ga_ref, gb_ref, x_ref, wpf_ref, wpr_ref, wo_ref, g_ref, o_ref):
    pa = jnp.dot(oa_ref[...], wpf_ref[...], preferred_element_type=f32)
    pb = jnp.dot(ob_ref[...], wpr_ref[...], preferred_element_type=f32)
    m = _sigmoid(ga_ref[...]) * pa + _sigmoid(gb_ref[...]) * pb
    z = x_ref[...] + jnp.dot(m.astype(bf16), wo_ref[...], preferred_element_type=f32)
    ms = jnp.mean(z * z, axis=-1, keepdims=True)
    o_ref[...] = z * lax.rsqrt(ms + RMS_EPS) * g_ref[...]


def _merge_out(oa, ob, wide, x2d, wpf, wpr, wo, gain):
    m, d = x2d.shape
    tm = 256
    resident = lambda shape: pl.BlockSpec(shape, lambda i: (0, 0), pipeline_mode=pl.Buffered(1))
    return pl.pallas_call(
        _out_kernel,
        grid=(m // tm,),
        in_specs=[
            pl.BlockSpec((tm, FOX_WIDTH), lambda i: (i, 0)),
            pl.BlockSpec((tm, RWKV_WIDTH), lambda i: (i, 0)),
            pl.BlockSpec((tm, d), lambda i: (i, COL_GA * LANES // d)),
            pl.BlockSpec((tm, d), lambda i: (i, COL_GB * LANES // d)),
            pl.BlockSpec((tm, d), lambda i: (i, 0)),
            resident((FOX_WIDTH, d)),
            resident((RWKV_WIDTH, d)),
            resident((d, d)),
            resident((1, d)),
        ],
        out_specs=pl.BlockSpec((tm, d), lambda i: (i, 0)),
        out_shape=jax.ShapeDtypeStruct((m, d), f32),
        compiler_params=pltpu.CompilerParams(
            dimension_semantics=("parallel",),
            vmem_limit_bytes=VMEM_LIMIT),
        name="merge_out",
    )(oa, ob, wide, wide, x2d, wpf, wpr, wo, gain)


def _branches(x2d, batch, seq, norm_gain, w_in, fox_forget_bias, rwkv_shift_mix, rwkv_w0, rwkv_w2,
              rwkv_a0, rwkv_a2, rwkv_k_k, rwkv_k_a, rwkv_r_k, rwkv_ln_w, rwkv_ln_b,
              w_proj_fox, w_proj_rwkv, w_out):
    fw, rw = FOX_WIDTH, RWKV_WIDTH
    w_t = w_in.T
    r_qkv = 0
    r_za = r_qkv + 3 * fw
    r_f = r_za + fw
    r_rkvz = r_f + FOX_HEADS
    r_wd = r_rkvz + 4 * rw
    r_ad = r_wd + LORA
    r_g = r_ad + LORA
    pieces = [(r_qkv + fw, 2 * fw), (r_g, 2 * D_MODEL), (r_za, fw), (r_rkvz, 4 * rw)]
    pad_rows = lambda a: jnp.pad(a, ((0, LANES - a.shape[0]), (0, 0)))
    w_tail = jnp.concatenate(
        [pad_rows(w_t[r_wd:r_wd + LORA]),
         pad_rows(jnp.concatenate([w_t[r_ad:r_ad + LORA], w_t[r_f:r_f + FOX_HEADS]], axis=0))], axis=0)
    h2d, q = _norm_q(x2d, norm_gain.reshape(1, D_MODEL), w_t, FOX_HEAD_DIM ** -0.5 * LOG2E)
    kv, wide, small = _in_proj(h2d, w_t, pieces, w_tail)

    bias_row = jnp.pad(fox_forget_bias.reshape(1, FOX_HEADS), ((0, 0), (F_LANE, LANES - F_LANE - FOX_HEADS)))
    c = _gate_cumsum(small, bias_row, batch, seq)
    oa, *w16 = _fox_attention(q.reshape(batch, seq, fw), kv.reshape(batch, seq, 2 * fw),
                              c.reshape(batch, seq, LANES), wide.reshape(batch, seq, WIDE_COLS),
                              w_proj_fox, w_proj_rwkv, w_out, batch, seq)
    oa = oa.reshape(batch * seq, fw)

    mu = rwkv_shift_mix
    pc = jnp.zeros((P_ROWS, rw), f32)
    rows = [mu[0:rw], mu[rw:2 * rw], mu[2 * rw:3 * rw], mu[3 * rw:4 * rw], rwkv_w0, rwkv_a0,
            rwkv_k_k, rwkv_k_a, rwkv_r_k.reshape(rw), rwkv_ln_w, rwkv_ln_b]
    pc = pc.at[:len(rows)].set(jnp.stack(rows))
    ps = jnp.zeros((SUBLANES, LANES), f32)
    ps = ps.at[0, :LORA].set(mu[4 * rw:4 * rw + LORA]).at[1, :LORA].set(mu[4 * rw + LORA:])
    w2p = jnp.pad(rwkv_w2, ((0, LANES - LORA), (0, 0))).astype(bf16)
    a2p = jnp.pad(rwkv_a2, ((0, LANES - LORA), (0, 0))).astype(bf16)
    ob = _rwkv_mix(wide, small, pc, ps, w2p, a2p, batch, seq)

    return oa, ob, wide, w16


def kernel(x, norm_gain, w_in, fox_forget_bias, rwkv_shift_mix, rwkv_w0, rwkv_w2, rwkv_a0, rwkv_a2, rwkv_k_k, rwkv_k_a, rwkv_r_k, rwkv_ln_w, rwkv_ln_b, w_proj_fox, w_proj_rwkv, w_out, final_norm_gain):
    batch, seq, d = x.shape
    depth = norm_gain.shape[0]
    assert depth == 1, "the final rmsnorm is fused into the single layer's output kernel"
    x2d = x.reshape(batch * seq, d)
    oa, ob, wide, (wpf16, wpr16, wo16) = _branches(
        x2d, batch, seq, norm_gain[0], w_in[0], fox_forget_bias[0], rwkv_shift_mix[0], rwkv_w0[0], rwkv_w2[0],
        rwkv_a0[0], rwkv_a2[0], rwkv_k_k[0], rwkv_k_a[0], rwkv_r_k[0], rwkv_ln_w[0], rwkv_ln_b[0],
        w_proj_fox[0], w_proj_rwkv[0], w_out[0])
    out = _merge_out(oa, ob, wide, x2d, wpf16, wpr16, wo16, final_norm_gain.reshape(1, d))
    return out.reshape(batch, seq, d)
```
